```python
import math
import jax
import jax.numpy as jnp
from jax import lax
import numpy as np

D_MODEL = 1024
BATCH = 8
SEQ = 2048
DEPTH = 1
DEC_BATCH = 32
DEC_SEQ = 8
PAST_LEN = 8192
PAGE_SIZE = 128

HEAD_DIM = 64
NSA_HEADS = D_MODEL // 128
NSA_KV_GROUPS = 2
NSA_REP = NSA_HEADS // NSA_KV_GROUPS
CMP_BLOCK = 32
CMP_STRIDE = 16
CMP_HIDDEN = 2 * HEAD_DIM
SEL_BLOCK = 64
N_SEL = 16
WINDOW = 512
DIFF_HEADS = D_MODEL // 256
DIFF_DV = 2 * HEAD_DIM
ROT_DIM = HEAD_DIM // 4
ROPE_THETA = 500000.0
N_EXPERTS = 32
TOP_K = 4
D_FF = D_MODEL
SWIGLU_LIMIT = 7.0
SWIGLU_ALPHA = 1.702
RMS_EPS = 1e-5
Q_BLOCK = 128
SEL_Q_BLOCK = 32
EXPERT_BLOCK = 128
FORCE_BONUS = 1e4
SCALE = HEAD_DIM ** -0.5
Q_NSA_W = NSA_HEADS * HEAD_DIM
KV_NSA_W = 2 * NSA_KV_GROUPS * HEAD_DIM
GATE_NSA_W = 3 * NSA_HEADS
Q_DIFF_W = DIFF_HEADS * 2 * HEAD_DIM
V_DIFF_W = DIFF_HEADS * DIFF_DV
IN_COLS = Q_NSA_W + 3 * KV_NSA_W + GATE_NSA_W + 2 * Q_DIFF_W + V_DIFF_W + 2 * D_MODEL

kernel_name = 'nsa_diffattn_gated_moe_step'


def rmsnorm(x, g):
    xf = x.astype(jnp.float32)
    y = xf * lax.rsqrt(jnp.mean(xf * xf, axis=-1, keepdims=True) + RMS_EPS)
    return (y * g.astype(jnp.float32)).astype(x.dtype)


def partial_rope(x, pos):
    half = ROT_DIM // 2
    inv = ROPE_THETA ** (-jnp.arange(half, dtype=jnp.float32) / half)
    ang = pos.astype(jnp.float32)[:, None] * inv[None, :]
    shape = (1, pos.shape[0]) + (1,) * (x.ndim - 3) + (half,)
    cos = jnp.cos(ang).reshape(shape)
    sin = jnp.sin(ang).reshape(shape)
    xf = x.astype(jnp.float32)
    x1 = xf[..., :half]
    x2 = xf[..., half:ROT_DIM]
    out = jnp.concatenate([x1 * cos - x2 * sin, x1 * sin + x2 * cos, xf[..., ROT_DIM:]], axis=-1)
    return out.astype(x.dtype)


def rope_kv(kv, pos):
    return jnp.stack([partial_rope(kv[:, :, 0], pos), kv[:, :, 1]], axis=2)


def masked_softmax(s, mask):
    s = jnp.where(mask, s.astype(jnp.float32), -jnp.inf)
    m = jnp.max(s, axis=-1, keepdims=True)
    m = jnp.where(jnp.isfinite(m), m, 0.0)
    e = jnp.exp(s - m)
    d = jnp.sum(e, axis=-1, keepdims=True)
    return e / jnp.where(d > 0, d, 1.0)


def gather_pages(pool, page_table):
    g = pool[page_table]
    return g.reshape((page_table.shape[0], page_table.shape[1] * pool.shape[1]) + pool.shape[2:])


def split_projection(xn, w_in):
    b, t, _ = xn.shape
    proj = jnp.einsum('btd,dc->btc', xn, w_in)
    sizes = (Q_NSA_W, KV_NSA_W, KV_NSA_W, KV_NSA_W, GATE_NSA_W, Q_DIFF_W, Q_DIFF_W, V_DIFF_W, D_MODEL, D_MODEL)
    cuts = [int(c) for c in np.cumsum(sizes)[:-1]]
    p = jnp.split(proj, cuts, axis=-1)
    q_nsa = p[0].reshape(b, t, NSA_KV_GROUPS, NSA_REP, HEAD_DIM)
    kv_cmp = p[1].reshape(b, t, 2, NSA_KV_GROUPS, HEAD_DIM)
    kv_sel = p[2].reshape(b, t, 2, NSA_KV_GROUPS, HEAD_DIM)
    kv_win = p[3].reshape(b, t, 2, NSA_KV_GROUPS, HEAD_DIM)
    g_nsa = jax.nn.sigmoid(p[4]).reshape(b, t, NSA_KV_GROUPS, NSA_REP, 3)
    q_diff = p[5].reshape(b, t, DIFF_HEADS, 2, HEAD_DIM)
    k_diff = p[6].reshape(b, t, DIFF_HEADS, 2, HEAD_DIM)
    v_diff = p[7].reshape(b, t, DIFF_HEADS, DIFF_DV)
    gate_nsa = jax.nn.sigmoid(p[8])
    gate_diff = jax.nn.sigmoid(p[9])
    return (q_nsa, kv_cmp, kv_sel, kv_win, g_nsa, q_diff, k_diff, v_diff, gate_nsa, gate_diff)


def compressed_attend(q, kv_all, pos, pe, w1, w2):
    b, tk = kv_all.shape[:2]
    n_cmp = (tk - CMP_BLOCK) // CMP_STRIDE + 1
    starts = jnp.arange(n_cmp) * CMP_STRIDE
    idx = starts[:, None] + jnp.arange(CMP_BLOCK)[None, :]
    blocks = kv_all[:, idx] + jnp.transpose(pe, (1, 0, 2))[:, :, None, :]
    flat = jnp.transpose(blocks, (0, 1, 3, 4, 2, 5)).reshape(b, n_cmp, 2, NSA_KV_GROUPS, CMP_BLOCK * HEAD_DIM)
    hid = jax.nn.gelu(jnp.einsum('bncgf,cfh->bncgh', flat, w1))
    comp = jnp.einsum('bncgh,chd->bncgd', hid, w2)
    k_c = comp[:, :, 0]
    v_c = comp[:, :, 1]
    ends = starts + CMP_BLOCK - 1
    s = jnp.einsum('btgrd,bngd->btgrn', q, k_c) * SCALE
    mask = (ends[None, :] <= pos[:, None])[None, :, None, None, :]
    p = masked_softmax(s, mask)
    o = jnp.einsum('btgrn,bngd->btgrd', p.astype(v_c.dtype), v_c)
    return o, p


def select_blocks(p_cmp, pos, tk):
    b, t, g, _, n_cmp = p_cmp.shape
    nb = (tk + SEL_BLOCK - 1) // SEL_BLOCK
    sub = SEL_BLOCK // CMP_STRIDE
    n_sub = nb * sub
    pg = jnp.pad(jnp.sum(p_cmp, axis=3), ((0, 0), (0, 0), (0, 0), (0, n_sub - n_cmp)))
    mass = pg
    for shift in range(1, CMP_BLOCK // CMP_STRIDE):
        mass = mass + jnp.pad(pg[..., :-shift], ((0, 0), (0, 0), (0, 0), (shift, 0)))
    imp = mass.reshape(b, t, g, nb, sub).sum(-1)
    j = jnp.arange(nb)[None, :]
    cur = (pos // SEL_BLOCK)[:, None]
    forced = ((j == 0) | (j == cur) | (j == cur - 1))[None, :, None, :]
    valid = (j * SEL_BLOCK <= pos[:, None])[None, :, None, :]
    score = jnp.where(valid, jnp.where(forced, imp + FORCE_BONUS, imp), -jnp.inf)
    _, idx = lax.top_k(score, min(N_SEL, nb))
    return idx


def selected_attend(q, kv_all, idx, pos):
    b, tk = kv_all.shape[:2]
    t = q.shape[1]
    nb = (tk + SEL_BLOCK - 1) // SEL_BLOCK
    n_top = idx.shape[-1]
    kv = jnp.pad(kv_all, ((0, 0), (0, nb * SEL_BLOCK - tk), (0, 0), (0, 0), (0, 0)))
    kv_blocks = jnp.transpose(kv.reshape(b, nb, SEL_BLOCK, 2, NSA_KV_GROUPS, HEAD_DIM), (0, 4, 1, 2, 3, 5))
    bi = jnp.arange(b)[:, None, None, None]
    gi = jnp.arange(NSA_KV_GROUPS)[None, None, :, None]
    qc = math.gcd(t, SEL_Q_BLOCK)
    nc = t // qc

    def chunk(args):
        q_c, idx_c, pos_c = args
        kvg = kv_blocks[bi, gi, idx_c]
        kpos = idx_c[..., None] * SEL_BLOCK + jnp.arange(SEL_BLOCK)
        s = jnp.einsum('bqgrd,bqgnld->bqgrnl', q_c, kvg[..., 0, :]) * SCALE
        s = s.reshape(b, qc, NSA_KV_GROUPS, NSA_REP, n_top * SEL_BLOCK)
        mask = (kpos <= pos_c[None, :, None, None, None]).reshape(b, qc, NSA_KV_GROUPS, 1, n_top * SEL_BLOCK)
        p = masked_softmax(s, mask).reshape(b, qc, NSA_KV_GROUPS, NSA_REP, n_top, SEL_BLOCK)
        return jnp.einsum('bqgrnl,bqgnld->bqgrd', p.astype(kvg.dtype), kvg[..., 1, :])

    q_c = q.reshape(b, nc, qc, NSA_KV_GROUPS, NSA_REP, HEAD_DIM).swapaxes(0, 1)
    idx_c = idx.reshape(b, nc, qc, NSA_KV_GROUPS, n_top).swapaxes(0, 1)
    pos_c = pos.reshape(nc, qc)
    o = lax.map(chunk, (q_c, idx_c, pos_c))
    return o.swapaxes(0, 1).reshape(b, t, NSA_KV_GROUPS, NSA_REP, HEAD_DIM)


def local_attend(q, kv, qp, kp):
    s = jnp.einsum('bqgrd,bkgd->bgrqk', q, kv[:, :, 0]) * SCALE
    kpr = kp[None, :]
    qpc = qp[:, None]
    mask = (kpr <= qpc) & (kpr > qpc - WINDOW) & (kpr >= 0)
    p = masked_softmax(s, mask)
    return jnp.einsum('bgrqk,bkgd->bqgrd', p.astype(kv.dtype), kv[:, :, 1])


def window_banded(q, kv, pos):
    b, t = q.shape[:2]
    qb = math.gcd(t, Q_BLOCK)
    nb = t // qb
    span = WINDOW + qb
    kv_pad = jnp.pad(kv, ((0, 0), (WINDOW, 0), (0, 0), (0, 0), (0, 0)))
    kpos = pos[0] - WINDOW + jnp.arange(WINDOW + t)
    idx = jnp.arange(nb)[:, None] * qb + jnp.arange(span)[None, :]
    out = jax.vmap(local_attend, in_axes=(1, 1, 0, 0), out_axes=1)(
        q.reshape(b, nb, qb, NSA_KV_GROUPS, NSA_REP, HEAD_DIM), kv_pad[:, idx], pos.reshape(nb, qb), kpos[idx])
    return out.reshape(b, t, NSA_KV_GROUPS, NSA_REP, HEAD_DIM)


def diff_attention(q, k, v, qp, kp, lam):
    b, t = q.shape[:2]
    qb = math.gcd(t, Q_BLOCK)
    nb = t // qb

    def block(args):
        q_b, qp_b = args
        s = jnp.einsum('bqhcd,bkhcd->bhcqk', q_b, k) * SCALE
        p = masked_softmax(s, kp[None, :] <= qp_b[:, None])
        p_diff = p[:, :, 0] - lam * p[:, :, 1]
        return jnp.einsum('bhqk,bkhe->bqhe', p_diff.astype(v.dtype), v)

    o = lax.map(block, (q.reshape(b, nb, qb, DIFF_HEADS, 2, HEAD_DIM).swapaxes(0, 1), qp.reshape(nb, qb)))
    return o.swapaxes(0, 1).reshape(b, t, DIFF_HEADS, DIFF_DV)


def moe_ffn(h, w_router, b_router, w_gate_up, b_gate_up, w_down, b_down):
    n, d = h.shape
    logits = (h @ w_router + b_router).astype(jnp.float32)
    top_val, top_idx = lax.top_k(logits, TOP_K)
    gates = jax.nn.softmax(top_val, axis=-1)
    n_assign = n * TOP_K
    flat_e = top_idx.reshape(-1)
    order = jnp.argsort(flat_e)
    sorted_e = flat_e[order]
    counts = jnp.bincount(flat_e, length=N_EXPERTS)
    padded = (counts + EXPERT_BLOCK - 1) // EXPERT_BLOCK * EXPERT_BLOCK
    pad_end = jnp.cumsum(padded)
    pad_start = pad_end - padded
    start = jnp.cumsum(counts) - counts
    dest = pad_start[sorted_e] + jnp.arange(n_assign) - start[sorted_e]
    n_blocks = (n_assign + N_EXPERTS * (EXPERT_BLOCK - 1) + EXPERT_BLOCK - 1) // EXPERT_BLOCK
    n_rows = n_blocks * EXPERT_BLOCK
    row_token = jnp.full((n_rows,), n, jnp.int32).at[dest].set((order // TOP_K).astype(jnp.int32))
    block_expert = jnp.minimum(jnp.searchsorted(pad_end, jnp.arange(n_blocks) * EXPERT_BLOCK, side='right'), N_EXPERTS - 1)
    h_ext = jnp.concatenate([h, jnp.zeros((1, d), h.dtype)], axis=0)
    xb = h_ext[row_token].reshape(n_blocks, EXPERT_BLOCK, d)

    def expert_block(args):
        x_e, e = args
        gu = x_e @ w_gate_up[e] + b_gate_up[e]
        gate = jnp.minimum(gu[:, :D_FF], SWIGLU_LIMIT)
        up = jnp.clip(gu[:, D_FF:], -SWIGLU_LIMIT, SWIGLU_LIMIT)
        act = (up + 1.0) * gate * jax.nn.sigmoid(SWIGLU_ALPHA * gate)
        return act @ w_down[e] + b_down[e]

    yb = lax.map(expert_block, (xb, block_expert)).reshape(n_rows, d)
    y_assign = jnp.zeros((n_assign, d), yb.dtype).at[order].set(yb[dest])
    return jnp.einsum('nk,nkd->nd', gates.astype(yb.dtype), y_assign.reshape(n, TOP_K, d))


def trunk_layer(x, pos, past, lambda_init, norm_mix, w_in, cmp_pe, w_cmp1, w_cmp2, diff_lambda, diff_subln,
                w_br_nsa, w_br_diff, w_out, norm_ffn, w_router, b_router, w_gate_up, b_gate_up, w_down, b_down):
    b, t, _ = x.shape
    xn = rmsnorm(x, norm_mix)
    (q_nsa, kv_cmp, kv_sel, kv_win, g_nsa, q_diff, k_diff, v_diff, gate_nsa, gate_diff) = split_projection(xn, w_in)
    q_rot = partial_rope(q_nsa, pos)
    kv_sel = rope_kv(kv_sel, pos)
    kv_win = rope_kv(kv_win, pos)
    q_diff = partial_rope(q_diff, pos)
    k_diff = partial_rope(k_diff, pos)
    win_buf = min(WINDOW, PAST_LEN)
    if past is None:
        cmp_all, sel_all, k_all, v_all = kv_cmp, kv_sel, k_diff, v_diff
        o_win = window_banded(q_rot, kv_win, pos)
        win_state = jnp.pad(kv_win, ((0, 0), (max(win_buf - t, 0), 0), (0, 0), (0, 0), (0, 0)))[:, -win_buf:]
    else:
        past_cmp, past_sel, past_win, past_k, past_v = past
        cmp_all = jnp.concatenate([past_cmp, kv_cmp], axis=1)
        sel_all = jnp.concatenate([past_sel, kv_sel], axis=1)
        k_all = jnp.concatenate([past_k, k_diff], axis=1)
        v_all = jnp.concatenate([past_v, v_diff], axis=1)
        win_all = jnp.concatenate([past_win, kv_win], axis=1)
        wpos = pos[0] - past_win.shape[1] + jnp.arange(win_all.shape[1])
        o_win = local_attend(q_rot, win_all, pos, wpos)
        win_state = win_all[:, -past_win.shape[1]:]
    kpos = jnp.arange(k_all.shape[1])
    o_cmp, p_cmp = compressed_attend(q_nsa, cmp_all, pos, cmp_pe, w_cmp1, w_cmp2)
    sel_idx = select_blocks(p_cmp, pos, sel_all.shape[1])
    o_sel = selected_attend(q_rot, sel_all, sel_idx, pos)
    o_nsa = g_nsa[..., 0:1] * o_cmp + g_nsa[..., 1:2] * o_sel + g_nsa[..., 2:3] * o_win
    lf = diff_lambda.astype(jnp.float32)
    lam = jnp.exp(jnp.sum(lf[0] * lf[1])) - jnp.exp(jnp.sum(lf[2] * lf[3])) + lambda_init
    o_diff = diff_attention(q_diff, k_all, v_all, pos, kpos, lam)
    o_diff = rmsnorm(o_diff, diff_subln) * (1.0 - lambda_init)
    y_nsa = jnp.einsum('btf,fd->btd', o_nsa.reshape(b, t, Q_NSA_W), w_br_nsa)
    y_diff = jnp.einsum('btf,fd->btd', o_diff.reshape(b, t, V_DIFF_W), w_br_diff)
    x = x + jnp.einsum('btd,de->bte', gate_nsa * y_nsa + gate_diff * y_diff, w_out)
    h = rmsnorm(x, norm_ffn).reshape(b * t, D_MODEL)
    x = x + moe_ffn(h, w_router, b_router, w_gate_up, b_gate_up, w_down, b_down).reshape(b, t, D_MODEL)
    return x, (kv_cmp, kv_sel, win_state, k_diff, v_diff)


def setup_inputs(seed: int = 0) -> dict:
    key = jax.random.key(seed)
    ks = jax.random.split(key, 26)
    n_pages = PAST_LEN // PAGE_SIZE
    n_pool = (5 * DEC_BATCH * n_pages + 3) // 4
    win_buf = min(WINDOW, PAST_LEN)

    def nrm(k, shape, scale=1.0):
        return jax.random.normal(k, shape, jnp.float32) * scale

    page_table = jax.random.permutation(ks[7], n_pool)[:DEC_BATCH * n_pages].reshape(DEC_BATCH, n_pages).astype(jnp.int32)
    return {
        'x_prompt': nrm(ks[0], (BATCH, SEQ, D_MODEL)),
        'x_sample': nrm(ks[1], (DEC_BATCH, DEC_SEQ, D_MODEL)),
        'cache_cmp_kv': nrm(ks[2], (DEPTH, n_pool, PAGE_SIZE, 2, NSA_KV_GROUPS, HEAD_DIM)),
        'cache_sel_kv': nrm(ks[3], (DEPTH, n_pool, PAGE_SIZE, 2, NSA_KV_GROUPS, HEAD_DIM)),
        'state_win_kv': nrm(ks[4], (DEPTH, DEC_BATCH, win_buf, 2, NSA_KV_GROUPS, HEAD_DIM)),
        'cache_diff_k': nrm(ks[5], (DEPTH, n_pool, PAGE_SIZE, DIFF_HEADS, 2, HEAD_DIM)),
        'cache_diff_v': nrm(ks[6], (DEPTH, n_pool, PAGE_SIZE, DIFF_HEADS, DIFF_DV)),
        'page_table': page_table,
        'norm_mix': 1.0 + nrm(ks[8], (DEPTH, D_MODEL), 0.02),
        'w_in': nrm(ks[9], (DEPTH, D_MODEL, IN_COLS), D_MODEL ** -0.5),
        'cmp_pe': nrm(ks[10], (DEPTH, 2, CMP_BLOCK, HEAD_DIM), 0.1),
        'w_cmp1': nrm(ks[11], (DEPTH, 2, CMP_BLOCK * HEAD_DIM, CMP_HIDDEN), (CMP_BLOCK * HEAD_DIM) ** -0.5),
        'w_cmp2': nrm(ks[12], (DEPTH, 2, CMP_HIDDEN, HEAD_DIM), CMP_HIDDEN ** -0.5),
        'diff_lambda': nrm(ks[13], (DEPTH, 4, HEAD_DIM), 0.1),
        'diff_subln': 1.0 + nrm(ks[14], (DEPTH, DIFF_DV), 0.02),
        'w_br_nsa': nrm(ks[15], (DEPTH, Q_NSA_W, D_MODEL), Q_NSA_W ** -0.5),
        'w_br_diff': nrm(ks[16], (DEPTH, V_DIFF_W, D_MODEL), V_DIFF_W ** -0.5),
        'w_out': nrm(ks[17], (DEPTH, D_MODEL, D_MODEL), D_MODEL ** -0.5),
        'norm_ffn': 1.0 + nrm(ks[18], (DEPTH, D_MODEL), 0.02),
        'w_router': nrm(ks[19], (DEPTH, D_MODEL, N_EXPERTS), D_MODEL ** -0.5),
        'b_router': nrm(ks[20], (DEPTH, N_EXPERTS), 0.01),
        'w_gate_up': nrm(ks[21], (DEPTH, N_EXPERTS, D_MODEL, 2 * D_FF), D_MODEL ** -0.5),
        'b_gate_up': nrm(ks[22], (DEPTH, N_EXPERTS, 2 * D_FF), 0.01),
        'w_down': nrm(ks[23], (DEPTH, N_EXPERTS, D_FF, D_MODEL), D_FF ** -0.5),
        'b_down': nrm(ks[24], (DEPTH, N_EXPERTS, D_MODEL), 0.01),
        'norm_final': 1.0 + nrm(ks[25], (D_MODEL,), 0.02),
    }


def reference(x_prompt, x_sample, cache_cmp_kv, cache_sel_kv, state_win_kv, cache_diff_k, cache_diff_v, page_table,
              norm_mix, w_in, cmp_pe, w_cmp1, w_cmp2, diff_lambda, diff_subln, w_br_nsa, w_br_diff, w_out,
              norm_ffn, w_router, b_router, w_gate_up, b_gate_up, w_down, b_down, norm_final):
    past_len = page_table.shape[1] * cache_cmp_kv.shape[2]
    pos_p = jnp.arange(x_prompt.shape[1], dtype=jnp.int32)
    pos_s = past_len + jnp.arange(x_sample.shape[1], dtype=jnp.int32)
    xp, xs = x_prompt, x_sample
    st_p, st_s = [], []
    for l in range(DEPTH):
        lambda_init = 0.8 - 0.6 * math.exp(-0.3 * l)
        weights = (norm_mix[l], w_in[l], cmp_pe[l], w_cmp1[l], w_cmp2[l], diff_lambda[l], diff_subln[l],
                   w_br_nsa[l], w_br_diff[l], w_out[l], norm_ffn[l], w_router[l], b_router[l],
                   w_gate_up[l], b_gate_up[l], w_down[l], b_down[l])
        past = (gather_pages(cache_cmp_kv[l], page_table), gather_pages(cache_sel_kv[l], page_table), state_win_kv[l],
                gather_pages(cache_diff_k[l], page_table), gather_pages(cache_diff_v[l], page_table))
        xp, sp = trunk_layer(xp, pos_p, None, lambda_init, *weights)
        xs, ss = trunk_layer(xs, pos_s, past, lambda_init, *weights)
        st_p.append(sp)
        st_s.append(ss)
    y_prompt = rmsnorm(xp, norm_final)
    y_sample = rmsnorm(xs, norm_final)
    new_cmp_kv_prompt = jnp.stack([s[0] for s in st_p])
    new_cmp_kv_sample = jnp.stack([s[0] for s in st_s])
    new_sel_kv_prompt = jnp.stack([s[1] for s in st_p])
    new_sel_kv_sample = jnp.stack([s[1] for s in st_s])
    new_win_kv_prompt = jnp.stack([s[2] for s in st_p])
    new_win_kv_sample = jnp.stack([s[2] for s in st_s])
    new_diff_k_prompt = jnp.stack([s[3] for s in st_p])
    new_diff_k_sample = jnp.stack([s[3] for s in st_s])
    new_diff_v_prompt = jnp.stack([s[4] for s in st_p])
    new_diff_v_sample = jnp.stack([s[4] for s in st_s])
    return (y_prompt, y_sample, new_cmp_kv_prompt, new_cmp_kv_sample, new_sel_kv_prompt, new_sel_kv_sample,
            new_win_kv_prompt, new_win_kv_sample, new_diff_k_prompt, new_diff_k_sample, new_diff_v_prompt, new_diff_v_sample)
```

```python
import functools
import math

import numpy as np
import jax
import jax.numpy as jnp
from jax import lax
from jax.experimental import pallas as pl
from jax.experimental.pallas import tpu as pltpu

D_MODEL = 1024
HEAD_DIM = 64
NSA_HEADS = 8
NSA_GROUPS = 2
NSA_REP = 4
CMP_BLOCK = 32
CMP_STRIDE = 16
SEL_BLOCK = 64
N_SEL = 16
WINDOW = 512
DIFF_HEADS = 4
DIFF_DV = 128
ROT_DIM = 16
ROPE_THETA = 500000.0
N_EXPERTS = 32
TOP_K = 4
D_FF = 1024
SWIGLU_LIMIT = 7.0
SWIGLU_ALPHA = 1.702
RMS_EPS = 1e-5
FORCE_BONUS = 1e4
SCALE = HEAD_DIM ** -0.5
LANES = 128

BF = jnp.bfloat16
F32 = jnp.float32
I32 = jnp.int32
NEG = -1e30
VMEM_LIMIT = 56 * 1024 * 1024

OFF_Q, OFF_KVC, OFF_KVS, OFF_KVW, OFF_QD, OFF_KD, OFF_VD, OFF_GX = 0, 512, 768, 1024, 1280, 1792, 2304, 2816
IN_COLS_K = 4352


def _dot(a, b):
    return jnp.dot(a, b, preferred_element_type=F32)


def _dot_nt(a, b):
    return lax.dot_general(a, b, (((1,), (1,)), ((), ())), preferred_element_type=F32)


def _dot_hi(a, b):
    return jnp.dot(a, b, preferred_element_type=F32, precision=lax.Precision.HIGHEST)


def _params(sem):
    return pltpu.CompilerParams(dimension_semantics=sem, vmem_limit_bytes=VMEM_LIMIT)


def _lane_iota(rows):
    return lax.broadcasted_iota(I32, (rows, LANES), 1)


def _pair(lo, a, b, g):
    if g == 0:
        return jnp.where(lo, a, pltpu.roll(b, 64, 1))
    return jnp.where(lo, pltpu.roll(a, 64, 1), b)


def _inproj_body(x_ref, nw_ref, w_ref, cos_ref, sin_ref, e_ref,
                 qraw_ref, qrot_ref, kvc_ref, kvs_ref, kvw_ref, ksa_ref, kwa_ref, vsb_ref, vwb_ref,
                 qd_ref, kd_ref, kdb_ref, vd_ref, vdb_ref, gx_ref):
    x = x_ref[...]
    tm = x.shape[0]
    xn = (x * lax.rsqrt(jnp.mean(x * x, axis=-1, keepdims=True) + RMS_EPS) * nw_ref[...]).astype(BF)
    cos = cos_ref[...]
    sin = sin_ref[...]
    epat = e_ref[...]
    lane = _lane_iota(tm)
    lo = lane < 64
    first8 = (lane & 63) < 8

    def mm(off):
        return _dot(xn, w_ref[:, off:off + LANES])

    def rope(y):
        sw = jnp.where(first8, pltpu.roll(y, LANES - 8, 1), pltpu.roll(y, 8, 1))
        return y * cos + sw * sin

    for c in range(4):
        y = mm(OFF_Q + c * LANES) * SCALE
        yr = rope(y)
        ys = pltpu.roll(y, 64, 1)
        yrs = pltpu.roll(yr, 64, 1)
        qraw_ref[:, (2 * c) * LANES:(2 * c + 1) * LANES] = jnp.where(lo, y, 0.0).astype(BF)
        qraw_ref[:, (2 * c + 1) * LANES:(2 * c + 2) * LANES] = jnp.where(lo, ys, 0.0).astype(BF)
        qrot_ref[:, (2 * c) * LANES:(2 * c + 1) * LANES] = jnp.where(lo, yr, 0.0).astype(BF)
        qrot_ref[:, (2 * c + 1) * LANES:(2 * c + 2) * LANES] = jnp.where(lo, yrs, 0.0).astype(BF)

    for c in range(2):
        kvc_ref[:, c * LANES:(c + 1) * LANES] = mm(OFF_KVC + c * LANES)

    for off, kv_ref, ka_ref, vb_ref in ((OFF_KVS, kvs_ref, ksa_ref, vsb_ref), (OFF_KVW, kvw_ref, kwa_ref, vwb_ref)):
        kr = rope(mm(off))
        v = mm(off + LANES)
        kv_ref[:, 0:LANES] = kr
        kv_ref[:, LANES:2 * LANES] = v
        ka_ref[:, 0:LANES] = jnp.where(lo, kr, epat).astype(BF)
        ka_ref[:, LANES:2 * LANES] = jnp.where(lo, pltpu.roll(kr, 64, 1), epat).astype(BF)
        vb_ref[...] = v.astype(BF)

    for c in range(4):
        sl = slice(c * LANES, (c + 1) * LANES)
        qd_ref[:, sl] = rope(mm(OFF_QD + c * LANES) * SCALE).astype(BF)
        kr = rope(mm(OFF_KD + c * LANES))
        kd_ref[:, sl] = kr
        kdb_ref[:, sl] = kr.astype(BF)
        v = mm(OFF_VD + c * LANES)
        vd_ref[:, sl] = v
        vdb_ref[:, sl] = v.astype(BF)

    for c in range(12):
        gx_ref[:, c * LANES:(c + 1) * LANES] = jax.nn.sigmoid(mm(OFF_GX + c * LANES))


def _inproj(x2d, norm_w, w_bf, cos_t, sin_t, e_t, tm):
    n = x2d.shape[0]
    nt = cos_t.shape[0] // tm
    row = lambda i: (i, 0)
    tab = lambda i: (i % nt, 0)
    const = lambda i: (0, 0)
    outs = [(1024, BF), (1024, BF), (256, F32), (256, F32), (256, F32), (256, BF), (256, BF), (128, BF), (128, BF),
            (512, BF), (512, F32), (512, BF), (512, F32), (512, BF), (1536, F32)]
    return pl.pallas_call(
        _inproj_body,
        grid=(n // tm,),
        in_specs=[pl.BlockSpec((tm, D_MODEL), row), pl.BlockSpec((1, D_MODEL), const),
                  pl.BlockSpec((D_MODEL, IN_COLS_K), const),
                  pl.BlockSpec((tm, LANES), tab), pl.BlockSpec((tm, LANES), tab), pl.BlockSpec((tm, LANES), tab)],
        out_specs=[pl.BlockSpec((tm, w), row) for w, _ in outs],
        out_shape=[jax.ShapeDtypeStruct((n, w), dt) for w, dt in outs],
        compiler_params=_params(("parallel",)),
        name="inproj",
    )(x2d, norm_w, w_bf, cos_t, sin_t, e_t)


def _compress_body(pt_ref, *refs, n_pages, rp, rt):
    page_refs = refs[:n_pages]
    pea_ref, peb_ref, w1a_ref, w1b_ref, w2_ref, out_ref, y_scr = refs[n_pages:]
    s = pl.program_id(1)
    for i in range(n_pages):
        x = page_refs[i][0]
        row0 = pl.multiple_of((s * n_pages + i) * rp, 8)
        for cg in range(4):
            ycg = jnp.concatenate(
                [x[:, l * 256 + cg * 64:l * 256 + cg * 64 + 64] for l in range(CMP_STRIDE)], axis=1)
            y_scr[cg, pl.ds(row0, rp), :] = ycg

    @pl.when(s == pl.num_programs(1) - 1)
    def _():
        for c in range(2):
            for g in range(2):
                cg = c * 2 + g
                y = y_scr[cg]
                za = _dot((y + pea_ref[c]).astype(BF), w1a_ref[c])
                zb = _dot((y + peb_ref[c]).astype(BF), w1b_ref[c])
                hid = jax.nn.gelu(za + pltpu.roll(zb, rt - 1, 0))
                out_ref[0, :, cg * LANES:(cg + 1) * LANES] = _dot(hid.astype(BF), w2_ref[c]).astype(BF)


def _compress(pool, page_table, pea, peb, w1a, w1b, w2p, n_pages):
    nb, pages = page_table.shape
    rp = pool.shape[1]
    rt = pages * rp
    steps = pages // n_pages
    page_specs = [pl.BlockSpec((1, rp, 4096), functools.partial(
        lambda b, s, pt, i: (pt[b, s * n_pages + i], 0, 0), i=i)) for i in range(n_pages)]
    c3 = lambda b, s, pt: (0, 0, 0)
    return pl.pallas_call(
        functools.partial(_compress_body, n_pages=n_pages, rp=rp, rt=rt),
        grid_spec=pltpu.PrefetchScalarGridSpec(
            num_scalar_prefetch=1,
            grid=(nb, steps),
            in_specs=page_specs + [pl.BlockSpec((2, 1, 1024), c3), pl.BlockSpec((2, 1, 1024), c3),
                                   pl.BlockSpec((2, 1024, LANES), c3), pl.BlockSpec((2, 1024, LANES), c3),
                                   pl.BlockSpec((2, LANES, LANES), c3)],
            out_specs=pl.BlockSpec((1, rt, 512), lambda b, s, pt: (b, 0, 0)),
            scratch_shapes=[pltpu.VMEM((4, rt, 1024), F32)]),
        out_shape=jax.ShapeDtypeStruct((nb, rt, 512), BF),
        compiler_params=_params(("parallel", "arbitrary")),
        name="compress",
    )(page_table, *([pool] * n_pages), pea, peb, w1a, w1b, w2p)


def _cmp_body(q_ref, kvc_ref, mimp_ref, ocmp_ref, selb_ref, *, tq, rt, nb, nbp, pos0):
    i = pl.program_id(1)
    pos_n = pos0 + i * tq + lax.broadcasted_iota(I32, (tq, rt), 0)
    n_i = lax.broadcasted_iota(I32, (tq, rt), 1)
    vis = (n_i * CMP_STRIDE + (CMP_BLOCK - 1)) <= pos_n
    lo = _lane_iota(tq) < 64
    pos_b = pos0 + i * tq + lax.broadcasted_iota(I32, (tq, nbp), 0)
    jb = lax.broadcasted_iota(I32, (tq, nbp), 1)
    cur = jnp.right_shift(pos_b, 6)
    valid = (jb * SEL_BLOCK <= pos_b) & (jb < nb)
    forced = (jb == 0) | (jb == cur) | (jb == cur - 1)
    mimp = mimp_ref[...]
    for g in range(NSA_GROUPS):
        kc = kvc_ref[0, :, g * LANES:(g + 1) * LANES]
        vc = kvc_ref[0, :, (2 + g) * LANES:(3 + g) * LANES]
        pg = jnp.zeros((tq, rt), F32)
        og = []
        for r in range(NSA_REP):
            h = g * NSA_REP + r
            q = q_ref[0, :, h * LANES:(h + 1) * LANES]
            s = jnp.where(vis, _dot_nt(q, kc), -jnp.inf)
            m = jnp.max(s, axis=-1, keepdims=True)
            m = jnp.where(m == -jnp.inf, 0.0, m)
            e = jnp.exp(s - m)
            d = jnp.sum(e, axis=-1, keepdims=True)
            p = e / jnp.where(d > 0, d, 1.0)
            og.append(_dot(p.astype(BF), vc))
            pg = pg + p
        imp = _dot_hi(pg, mimp)
        score = jnp.where(valid, jnp.where(forced, imp + FORCE_BONUS, imp), -jnp.inf)
        rank = jnp.zeros((tq, nbp), I32)
        for jj in range(nb):
            cj = score[:, jj:jj + 1]
            beats = (cj > score) | ((cj == score) & (jb > jj))
            rank = rank + beats.astype(I32)
        sel = (rank < N_SEL) & valid
        selb_ref[0, :, g * nbp:(g + 1) * nbp] = jnp.where(sel, 0.0, NEG)
        for c2 in range(2):
            ocmp_ref[0, :, (g * 2 + c2) * LANES:(g * 2 + c2 + 1) * LANES] = _pair(lo, og[2 * c2], og[2 * c2 + 1], 0)


def _cmp_attend(qraw3, kvc, mimp, nq, nb, nbp, pos0):
    nbat = kvc.shape[0]
    _, tq, _ = qraw3.shape
    rt = kvc.shape[1]
    return pl.pallas_call(
        functools.partial(_cmp_body, tq=tq, rt=rt, nb=nb, nbp=nbp, pos0=pos0),
        grid=(nbat, nq),
        in_specs=[pl.BlockSpec((1, tq, 1024), lambda b, i: (b * nq + i, 0, 0)),
                  pl.BlockSpec((1, rt, 512), lambda b, i: (b, 0, 0)),
                  pl.BlockSpec((rt, nbp), lambda b, i: (0, 0))],
        out_specs=[pl.BlockSpec((1, tq, 512), lambda b, i: (b * nq + i, 0, 0)),
                   pl.BlockSpec((1, tq, 2 * nbp), lambda b, i: (b * nq + i, 0, 0))],
        out_shape=[jax.ShapeDtypeStruct((nbat * nq, tq, 512), F32),
                   jax.ShapeDtypeStruct((nbat * nq, tq, 2 * nbp), F32)],
        compiler_params=_params(("parallel", "parallel")),
        name="cmp_attend",
    )(qraw3, kvc, mimp)


def _online(s, v, m_s, l_s, acc, idx):
    m_old = m_s[idx]
    m_new = jnp.maximum(m_old, jnp.max(s, axis=-1, keepdims=True))
    alpha = jnp.exp(m_old - m_new)
    p = jnp.exp(s - m_new)
    l_s[idx] = alpha * l_s[idx] + jnp.sum(p, axis=-1, keepdims=True)
    acc[idx] = alpha * acc[idx] + _dot(p.astype(BF), v)
    m_s[idx] = m_new


def _nsa_flash_body(*refs, mode, tq, tk, nk):
    if mode == "sel":
        q_ref, selb_ref, k_ref, v_ref, o_ref, qs, m_s, l_s, acc = refs
    else:
        q_ref, k_ref, v_ref, o_ref, qs, m_s, l_s, acc = refs
    i = pl.program_id(1)
    j = pl.program_id(2)
    q0 = i * tq
    rows = NSA_REP * tq
    lane = _lane_iota(tq)
    lo = lane < 64

    @pl.when(j == 0)
    def _():
        m_s[...] = jnp.full(m_s.shape, NEG, F32)
        l_s[...] = jnp.zeros(l_s.shape, F32)
        acc[...] = jnp.zeros(acc.shape, F32)
        for g in range(NSA_GROUPS):
            if mode == "sel":
                sb = pltpu.roll(selb_ref[0, :, g * LANES:(g + 1) * LANES], 64, 1)
                sb = jnp.where((lane >= 64) & (lane < 96), sb, 0.0)
            for r in range(NSA_REP):
                h = g * NSA_REP + r
                q = q_ref[0, :, h * LANES:(h + 1) * LANES]
                if mode == "sel":
                    q = (q.astype(F32) + sb).astype(BF)
                qs[g, r * tq:(r + 1) * tq, :] = q

    if mode == "sel":
        jt = j
        active = j <= (q0 + tq - 1) // tk
        need_mask = j * tk + tk - 1 > q0
    else:
        jt = i - (nk - 1) + j
        active = jt >= 0
        need_mask = (j == 0) | (j == nk - 1)

    def step(masked):
        for g in range(NSA_GROUPS):
            s = _dot_nt(qs[g], k_ref[:, g * LANES:(g + 1) * LANES])
            if masked:
                rpos = q0 + (lax.broadcasted_iota(I32, (rows, tk), 0) & (tq - 1))
                kpos = jt * tk + lax.broadcasted_iota(I32, (rows, tk), 1)
                vis = kpos <= rpos
                if mode == "win":
                    vis = vis & (kpos > rpos - WINDOW)
                s = jnp.where(vis, s, NEG)
            _online(s, v_ref[...], m_s, l_s, acc, g)

    @pl.when(active & need_mask)
    def _():
        step(True)

    @pl.when(active & jnp.logical_not(need_mask))
    def _():
        step(False)

    @pl.when(j == nk - 1)
    def _():
        for g in range(NSA_GROUPS):
            on = acc[g] / l_s[g]
            for c2 in range(2):
                a = on[(2 * c2) * tq:(2 * c2 + 1) * tq]
                b = on[(2 * c2 + 1) * tq:(2 * c2 + 2) * tq]
                o_ref[0, :, (g * 2 + c2) * LANES:(g * 2 + c2 + 1) * LANES] = _pair(lo, a, b, g)


def _nsa_flash(mode, q3, selb3, k2, v2, nbat, t, tq, tk):
    nq = t // tq
    nkv = t // tk
    rows = NSA_REP * tq
    if mode == "sel":
        nk = nkv
        kmap = lambda b, i, j: (b * nkv + jnp.minimum(j, (i * tq + tq - 1) // tk), 0)
    else:
        nk = WINDOW // tk + 1
        kmap = lambda b, i, j: (b * nkv + jnp.maximum(i - (nk - 1) + j, 0), 0)
    qmap = lambda b, i, j: (b * nq + i, 0, 0)
    in_specs = [pl.BlockSpec((1, tq, 1024), qmap)]
    args = [q3]
    if mode == "sel":
        in_specs.append(pl.BlockSpec((1, tq, 2 * LANES), qmap))
        args.append(selb3)
    in_specs += [pl.BlockSpec((tk, 2 * LANES), kmap), pl.BlockSpec((tk, LANES), kmap)]
    args += [k2, v2]
    return pl.pallas_call(
        functools.partial(_nsa_flash_body, mode=mode, tq=tq, tk=tk, nk=nk),
        grid=(nbat, nq, nk),
        in_specs=in_specs,
        out_specs=pl.BlockSpec((1, tq, 512), qmap),
        out_shape=jax.ShapeDtypeStruct((nbat * nq, tq, 512), F32),
        scratch_shapes=[pltpu.VMEM((NSA_GROUPS, rows, LANES), BF), pltpu.VMEM((NSA_GROUPS, rows, 1), F32),
                        pltpu.VMEM((NSA_GROUPS, rows, 1), F32), pltpu.VMEM((NSA_GROUPS, rows, LANES), F32)],
        compiler_params=_params(("parallel", "parallel", "arbitrary")),
        name="nsa_flash_" + mode,
    )(*args)


def _diff_lambda(dl, lambda_init):
    a = jnp.sum(dl[0:1] * dl[1:2], axis=1, keepdims=True)
    b = jnp.sum(dl[2:3] * dl[3:4], axis=1, keepdims=True)
    return jnp.exp(a) - jnp.exp(b) + lambda_init


def _diff_finish(a0, a1, lam, subln, lambda_init):
    o = a0 - lam * a1
    o = o * lax.rsqrt(jnp.mean(o * o, axis=-1, keepdims=True) + RMS_EPS) * subln
    return o * (1.0 - lambda_init)


def _diff_flash_body(q_ref, k_ref, v_ref, dl_ref, sub_ref, o_ref, qs, m_s, l_s, acc, *, tq, tk, nk, lambda_init):
    i = pl.program_id(1)
    j = pl.program_id(2)
    q0 = i * tq
    rows = 2 * tq
    lo = _lane_iota(tq) < 64

    @pl.when(j == 0)
    def _():
        m_s[...] = jnp.full(m_s.shape, NEG, F32)
        l_s[...] = jnp.zeros(l_s.shape, F32)
        acc[...] = jnp.zeros(acc.shape, F32)
        for h in range(DIFF_HEADS):
            q = q_ref[0, :, h * LANES:(h + 1) * LANES].astype(F32)
            qs[h, 0:tq, :] = jnp.where(lo, q, 0.0).astype(BF)
            qs[h, tq:2 * tq, :] = jnp.where(lo, 0.0, q).astype(BF)

    active = j <= (q0 + tq - 1) // tk
    need_mask = j * tk + tk - 1 > q0

    def step(masked):
        for h in range(DIFF_HEADS):
            s = _dot_nt(qs[h], k_ref[:, h * LANES:(h + 1) * LANES])
            if masked:
                rpos = q0 + (lax.broadcasted_iota(I32, (rows, tk), 0) & (tq - 1))
                kpos = j * tk + lax.broadcasted_iota(I32, (rows, tk), 1)
                s = jnp.where(kpos <= rpos, s, NEG)
            _online(s, v_ref[:, h * LANES:(h + 1) * LANES], m_s, l_s, acc, h)

    @pl.when(active & need_mask)
    def _():
        step(True)

    @pl.when(active & jnp.logical_not(need_mask))
    def _():
        step(False)

    @pl.when(j == nk - 1)
    def _():
        lam = _diff_lambda(dl_ref[...], lambda_init)
        for h in range(DIFF_HEADS):
            on = acc[h] / l_s[h]
            o_ref[0, :, h * LANES:(h + 1) * LANES] = _diff_finish(on[0:tq], on[tq:2 * tq], lam, sub_ref[...], lambda_init)


def _diff_flash(q3, k2, v2, dl, subln, nbat, t, tq, tk, lambda_init):
    nq = t // tq
    nk = t // tk
    qmap = lambda b, i, j: (b * nq + i, 0, 0)
    kmap = lambda b, i, j: (b * nk + jnp.minimum(j, (i * tq + tq - 1) // tk), 0)
    const = lambda b, i, j: (0, 0)
    rows = 2 * tq
    return pl.pallas_call(
        functools.partial(_diff_flash_body, tq=tq, tk=tk, nk=nk, lambda_init=lambda_init),
        grid=(nbat, nq, nk),
        in_specs=[pl.BlockSpec((1, tq, 512), qmap), pl.BlockSpec((tk, 512), kmap), pl.BlockSpec((tk, 512), kmap),
                  pl.BlockSpec((4, HEAD_DIM), const), pl.BlockSpec((1, DIFF_DV), const)],
        out_specs=pl.BlockSpec((1, tq, 512), qmap),
        out_shape=jax.ShapeDtypeStruct((nbat * nq, tq, 512), F32),
        scratch_shapes=[pltpu.VMEM((DIFF_HEADS, rows, LANES), BF), pltpu.VMEM((DIFF_HEADS, rows, 1), F32),
                        pltpu.VMEM((DIFF_HEADS, rows, 1), F32), pltpu.VMEM((DIFF_HEADS, rows, LANES), F32)],
        compiler_params=_params(("parallel", "parallel", "arbitrary")),
        name="diff_flash",
    )(q3, k2, v2, dl, subln)


def _paged_body(pt_ref, *refs, mode, n_pages, past_len, lambda_init):
    q_ref = refs[0]
    k = 1
    if mode == "sel":
        selb_ref = refs[k]
        k += 1
    kpages = refs[k:k + n_pages]
    k += n_pages
    if mode == "diff":
        vpages = refs[k:k + n_pages]
        k += n_pages
    newk_ref = refs[k]
    k += 1
    if mode == "diff":
        newv_ref, dl_ref, sub_ref = refs[k:k + 3]
        k += 3
    o_ref = refs[k]
    k += 1
    if mode == "sel":
        qbd, bias, m_s, l_s, acc = refs[k:]
    else:
        qbd, m_s, l_s, acc = refs[k:]
    ck = 512 if mode == "diff" else LANES
    st = pl.program_id(1)
    nq = 8
    rows = 64
    lane8 = _lane_iota(nq)
    lo8 = lane8 < 64
    t_row = lax.broadcasted_iota(I32, (rows, LANES), 0) & (nq - 1)
    col = lax.broadcasted_iota(I32, (rows, LANES), 1)

    @pl.when(st == 0)
    def _():
        m_s[0] = jnp.full((rows, 1), NEG, F32)
        l_s[0] = jnp.zeros((rows, 1), F32)
        acc[0] = jnp.zeros(acc.shape[1:], F32)
        if mode == "diff":
            qbd[...] = jnp.zeros(qbd.shape, F32)
            for h in range(DIFF_HEADS):
                q = q_ref[0, :, h * LANES:(h + 1) * LANES].astype(F32)
                qbd[(2 * h) * nq:(2 * h + 1) * nq, h * LANES:(h + 1) * LANES] = jnp.where(lo8, q, 0.0)
                qbd[(2 * h + 1) * nq:(2 * h + 2) * nq, h * LANES:(h + 1) * LANES] = jnp.where(lo8, 0.0, q)
        else:
            for h in range(NSA_HEADS):
                g = h // NSA_REP
                q = q_ref[0, :, h * LANES:(h + 1) * LANES].astype(F32)
                qbd[h * nq:(h + 1) * nq, :] = q if g == 0 else pltpu.roll(q, 64, 1)
                if mode == "sel":
                    bias[h * nq:(h + 1) * nq, :] = selb_ref[0, :, g * 256:(g + 1) * 256]
        kn = newk_ref[0]
        if mode == "diff":
            kk, vv = kn, newv_ref[0]
        else:
            kk, vv = kn[:, 0:LANES], kn[:, LANES:2 * LANES]
        kpad = jnp.concatenate([kk, jnp.zeros((LANES - nq, kk.shape[1]), F32)], axis=0).astype(BF)
        vpad = jnp.concatenate([vv, jnp.zeros((LANES - nq, vv.shape[1]), F32)], axis=0).astype(BF)
        s = _dot_nt(qbd[...].astype(BF), kpad)
        if mode == "sel":
            jnew = past_len // SEL_BLOCK
            s = s + bias[:, jnew:jnew + 1]
        s = jnp.where((col < nq) & (col <= t_row), s, NEG)
        _online(s, vpad, m_s, l_s, acc, 0)

    qb = qbd[...].astype(BF)
    for i in range(n_pages):
        pg = st * n_pages + i
        page = kpages[i][0]
        s = _dot_nt(qb, page[:, 0:ck].astype(BF))
        if mode == "sel":
            krow = lax.broadcasted_iota(I32, (LANES, 256), 0)
            jbi = lax.broadcasted_iota(I32, (LANES, 256), 1)
            ep = jnp.where(jbi == 2 * pg + jnp.right_shift(krow, 6), 1.0, 0.0).astype(BF)
            s = s + _dot_nt(bias[...].astype(BF), ep)
        if mode == "win":
            s = jnp.where(pg * LANES + col > t_row, s, NEG)
        if mode == "diff":
            v = vpages[i][0].astype(BF)
        else:
            v = page[:, LANES:2 * LANES].astype(BF)
        _online(s, v, m_s, l_s, acc, 0)

    @pl.when(st == pl.num_programs(1) - 1)
    def _():
        on = acc[0] / l_s[0]
        if mode == "diff":
            lam = _diff_lambda(dl_ref[...], lambda_init)
            for h in range(DIFF_HEADS):
                a0 = on[(2 * h) * nq:(2 * h + 1) * nq, h * LANES:(h + 1) * LANES]
                a1 = on[(2 * h + 1) * nq:(2 * h + 2) * nq, h * LANES:(h + 1) * LANES]
                o_ref[0, :, h * LANES:(h + 1) * LANES] = _diff_finish(a0, a1, lam, sub_ref[...], lambda_init)
        else:
            for c in range(4):
                a = on[(2 * c) * nq:(2 * c + 1) * nq]
                b = on[(2 * c + 1) * nq:(2 * c + 2) * nq]
                o_ref[0, :, c * LANES:(c + 1) * LANES] = _pair(lo8, a, b, c // 2)


def _paged_attend(mode, q3, selb3, pool_k, pool_v, page_table, newk3, newv3, dl, subln, n_pages, past_len, lambda_init):
    nbat, pages = page_table.shape
    steps = pages // n_pages
    wq = q3.shape[2]
    wk = pool_k.shape[2]
    bmap = lambda b, s, pt: (b, 0, 0)
    c2 = lambda b, s, pt: (0, 0)

    def pmap(i):
        return lambda b, s, pt: (pt[b, s * n_pages + i], 0, 0)

    in_specs = [pl.BlockSpec((1, 8, wq), bmap)]
    args = [q3]
    if mode == "sel":
        in_specs.append(pl.BlockSpec((1, 8, 512), bmap))
        args.append(selb3)
    in_specs += [pl.BlockSpec((1, LANES, wk), pmap(i)) for i in range(n_pages)]
    args += [pool_k] * n_pages
    if mode == "diff":
        in_specs += [pl.BlockSpec((1, LANES, 512), pmap(i)) for i in range(n_pages)]
        args += [pool_v] * n_pages
    in_specs.append(pl.BlockSpec((1, 8, wk), bmap))
    args.append(newk3)
    if mode == "diff":
        in_specs += [pl.BlockSpec((1, 8, 512), bmap), pl.BlockSpec((4, HEAD_DIM), c2), pl.BlockSpec((1, DIFF_DV), c2)]
        args += [newv3, dl, subln]
    cv = 512 if mode == "diff" else LANES
    ck = 512 if mode == "diff" else LANES
    scratch = [pltpu.VMEM((64, ck), F32)]
    if mode == "sel":
        scratch.append(pltpu.VMEM((64, 256), F32))
    scratch += [pltpu.VMEM((1, 64, 1), F32), pltpu.VMEM((1, 64, 1), F32), pltpu.VMEM((1, 64, cv), F32)]
    return pl.pallas_call(
        functools.partial(_paged_body, mode=mode, n_pages=n_pages, past_len=past_len, lambda_init=lambda_init),
        grid_spec=pltpu.PrefetchScalarGridSpec(
            num_scalar_prefetch=1, grid=(nbat, steps), in_specs=in_specs,
            out_specs=pl.BlockSpec((1, 8, 512), bmap), scratch_shapes=scratch),
        out_shape=jax.ShapeDtypeStruct((nbat, 8, 512), F32),
        compiler_params=_params(("parallel", "arbitrary")),
        name="paged_" + mode,
    )(page_table, *args)


def _merge_body(x_ref, oc_ref, os_ref, ow_ref, od_ref, gx_ref, nmix_ref, nffn_ref, wg_ref, wbn_ref, wbd_ref, wo_ref,
                wr_ref, br_ref, x1_ref, h_ref, idx_ref, gate_ref):
    x = x_ref[...]
    tm = x.shape[0]
    xn = (x * lax.rsqrt(jnp.mean(x * x, axis=-1, keepdims=True) + RMS_EPS) * nmix_ref[...]).astype(BF)
    gx = gx_ref[...]
    o_nsa = gx[:, 0:512] * oc_ref[...] + gx[:, 512:1024] * os_ref[...] + gx[:, 1024:1536] * ow_ref[...]
    y_nsa = _dot(o_nsa.astype(BF), wbn_ref[...])
    y_diff = _dot(od_ref[...].astype(BF), wbd_ref[...])
    gates = jax.nn.sigmoid(_dot(xn, wg_ref[...]))
    mrg = gates[:, 0:D_MODEL] * y_nsa + gates[:, D_MODEL:2 * D_MODEL] * y_diff
    x1 = x + _dot(mrg.astype(BF), wo_ref[...])
    x1_ref[...] = x1
    h = x1 * lax.rsqrt(jnp.mean(x1 * x1, axis=-1, keepdims=True) + RMS_EPS) * nffn_ref[...]
    h_ref[...] = h.astype(BF)
    logits = _dot_hi(h, wr_ref[...]) + br_ref[...]
    lane = _lane_iota(tm)
    vals, idxs = [], []
    for _ in range(TOP_K):
        m = jnp.max(logits, axis=-1, keepdims=True)
        idx = jnp.min(jnp.where(logits == m, lane, LANES), axis=-1, keepdims=True)
        vals.append(m)
        idxs.append(idx)
        logits = jnp.where(lane == idx, -jnp.inf, logits)
    es = [jnp.exp(v - vals[0]) for v in vals]
    den = es[0] + es[1] + es[2] + es[3]
    idx_out = jnp.zeros((tm, LANES), I32)
    gate_out = jnp.zeros((tm, LANES), F32)
    for k in range(TOP_K):
        idx_out = jnp.where(lane == k, idxs[k], idx_out)
        gate_out = jnp.where(lane == k, es[k] / den, gate_out)
    idx_ref[...] = idx_out
    gate_ref[...] = gate_out


def _merge(x2d, oc, osel, ow, od, gx, nmix, nffn, wg, wbn, wbd, wo, wr, br, tm):
    n = x2d.shape[0]
    row = lambda i: (i, 0)
    const = lambda i: (0, 0)
    return pl.pallas_call(
        _merge_body,
        grid=(n // tm,),
        in_specs=[pl.BlockSpec((tm, D_MODEL), row)] + [pl.BlockSpec((tm, 512), row)] * 4 +
                 [pl.BlockSpec((tm, 1536), row), pl.BlockSpec((1, D_MODEL), const), pl.BlockSpec((1, D_MODEL), const),
                  pl.BlockSpec((D_MODEL, 2 * D_MODEL), const), pl.BlockSpec((512, D_MODEL), const),
                  pl.BlockSpec((512, D_MODEL), const), pl.BlockSpec((D_MODEL, D_MODEL), const),
                  pl.BlockSpec((D_MODEL, LANES), const), pl.BlockSpec((1, LANES), const)],
        out_specs=[pl.BlockSpec((tm, D_MODEL), row), pl.BlockSpec((tm, D_MODEL), row),
                   pl.BlockSpec((tm, LANES), row), pl.BlockSpec((tm, LANES), row)],
        out_shape=[jax.ShapeDtypeStruct((n, D_MODEL), F32), jax.ShapeDtypeStruct((n, D_MODEL), BF),
                   jax.ShapeDtypeStruct((n, LANES), I32), jax.ShapeDtypeStruct((n, LANES), F32)],
        compiler_params=_params(("parallel",)),
        name="merge",
    )(x2d, oc, osel, ow, od, gx, nmix, nffn, wg, wbn, wbd, wo, wr, br)


def _moe_body(be_ref, nused_ref, x_ref, wgu_ref, bgu_ref, wd_ref, bd_ref, o_ref, wgu_s, wd_s):
    i = pl.program_id(0)
    e = be_ref[i]
    prev = be_ref[jnp.maximum(i - 1, 0)]

    @pl.when((i == 0) | (e != prev))
    def _():
        def cast_rows(c, carry):
            r0 = pl.multiple_of(c * LANES, LANES)
            wgu_s[pl.ds(r0, LANES), :] = wgu_ref[0, pl.ds(r0, LANES), :].astype(BF)
            wd_s[pl.ds(r0, LANES), :] = wd_ref[0, pl.ds(r0, LANES), :].astype(BF)
            return carry
        lax.fori_loop(0, D_MODEL // LANES, cast_rows, 0)

    @pl.when(i < nused_ref[0])
    def _():
        gu = _dot(x_ref[...], wgu_s[...]) + bgu_ref[0]
        gate = jnp.minimum(gu[:, 0:D_FF], SWIGLU_LIMIT)
        up = jnp.clip(gu[:, D_FF:2 * D_FF], -SWIGLU_LIMIT, SWIGLU_LIMIT)
        act = (up + 1.0) * gate * jax.nn.sigmoid(SWIGLU_ALPHA * gate)
        o_ref[...] = _dot(act.astype(BF), wd_s[...]) + bd_ref[0]

    @pl.when(i >= nused_ref[0])
    def _():
        o_ref[...] = jnp.zeros(o_ref.shape, F32)


def _moe(block_expert, n_used, xs, wgu, bgu, wd, bd, tmoe):
    n_rows = xs.shape[0]
    nblk = n_rows // tmoe
    return pl.pallas_call(
        _moe_body,
        grid_spec=pltpu.PrefetchScalarGridSpec(
            num_scalar_prefetch=2, grid=(nblk,),
            in_specs=[pl.BlockSpec((tmoe, D_MODEL), lambda i, be, nu: (i, 0)),
                      pl.BlockSpec((1, D_MODEL, 2 * D_FF), lambda i, be, nu: (be[i], 0, 0)),
                      pl.BlockSpec((1, 1, 2 * D_FF), lambda i, be, nu: (be[i], 0, 0)),
                      pl.BlockSpec((1, D_FF, D_MODEL), lambda i, be, nu: (be[i], 0, 0)),
                      pl.BlockSpec((1, 1, D_MODEL), lambda i, be, nu: (be[i], 0, 0))],
            out_specs=pl.BlockSpec((tmoe, D_MODEL), lambda i, be, nu: (i, 0)),
            scratch_shapes=[pltpu.VMEM((D_MODEL, 2 * D_FF), BF), pltpu.VMEM((D_FF, D_MODEL), BF)]),
        out_shape=jax.ShapeDtypeStruct((n_rows, D_MODEL), F32),
        compiler_params=_params(("arbitrary",)),
        name="moe_experts",
    )(block_expert, n_used, xs, wgu, bgu, wd, bd)


def _final_body(x1_ref, yg_ref, gate_ref, nf_ref, y_ref):
    gate = gate_ref[...]
    x = x1_ref[...]
    for k in range(TOP_K):
        x = x + gate[:, k:k + 1] * yg_ref[:, k * D_MODEL:(k + 1) * D_MODEL]
    y_ref[...] = x * lax.rsqrt(jnp.mean(x * x, axis=-1, keepdims=True) + RMS_EPS) * nf_ref[...]


def _final(x1, yg, gates, nf, tm):
    n = x1.shape[0]
    row = lambda i: (i, 0)
    return pl.pallas_call(
        _final_body,
        grid=(n // tm,),
        in_specs=[pl.BlockSpec((tm, D_MODEL), row), pl.BlockSpec((tm, TOP_K * D_MODEL), row),
                  pl.BlockSpec((tm, LANES), row), pl.BlockSpec((1, D_MODEL), lambda i: (0, 0))],
        out_specs=pl.BlockSpec((tm, D_MODEL), row),
        out_shape=jax.ShapeDtypeStruct((n, D_MODEL), F32),
        compiler_params=_params(("parallel",)),
        name="final_norm",
    )(x1, yg, gates, nf)


def _rope_tables(pos, nb_lanes):
    half = ROT_DIM // 2
    inv = ROPE_THETA ** (-jnp.arange(half, dtype=F32) / half)
    ang = pos.astype(F32)[:, None] * inv[None, :]
    cos, sin = jnp.cos(ang), jnp.sin(ang)
    t = pos.shape[0]
    c64 = jnp.concatenate([cos, cos, jnp.ones((t, HEAD_DIM - ROT_DIM), F32)], axis=1)
    s64 = jnp.concatenate([-sin, sin, jnp.zeros((t, HEAD_DIM - ROT_DIM), F32)], axis=1)
    lane = jnp.arange(LANES)[None, :]
    e = ((lane >= 64) & (lane < 64 + nb_lanes) & ((pos // SEL_BLOCK)[:, None] == lane - 64)).astype(F32)
    return jnp.tile(c64, (1, 2)), jnp.tile(s64, (1, 2)), e


def _imp_matrix(rt, n_cmp, nb, nbp):
    m = np.zeros((rt, nbp), np.float32)
    sub = SEL_BLOCK // CMP_STRIDE
    for n in range(n_cmp):
        for shift in range(CMP_BLOCK // CMP_STRIDE):
            j = (n + shift) // sub
            if j < nb:
                m[n, j] += 1.0
    return jnp.asarray(m)


def kernel(x_prompt, x_sample, cache_cmp_kv, cache_sel_kv, state_win_kv, cache_diff_k, cache_diff_v, page_table, norm_mix, w_in, cmp_pe, w_cmp1, w_cmp2, diff_lambda, diff_subln, w_br_nsa, w_br_diff, w_out, norm_ffn, w_router, b_router, w_gate_up, b_gate_up, w_down, b_down, norm_final):
    bp, t, _ = x_prompt.shape
    bs, ts, _ = x_sample.shape
    assert ts == 8 and t % 256 == 0 and t <= 2048 and w_in.shape[0] == 1
    n_pool, page = cache_cmp_kv.shape[1], cache_cmp_kv.shape[2]
    pages = page_table.shape[1]
    past_len = pages * page
    assert page == LANES and past_len % SEL_BLOCK == 0
    lambda_init = 0.8 - 0.6 * math.exp(-0.3 * 0)
    n_p, n_s = bp * t, bs * ts

    w = w_in[0]
    gcols = w[:, 1280:1304].reshape(D_MODEL, NSA_HEADS, 3).transpose(0, 2, 1)
    gexp = jnp.broadcast_to(gcols[..., None], (D_MODEL, 3, NSA_HEADS, HEAD_DIM)).reshape(D_MODEL, 1536)
    w_proj = jnp.concatenate([w[:, 0:1280], w[:, 1304:2840], gexp], axis=1).astype(BF)
    w_gates = w[:, 2840:4888].astype(BF)
    nmix = norm_mix[0][None, :]
    nffn = norm_ffn[0][None, :]
    pe = cmp_pe[0]
    pea = pe[:, :CMP_STRIDE].reshape(2, 1, 1024)
    peb = pe[:, CMP_STRIDE:].reshape(2, 1, 1024)
    w1a = w_cmp1[0][:, :1024].astype(BF)
    w1b = w_cmp1[0][:, 1024:].astype(BF)
    w2p = jnp.pad(w_cmp2[0], ((0, 0), (0, 0), (0, LANES - HEAD_DIM))).astype(BF)
    wbn = w_br_nsa[0].astype(BF)
    wbd = w_br_diff[0].astype(BF)
    wo = w_out[0].astype(BF)
    wr = jnp.pad(w_router[0], ((0, 0), (0, LANES - N_EXPERTS)))
    br = jnp.concatenate([b_router[0], jnp.full((LANES - N_EXPERTS,), NEG, F32)])[None, :]
    dl = diff_lambda[0]
    subln = diff_subln[0][None, :]

    pos_p = jnp.arange(t, dtype=I32)
    pos_s = jnp.tile(past_len + jnp.arange(ts, dtype=I32), bs)
    tm = 256
    outs_p = _inproj(x_prompt.reshape(n_p, D_MODEL), nmix, w_proj, *_rope_tables(pos_p, t // SEL_BLOCK), tm)
    outs_s = _inproj(x_sample.reshape(n_s, D_MODEL), nmix, w_proj, *_rope_tables(pos_s, 0), n_s)
    (qraw_p, qrot_p, kvc_p, kvs_p, kvw_p, ksa_p, kwa_p, vsb_p, vwb_p, qd_p, kd_p, kdb_p, vd_p, vdb_p, gx_p) = outs_p
    (qraw_s, qrot_s, kvc_s, kvs_s, kvw_s, _, _, _, _, qd_s, kd_s, _, vd_s, _, gx_s) = outs_s

    rt_p = t // CMP_STRIDE
    n_cmp_p = (t - CMP_BLOCK) // CMP_STRIDE + 1
    nb_p = t // SEL_BLOCK
    kvcmp_p = _compress(kvc_p.reshape(bp, rt_p, 4096), jnp.arange(bp, dtype=I32)[:, None], pea, peb, w1a, w1b, w2p, 1)
    tq = 128
    nq = t // tq
    ocmp_p, selb_p = _cmp_attend(qraw_p.reshape(bp * nq, tq, 1024), kvcmp_p, _imp_matrix(rt_p, n_cmp_p, nb_p, LANES),
                                 nq, nb_p, LANES, 0)
    qrot3 = qrot_p.reshape(bp * nq, tq, 1024)
    osel_p = _nsa_flash("sel", qrot3, selb_p, ksa_p, vsb_p, bp, t, tq, 256)
    owin_p = _nsa_flash("win", qrot3, None, kwa_p, vwb_p, bp, t, tq, tq)
    odiff_p = _diff_flash(qd_p.reshape(bp * nq, tq, 512), kdb_p, vdb_p, dl, subln, bp, t, tq, 256, lambda_init)

    n_cmp_s = (past_len + ts - CMP_BLOCK) // CMP_STRIDE + 1
    rt_s = past_len // CMP_STRIDE
    assert n_cmp_s <= rt_s
    nb_s = (past_len + ts + SEL_BLOCK - 1) // SEL_BLOCK
    assert nb_s <= 256
    ppg = 8 if pages % 8 == 0 else 1
    kvcmp_s = _compress(cache_cmp_kv[0].reshape(n_pool, page // CMP_STRIDE, 4096), page_table, pea, peb, w1a, w1b,
                        w2p, ppg)
    ocmp_s, selb_s = _cmp_attend(qraw_s.reshape(bs, ts, 1024), kvcmp_s, _imp_matrix(rt_s, n_cmp_s, nb_s, 256),
                                 1, nb_s, 256, past_len)
    qrot_s3 = qrot_s.reshape(bs, ts, 1024)
    osel_s = _paged_attend("sel", qrot_s3, selb_s, cache_sel_kv[0].reshape(n_pool, page, 256), None, page_table,
                           kvs_s.reshape(bs, ts, 256), None, None, None, ppg, past_len, lambda_init)
    wbuf = state_win_kv.shape[2]
    wpages = wbuf // LANES
    win_pt = (jnp.arange(bs, dtype=I32)[:, None] * wpages + jnp.arange(wpages, dtype=I32)[None, :])
    owin_s = _paged_attend("win", qrot_s3, None, state_win_kv[0].reshape(bs * wpages, LANES, 256), None, win_pt,
                           kvw_s.reshape(bs, ts, 256), None, None, None, wpages, past_len, lambda_init)
    odiff_s = _paged_attend("diff", qd_s.reshape(bs, ts, 512), None, cache_diff_k[0].reshape(n_pool, page, 512),
                            cache_diff_v[0].reshape(n_pool, page, 512), page_table, kd_s.reshape(bs, ts, 512),
                            vd_s.reshape(bs, ts, 512), dl, subln, ppg, past_len, lambda_init)

    x1_p, h_p, idx_p, gate_p = _merge(x_prompt.reshape(n_p, D_MODEL), ocmp_p.reshape(n_p, 512), osel_p.reshape(n_p, 512),
                                      owin_p.reshape(n_p, 512), odiff_p.reshape(n_p, 512), gx_p, nmix, nffn, w_gates,
                                      wbn, wbd, wo, wr, br, tm)
    x1_s, h_s, idx_s, gate_s = _merge(x_sample.reshape(n_s, D_MODEL), ocmp_s.reshape(n_s, 512), osel_s.reshape(n_s, 512),
                                      owin_s.reshape(n_s, 512), odiff_s.reshape(n_s, 512), gx_s, nmix, nffn, w_gates,
                                      wbn, wbd, wo, wr, br, n_s)

    n_tok = n_p + n_s
    n_assign = n_tok * TOP_K
    tmoe = 256
    h_all = jnp.concatenate([h_p, h_s, jnp.zeros((1, D_MODEL), BF)], axis=0)
    flat_e = jnp.concatenate([idx_p[:, :TOP_K], idx_s[:, :TOP_K]], axis=0).reshape(-1)
    order = jnp.argsort(flat_e)
    sorted_e = flat_e[order]
    counts = jnp.bincount(flat_e, length=N_EXPERTS)
    padded = (counts + tmoe - 1) // tmoe * tmoe
    pad_end = jnp.cumsum(padded)
    pad_start = pad_end - padded
    start = jnp.cumsum(counts) - counts
    dest = (pad_start[sorted_e] + jnp.arange(n_assign) - start[sorted_e]).astype(I32)
    n_blocks = (n_assign + N_EXPERTS * (tmoe - 1) + tmoe - 1) // tmoe
    n_rows = n_blocks * tmoe
    row_token = jnp.full((n_rows,), n_tok, I32).at[dest].set((order // TOP_K).astype(I32))
    block_expert = jnp.minimum(jnp.searchsorted(pad_end, jnp.arange(n_blocks) * tmoe, side="right"),
                               N_EXPERTS - 1).astype(I32)
    n_used = (pad_end[-1] // tmoe).astype(I32)[None]
    xs = jnp.take(h_all, row_token, axis=0)
    ys = _moe(block_expert, n_used, xs, w_gate_up[0], b_gate_up[0][:, None, :], w_down[0], b_down[0][:, None, :], tmoe)
    dest_by_assign = jnp.zeros((n_assign,), I32).at[order].set(dest)
    yg_p = jnp.take(ys, dest_by_assign[:n_p * TOP_K], axis=0).reshape(n_p, TOP_K * D_MODEL)
    yg_s = jnp.take(ys, dest_by_assign[n_p * TOP_K:], axis=0).reshape(n_s, TOP_K * D_MODEL)

    nf = norm_final[None, :]
    y_p = _final(x1_p, yg_p, gate_p, nf, tm)
    y_s = _final(x1_s, yg_s, gate_s, nf, n_s)

    g, hd = NSA_GROUPS, HEAD_DIM
    new_win_p = kvw_p.reshape(bp, t, 256)[:, t - wbuf:]
    new_win_s = jnp.concatenate([state_win_kv[0].reshape(bs, wbuf, 256)[:, ts:], kvw_s.reshape(bs, ts, 256)], axis=1)
    return (y_p.reshape(bp, t, D_MODEL), y_s.reshape(bs, ts, D_MODEL),
            kvc_p.reshape(1, bp, t, 2, g, hd), kvc_s.reshape(1, bs, ts, 2, g, hd),
            kvs_p.reshape(1, bp, t, 2, g, hd), kvs_s.reshape(1, bs, ts, 2, g, hd),
            new_win_p.reshape(1, bp, wbuf, 2, g, hd), new_win_s.reshape(1, bs, wbuf, 2, g, hd),
            kd_p.reshape(1, bp, t, DIFF_HEADS, 2, hd), kd_s.reshape(1, bs, ts, DIFF_HEADS, 2, hd),
            vd_p.reshape(1, bp, t, DIFF_HEADS, DIFF_DV), vd_s.reshape(1, bs, ts, DIFF_HEADS, DIFF_DV))
```

```python
import functools
import math

import numpy as np
import jax
import jax.numpy as jnp
from jax import lax
from jax.experimental import pallas as pl
from jax.experimental.pallas import tpu as pltpu

D_MODEL = 1024
HEAD_DIM = 64
NSA_HEADS = 8
NSA_GROUPS = 2
NSA_REP = 4
CMP_BLOCK = 32
CMP_STRIDE = 16
SEL_BLOCK = 64
N_SEL = 16
WINDOW = 512
DIFF_HEADS = 4
DIFF_DV = 128
ROT_DIM = 16
ROPE_THETA = 500000.0
N_EXPERTS = 32
TOP_K = 4
D_FF = 1024
SWIGLU_LIMIT = 7.0
SWIGLU_ALPHA = 1.702
RMS_EPS = 1e-5
FORCE_BONUS = 1e4
SCALE = HEAD_DIM ** -0.5
LANES = 128

BF = jnp.bfloat16
F32 = jnp.float32
I32 = jnp.int32
NEG = -1e30
VMEM_LIMIT = 56 * 1024 * 1024

OFF_Q, OFF_KVC, OFF_KVS, OFF_KVW, OFF_QD, OFF_KD, OFF_VD, OFF_GX = 0, 512, 768, 1024, 1280, 1792, 2304, 2816
IN_COLS_K = 4352
TOFF_Q, TOFF_KVC, TOFF_QD, TOFF_VD, TOFF_GX = 0, 512, 768, 1280, 1792
TIN_COLS = 3328
TIN_ROWS = 1280


def _dot(a, b):
    return jnp.dot(a, b, preferred_element_type=F32)


def _dot_nt(a, b):
    return lax.dot_general(a, b, (((1,), (1,)), ((), ())), preferred_element_type=F32)


def _dot_hi(a, b):
    return jnp.dot(a, b, preferred_element_type=F32, precision=lax.Precision.HIGHEST)


def _params(sem):
    return pltpu.CompilerParams(dimension_semantics=sem, vmem_limit_bytes=VMEM_LIMIT)


def _lane_iota(rows):
    return lax.broadcasted_iota(I32, (rows, LANES), 1)


def _pair(lo, a, b, g):
    if g == 0:
        return jnp.where(lo, a, pltpu.roll(b, 64, 1))
    return jnp.where(lo, pltpu.roll(a, 64, 1), b)


def _inproj_body(x_ref, nw_ref, w_ref, cos_ref, sin_ref, e_ref,
                 qraw_ref, qrot_ref, kvc_ref, kvs_ref, kvw_ref, ksa_ref, kwa_ref, vsb_ref, vwb_ref,
                 qd_ref, kd_ref, kdb_ref, vd_ref, vdb_ref, gx_ref):
    x = x_ref[...]
    tm = x.shape[0]
    xn = (x * lax.rsqrt(jnp.mean(x * x, axis=-1, keepdims=True) + RMS_EPS) * nw_ref[...]).astype(BF)
    cos = cos_ref[...]
    sin = sin_ref[...]
    epat = e_ref[...]
    lane = _lane_iota(tm)
    lo = lane < 64
    first8 = (lane & 63) < 8

    def mm(off):
        return _dot(xn, w_ref[:, off:off + LANES])

    def rope(y):
        sw = jnp.where(first8, pltpu.roll(y, LANES - 8, 1), pltpu.roll(y, 8, 1))
        return y * cos + sw * sin

    for c in range(4):
        y = mm(OFF_Q + c * LANES) * SCALE
        yr = rope(y)
        ys = pltpu.roll(y, 64, 1)
        yrs = pltpu.roll(yr, 64, 1)
        qraw_ref[:, (2 * c) * LANES:(2 * c + 1) * LANES] = jnp.where(lo, y, 0.0).astype(BF)
        qraw_ref[:, (2 * c + 1) * LANES:(2 * c + 2) * LANES] = jnp.where(lo, ys, 0.0).astype(BF)
        qrot_ref[:, (2 * c) * LANES:(2 * c + 1) * LANES] = jnp.where(lo, yr, 0.0).astype(BF)
        qrot_ref[:, (2 * c + 1) * LANES:(2 * c + 2) * LANES] = jnp.where(lo, yrs, 0.0).astype(BF)

    for c in range(2):
        kvc_ref[:, c * LANES:(c + 1) * LANES] = mm(OFF_KVC + c * LANES)

    for off, kv_ref, ka_ref, vb_ref in ((OFF_KVS, kvs_ref, ksa_ref, vsb_ref), (OFF_KVW, kvw_ref, kwa_ref, vwb_ref)):
        kr = rope(mm(off))
        v = mm(off + LANES)
        kv_ref[:, 0:LANES] = kr
        kv_ref[:, LANES:2 * LANES] = v
        ka_ref[:, 0:LANES] = jnp.where(lo, kr, epat).astype(BF)
        ka_ref[:, LANES:2 * LANES] = jnp.where(lo, pltpu.roll(kr, 64, 1), epat).astype(BF)
        vb_ref[...] = v.astype(BF)

    for c in range(4):
        sl = slice(c * LANES, (c + 1) * LANES)
        qd_ref[:, sl] = rope(mm(OFF_QD + c * LANES) * SCALE).astype(BF)
        kr = rope(mm(OFF_KD + c * LANES))
        kd_ref[:, sl] = kr
        kdb_ref[:, sl] = kr.astype(BF)
        v = mm(OFF_VD + c * LANES)
        vd_ref[:, sl] = v
        vdb_ref[:, sl] = v.astype(BF)

    for c in range(12):
        gx_ref[:, c * LANES:(c + 1) * LANES] = jax.nn.sigmoid(mm(OFF_GX + c * LANES))


def _inproj(x2d, norm_w, w_bf, cos_t, sin_t, e_t, tm):
    n = x2d.shape[0]
    nt = cos_t.shape[0] // tm
    row = lambda i: (i, 0)
    tab = lambda i: (i % nt, 0)
    const = lambda i: (0, 0)
    outs = [(1024, BF), (1024, BF), (256, F32), (256, F32), (256, F32), (256, BF), (256, BF), (128, BF), (128, BF),
            (512, BF), (512, F32), (512, BF), (512, F32), (512, BF), (1536, F32)]
    return pl.pallas_call(
        _inproj_body,
        grid=(n // tm,),
        in_specs=[pl.BlockSpec((tm, D_MODEL), row), pl.BlockSpec((1, D_MODEL), const),
                  pl.BlockSpec((D_MODEL, IN_COLS_K), const),
                  pl.BlockSpec((tm, LANES), tab), pl.BlockSpec((tm, LANES), tab), pl.BlockSpec((tm, LANES), tab)],
        out_specs=[pl.BlockSpec((tm, w), row) for w, _ in outs],
        out_shape=[jax.ShapeDtypeStruct((n, w), dt) for w, dt in outs],
        compiler_params=_params(("parallel",)),
        name="inproj",
    )(x2d, norm_w, w_bf, cos_t, sin_t, e_t)


def _inproj_t_body(x_ref, nw_ref, w_ref, wt_ref, cos_ref, sin_ref, cost_ref, sint_ref,
                   qraw_ref, qrot_ref, kvc_ref, kvct_ref, kvst_ref, kvwt_ref, ksat_ref, kwat_ref, vst_ref, vwt_ref,
                   qd_ref, kdt_ref, kdtb_ref, vd4_ref, vdb_ref, gx_ref):
    x = x_ref[...]
    tm = x.shape[0]
    xn = (x * lax.rsqrt(jnp.mean(x * x, axis=-1, keepdims=True) + RMS_EPS) * nw_ref[...]).astype(BF)
    cos = cos_ref[...]
    sin = sin_ref[...]
    cos_t = cost_ref[...]
    sin_t = sint_ref[...]
    lane = _lane_iota(tm)
    lo = lane < 64
    first8 = (lane & 63) < 8

    def mm(off):
        return _dot(xn, w_ref[:, off:off + LANES])

    def mm_t(off, n):
        return _dot_nt(wt_ref[off:off + n, :], xn)

    def rope(y):
        sw = jnp.where(first8, pltpu.roll(y, LANES - 8, 1), pltpu.roll(y, 8, 1))
        return y * cos + sw * sin

    def rope_t(y):
        parts = []
        for hb in range(0, y.shape[0], HEAD_DIM):
            x1 = y[hb:hb + 8]
            x2 = y[hb + 8:hb + 16]
            parts += [x1 * cos_t - x2 * sin_t, x1 * sin_t + x2 * cos_t, y[hb + 16:hb + HEAD_DIM]]
        return jnp.concatenate(parts, axis=0)

    for c in range(4):
        y = mm(TOFF_Q + c * LANES) * SCALE
        yr = rope(y)
        ys = pltpu.roll(y, 64, 1)
        yrs = pltpu.roll(yr, 64, 1)
        qraw_ref[:, (2 * c) * LANES:(2 * c + 1) * LANES] = jnp.where(lo, y, 0.0).astype(BF)
        qraw_ref[:, (2 * c + 1) * LANES:(2 * c + 2) * LANES] = jnp.where(lo, ys, 0.0).astype(BF)
        qrot_ref[:, (2 * c) * LANES:(2 * c + 1) * LANES] = jnp.where(lo, yr, 0.0).astype(BF)
        qrot_ref[:, (2 * c + 1) * LANES:(2 * c + 2) * LANES] = jnp.where(lo, yrs, 0.0).astype(BF)

    for c in range(2):
        kvc_ref[:, c * LANES:(c + 1) * LANES] = mm(TOFF_KVC + c * LANES)
    kvct_ref[0] = mm_t(0, 256)

    tpos = pl.program_id(1) * tm + lax.broadcasted_iota(I32, (32, tm), 1)
    e_t = jnp.where(jnp.right_shift(tpos, 6) == lax.broadcasted_iota(I32, (32, tm), 0), 1.0, 0.0)
    z_t = jnp.zeros((32, tm), F32)
    for off, kvt_ref, kat_ref, vt_ref in ((256, kvst_ref, ksat_ref, vst_ref), (512, kvwt_ref, kwat_ref, vwt_ref)):
        y = mm_t(off, 256)
        kr = rope_t(y[0:128])
        v = y[128:256]
        kvt_ref[0, 0:128, :] = kr
        kvt_ref[0, 128:256, :] = v
        for g in range(NSA_GROUPS):
            kat_ref[0, g] = jnp.concatenate([kr[g * 64:(g + 1) * 64], e_t, z_t], axis=0).astype(BF)
        vt_ref[0] = v.astype(BF)

    kd = rope_t(mm_t(768, 512))
    kdt_ref[0] = kd
    kdtb_ref[0] = kd.astype(BF)
    for c in range(4):
        sl = slice(c * LANES, (c + 1) * LANES)
        qd_ref[:, sl] = rope(mm(TOFF_QD + c * LANES) * SCALE).astype(BF)
        v = mm(TOFF_VD + c * LANES)
        vd4_ref[0, :, c, :] = v
        vdb_ref[:, sl] = v.astype(BF)

    for c in range(12):
        gx_ref[:, c * LANES:(c + 1) * LANES] = jax.nn.sigmoid(mm(TOFF_GX + c * LANES))


def _inproj_t(x2d, norm_w, w_bf, wt_bf, cos_t, sin_t, cos_tt, sin_tt, nbat, t, tm):
    n = x2d.shape[0]
    nt = t // tm
    row = lambda b, i: (b * nt + i, 0)
    tab = lambda b, i: (i, 0)
    tab_t = lambda b, i: (0, i)
    const = lambda b, i: (0, 0)
    tr = lambda b, i: (b, 0, i)
    tr4 = lambda b, i: (b, 0, 0, i)
    rm = lambda w, dt: (pl.BlockSpec((tm, w), row), jax.ShapeDtypeStruct((n, w), dt))
    tp = lambda r, dt: (pl.BlockSpec((1, r, tm), tr), jax.ShapeDtypeStruct((nbat, r, t), dt))
    aug = (pl.BlockSpec((1, NSA_GROUPS, LANES, tm), tr4), jax.ShapeDtypeStruct((nbat, NSA_GROUPS, LANES, t), BF))
    vd4 = (pl.BlockSpec((1, tm, DIFF_HEADS, DIFF_DV), lambda b, i: (b, i, 0, 0)),
           jax.ShapeDtypeStruct((nbat, t, DIFF_HEADS, DIFF_DV), F32))
    outs = [rm(1024, BF), rm(1024, BF), rm(256, F32), tp(256, F32), tp(256, F32), tp(256, F32), aug, aug,
            tp(128, BF), tp(128, BF), rm(512, BF), tp(512, F32), tp(512, BF), vd4, rm(512, BF), rm(1536, F32)]
    return pl.pallas_call(
        _inproj_t_body,
        grid=(nbat, nt),
        in_specs=[pl.BlockSpec((tm, D_MODEL), row), pl.BlockSpec((1, D_MODEL), const),
                  pl.BlockSpec((D_MODEL, TIN_COLS), const), pl.BlockSpec((TIN_ROWS, D_MODEL), const),
                  pl.BlockSpec((tm, LANES), tab), pl.BlockSpec((tm, LANES), tab),
                  pl.BlockSpec((8, tm), tab_t), pl.BlockSpec((8, tm), tab_t)],
        out_specs=[o[0] for o in outs],
        out_shape=[o[1] for o in outs],
        compiler_params=_params(("parallel", "parallel")),
        name="inproj_t",
    )(x2d, norm_w, w_bf, wt_bf, cos_t, sin_t, cos_tt, sin_tt)


def _compress_body(pt_ref, *refs, n_pages, rp, rt):
    page_refs = refs[:n_pages]
    pea_ref, peb_ref, w1a_ref, w1b_ref, w2_ref, out_ref, y_scr = refs[n_pages:]
    s = pl.program_id(1)
    for i in range(n_pages):
        x = page_refs[i][0]
        row0 = pl.multiple_of((s * n_pages + i) * rp, 8)
        for cg in range(4):
            ycg = jnp.concatenate(
                [x[:, l * 256 + cg * 64:l * 256 + cg * 64 + 64] for l in range(CMP_STRIDE)], axis=1)
            y_scr[cg, pl.ds(row0, rp), :] = ycg

    @pl.when(s == pl.num_programs(1) - 1)
    def _():
        for c in range(2):
            for g in range(2):
                cg = c * 2 + g
                y = y_scr[cg]
                za = _dot((y + pea_ref[c]).astype(BF), w1a_ref[c])
                zb = _dot((y + peb_ref[c]).astype(BF), w1b_ref[c])
                hid = jax.nn.gelu(za + pltpu.roll(zb, rt - 1, 0))
                out_ref[0, :, cg * LANES:(cg + 1) * LANES] = _dot(hid.astype(BF), w2_ref[c]).astype(BF)


def _compress(pool, page_table, pea, peb, w1a, w1b, w2p, n_pages):
    nb, pages = page_table.shape
    rp = pool.shape[1]
    rt = pages * rp
    steps = pages // n_pages
    page_specs = [pl.BlockSpec((1, rp, 4096), functools.partial(
        lambda b, s, pt, i: (pt[b, s * n_pages + i], 0, 0), i=i)) for i in range(n_pages)]
    c3 = lambda b, s, pt: (0, 0, 0)
    return pl.pallas_call(
        functools.partial(_compress_body, n_pages=n_pages, rp=rp, rt=rt),
        grid_spec=pltpu.PrefetchScalarGridSpec(
            num_scalar_prefetch=1,
            grid=(nb, steps),
            in_specs=page_specs + [pl.BlockSpec((2, 1, 1024), c3), pl.BlockSpec((2, 1, 1024), c3),
                                   pl.BlockSpec((2, 1024, LANES), c3), pl.BlockSpec((2, 1024, LANES), c3),
                                   pl.BlockSpec((2, LANES, LANES), c3)],
            out_specs=pl.BlockSpec((1, rt, 512), lambda b, s, pt: (b, 0, 0)),
            scratch_shapes=[pltpu.VMEM((4, rt, 1024), F32)]),
        out_shape=jax.ShapeDtypeStruct((nb, rt, 512), BF),
        compiler_params=_params(("parallel", "arbitrary")),
        name="compress",
    )(page_table, *([pool] * n_pages), pea, peb, w1a, w1b, w2p)


def _cmp_body(q_ref, kvc_ref, mimp_ref, ocmp_ref, selb_ref, *, tq, rt, nb, nbp, pos0):
    i = pl.program_id(1)
    pos_n = pos0 + i * tq + lax.broadcasted_iota(I32, (tq, rt), 0)
    n_i = lax.broadcasted_iota(I32, (tq, rt), 1)
    vis = (n_i * CMP_STRIDE + (CMP_BLOCK - 1)) <= pos_n
    lo = _lane_iota(tq) < 64
    pos_b = pos0 + i * tq + lax.broadcasted_iota(I32, (tq, nbp), 0)
    jb = lax.broadcasted_iota(I32, (tq, nbp), 1)
    cur = jnp.right_shift(pos_b, 6)
    valid = (jb * SEL_BLOCK <= pos_b) & (jb < nb)
    forced = (jb == 0) | (jb == cur) | (jb == cur - 1)
    mimp = mimp_ref[...]
    for g in range(NSA_GROUPS):
        kc = kvc_ref[0, :, g * LANES:(g + 1) * LANES]
        vc = kvc_ref[0, :, (2 + g) * LANES:(3 + g) * LANES]
        pg = jnp.zeros((tq, rt), F32)
        og = []
        for r in range(NSA_REP):
            h = g * NSA_REP + r
            q = q_ref[0, :, h * LANES:(h + 1) * LANES]
            s = jnp.where(vis, _dot_nt(q, kc), -jnp.inf)
            m = jnp.max(s, axis=-1, keepdims=True)
            m = jnp.where(m == -jnp.inf, 0.0, m)
            e = jnp.exp(s - m)
            d = jnp.sum(e, axis=-1, keepdims=True)
            p = e / jnp.where(d > 0, d, 1.0)
            og.append(_dot(p.astype(BF), vc))
            pg = pg + p
        imp = _dot_hi(pg, mimp)
        score = jnp.where(valid, jnp.where(forced, imp + FORCE_BONUS, imp), -jnp.inf)
        rank = jnp.zeros((tq, nbp), I32)
        for jj in range(nb):
            cj = score[:, jj:jj + 1]
            beats = (cj > score) | ((cj == score) & (jb > jj))
            rank = rank + beats.astype(I32)
        sel = (rank < N_SEL) & valid
        selb_ref[0, :, g * nbp:(g + 1) * nbp] = jnp.where(sel, 0.0, NEG)
        for c2 in range(2):
            ocmp_ref[0, :, (g * 2 + c2) * LANES:(g * 2 + c2 + 1) * LANES] = _pair(lo, og[2 * c2], og[2 * c2 + 1], 0)


def _cmp_attend(qraw3, kvc, mimp, nq, nb, nbp, pos0):
    nbat = kvc.shape[0]
    _, tq, _ = qraw3.shape
    rt = kvc.shape[1]
    return pl.pallas_call(
        functools.partial(_cmp_body, tq=tq, rt=rt, nb=nb, nbp=nbp, pos0=pos0),
        grid=(nbat, nq),
        in_specs=[pl.BlockSpec((1, tq, 1024), lambda b, i: (b * nq + i, 0, 0)),
                  pl.BlockSpec((1, rt, 512), lambda b, i: (b, 0, 0)),
                  pl.BlockSpec((rt, nbp), lambda b, i: (0, 0))],
        out_specs=[pl.BlockSpec((1, tq, 512), lambda b, i: (b * nq + i, 0, 0)),
                   pl.BlockSpec((1, tq, 2 * nbp), lambda b, i: (b * nq + i, 0, 0))],
        out_shape=[jax.ShapeDtypeStruct((nbat * nq, tq, 512), F32),
                   jax.ShapeDtypeStruct((nbat * nq, tq, 2 * nbp), F32)],
        compiler_params=_params(("parallel", "parallel")),
        name="cmp_attend",
    )(qraw3, kvc, mimp)


def _online(s, v, m_s, l_s, acc, idx, v_transposed=False):
    m_old = m_s[idx]
    m_new = jnp.maximum(m_old, jnp.max(s, axis=-1, keepdims=True))
    alpha = jnp.exp(m_old - m_new)
    p = jnp.exp(s - m_new)
    l_s[idx] = alpha * l_s[idx] + jnp.sum(p, axis=-1, keepdims=True)
    pv = _dot_nt(p.astype(BF), v) if v_transposed else _dot(p.astype(BF), v)
    acc[idx] = alpha * acc[idx] + pv
    m_s[idx] = m_new


def _nsa_flash_body(*refs, mode, tq, tk, nk):
    if mode == "sel":
        q_ref, selb_ref, k_ref, v_ref, o_ref, qs, m_s, l_s, acc = refs
    else:
        q_ref, k_ref, v_ref, o_ref, qs, m_s, l_s, acc = refs
    i = pl.program_id(1)
    j = pl.program_id(2)
    q0 = i * tq
    rows = NSA_REP * tq
    lane = _lane_iota(tq)
    lo = lane < 64

    @pl.when(j == 0)
    def _():
        m_s[...] = jnp.full(m_s.shape, NEG, F32)
        l_s[...] = jnp.zeros(l_s.shape, F32)
        acc[...] = jnp.zeros(acc.shape, F32)
        for g in range(NSA_GROUPS):
            if mode == "sel":
                sb = pltpu.roll(selb_ref[0, :, g * LANES:(g + 1) * LANES], 64, 1)
                sb = jnp.where((lane >= 64) & (lane < 96), sb, 0.0)
            for r in range(NSA_REP):
                h = g * NSA_REP + r
                q = q_ref[0, :, h * LANES:(h + 1) * LANES]
                if mode == "sel":
                    q = (q.astype(F32) + sb).astype(BF)
                qs[g, r * tq:(r + 1) * tq, :] = q

    if mode == "sel":
        jt = j
        active = j <= (q0 + tq - 1) // tk
        need_mask = j * tk + tk - 1 > q0
    else:
        jt = i - (nk - 1) + j
        active = jt >= 0
        need_mask = (j == 0) | (j == nk - 1)

    def step(masked):
        for g in range(NSA_GROUPS):
            s = _dot(qs[g], k_ref[0, g])
            if masked:
                rpos = q0 + (lax.broadcasted_iota(I32, (rows, tk), 0) & (tq - 1))
                kpos = jt * tk + lax.broadcasted_iota(I32, (rows, tk), 1)
                vis = kpos <= rpos
                if mode == "win":
                    vis = vis & (kpos > rpos - WINDOW)
                s = jnp.where(vis, s, NEG)
            _online(s, v_ref[0], m_s, l_s, acc, g, v_transposed=True)

    @pl.when(active & need_mask)
    def _():
        step(True)

    @pl.when(active & jnp.logical_not(need_mask))
    def _():
        step(False)

    @pl.when(j == nk - 1)
    def _():
        for g in range(NSA_GROUPS):
            on = acc[g] / l_s[g]
            for c2 in range(2):
                a = on[(2 * c2) * tq:(2 * c2 + 1) * tq]
                b = on[(2 * c2 + 1) * tq:(2 * c2 + 2) * tq]
                o_ref[0, :, (g * 2 + c2) * LANES:(g * 2 + c2 + 1) * LANES] = _pair(lo, a, b, g)


def _nsa_flash(mode, q3, selb3, k2, v2, nbat, t, tq, tk):
    nq = t // tq
    nkv = t // tk
    rows = NSA_REP * tq
    if mode == "sel":
        nk = nkv
        jmap = lambda i, j: jnp.minimum(j, (i * tq + tq - 1) // tk)
    else:
        nk = WINDOW // tk + 1
        jmap = lambda i, j: jnp.maximum(i - (nk - 1) + j, 0)
    qmap = lambda b, i, j: (b * nq + i, 0, 0)
    in_specs = [pl.BlockSpec((1, tq, 1024), qmap)]
    args = [q3]
    if mode == "sel":
        in_specs.append(pl.BlockSpec((1, tq, 2 * LANES), qmap))
        args.append(selb3)
    in_specs += [pl.BlockSpec((1, NSA_GROUPS, LANES, tk), lambda b, i, j: (b, 0, 0, jmap(i, j))),
                 pl.BlockSpec((1, LANES, tk), lambda b, i, j: (b, 0, jmap(i, j)))]
    args += [k2, v2]
    return pl.pallas_call(
        functools.partial(_nsa_flash_body, mode=mode, tq=tq, tk=tk, nk=nk),
        grid=(nbat, nq, nk),
        in_specs=in_specs,
        out_specs=pl.BlockSpec((1, tq, 512), qmap),
        out_shape=jax.ShapeDtypeStruct((nbat * nq, tq, 512), F32),
        scratch_shapes=[pltpu.VMEM((NSA_GROUPS, rows, LANES), BF), pltpu.VMEM((NSA_GROUPS, rows, 1), F32),
                        pltpu.VMEM((NSA_GROUPS, rows, 1), F32), pltpu.VMEM((NSA_GROUPS, rows, LANES), F32)],
        compiler_params=_params(("parallel", "parallel", "arbitrary")),
        name="nsa_flash_" + mode,
    )(*args)


def _diff_lambda(dl, lambda_init):
    a = jnp.sum(dl[0:1] * dl[1:2], axis=1, keepdims=True)
    b = jnp.sum(dl[2:3] * dl[3:4], axis=1, keepdims=True)
    return jnp.exp(a) - jnp.exp(b) + lambda_init


def _diff_finish(a0, a1, lam, subln, lambda_init):
    o = a0 - lam * a1
    o = o * lax.rsqrt(jnp.mean(o * o, axis=-1, keepdims=True) + RMS_EPS) * subln
    return o * (1.0 - lambda_init)


def _diff_flash_body(q_ref, k_ref, v_ref, dl_ref, sub_ref, o_ref, qs, m_s, l_s, acc, *, tq, tk, nk, lambda_init):
    i = pl.program_id(1)
    j = pl.program_id(2)
    q0 = i * tq
    rows = 2 * tq
    lo = _lane_iota(tq) < 64

    @pl.when(j == 0)
    def _():
        m_s[...] = jnp.full(m_s.shape, NEG, F32)
        l_s[...] = jnp.zeros(l_s.shape, F32)
        acc[...] = jnp.zeros(acc.shape, F32)
        for h in range(DIFF_HEADS):
            q = q_ref[0, :, h * LANES:(h + 1) * LANES].astype(F32)
            qs[h, 0:tq, :] = jnp.where(lo, q, 0.0).astype(BF)
            qs[h, tq:2 * tq, :] = jnp.where(lo, 0.0, q).astype(BF)

    active = j <= (q0 + tq - 1) // tk
    need_mask = j * tk + tk - 1 > q0

    def step(masked):
        for h in range(DIFF_HEADS):
            s = _dot(qs[h], k_ref[0, h * LANES:(h + 1) * LANES, :])
            if masked:
                rpos = q0 + (lax.broadcasted_iota(I32, (rows, tk), 0) & (tq - 1))
                kpos = j * tk + lax.broadcasted_iota(I32, (rows, tk), 1)
                s = jnp.where(kpos <= rpos, s, NEG)
            _online(s, v_ref[:, h * LANES:(h + 1) * LANES], m_s, l_s, acc, h)

    @pl.when(active & need_mask)
    def _():
        step(True)

    @pl.when(active & jnp.logical_not(need_mask))
    def _():
        step(False)

    @pl.when(j == nk - 1)
    def _():
        lam = _diff_lambda(dl_ref[...], lambda_init)
        for h in range(DIFF_HEADS):
            on = acc[h] / l_s[h]
            o_ref[0, :, h * LANES:(h + 1) * LANES] = _diff_finish(on[0:tq], on[tq:2 * tq], lam, sub_ref[...], lambda_init)


def _diff_flash(q3, k2, v2, dl, subln, nbat, t, tq, tk, lambda_init):
    nq = t // tq
    nk = t // tk
    qmap = lambda b, i, j: (b * nq + i, 0, 0)
    jmap = lambda i, j: jnp.minimum(j, (i * tq + tq - 1) // tk)
    kmap = lambda b, i, j: (b, 0, jmap(i, j))
    vmap = lambda b, i, j: (b * nk + jmap(i, j), 0)
    const = lambda b, i, j: (0, 0)
    rows = 2 * tq
    return pl.pallas_call(
        functools.partial(_diff_flash_body, tq=tq, tk=tk, nk=nk, lambda_init=lambda_init),
        grid=(nbat, nq, nk),
        in_specs=[pl.BlockSpec((1, tq, 512), qmap), pl.BlockSpec((1, 512, tk), kmap), pl.BlockSpec((tk, 512), vmap),
                  pl.BlockSpec((4, HEAD_DIM), const), pl.BlockSpec((1, DIFF_DV), const)],
        out_specs=pl.BlockSpec((1, tq, 512), qmap),
        out_shape=jax.ShapeDtypeStruct((nbat * nq, tq, 512), F32),
        scratch_shapes=[pltpu.VMEM((DIFF_HEADS, rows, LANES), BF), pltpu.VMEM((DIFF_HEADS, rows, 1), F32),
                        pltpu.VMEM((DIFF_HEADS, rows, 1), F32), pltpu.VMEM((DIFF_HEADS, rows, LANES), F32)],
        compiler_params=_params(("parallel", "parallel", "arbitrary")),
        name="diff_flash",
    )(q3, k2, v2, dl, subln)


def _paged_body(pt_ref, *refs, mode, n_pages, past_len, lambda_init):
    q_ref = refs[0]
    k = 1
    if mode == "sel":
        selb_ref = refs[k]
        k += 1
    kpages = refs[k:k + n_pages]
    k += n_pages
    if mode == "diff":
        vpages = refs[k:k + n_pages]
        k += n_pages
    newk_ref = refs[k]
    k += 1
    if mode == "diff":
        newv_ref, dl_ref, sub_ref = refs[k:k + 3]
        k += 3
    o_ref = refs[k]
    k += 1
    if mode == "sel":
        qbd, bias, m_s, l_s, acc = refs[k:]
    else:
        qbd, m_s, l_s, acc = refs[k:]
    kw = WINDOW if mode == "win" else LANES
    st = pl.program_id(1)
    nq = 8
    rows = 64
    lane8 = _lane_iota(nq)
    lo8 = lane8 < 64
    t_row = lax.broadcasted_iota(I32, (rows, LANES), 0) & (nq - 1)
    col = lax.broadcasted_iota(I32, (rows, LANES), 1)

    def update(s, pv):
        m_old = m_s[0]
        m_new = jnp.maximum(m_old, jnp.max(s, axis=-1, keepdims=True))
        alpha = jnp.exp(m_old - m_new)
        p = jnp.exp(s - m_new)
        l_s[0] = alpha * l_s[0] + jnp.sum(p, axis=-1, keepdims=True)
        acc[0] = alpha * acc[0] + pv(p.astype(BF))
        m_s[0] = m_new

    def per_head(p, v_of_head):
        return jnp.concatenate([_dot(p[h * 16:(h + 1) * 16], v_of_head(h)) for h in range(DIFF_HEADS)], axis=0)

    @pl.when(st == 0)
    def _():
        m_s[0] = jnp.full((rows, 1), NEG, F32)
        l_s[0] = jnp.zeros((rows, 1), F32)
        acc[0] = jnp.zeros(acc.shape[1:], F32)
        if mode == "diff":
            qbd[...] = jnp.zeros(qbd.shape, F32)
            for h in range(DIFF_HEADS):
                q = q_ref[0, :, h * LANES:(h + 1) * LANES].astype(F32)
                qbd[(2 * h) * nq:(2 * h + 1) * nq, h * LANES:(h + 1) * LANES] = jnp.where(lo8, q, 0.0)
                qbd[(2 * h + 1) * nq:(2 * h + 2) * nq, h * LANES:(h + 1) * LANES] = jnp.where(lo8, 0.0, q)
        else:
            for h in range(NSA_HEADS):
                g = h // NSA_REP
                q = q_ref[0, :, h * LANES:(h + 1) * LANES].astype(F32)
                qbd[h * nq:(h + 1) * nq, :] = q if g == 0 else pltpu.roll(q, 64, 1)
                if mode == "sel":
                    bias[h * nq:(h + 1) * nq, :] = selb_ref[0, :, g * 256:(g + 1) * 256]
        kn = newk_ref[0]
        if mode == "diff":
            kk, vv = kn, newv_ref[0]
        else:
            kk, vv = kn[:, 0:LANES], kn[:, LANES:2 * LANES]
        kpad = jnp.concatenate([kk, jnp.zeros((LANES - nq, kk.shape[1]), F32)], axis=0).astype(BF)
        vpad = jnp.concatenate([vv, jnp.zeros((LANES - nq, vv.shape[1]), F32)], axis=0).astype(BF)
        s = _dot_nt(qbd[...].astype(BF), kpad)
        if mode == "sel":
            jnew = past_len // SEL_BLOCK
            s = s + bias[:, jnew:jnew + 1]
        s = jnp.where((col < nq) & (col <= t_row), s, NEG)
        if mode == "diff":
            update(s, lambda p: per_head(p, lambda h: vpad[:, h * LANES:(h + 1) * LANES]))
        else:
            update(s, lambda p: _dot(p, vpad))

    qb = qbd[...].astype(BF)
    for i in range(n_pages):
        pg = st * n_pages + i
        if mode == "diff":
            kt = kpages[i][0].reshape(4 * LANES, LANES).astype(BF)
        else:
            kt = kpages[i][0, 0].reshape(LANES, kw).astype(BF)
        s = _dot(qb, kt)
        if mode == "sel":
            krow = lax.broadcasted_iota(I32, (LANES, 256), 0)
            jbi = lax.broadcasted_iota(I32, (LANES, 256), 1)
            ep = jnp.where(jbi == 2 * pg + jnp.right_shift(krow, 6), 1.0, 0.0).astype(BF)
            s = s + _dot_nt(bias[...].astype(BF), ep)
        if mode == "win":
            t_w = lax.broadcasted_iota(I32, (rows, kw), 0) & (nq - 1)
            s = jnp.where(lax.broadcasted_iota(I32, (rows, kw), 1) > t_w, s, NEG)
        if mode == "diff":
            vpage = vpages[i]
            update(s, lambda p: per_head(p, lambda h: vpage[0, :, h, :].astype(BF)))
        else:
            vt = kpages[i][0, 1].reshape(LANES, kw).astype(BF)
            update(s, lambda p: _dot_nt(p, vt))

    @pl.when(st == pl.num_programs(1) - 1)
    def _():
        on = acc[0] / l_s[0]
        if mode == "diff":
            lam = _diff_lambda(dl_ref[...], lambda_init)
            for h in range(DIFF_HEADS):
                a0 = on[h * 16:h * 16 + nq]
                a1 = on[h * 16 + nq:h * 16 + 2 * nq]
                o_ref[0, :, h * LANES:(h + 1) * LANES] = _diff_finish(a0, a1, lam, sub_ref[...], lambda_init)
        else:
            for c in range(4):
                a = on[(2 * c) * nq:(2 * c + 1) * nq]
                b = on[(2 * c + 1) * nq:(2 * c + 2) * nq]
                o_ref[0, :, c * LANES:(c + 1) * LANES] = _pair(lo8, a, b, c // 2)


def _paged_attend(mode, q3, selb3, pool_k, pool_v, page_table, newk3, newv3, dl, subln, n_pages, past_len, lambda_init):
    nbat, pages = page_table.shape
    steps = pages // n_pages
    wq = q3.shape[2]
    wk = newk3.shape[2]
    bmap = lambda b, s, pt: (b, 0, 0)
    c2 = lambda b, s, pt: (0, 0)

    def pmap(i, rank):
        return lambda b, s, pt: (pt[b, s * n_pages + i],) + (0,) * (rank - 1)

    in_specs = [pl.BlockSpec((1, 8, wq), bmap)]
    args = [q3]
    if mode == "sel":
        in_specs.append(pl.BlockSpec((1, 8, 512), bmap))
        args.append(selb3)
    kblock = (1,) + pool_k.shape[1:]
    in_specs += [pl.BlockSpec(kblock, pmap(i, len(kblock))) for i in range(n_pages)]
    args += [pool_k] * n_pages
    if mode == "diff":
        in_specs += [pl.BlockSpec((1, LANES, DIFF_HEADS, DIFF_DV), pmap(i, 4)) for i in range(n_pages)]
        args += [pool_v] * n_pages
    in_specs.append(pl.BlockSpec((1, 8, wk), bmap))
    args.append(newk3)
    if mode == "diff":
        in_specs += [pl.BlockSpec((1, 8, 512), bmap), pl.BlockSpec((4, HEAD_DIM), c2), pl.BlockSpec((1, DIFF_DV), c2)]
        args += [newv3, dl, subln]
    ck = 512 if mode == "diff" else LANES
    scratch = [pltpu.VMEM((64, ck), F32)]
    if mode == "sel":
        scratch.append(pltpu.VMEM((64, 256), F32))
    scratch += [pltpu.VMEM((1, 64, 1), F32), pltpu.VMEM((1, 64, 1), F32), pltpu.VMEM((1, 64, LANES), F32)]
    return pl.pallas_call(
        functools.partial(_paged_body, mode=mode, n_pages=n_pages, past_len=past_len, lambda_init=lambda_init),
        grid_spec=pltpu.PrefetchScalarGridSpec(
            num_scalar_prefetch=1, grid=(nbat, steps), in_specs=in_specs,
            out_specs=pl.BlockSpec((1, 8, 512), bmap), scratch_shapes=scratch),
        out_shape=jax.ShapeDtypeStruct((nbat, 8, 512), F32),
        compiler_params=_params(("parallel", "arbitrary")),
        name="paged_" + mode,
    )(page_table, *args)


def _merge_body(x_ref, oc_ref, os_ref, ow_ref, od_ref, gx_ref, nmix_ref, nffn_ref, wg_ref, wbn_ref, wbd_ref, wo_ref,
                wr_ref, br_ref, x1_ref, h_ref, idx_ref, gate_ref):
    x = x_ref[...]
    tm = x.shape[0]
    xn = (x * lax.rsqrt(jnp.mean(x * x, axis=-1, keepdims=True) + RMS_EPS) * nmix_ref[...]).astype(BF)
    gx = gx_ref[...]
    o_nsa = gx[:, 0:512] * oc_ref[...] + gx[:, 512:1024] * os_ref[...] + gx[:, 1024:1536] * ow_ref[...]
    y_nsa = _dot(o_nsa.astype(BF), wbn_ref[...])
    y_diff = _dot(od_ref[...].astype(BF), wbd_ref[...])
    gates = jax.nn.sigmoid(_dot(xn, wg_ref[...]))
    mrg = gates[:, 0:D_MODEL] * y_nsa + gates[:, D_MODEL:2 * D_MODEL] * y_diff
    x1 = x + _dot(mrg.astype(BF), wo_ref[...])
    x1_ref[...] = x1
    h = x1 * lax.rsqrt(jnp.mean(x1 * x1, axis=-1, keepdims=True) + RMS_EPS) * nffn_ref[...]
    h_ref[...] = h.astype(BF)
    logits = _dot_hi(h, wr_ref[...]) + br_ref[...]
    lane = _lane_iota(tm)
    vals, idxs = [], []
    for _ in range(TOP_K):
        m = jnp.max(logits, axis=-1, keepdims=True)
        idx = jnp.min(jnp.where(logits == m, lane, LANES), axis=-1, keepdims=True)
        vals.append(m)
        idxs.append(idx)
        logits = jnp.where(lane == idx, -jnp.inf, logits)
    es = [jnp.exp(v - vals[0]) for v in vals]
    den = es[0] + es[1] + es[2] + es[3]
    idx_out = jnp.zeros((tm, LANES), I32)
    gate_out = jnp.zeros((tm, LANES), F32)
    for k in range(TOP_K):
        idx_out = jnp.where(lane == k, idxs[k], idx_out)
        gate_out = jnp.where(lane == k, es[k] / den, gate_out)
    idx_ref[...] = idx_out
    gate_ref[...] = gate_out


def _merge(x2d, oc, osel, ow, od, gx, nmix, nffn, wg, wbn, wbd, wo, wr, br, tm):
    n = x2d.shape[0]
    row = lambda i: (i, 0)
    const = lambda i: (0, 0)
    return pl.pallas_call(
        _merge_body,
        grid=(n // tm,),
        in_specs=[pl.BlockSpec((tm, D_MODEL), row)] + [pl.BlockSpec((tm, 512), row)] * 4 +
                 [pl.BlockSpec((tm, 1536), row), pl.BlockSpec((1, D_MODEL), const), pl.BlockSpec((1, D_MODEL), const),
                  pl.BlockSpec((D_MODEL, 2 * D_MODEL), const), pl.BlockSpec((512, D_MODEL), const),
                  pl.BlockSpec((512, D_MODEL), const), pl.BlockSpec((D_MODEL, D_MODEL), const),
                  pl.BlockSpec((D_MODEL, LANES), const), pl.BlockSpec((1, LANES), const)],
        out_specs=[pl.BlockSpec((tm, D_MODEL), row), pl.BlockSpec((tm, D_MODEL), row),
                   pl.BlockSpec((tm, LANES), row), pl.BlockSpec((tm, LANES), row)],
        out_shape=[jax.ShapeDtypeStruct((n, D_MODEL), F32), jax.ShapeDtypeStruct((n, D_MODEL), BF),
                   jax.ShapeDtypeStruct((n, LANES), I32), jax.ShapeDtypeStruct((n, LANES), F32)],
        compiler_params=_params(("parallel",)),
        name="merge",
    )(x2d, oc, osel, ow, od, gx, nmix, nffn, wg, wbn, wbd, wo, wr, br)


def _moe_body(be_ref, nused_ref, x_ref, wgu_ref, bgu_ref, wd_ref, bd_ref, o_ref, wgu_s, wd_s):
    i = pl.program_id(0)
    e = be_ref[i]
    prev = be_ref[jnp.maximum(i - 1, 0)]

    @pl.when((i == 0) | (e != prev))
    def _():
        def cast_rows(c, carry):
            r0 = pl.multiple_of(c * LANES, LANES)
            wgu_s[pl.ds(r0, LANES), :] = wgu_ref[0, pl.ds(r0, LANES), :].astype(BF)
            wd_s[pl.ds(r0, LANES), :] = wd_ref[0, pl.ds(r0, LANES), :].astype(BF)
            return carry
        lax.fori_loop(0, D_MODEL // LANES, cast_rows, 0)

    @pl.when(i < nused_ref[0])
    def _():
        gu = _dot(x_ref[...], wgu_s[...]) + bgu_ref[0]
        gate = jnp.minimum(gu[:, 0:D_FF], SWIGLU_LIMIT)
        up = jnp.clip(gu[:, D_FF:2 * D_FF], -SWIGLU_LIMIT, SWIGLU_LIMIT)
        act = (up + 1.0) * gate * jax.nn.sigmoid(SWIGLU_ALPHA * gate)
        o_ref[...] = _dot(act.astype(BF), wd_s[...]) + bd_ref[0]

    @pl.when(i >= nused_ref[0])
    def _():
        o_ref[...] = jnp.zeros(o_ref.shape, F32)


def _moe(block_expert, n_used, xs, wgu, bgu, wd, bd, tmoe):
    n_rows = xs.shape[0]
    nblk = n_rows // tmoe
    return pl.pallas_call(
        _moe_body,
        grid_spec=pltpu.PrefetchScalarGridSpec(
            num_scalar_prefetch=2, grid=(nblk,),
            in_specs=[pl.BlockSpec((tmoe, D_MODEL), lambda i, be, nu: (i, 0)),
                      pl.BlockSpec((1, D_MODEL, 2 * D_FF), lambda i, be, nu: (be[i], 0, 0)),
                      pl.BlockSpec((1, 1, 2 * D_FF), lambda i, be, nu: (be[i], 0, 0)),
                      pl.BlockSpec((1, D_FF, D_MODEL), lambda i, be, nu: (be[i], 0, 0)),
                      pl.BlockSpec((1, 1, D_MODEL), lambda i, be, nu: (be[i], 0, 0))],
            out_specs=pl.BlockSpec((tmoe, D_MODEL), lambda i, be, nu: (i, 0)),
            scratch_shapes=[pltpu.VMEM((D_MODEL, 2 * D_FF), BF), pltpu.VMEM((D_FF, D_MODEL), BF)]),
        out_shape=jax.ShapeDtypeStruct((n_rows, D_MODEL), F32),
        compiler_params=_params(("arbitrary",)),
        name="moe_experts",
    )(block_expert, n_used, xs, wgu, bgu, wd, bd)


def _final_body(x1_ref, yg_ref, gate_ref, nf_ref, y_ref):
    gate = gate_ref[...]
    x = x1_ref[...]
    for k in range(TOP_K):
        x = x + gate[:, k:k + 1] * yg_ref[:, k * D_MODEL:(k + 1) * D_MODEL]
    y_ref[...] = x * lax.rsqrt(jnp.mean(x * x, axis=-1, keepdims=True) + RMS_EPS) * nf_ref[...]


def _final(x1, yg, gates, nf, tm):
    n = x1.shape[0]
    row = lambda i: (i, 0)
    return pl.pallas_call(
        _final_body,
        grid=(n // tm,),
        in_specs=[pl.BlockSpec((tm, D_MODEL), row), pl.BlockSpec((tm, TOP_K * D_MODEL), row),
                  pl.BlockSpec((tm, LANES), row), pl.BlockSpec((1, D_MODEL), lambda i: (0, 0))],
        out_specs=pl.BlockSpec((tm, D_MODEL), row),
        out_shape=jax.ShapeDtypeStruct((n, D_MODEL), F32),
        compiler_params=_params(("parallel",)),
        name="final_norm",
    )(x1, yg, gates, nf)


def _rope_tables(pos, nb_lanes):
    half = ROT_DIM // 2
    inv = ROPE_THETA ** (-jnp.arange(half, dtype=F32) / half)
    ang = pos.astype(F32)[:, None] * inv[None, :]
    cos, sin = jnp.cos(ang), jnp.sin(ang)
    t = pos.shape[0]
    c64 = jnp.concatenate([cos, cos, jnp.ones((t, HEAD_DIM - ROT_DIM), F32)], axis=1)
    s64 = jnp.concatenate([-sin, sin, jnp.zeros((t, HEAD_DIM - ROT_DIM), F32)], axis=1)
    lane = jnp.arange(LANES)[None, :]
    e = ((lane >= 64) & (lane < 64 + nb_lanes) & ((pos // SEL_BLOCK)[:, None] == lane - 64)).astype(F32)
    return jnp.tile(c64, (1, 2)), jnp.tile(s64, (1, 2)), e


def _imp_matrix(rt, n_cmp, nb, nbp):
    m = np.zeros((rt, nbp), np.float32)
    sub = SEL_BLOCK // CMP_STRIDE
    for n in range(n_cmp):
        for shift in range(CMP_BLOCK // CMP_STRIDE):
            j = (n + shift) // sub
            if j < nb:
                m[n, j] += 1.0
    return jnp.asarray(m)


def kernel(x_prompt, x_sample, cache_cmp_kv, cache_sel_kv, state_win_kv, cache_diff_k, cache_diff_v, page_table, norm_mix, w_in, cmp_pe, w_cmp1, w_cmp2, diff_lambda, diff_subln, w_br_nsa, w_br_diff, w_out, norm_ffn, w_router, b_router, w_gate_up, b_gate_up, w_down, b_down, norm_final):
    bp, t, _ = x_prompt.shape
    bs, ts, _ = x_sample.shape
    assert ts == 8 and t % 256 == 0 and t <= 2048 and w_in.shape[0] == 1
    n_pool, page = cache_cmp_kv.shape[1], cache_cmp_kv.shape[2]
    pages = page_table.shape[1]
    past_len = pages * page
    assert page == LANES and past_len % SEL_BLOCK == 0
    lambda_init = 0.8 - 0.6 * math.exp(-0.3 * 0)
    n_p, n_s = bp * t, bs * ts

    w = w_in[0]
    gcols = w[:, 1280:1304].reshape(D_MODEL, NSA_HEADS, 3).transpose(0, 2, 1)
    gexp = jnp.broadcast_to(gcols[..., None], (D_MODEL, 3, NSA_HEADS, HEAD_DIM)).reshape(D_MODEL, 1536)
    w_proj = jnp.concatenate([w[:, 0:1280], w[:, 1304:2840], gexp], axis=1).astype(BF)
    w_gates = w[:, 2840:4888].astype(BF)
    nmix = norm_mix[0][None, :]
    nffn = norm_ffn[0][None, :]
    pe = cmp_pe[0]
    pea = pe[:, :CMP_STRIDE].reshape(2, 1, 1024)
    peb = pe[:, CMP_STRIDE:].reshape(2, 1, 1024)
    w1a = w_cmp1[0][:, :1024].astype(BF)
    w1b = w_cmp1[0][:, 1024:].astype(BF)
    w2p = jnp.pad(w_cmp2[0], ((0, 0), (0, 0), (0, LANES - HEAD_DIM))).astype(BF)
    wbn = w_br_nsa[0].astype(BF)
    wbd = w_br_diff[0].astype(BF)
    wo = w_out[0].astype(BF)
    wr = jnp.pad(w_router[0], ((0, 0), (0, LANES - N_EXPERTS)))
    br = jnp.concatenate([b_router[0], jnp.full((LANES - N_EXPERTS,), NEG, F32)])[None, :]
    dl = diff_lambda[0]
    subln = diff_subln[0][None, :]

    pos_p = jnp.arange(t, dtype=I32)
    pos_s = jnp.tile(past_len + jnp.arange(ts, dtype=I32), bs)
    tm = 256
    w_rows = jnp.concatenate([w[:, 0:768], w[:, 1304:1816], w[:, 2328:2840], gexp], axis=1).astype(BF)
    w_cols = jnp.concatenate([w[:, 512:1280], w[:, 1816:2328]], axis=1).T.astype(BF)
    cos_p, sin_p, _ = _rope_tables(pos_p, 0)
    outs_p = _inproj_t(x_prompt.reshape(n_p, D_MODEL), nmix, w_rows, w_cols, cos_p, sin_p,
                       cos_p[:, 0:8].T, -sin_p[:, 0:8].T, bp, t, tm)
    outs_s = _inproj(x_sample.reshape(n_s, D_MODEL), nmix, w_proj, *_rope_tables(pos_s, 0), n_s)
    (qraw_p, qrot_p, kvc_p, kvct_p, kvst_p, kvwt_p, ksat_p, kwat_p, vst_p, vwt_p, qd_p, kdt_p, kdtb_p, vd4_p, vdb_p,
     gx_p) = outs_p
    (qraw_s, qrot_s, kvc_s, kvs_s, kvw_s, _, _, _, _, qd_s, kd_s, _, vd_s, _, gx_s) = outs_s

    rt_p = t // CMP_STRIDE
    n_cmp_p = (t - CMP_BLOCK) // CMP_STRIDE + 1
    nb_p = t // SEL_BLOCK
    kvcmp_p = _compress(kvc_p.reshape(bp, rt_p, 4096), jnp.arange(bp, dtype=I32)[:, None], pea, peb, w1a, w1b, w2p, 1)
    tq = 128
    nq = t // tq
    ocmp_p, selb_p = _cmp_attend(qraw_p.reshape(bp * nq, tq, 1024), kvcmp_p, _imp_matrix(rt_p, n_cmp_p, nb_p, LANES),
                                 nq, nb_p, LANES, 0)
    qrot3 = qrot_p.reshape(bp * nq, tq, 1024)
    osel_p = _nsa_flash("sel", qrot3, selb_p, ksat_p, vst_p, bp, t, tq, 256)
    owin_p = _nsa_flash("win", qrot3, None, kwat_p, vwt_p, bp, t, tq, tq)
    odiff_p = _diff_flash(qd_p.reshape(bp * nq, tq, 512), kdtb_p, vdb_p, dl, subln, bp, t, tq, 256, lambda_init)

    n_cmp_s = (past_len + ts - CMP_BLOCK) // CMP_STRIDE + 1
    rt_s = past_len // CMP_STRIDE
    assert n_cmp_s <= rt_s
    nb_s = (past_len + ts + SEL_BLOCK - 1) // SEL_BLOCK
    assert nb_s <= 256
    ppg = 8 if pages % 8 == 0 else 1
    kvcmp_s = _compress(cache_cmp_kv[0].reshape(n_pool, page // CMP_STRIDE, 4096), page_table, pea, peb, w1a, w1b,
                        w2p, ppg)
    ocmp_s, selb_s = _cmp_attend(qraw_s.reshape(bs, ts, 1024), kvcmp_s, _imp_matrix(rt_s, n_cmp_s, nb_s, 256),
                                 1, nb_s, 256, past_len)
    qrot_s3 = qrot_s.reshape(bs, ts, 1024)
    sel_t = jnp.transpose(cache_sel_kv[0], (0, 2, 3, 4, 1))
    win_t = jnp.transpose(state_win_kv[0], (0, 2, 3, 4, 1))
    dk_t = jnp.transpose(cache_diff_k[0], (0, 2, 3, 4, 1))
    osel_s = _paged_attend("sel", qrot_s3, selb_s, sel_t, None, page_table,
                           kvs_s.reshape(bs, ts, 256), None, None, None, ppg, past_len, lambda_init)
    wbuf = state_win_kv.shape[2]
    assert wbuf == WINDOW
    owin_s = _paged_attend("win", qrot_s3, None, win_t, None, jnp.arange(bs, dtype=I32)[:, None],
                           kvw_s.reshape(bs, ts, 256), None, None, None, 1, past_len, lambda_init)
    odiff_s = _paged_attend("diff", qd_s.reshape(bs, ts, 512), None, dk_t, cache_diff_v[0], page_table,
                            kd_s.reshape(bs, ts, 512), vd_s.reshape(bs, ts, 512), dl, subln, ppg, past_len, lambda_init)

    x1_p, h_p, idx_p, gate_p = _merge(x_prompt.reshape(n_p, D_MODEL), ocmp_p.reshape(n_p, 512), osel_p.reshape(n_p, 512),
                                      owin_p.reshape(n_p, 512), odiff_p.reshape(n_p, 512), gx_p, nmix, nffn, w_gates,
                                      wbn, wbd, wo, wr, br, tm)
    x1_s, h_s, idx_s, gate_s = _merge(x_sample.reshape(n_s, D_MODEL), ocmp_s.reshape(n_s, 512), osel_s.reshape(n_s, 512),
                                      owin_s.reshape(n_s, 512), odiff_s.reshape(n_s, 512), gx_s, nmix, nffn, w_gates,
                                      wbn, wbd, wo, wr, br, n_s)

    n_tok = n_p + n_s
    n_assign = n_tok * TOP_K
    tmoe = 256
    h_all = jnp.concatenate([h_p, h_s, jnp.zeros((1, D_MODEL), BF)], axis=0)
    flat_e = jnp.concatenate([idx_p[:, :TOP_K], idx_s[:, :TOP_K]], axis=0).reshape(-1)
    order = jnp.argsort(flat_e)
    sorted_e = flat_e[order]
    counts = jnp.bincount(flat_e, length=N_EXPERTS)
    padded = (counts + tmoe - 1) // tmoe * tmoe
    pad_end = jnp.cumsum(padded)
    pad_start = pad_end - padded
    start = jnp.cumsum(counts) - counts
    dest = (pad_start[sorted_e] + jnp.arange(n_assign) - start[sorted_e]).astype(I32)
    n_blocks = (n_assign + N_EXPERTS * (tmoe - 1) + tmoe - 1) // tmoe
    n_rows = n_blocks * tmoe
    row_token = jnp.full((n_rows,), n_tok, I32).at[dest].set((order // TOP_K).astype(I32))
    block_expert = jnp.minimum(jnp.searchsorted(pad_end, jnp.arange(n_blocks) * tmoe, side="right"),
                               N_EXPERTS - 1).astype(I32)
    n_used = (pad_end[-1] // tmoe).astype(I32)[None]
    xs = jnp.take(h_all, row_token, axis=0)
    ys = _moe(block_expert, n_used, xs, w_gate_up[0], b_gate_up[0][:, None, :], w_down[0], b_down[0][:, None, :], tmoe)
    dest_by_assign = jnp.zeros((n_assign,), I32).at[order].set(dest)
    yg_p = jnp.take(ys, dest_by_assign[:n_p * TOP_K], axis=0).reshape(n_p, TOP_K * D_MODEL)
    yg_s = jnp.take(ys, dest_by_assign[n_p * TOP_K:], axis=0).reshape(n_s, TOP_K * D_MODEL)

    nf = norm_final[None, :]
    y_p = _final(x1_p, yg_p, gate_p, nf, tm)
    y_s = _final(x1_s, yg_s, gate_s, nf, n_s)

    g, hd = NSA_GROUPS, HEAD_DIM
    def from_t(a, heads, length):
        return jnp.transpose(a.reshape(bp, heads, 2, hd, length), (0, 4, 1, 2, 3))[None]

    new_win_p = from_t(kvwt_p[:, :, t - wbuf:], 2, wbuf)
    new_win_s = jnp.concatenate([state_win_kv[0].reshape(bs, wbuf, 256)[:, ts:], kvw_s.reshape(bs, ts, 256)], axis=1)
    return (y_p.reshape(bp, t, D_MODEL), y_s.reshape(bs, ts, D_MODEL),
            from_t(kvct_p, 2, t), kvc_s.reshape(1, bs, ts, 2, g, hd),
            from_t(kvst_p, 2, t), kvs_s.reshape(1, bs, ts, 2, g, hd),
            new_win_p, new_win_s.reshape(1, bs, wbuf, 2, g, hd),
            from_t(kdt_p, DIFF_HEADS, t), kd_s.reshape(1, bs, ts, DIFF_HEADS, 2, hd),
            vd4_p[None], vd_s.reshape(1, bs, ts, DIFF_HEADS, DIFF_DV))
```

```python
import functools
import math

import numpy as np
import jax
import jax.numpy as jnp
from jax import lax
from jax.experimental import pallas as pl
from jax.experimental.pallas import tpu as pltpu

D_MODEL = 1024
HEAD_DIM = 64
NSA_HEADS = 8
NSA_GROUPS = 2
NSA_REP = 4
CMP_BLOCK = 32
CMP_STRIDE = 16
SEL_BLOCK = 64
N_SEL = 16
WINDOW = 512
DIFF_HEADS = 4
DIFF_DV = 128
ROT_DIM = 16
ROPE_THETA = 500000.0
N_EXPERTS = 32
TOP_K = 4
D_FF = 1024
SWIGLU_LIMIT = 7.0
SWIGLU_ALPHA = 1.702
RMS_EPS = 1e-5
FORCE_BONUS = 1e4
SCALE = HEAD_DIM ** -0.5
LANES = 128

BF = jnp.bfloat16
F32 = jnp.float32
I32 = jnp.int32
NEG = -1e30
VMEM_LIMIT = 56 * 1024 * 1024

OFF_Q, OFF_KVC, OFF_KVS, OFF_KVW, OFF_QD, OFF_KD, OFF_VD, OFF_GX = 0, 512, 768, 1024, 1280, 1792, 2304, 2816
IN_COLS_K = 4352
TOFF_Q, TOFF_KVC, TOFF_QD, TOFF_VD, TOFF_GX = 0, 512, 768, 1280, 1792
TIN_COLS = 3328
TIN_ROWS = 1280


def _dot(a, b):
    return jnp.dot(a, b, preferred_element_type=F32)


def _dot_nt(a, b):
    return lax.dot_general(a, b, (((1,), (1,)), ((), ())), preferred_element_type=F32)


def _dot_hi(a, b):
    return jnp.dot(a, b, preferred_element_type=F32, precision=lax.Precision.HIGHEST)


def _params(sem):
    return pltpu.CompilerParams(dimension_semantics=sem, vmem_limit_bytes=VMEM_LIMIT)


def _lane_iota(rows):
    return lax.broadcasted_iota(I32, (rows, LANES), 1)


def _pair(lo, a, b, g):
    if g == 0:
        return jnp.where(lo, a, pltpu.roll(b, 64, 1))
    return jnp.where(lo, pltpu.roll(a, 64, 1), b)


def _inproj_body(x_ref, nw_ref, w_ref, cos_ref, sin_ref, e_ref,
                 qraw_ref, qrot_ref, kvc_ref, kvs_ref, kvw_ref, ksa_ref, kwa_ref, vsb_ref, vwb_ref,
                 qd_ref, kd_ref, kdb_ref, vd_ref, vdb_ref, gx_ref):
    x = x_ref[...]
    tm = x.shape[0]
    xn = (x * lax.rsqrt(jnp.mean(x * x, axis=-1, keepdims=True) + RMS_EPS) * nw_ref[...]).astype(BF)
    cos = cos_ref[...]
    sin = sin_ref[...]
    epat = e_ref[...]
    lane = _lane_iota(tm)
    lo = lane < 64
    first8 = (lane & 63) < 8

    def mm(off):
        return _dot(xn, w_ref[:, off:off + LANES])

    def rope(y):
        sw = jnp.where(first8, pltpu.roll(y, LANES - 8, 1), pltpu.roll(y, 8, 1))
        return y * cos + sw * sin

    for c in range(4):
        y = mm(OFF_Q + c * LANES) * SCALE
        yr = rope(y)
        ys = pltpu.roll(y, 64, 1)
        yrs = pltpu.roll(yr, 64, 1)
        qraw_ref[:, (2 * c) * LANES:(2 * c + 1) * LANES] = jnp.where(lo, y, 0.0).astype(BF)
        qraw_ref[:, (2 * c + 1) * LANES:(2 * c + 2) * LANES] = jnp.where(lo, ys, 0.0).astype(BF)
        qrot_ref[:, (2 * c) * LANES:(2 * c + 1) * LANES] = jnp.where(lo, yr, 0.0).astype(BF)
        qrot_ref[:, (2 * c + 1) * LANES:(2 * c + 2) * LANES] = jnp.where(lo, yrs, 0.0).astype(BF)

    for c in range(2):
        kvc_ref[:, c * LANES:(c + 1) * LANES] = mm(OFF_KVC + c * LANES)

    for off, kv_ref, ka_ref, vb_ref in ((OFF_KVS, kvs_ref, ksa_ref, vsb_ref), (OFF_KVW, kvw_ref, kwa_ref, vwb_ref)):
        kr = rope(mm(off))
        v = mm(off + LANES)
        kv_ref[:, 0:LANES] = kr
        kv_ref[:, LANES:2 * LANES] = v
        ka_ref[:, 0:LANES] = jnp.where(lo, kr, epat).astype(BF)
        ka_ref[:, LANES:2 * LANES] = jnp.where(lo, pltpu.roll(kr, 64, 1), epat).astype(BF)
        vb_ref[...] = v.astype(BF)

    for c in range(4):
        sl = slice(c * LANES, (c + 1) * LANES)
        qd_ref[:, sl] = rope(mm(OFF_QD + c * LANES) * SCALE).astype(BF)
        kr = rope(mm(OFF_KD + c * LANES))
        kd_ref[:, sl] = kr
        kdb_ref[:, sl] = kr.astype(BF)
        v = mm(OFF_VD + c * LANES)
        vd_ref[:, sl] = v
        vdb_ref[:, sl] = v.astype(BF)

    for c in range(12):
        gx_ref[:, c * LANES:(c + 1) * LANES] = jax.nn.sigmoid(mm(OFF_GX + c * LANES))


def _inproj(x2d, norm_w, w_bf, cos_t, sin_t, e_t, tm):
    n = x2d.shape[0]
    nt = cos_t.shape[0] // tm
    row = lambda i: (i, 0)
    tab = lambda i: (i % nt, 0)
    const = lambda i: (0, 0)
    outs = [(1024, BF), (1024, BF), (256, F32), (256, F32), (256, F32), (256, BF), (256, BF), (128, BF), (128, BF),
            (512, BF), (512, F32), (512, BF), (512, F32), (512, BF), (1536, F32)]
    return pl.pallas_call(
        _inproj_body,
        grid=(n // tm,),
        in_specs=[pl.BlockSpec((tm, D_MODEL), row), pl.BlockSpec((1, D_MODEL), const),
                  pl.BlockSpec((D_MODEL, IN_COLS_K), const),
                  pl.BlockSpec((tm, LANES), tab), pl.BlockSpec((tm, LANES), tab), pl.BlockSpec((tm, LANES), tab)],
        out_specs=[pl.BlockSpec((tm, w), row) for w, _ in outs],
        out_shape=[jax.ShapeDtypeStruct((n, w), dt) for w, dt in outs],
        compiler_params=_params(("parallel",)),
        name="inproj",
    )(x2d, norm_w, w_bf, cos_t, sin_t, e_t)


def _inproj_t_body(x_ref, nw_ref, w_ref, wt_ref, cos_ref, sin_ref, cost_ref, sint_ref,
                   qraw_ref, qrot_ref, kvc_ref, kvct_ref, kvst_ref, kvwt_ref, ksat_ref, kwat_ref, vst_ref, vwt_ref,
                   qd_ref, kdt_ref, kdtb_ref, vd4_ref, vdb_ref, gx_ref):
    x = x_ref[...]
    tm = x.shape[0]
    xn = (x * lax.rsqrt(jnp.mean(x * x, axis=-1, keepdims=True) + RMS_EPS) * nw_ref[...]).astype(BF)
    cos = cos_ref[...]
    sin = sin_ref[...]
    cos_t = cost_ref[...]
    sin_t = sint_ref[...]
    lane = _lane_iota(tm)
    lo = lane < 64
    first8 = (lane & 63) < 8

    def mm(off):
        return _dot(xn, w_ref[:, off:off + LANES])

    def mm_t(off, n):
        return _dot_nt(wt_ref[off:off + n, :], xn)

    def rope(y):
        sw = jnp.where(first8, pltpu.roll(y, LANES - 8, 1), pltpu.roll(y, 8, 1))
        return y * cos + sw * sin

    def rope_t(y):
        parts = []
        for hb in range(0, y.shape[0], HEAD_DIM):
            x1 = y[hb:hb + 8]
            x2 = y[hb + 8:hb + 16]
            parts += [x1 * cos_t - x2 * sin_t, x1 * sin_t + x2 * cos_t, y[hb + 16:hb + HEAD_DIM]]
        return jnp.concatenate(parts, axis=0)

    for c in range(4):
        y = mm(TOFF_Q + c * LANES) * SCALE
        yr = rope(y)
        ys = pltpu.roll(y, 64, 1)
        yrs = pltpu.roll(yr, 64, 1)
        qraw_ref[:, (2 * c) * LANES:(2 * c + 1) * LANES] = jnp.where(lo, y, 0.0).astype(BF)
        qraw_ref[:, (2 * c + 1) * LANES:(2 * c + 2) * LANES] = jnp.where(lo, ys, 0.0).astype(BF)
        qrot_ref[:, (2 * c) * LANES:(2 * c + 1) * LANES] = jnp.where(lo, yr, 0.0).astype(BF)
        qrot_ref[:, (2 * c + 1) * LANES:(2 * c + 2) * LANES] = jnp.where(lo, yrs, 0.0).astype(BF)

    for c in range(2):
        kvc_ref[:, c * LANES:(c + 1) * LANES] = mm(TOFF_KVC + c * LANES)
    kvct_ref[0] = mm_t(0, 256)

    tpos = pl.program_id(1) * tm + lax.broadcasted_iota(I32, (32, tm), 1)
    e_t = jnp.where(jnp.right_shift(tpos, 6) == lax.broadcasted_iota(I32, (32, tm), 0), 1.0, 0.0)
    z_t = jnp.zeros((32, tm), F32)
    for off, kvt_ref, kat_ref, vt_ref in ((256, kvst_ref, ksat_ref, vst_ref), (512, kvwt_ref, kwat_ref, vwt_ref)):
        y = mm_t(off, 256)
        kr = rope_t(y[0:128])
        v = y[128:256]
        kvt_ref[0, 0:128, :] = kr
        kvt_ref[0, 128:256, :] = v
        for g in range(NSA_GROUPS):
            kat_ref[0, g] = jnp.concatenate([kr[g * 64:(g + 1) * 64], e_t, z_t], axis=0).astype(BF)
        vt_ref[0] = v.astype(BF)

    kd = rope_t(mm_t(768, 512))
    kdt_ref[0] = kd
    kdtb_ref[0] = kd.astype(BF)
    for c in range(4):
        sl = slice(c * LANES, (c + 1) * LANES)
        qd_ref[:, sl] = rope(mm(TOFF_QD + c * LANES) * SCALE).astype(BF)
        v = mm(TOFF_VD + c * LANES)
        vd4_ref[0, :, c, :] = v
        vdb_ref[:, sl] = v.astype(BF)

    for c in range(12):
        gx_ref[:, c * LANES:(c + 1) * LANES] = jax.nn.sigmoid(mm(TOFF_GX + c * LANES))


def _inproj_t(x2d, norm_w, w_bf, wt_bf, cos_t, sin_t, cos_tt, sin_tt, nbat, t, tm):
    n = x2d.shape[0]
    nt = t // tm
    row = lambda b, i: (b * nt + i, 0)
    tab = lambda b, i: (i, 0)
    tab_t = lambda b, i: (0, i)
    const = lambda b, i: (0, 0)
    tr = lambda b, i: (b, 0, i)
    tr4 = lambda b, i: (b, 0, 0, i)
    rm = lambda w, dt: (pl.BlockSpec((tm, w), row), jax.ShapeDtypeStruct((n, w), dt))
    tp = lambda r, dt: (pl.BlockSpec((1, r, tm), tr), jax.ShapeDtypeStruct((nbat, r, t), dt))
    aug = (pl.BlockSpec((1, NSA_GROUPS, LANES, tm), tr4), jax.ShapeDtypeStruct((nbat, NSA_GROUPS, LANES, t), BF))
    vd4 = (pl.BlockSpec((1, tm, DIFF_HEADS, DIFF_DV), lambda b, i: (b, i, 0, 0)),
           jax.ShapeDtypeStruct((nbat, t, DIFF_HEADS, DIFF_DV), F32))
    outs = [rm(1024, BF), rm(1024, BF), rm(256, F32), tp(256, F32), tp(256, F32), tp(256, F32), aug, aug,
            tp(128, BF), tp(128, BF), rm(512, BF), tp(512, F32), tp(512, BF), vd4, rm(512, BF), rm(1536, F32)]
    return pl.pallas_call(
        _inproj_t_body,
        grid=(nbat, nt),
        in_specs=[pl.BlockSpec((tm, D_MODEL), row), pl.BlockSpec((1, D_MODEL), const),
                  pl.BlockSpec((D_MODEL, TIN_COLS), const), pl.BlockSpec((TIN_ROWS, D_MODEL), const),
                  pl.BlockSpec((tm, LANES), tab), pl.BlockSpec((tm, LANES), tab),
                  pl.BlockSpec((8, tm), tab_t), pl.BlockSpec((8, tm), tab_t)],
        out_specs=[o[0] for o in outs],
        out_shape=[o[1] for o in outs],
        compiler_params=_params(("parallel", "parallel")),
        name="inproj_t",
    )(x2d, norm_w, w_bf, wt_bf, cos_t, sin_t, cos_tt, sin_tt)


def _compress_body(pt_ref, *refs, n_pages, rp, rt):
    page_refs = refs[:n_pages]
    pea_ref, peb_ref, w1a_ref, w1b_ref, w2_ref, out_ref, y_scr = refs[n_pages:]
    s = pl.program_id(1)
    for i in range(n_pages):
        x = page_refs[i][0]
        row0 = pl.multiple_of((s * n_pages + i) * rp, 8)
        for cg in range(4):
            ycg = jnp.concatenate(
                [x[:, l * 256 + cg * 64:l * 256 + cg * 64 + 64] for l in range(CMP_STRIDE)], axis=1)
            y_scr[cg, pl.ds(row0, rp), :] = ycg

    @pl.when(s == pl.num_programs(1) - 1)
    def _():
        for c in range(2):
            for g in range(2):
                cg = c * 2 + g
                y = y_scr[cg]
                za = _dot((y + pea_ref[c]).astype(BF), w1a_ref[c])
                zb = _dot((y + peb_ref[c]).astype(BF), w1b_ref[c])
                hid = jax.nn.gelu(za + pltpu.roll(zb, rt - 1, 0))
                out_ref[0, :, cg * LANES:(cg + 1) * LANES] = _dot(hid.astype(BF), w2_ref[c]).astype(BF)


def _compress(pool, page_table, pea, peb, w1a, w1b, w2p, n_pages):
    nb, pages = page_table.shape
    rp = pool.shape[1]
    rt = pages * rp
    steps = pages // n_pages
    page_specs = [pl.BlockSpec((1, rp, 4096), functools.partial(
        lambda b, s, pt, i: (pt[b, s * n_pages + i], 0, 0), i=i)) for i in range(n_pages)]
    c3 = lambda b, s, pt: (0, 0, 0)
    return pl.pallas_call(
        functools.partial(_compress_body, n_pages=n_pages, rp=rp, rt=rt),
        grid_spec=pltpu.PrefetchScalarGridSpec(
            num_scalar_prefetch=1,
            grid=(nb, steps),
            in_specs=page_specs + [pl.BlockSpec((2, 1, 1024), c3), pl.BlockSpec((2, 1, 1024), c3),
                                   pl.BlockSpec((2, 1024, LANES), c3), pl.BlockSpec((2, 1024, LANES), c3),
                                   pl.BlockSpec((2, LANES, LANES), c3)],
            out_specs=pl.BlockSpec((1, rt, 512), lambda b, s, pt: (b, 0, 0)),
            scratch_shapes=[pltpu.VMEM((4, rt, 1024), F32)]),
        out_shape=jax.ShapeDtypeStruct((nb, rt, 512), BF),
        compiler_params=_params(("parallel", "arbitrary")),
        name="compress",
    )(page_table, *([pool] * n_pages), pea, peb, w1a, w1b, w2p)


def _cmp_body(q_ref, kvc_ref, mimp_ref, ocmp_ref, selb_ref, *, tq, rt, nb, nbp, pos0):
    i = pl.program_id(1)
    pos_n = pos0 + i * tq + lax.broadcasted_iota(I32, (tq, rt), 0)
    n_i = lax.broadcasted_iota(I32, (tq, rt), 1)
    vis = (n_i * CMP_STRIDE + (CMP_BLOCK - 1)) <= pos_n
    lo = _lane_iota(tq) < 64
    pos_b = pos0 + i * tq + lax.broadcasted_iota(I32, (tq, nbp), 0)
    jb = lax.broadcasted_iota(I32, (tq, nbp), 1)
    cur = jnp.right_shift(pos_b, 6)
    valid = (jb * SEL_BLOCK <= pos_b) & (jb < nb)
    forced = (jb == 0) | (jb == cur) | (jb == cur - 1)
    mimp = mimp_ref[...]
    for g in range(NSA_GROUPS):
        kc = kvc_ref[0, :, g * LANES:(g + 1) * LANES]
        vc = kvc_ref[0, :, (2 + g) * LANES:(3 + g) * LANES]
        pg = jnp.zeros((tq, rt), F32)
        og = []
        for r in range(NSA_REP):
            h = g * NSA_REP + r
            q = q_ref[0, :, h * LANES:(h + 1) * LANES]
            s = jnp.where(vis, _dot_nt(q, kc), -jnp.inf)
            m = jnp.max(s, axis=-1, keepdims=True)
            m = jnp.where(m == -jnp.inf, 0.0, m)
            e = jnp.exp(s - m)
            d = jnp.sum(e, axis=-1, keepdims=True)
            p = e / jnp.where(d > 0, d, 1.0)
            og.append(_dot(p.astype(BF), vc))
            pg = pg + p
        imp = _dot_hi(pg, mimp)
        score = jnp.where(valid, jnp.where(forced, imp + FORCE_BONUS, imp), -jnp.inf)
        rank = jnp.zeros((tq, nbp), I32)
        for jj in range(nb):
            cj = score[:, jj:jj + 1]
            beats = (cj > score) | ((cj == score) & (jb > jj))
            rank = rank + beats.astype(I32)
        sel = (rank < N_SEL) & valid
        selb_ref[0, :, g * nbp:(g + 1) * nbp] = jnp.where(sel, 0.0, NEG)
        for c2 in range(2):
            ocmp_ref[0, :, (g * 2 + c2) * LANES:(g * 2 + c2 + 1) * LANES] = _pair(lo, og[2 * c2], og[2 * c2 + 1], 0)


def _cmp_attend(qraw3, kvc, mimp, nq, nb, nbp, pos0):
    nbat = kvc.shape[0]
    _, tq, _ = qraw3.shape
    rt = kvc.shape[1]
    return pl.pallas_call(
        functools.partial(_cmp_body, tq=tq, rt=rt, nb=nb, nbp=nbp, pos0=pos0),
        grid=(nbat, nq),
        in_specs=[pl.BlockSpec((1, tq, 1024), lambda b, i: (b * nq + i, 0, 0)),
                  pl.BlockSpec((1, rt, 512), lambda b, i: (b, 0, 0)),
                  pl.BlockSpec((rt, nbp), lambda b, i: (0, 0))],
        out_specs=[pl.BlockSpec((1, tq, 512), lambda b, i: (b * nq + i, 0, 0)),
                   pl.BlockSpec((1, tq, 2 * nbp), lambda b, i: (b * nq + i, 0, 0))],
        out_shape=[jax.ShapeDtypeStruct((nbat * nq, tq, 512), F32),
                   jax.ShapeDtypeStruct((nbat * nq, tq, 2 * nbp), F32)],
        compiler_params=_params(("parallel", "parallel")),
        name="cmp_attend",
    )(qraw3, kvc, mimp)


def _online(s, v, m_s, l_s, acc, idx, v_transposed=False):
    m_old = m_s[idx]
    m_new = jnp.maximum(m_old, jnp.max(s, axis=-1, keepdims=True))
    alpha = jnp.exp(m_old - m_new)
    p = jnp.exp(s - m_new)
    l_s[idx] = alpha * l_s[idx] + jnp.sum(p, axis=-1, keepdims=True)
    pv = _dot_nt(p.astype(BF), v) if v_transposed else _dot(p.astype(BF), v)
    acc[idx] = alpha * acc[idx] + pv
    m_s[idx] = m_new


def _nsa_flash_body(*refs, mode, tq, tk, nk):
    if mode == "sel":
        q_ref, selb_ref, k_ref, v_ref, o_ref, qs, m_s, l_s, acc = refs
    else:
        q_ref, k_ref, v_ref, o_ref, qs, m_s, l_s, acc = refs
    i = pl.program_id(1)
    j = pl.program_id(2)
    q0 = i * tq
    rows = NSA_REP * tq
    lane = _lane_iota(tq)
    lo = lane < 64

    @pl.when(j == 0)
    def _():
        m_s[...] = jnp.full(m_s.shape, NEG, F32)
        l_s[...] = jnp.zeros(l_s.shape, F32)
        acc[...] = jnp.zeros(acc.shape, F32)
        for g in range(NSA_GROUPS):
            if mode == "sel":
                sb = pltpu.roll(selb_ref[0, :, g * LANES:(g + 1) * LANES], 64, 1)
                sb = jnp.where((lane >= 64) & (lane < 96), sb, 0.0)
            for r in range(NSA_REP):
                h = g * NSA_REP + r
                q = q_ref[0, :, h * LANES:(h + 1) * LANES]
                if mode == "sel":
                    q = (q.astype(F32) + sb).astype(BF)
                qs[g, r * tq:(r + 1) * tq, :] = q

    if mode == "sel":
        jt = j
        active = j <= (q0 + tq - 1) // tk
        need_mask = j * tk + tk - 1 > q0
    else:
        jt = i - (nk - 1) + j
        active = jt >= 0
        need_mask = (j == 0) | (j == nk - 1)

    def step(masked):
        for g in range(NSA_GROUPS):
            s = _dot(qs[g], k_ref[0, g])
            if masked:
                rpos = q0 + (lax.broadcasted_iota(I32, (rows, tk), 0) & (tq - 1))
                kpos = jt * tk + lax.broadcasted_iota(I32, (rows, tk), 1)
                vis = kpos <= rpos
                if mode == "win":
                    vis = vis & (kpos > rpos - WINDOW)
                s = jnp.where(vis, s, NEG)
            _online(s, v_ref[0], m_s, l_s, acc, g, v_transposed=True)

    @pl.when(active & need_mask)
    def _():
        step(True)

    @pl.when(active & jnp.logical_not(need_mask))
    def _():
        step(False)

    @pl.when(j == nk - 1)
    def _():
        for g in range(NSA_GROUPS):
            on = acc[g] / l_s[g]
            for c2 in range(2):
                a = on[(2 * c2) * tq:(2 * c2 + 1) * tq]
                b = on[(2 * c2 + 1) * tq:(2 * c2 + 2) * tq]
                o_ref[0, :, (g * 2 + c2) * LANES:(g * 2 + c2 + 1) * LANES] = _pair(lo, a, b, g)


def _nsa_flash(mode, q3, selb3, k2, v2, nbat, t, tq, tk):
    nq = t // tq
    nkv = t // tk
    rows = NSA_REP * tq
    if mode == "sel":
        nk = nkv
        jmap = lambda i, j: jnp.minimum(j, (i * tq + tq - 1) // tk)
    else:
        nk = WINDOW // tk + 1
        jmap = lambda i, j: jnp.maximum(i - (nk - 1) + j, 0)
    qmap = lambda b, i, j: (b * nq + i, 0, 0)
    in_specs = [pl.BlockSpec((1, tq, 1024), qmap)]
    args = [q3]
    if mode == "sel":
        in_specs.append(pl.BlockSpec((1, tq, 2 * LANES), qmap))
        args.append(selb3)
    in_specs += [pl.BlockSpec((1, NSA_GROUPS, LANES, tk), lambda b, i, j: (b, 0, 0, jmap(i, j))),
                 pl.BlockSpec((1, LANES, tk), lambda b, i, j: (b, 0, jmap(i, j)))]
    args += [k2, v2]
    return pl.pallas_call(
        functools.partial(_nsa_flash_body, mode=mode, tq=tq, tk=tk, nk=nk),
        grid=(nbat, nq, nk),
        in_specs=in_specs,
        out_specs=pl.BlockSpec((1, tq, 512), qmap),
        out_shape=jax.ShapeDtypeStruct((nbat * nq, tq, 512), F32),
        scratch_shapes=[pltpu.VMEM((NSA_GROUPS, rows, LANES), BF), pltpu.VMEM((NSA_GROUPS, rows, 1), F32),
                        pltpu.VMEM((NSA_GROUPS, rows, 1), F32), pltpu.VMEM((NSA_GROUPS, rows, LANES), F32)],
        compiler_params=_params(("parallel", "parallel", "arbitrary")),
        name="nsa_flash_" + mode,
    )(*args)


def _diff_lambda(dl, lambda_init):
    a = jnp.sum(dl[0:1] * dl[1:2], axis=1, keepdims=True)
    b = jnp.sum(dl[2:3] * dl[3:4], axis=1, keepdims=True)
    return jnp.exp(a) - jnp.exp(b) + lambda_init


def _diff_finish(a0, a1, lam, subln, lambda_init):
    o = a0 - lam * a1
    o = o * lax.rsqrt(jnp.mean(o * o, axis=-1, keepdims=True) + RMS_EPS) * subln
    return o * (1.0 - lambda_init)


def _diff_flash_body(q_ref, k_ref, v_ref, dl_ref, sub_ref, o_ref, qs, m_s, l_s, acc, *, tq, tk, nk, lambda_init):
    i = pl.program_id(1)
    j = pl.program_id(2)
    q0 = i * tq
    rows = 2 * tq
    lo = _lane_iota(tq) < 64

    @pl.when(j == 0)
    def _():
        m_s[...] = jnp.full(m_s.shape, NEG, F32)
        l_s[...] = jnp.zeros(l_s.shape, F32)
        acc[...] = jnp.zeros(acc.shape, F32)
        for h in range(DIFF_HEADS):
            q = q_ref[0, :, h * LANES:(h + 1) * LANES].astype(F32)
            qs[h, 0:tq, :] = jnp.where(lo, q, 0.0).astype(BF)
            qs[h, tq:2 * tq, :] = jnp.where(lo, 0.0, q).astype(BF)

    active = j <= (q0 + tq - 1) // tk
    need_mask = j * tk + tk - 1 > q0

    def step(masked):
        for h in range(DIFF_HEADS):
            s = _dot(qs[h], k_ref[0, h * LANES:(h + 1) * LANES, :])
            if masked:
                rpos = q0 + (lax.broadcasted_iota(I32, (rows, tk), 0) & (tq - 1))
                kpos = j * tk + lax.broadcasted_iota(I32, (rows, tk), 1)
                s = jnp.where(kpos <= rpos, s, NEG)
            _online(s, v_ref[:, h * LANES:(h + 1) * LANES], m_s, l_s, acc, h)

    @pl.when(active & need_mask)
    def _():
        step(True)

    @pl.when(active & jnp.logical_not(need_mask))
    def _():
        step(False)

    @pl.when(j == nk - 1)
    def _():
        lam = _diff_lambda(dl_ref[...], lambda_init)
        for h in range(DIFF_HEADS):
            on = acc[h] / l_s[h]
            o_ref[0, :, h * LANES:(h + 1) * LANES] = _diff_finish(on[0:tq], on[tq:2 * tq], lam, sub_ref[...], lambda_init)


def _diff_flash(q3, k2, v2, dl, subln, nbat, t, tq, tk, lambda_init):
    nq = t // tq
    nk = t // tk
    qmap = lambda b, i, j: (b * nq + i, 0, 0)
    jmap = lambda i, j: jnp.minimum(j, (i * tq + tq - 1) // tk)
    kmap = lambda b, i, j: (b, 0, jmap(i, j))
    vmap = lambda b, i, j: (b * nk + jmap(i, j), 0)
    const = lambda b, i, j: (0, 0)
    rows = 2 * tq
    return pl.pallas_call(
        functools.partial(_diff_flash_body, tq=tq, tk=tk, nk=nk, lambda_init=lambda_init),
        grid=(nbat, nq, nk),
        in_specs=[pl.BlockSpec((1, tq, 512), qmap), pl.BlockSpec((1, 512, tk), kmap), pl.BlockSpec((tk, 512), vmap),
                  pl.BlockSpec((4, HEAD_DIM), const), pl.BlockSpec((1, DIFF_DV), const)],
        out_specs=pl.BlockSpec((1, tq, 512), qmap),
        out_shape=jax.ShapeDtypeStruct((nbat * nq, tq, 512), F32),
        scratch_shapes=[pltpu.VMEM((DIFF_HEADS, rows, LANES), BF), pltpu.VMEM((DIFF_HEADS, rows, 1), F32),
                        pltpu.VMEM((DIFF_HEADS, rows, 1), F32), pltpu.VMEM((DIFF_HEADS, rows, LANES), F32)],
        compiler_params=_params(("parallel", "parallel", "arbitrary")),
        name="diff_flash",
    )(q3, k2, v2, dl, subln)


def _paged_body(pt_ref, *refs, mode, n_pages, past_len, lambda_init):
    q_ref = refs[0]
    k = 1
    if mode == "sel":
        selb_ref = refs[k]
        k += 1
    kpages = refs[k:k + n_pages]
    k += n_pages
    if mode == "diff":
        vpages = refs[k:k + n_pages]
        k += n_pages
    newk_ref = refs[k]
    k += 1
    if mode == "diff":
        newv_ref, dl_ref, sub_ref = refs[k:k + 3]
        k += 3
    o_ref = refs[k]
    k += 1
    if mode == "sel":
        qbd, bias, m_s, l_s, acc = refs[k:]
    else:
        qbd, m_s, l_s, acc = refs[k:]
    kw = WINDOW if mode == "win" else LANES
    st = pl.program_id(1)
    nq = 8
    rows = 64
    lane8 = _lane_iota(nq)
    lo8 = lane8 < 64
    t_row = lax.broadcasted_iota(I32, (rows, LANES), 0) & (nq - 1)
    col = lax.broadcasted_iota(I32, (rows, LANES), 1)

    def update(s, pv):
        m_old = m_s[0]
        m_new = jnp.maximum(m_old, jnp.max(s, axis=-1, keepdims=True))
        alpha = jnp.exp(m_old - m_new)
        p = jnp.exp(s - m_new)
        l_s[0] = alpha * l_s[0] + jnp.sum(p, axis=-1, keepdims=True)
        acc[0] = alpha * acc[0] + pv(p.astype(BF))
        m_s[0] = m_new

    def per_head(p, v_of_head):
        return jnp.concatenate([_dot(p[h * 16:(h + 1) * 16], v_of_head(h)) for h in range(DIFF_HEADS)], axis=0)

    @pl.when(st == 0)
    def _():
        m_s[0] = jnp.full((rows, 1), NEG, F32)
        l_s[0] = jnp.zeros((rows, 1), F32)
        acc[0] = jnp.zeros(acc.shape[1:], F32)
        if mode == "diff":
            qbd[...] = jnp.zeros(qbd.shape, F32)
            for h in range(DIFF_HEADS):
                q = q_ref[0, :, h * LANES:(h + 1) * LANES].astype(F32)
                qbd[(2 * h) * nq:(2 * h + 1) * nq, h * LANES:(h + 1) * LANES] = jnp.where(lo8, q, 0.0)
                qbd[(2 * h + 1) * nq:(2 * h + 2) * nq, h * LANES:(h + 1) * LANES] = jnp.where(lo8, 0.0, q)
        else:
            for h in range(NSA_HEADS):
                g = h // NSA_REP
                q = q_ref[0, :, h * LANES:(h + 1) * LANES].astype(F32)
                qbd[h * nq:(h + 1) * nq, :] = q if g == 0 else pltpu.roll(q, 64, 1)
                if mode == "sel":
                    bias[h * nq:(h + 1) * nq, :] = selb_ref[0, :, g * 256:(g + 1) * 256]
        kn = newk_ref[0]
        if mode == "diff":
            kk, vv = kn, newv_ref[0]
        else:
            kk, vv = kn[:, 0:LANES], kn[:, LANES:2 * LANES]
        kpad = jnp.concatenate([kk, jnp.zeros((LANES - nq, kk.shape[1]), F32)], axis=0).astype(BF)
        vpad = jnp.concatenate([vv, jnp.zeros((LANES - nq, vv.shape[1]), F32)], axis=0).astype(BF)
        s = _dot_nt(qbd[...].astype(BF), kpad)
        if mode == "sel":
            jnew = past_len // SEL_BLOCK
            s = s + bias[:, jnew:jnew + 1]
        s = jnp.where((col < nq) & (col <= t_row), s, NEG)
        if mode == "diff":
            update(s, lambda p: per_head(p, lambda h: vpad[:, h * LANES:(h + 1) * LANES]))
        else:
            update(s, lambda p: _dot(p, vpad))

    qb = qbd[...].astype(BF)
    for i in range(n_pages):
        pg = st * n_pages + i
        if mode == "diff":
            kt = kpages[i][0].reshape(4 * LANES, LANES).astype(BF)
        else:
            kt = kpages[i][0, 0].reshape(LANES, kw).astype(BF)
        s = _dot(qb, kt)
        if mode == "sel":
            krow = lax.broadcasted_iota(I32, (LANES, 256), 0)
            jbi = lax.broadcasted_iota(I32, (LANES, 256), 1)
            ep = jnp.where(jbi == 2 * pg + jnp.right_shift(krow, 6), 1.0, 0.0).astype(BF)
            s = s + _dot_nt(bias[...].astype(BF), ep)
        if mode == "win":
            t_w = lax.broadcasted_iota(I32, (rows, kw), 0) & (nq - 1)
            s = jnp.where(lax.broadcasted_iota(I32, (rows, kw), 1) > t_w, s, NEG)
        if mode == "diff":
            vpage = vpages[i]
            update(s, lambda p: per_head(p, lambda h: vpage[0, :, h, :].astype(BF)))
        else:
            vt = kpages[i][0, 1].reshape(LANES, kw).astype(BF)
            update(s, lambda p: _dot_nt(p, vt))

    @pl.when(st == pl.num_programs(1) - 1)
    def _():
        on = acc[0] / l_s[0]
        if mode == "diff":
            lam = _diff_lambda(dl_ref[...], lambda_init)
            for h in range(DIFF_HEADS):
                a0 = on[h * 16:h * 16 + nq]
                a1 = on[h * 16 + nq:h * 16 + 2 * nq]
                o_ref[0, :, h * LANES:(h + 1) * LANES] = _diff_finish(a0, a1, lam, sub_ref[...], lambda_init)
        else:
            for c in range(4):
                a = on[(2 * c) * nq:(2 * c + 1) * nq]
                b = on[(2 * c + 1) * nq:(2 * c + 2) * nq]
                o_ref[0, :, c * LANES:(c + 1) * LANES] = _pair(lo8, a, b, c // 2)


def _paged_attend(mode, q3, selb3, pool_k, pool_v, page_table, newk3, newv3, dl, subln, n_pages, past_len, lambda_init):
    nbat, pages = page_table.shape
    steps = pages // n_pages
    wq = q3.shape[2]
    wk = newk3.shape[2]
    bmap = lambda b, s, pt: (b, 0, 0)
    c2 = lambda b, s, pt: (0, 0)

    def pmap(i, rank):
        return lambda b, s, pt: (pt[b, s * n_pages + i],) + (0,) * (rank - 1)

    in_specs = [pl.BlockSpec((1, 8, wq), bmap)]
    args = [q3]
    if mode == "sel":
        in_specs.append(pl.BlockSpec((1, 8, 512), bmap))
        args.append(selb3)
    kblock = (1,) + pool_k.shape[1:]
    in_specs += [pl.BlockSpec(kblock, pmap(i, len(kblock))) for i in range(n_pages)]
    args += [pool_k] * n_pages
    if mode == "diff":
        in_specs += [pl.BlockSpec((1, LANES, DIFF_HEADS, DIFF_DV), pmap(i, 4)) for i in range(n_pages)]
        args += [pool_v] * n_pages
    in_specs.append(pl.BlockSpec((1, 8, wk), bmap))
    args.append(newk3)
    if mode == "diff":
        in_specs += [pl.BlockSpec((1, 8, 512), bmap), pl.BlockSpec((4, HEAD_DIM), c2), pl.BlockSpec((1, DIFF_DV), c2)]
        args += [newv3, dl, subln]
    ck = 512 if mode == "diff" else LANES
    scratch = [pltpu.VMEM((64, ck), F32)]
    if mode == "sel":
        scratch.append(pltpu.VMEM((64, 256), F32))
    scratch += [pltpu.VMEM((1, 64, 1), F32), pltpu.VMEM((1, 64, 1), F32), pltpu.VMEM((1, 64, LANES), F32)]
    return pl.pallas_call(
        functools.partial(_paged_body, mode=mode, n_pages=n_pages, past_len=past_len, lambda_init=lambda_init),
        grid_spec=pltpu.PrefetchScalarGridSpec(
            num_scalar_prefetch=1, grid=(nbat, steps), in_specs=in_specs,
            out_specs=pl.BlockSpec((1, 8, 512), bmap), scratch_shapes=scratch),
        out_shape=jax.ShapeDtypeStruct((nbat, 8, 512), F32),
        compiler_params=_params(("parallel", "arbitrary")),
        name="paged_" + mode,
    )(page_table, *args)


def _merge_body(x_ref, oc_ref, os_ref, ow_ref, od_ref, gx_ref, nmix_ref, nffn_ref, wg_ref, wbn_ref, wbd_ref, wo_ref,
                wr_ref, br_ref, x1_ref, h_ref, idx_ref, gate_ref):
    x = x_ref[...]
    tm = x.shape[0]
    xn = (x * lax.rsqrt(jnp.mean(x * x, axis=-1, keepdims=True) + RMS_EPS) * nmix_ref[...]).astype(BF)
    gx = gx_ref[...]
    o_nsa = gx[:, 0:512] * oc_ref[...] + gx[:, 512:1024] * os_ref[...] + gx[:, 1024:1536] * ow_ref[...]
    y_nsa = _dot(o_nsa.astype(BF), wbn_ref[...])
    y_diff = _dot(od_ref[...].astype(BF), wbd_ref[...])
    gates = jax.nn.sigmoid(_dot(xn, wg_ref[...]))
    mrg = gates[:, 0:D_MODEL] * y_nsa + gates[:, D_MODEL:2 * D_MODEL] * y_diff
    x1 = x + _dot(mrg.astype(BF), wo_ref[...])
    x1_ref[...] = x1
    h = x1 * lax.rsqrt(jnp.mean(x1 * x1, axis=-1, keepdims=True) + RMS_EPS) * nffn_ref[...]
    h_ref[...] = h
    logits = _dot_hi(h, wr_ref[...]) + br_ref[...]
    lane = _lane_iota(tm)
    vals, idxs = [], []
    for _ in range(TOP_K):
        m = jnp.max(logits, axis=-1, keepdims=True)
        idx = jnp.min(jnp.where(logits == m, lane, LANES), axis=-1, keepdims=True)
        vals.append(m)
        idxs.append(idx)
        logits = jnp.where(lane == idx, -jnp.inf, logits)
    es = [jnp.exp(v - vals[0]) for v in vals]
    den = es[0] + es[1] + es[2] + es[3]
    idx_out = jnp.zeros((tm, LANES), I32)
    gate_out = jnp.zeros((tm, LANES), F32)
    for k in range(TOP_K):
        idx_out = jnp.where(lane == k, idxs[k], idx_out)
        gate_out = jnp.where(lane == k, es[k] / den, gate_out)
    idx_ref[...] = idx_out
    gate_ref[...] = gate_out


def _merge(x2d, oc, osel, ow, od, gx, nmix, nffn, wg, wbn, wbd, wo, wr, br, tm):
    n = x2d.shape[0]
    row = lambda i: (i, 0)
    const = lambda i: (0, 0)
    return pl.pallas_call(
        _merge_body,
        grid=(n // tm,),
        in_specs=[pl.BlockSpec((tm, D_MODEL), row)] + [pl.BlockSpec((tm, 512), row)] * 4 +
                 [pl.BlockSpec((tm, 1536), row), pl.BlockSpec((1, D_MODEL), const), pl.BlockSpec((1, D_MODEL), const),
                  pl.BlockSpec((D_MODEL, 2 * D_MODEL), const), pl.BlockSpec((512, D_MODEL), const),
                  pl.BlockSpec((512, D_MODEL), const), pl.BlockSpec((D_MODEL, D_MODEL), const),
                  pl.BlockSpec((D_MODEL, LANES), const), pl.BlockSpec((1, LANES), const)],
        out_specs=[pl.BlockSpec((tm, D_MODEL), row), pl.BlockSpec((tm, D_MODEL), row),
                   pl.BlockSpec((tm, LANES), row), pl.BlockSpec((tm, LANES), row)],
        out_shape=[jax.ShapeDtypeStruct((n, D_MODEL), F32), jax.ShapeDtypeStruct((n, D_MODEL), F32),
                   jax.ShapeDtypeStruct((n, LANES), I32), jax.ShapeDtypeStruct((n, LANES), F32)],
        compiler_params=_params(("parallel",)),
        name="merge",
    )(x2d, oc, osel, ow, od, gx, nmix, nffn, wg, wbn, wbd, wo, wr, br)


def _moe_body(be_ref, nused_ref, src_ref, dst_ref, gate_ref, wgu_ref, bgu_ref, wd_ref, bd_ref, h_hbm, yg_hbm,
              wgu_s, wd_s, xbuf, obuf, gsem, ssem, *, tmoe, nblk):
    i = pl.program_id(0)
    n_used = nused_ref[0]
    slot = i % 2

    def gather(blk, sl):
        def body(r, c):
            t = src_ref[blk * tmoe + r]
            pltpu.make_async_copy(h_hbm.at[pl.ds(t, 1)], xbuf.at[sl, pl.ds(r, 1)], gsem.at[sl]).start()
            return c
        lax.fori_loop(0, tmoe, body, 0)

    def scatter(blk, sl):
        def body(r, c):
            d = dst_ref[blk * tmoe + r]
            pltpu.make_async_copy(obuf.at[sl, pl.ds(r, 1)], yg_hbm.at[pl.ds(d, 1)], ssem.at[sl]).start()
            return c
        lax.fori_loop(0, tmoe, body, 0)

    def wait_gather(sl):
        pltpu.make_async_copy(xbuf.at[sl], xbuf.at[sl], gsem.at[sl]).wait()

    def wait_scatter(sl):
        pltpu.make_async_copy(obuf.at[sl], obuf.at[sl], ssem.at[sl]).wait()

    @pl.when(i == 0)
    def _():
        obuf[0] = jnp.zeros((tmoe, D_MODEL), F32)
        spare0 = yg_hbm.shape[0] - 2 * tmoe
        fills = [pltpu.make_async_copy(obuf.at[0], yg_hbm.at[pl.ds(spare0 + k * tmoe, tmoe)], ssem.at[0])
                 for k in range(2)]
        for f in fills:
            f.start()
        for f in fills:
            f.wait()

    @pl.when((i == 0) & (n_used > 0))
    def _():
        gather(0, 0)

    e = be_ref[i]
    prev = be_ref[jnp.maximum(i - 1, 0)]

    @pl.when((i == 0) | (e != prev))
    def _():
        def cast_rows(c, carry):
            r0 = pl.multiple_of(c * LANES, LANES)
            wgu_s[pl.ds(r0, LANES), :] = wgu_ref[0, pl.ds(r0, LANES), :].astype(BF)
            wd_s[pl.ds(r0, LANES), :] = wd_ref[0, pl.ds(r0, LANES), :].astype(BF)
            return carry
        lax.fori_loop(0, D_MODEL // LANES, cast_rows, 0)

    @pl.when(i + 1 < n_used)
    def _():
        gather(i + 1, 1 - slot)

    @pl.when((i >= 2) & (i - 2 < n_used))
    def _():
        wait_scatter(slot)

    @pl.when(i < n_used)
    def _():
        wait_gather(slot)
        gu = _dot(xbuf[slot].astype(BF), wgu_s[...]) + bgu_ref[0]
        gate = jnp.minimum(gu[:, 0:D_FF], SWIGLU_LIMIT)
        up = jnp.clip(gu[:, D_FF:2 * D_FF], -SWIGLU_LIMIT, SWIGLU_LIMIT)
        act = (up + 1.0) * gate * jax.nn.sigmoid(SWIGLU_ALPHA * gate)
        obuf[slot] = (_dot(act.astype(BF), wd_s[...]) + bd_ref[0]) * gate_ref[...]
        scatter(i, slot)

    @pl.when(i == nblk - 1)
    def _():
        @pl.when(nblk - 2 < n_used)
        def _():
            wait_scatter((nblk - 2) % 2)

        @pl.when(nblk - 1 < n_used)
        def _():
            wait_scatter((nblk - 1) % 2)


def _moe(block_expert, n_used, src_row, dst_row, gate_rows, h_all, wgu, bgu, wd, bd, n_out_rows, tmoe):
    n_rows = src_row.shape[0]
    nblk = n_rows // tmoe
    emap = lambda i, be, nu, sr, ds: (be[i], 0, 0)
    return pl.pallas_call(
        functools.partial(_moe_body, tmoe=tmoe, nblk=nblk),
        grid_spec=pltpu.PrefetchScalarGridSpec(
            num_scalar_prefetch=4, grid=(nblk,),
            in_specs=[pl.BlockSpec((tmoe, 1), lambda i, be, nu, sr, ds: (i, 0)),
                      pl.BlockSpec((1, D_MODEL, 2 * D_FF), emap), pl.BlockSpec((1, 1, 2 * D_FF), emap),
                      pl.BlockSpec((1, D_FF, D_MODEL), emap), pl.BlockSpec((1, 1, D_MODEL), emap),
                      pl.BlockSpec(memory_space=pl.ANY)],
            out_specs=pl.BlockSpec(memory_space=pl.ANY),
            scratch_shapes=[pltpu.VMEM((D_MODEL, 2 * D_FF), BF), pltpu.VMEM((D_FF, D_MODEL), BF),
                            pltpu.VMEM((2, tmoe, D_MODEL), F32), pltpu.VMEM((2, tmoe, D_MODEL), F32),
                            pltpu.SemaphoreType.DMA((2,)), pltpu.SemaphoreType.DMA((2,))]),
        out_shape=jax.ShapeDtypeStruct((n_out_rows, D_MODEL), F32),
        compiler_params=_params(("arbitrary",)),
        name="moe_experts",
    )(block_expert, n_used, src_row, dst_row, gate_rows, wgu, bgu, wd, bd, h_all)


def _final_body(x1_ref, yg_ref, nf_ref, y_ref):
    x = x1_ref[...]
    for k in range(TOP_K):
        x = x + yg_ref[:, k * D_MODEL:(k + 1) * D_MODEL]
    y_ref[...] = x * lax.rsqrt(jnp.mean(x * x, axis=-1, keepdims=True) + RMS_EPS) * nf_ref[...]


def _final(x1, yg, nf, tm, block0):
    n = x1.shape[0]
    row = lambda i: (i, 0)
    return pl.pallas_call(
        _final_body,
        grid=(n // tm,),
        in_specs=[pl.BlockSpec((tm, D_MODEL), row), pl.BlockSpec((tm, TOP_K * D_MODEL), lambda i: (block0 + i, 0)),
                  pl.BlockSpec((1, D_MODEL), lambda i: (0, 0))],
        out_specs=pl.BlockSpec((tm, D_MODEL), row),
        out_shape=jax.ShapeDtypeStruct((n, D_MODEL), F32),
        compiler_params=_params(("parallel",)),
        name="final_norm",
    )(x1, yg, nf)


def _rope_tables(pos, nb_lanes):
    half = ROT_DIM // 2
    inv = ROPE_THETA ** (-jnp.arange(half, dtype=F32) / half)
    ang = pos.astype(F32)[:, None] * inv[None, :]
    cos, sin = jnp.cos(ang), jnp.sin(ang)
    t = pos.shape[0]
    c64 = jnp.concatenate([cos, cos, jnp.ones((t, HEAD_DIM - ROT_DIM), F32)], axis=1)
    s64 = jnp.concatenate([-sin, sin, jnp.zeros((t, HEAD_DIM - ROT_DIM), F32)], axis=1)
    lane = jnp.arange(LANES)[None, :]
    e = ((lane >= 64) & (lane < 64 + nb_lanes) & ((pos // SEL_BLOCK)[:, None] == lane - 64)).astype(F32)
    return jnp.tile(c64, (1, 2)), jnp.tile(s64, (1, 2)), e


def _imp_matrix(rt, n_cmp, nb, nbp):
    m = np.zeros((rt, nbp), np.float32)
    sub = SEL_BLOCK // CMP_STRIDE
    for n in range(n_cmp):
        for shift in range(CMP_BLOCK // CMP_STRIDE):
            j = (n + shift) // sub
            if j < nb:
                m[n, j] += 1.0
    return jnp.asarray(m)


def kernel(x_prompt, x_sample, cache_cmp_kv, cache_sel_kv, state_win_kv, cache_diff_k, cache_diff_v, page_table, norm_mix, w_in, cmp_pe, w_cmp1, w_cmp2, diff_lambda, diff_subln, w_br_nsa, w_br_diff, w_out, norm_ffn, w_router, b_router, w_gate_up, b_gate_up, w_down, b_down, norm_final):
    bp, t, _ = x_prompt.shape
    bs, ts, _ = x_sample.shape
    assert ts == 8 and t % 256 == 0 and t <= 2048 and w_in.shape[0] == 1
    n_pool, page = cache_cmp_kv.shape[1], cache_cmp_kv.shape[2]
    pages = page_table.shape[1]
    past_len = pages * page
    assert page == LANES and past_len % SEL_BLOCK == 0
    lambda_init = 0.8 - 0.6 * math.exp(-0.3 * 0)
    n_p, n_s = bp * t, bs * ts

    w = w_in[0]
    gcols = w[:, 1280:1304].reshape(D_MODEL, NSA_HEADS, 3).transpose(0, 2, 1)
    gexp = jnp.broadcast_to(gcols[..., None], (D_MODEL, 3, NSA_HEADS, HEAD_DIM)).reshape(D_MODEL, 1536)
    w_proj = jnp.concatenate([w[:, 0:1280], w[:, 1304:2840], gexp], axis=1).astype(BF)
    w_gates = w[:, 2840:4888].astype(BF)
    nmix = norm_mix[0][None, :]
    nffn = norm_ffn[0][None, :]
    pe = cmp_pe[0]
    pea = pe[:, :CMP_STRIDE].reshape(2, 1, 1024)
    peb = pe[:, CMP_STRIDE:].reshape(2, 1, 1024)
    w1a = w_cmp1[0][:, :1024].astype(BF)
    w1b = w_cmp1[0][:, 1024:].astype(BF)
    w2p = jnp.pad(w_cmp2[0], ((0, 0), (0, 0), (0, LANES - HEAD_DIM))).astype(BF)
    wbn = w_br_nsa[0].astype(BF)
    wbd = w_br_diff[0].astype(BF)
    wo = w_out[0].astype(BF)
    wr = jnp.pad(w_router[0], ((0, 0), (0, LANES - N_EXPERTS)))
    br = jnp.concatenate([b_router[0], jnp.full((LANES - N_EXPERTS,), NEG, F32)])[None, :]
    dl = diff_lambda[0]
    subln = diff_subln[0][None, :]

    pos_p = jnp.arange(t, dtype=I32)
    pos_s = jnp.tile(past_len + jnp.arange(ts, dtype=I32), bs)
    tm = 256
    w_rows = jnp.concatenate([w[:, 0:768], w[:, 1304:1816], w[:, 2328:2840], gexp], axis=1).astype(BF)
    w_cols = jnp.concatenate([w[:, 512:1280], w[:, 1816:2328]], axis=1).T.astype(BF)
    cos_p, sin_p, _ = _rope_tables(pos_p, 0)
    outs_p = _inproj_t(x_prompt.reshape(n_p, D_MODEL), nmix, w_rows, w_cols, cos_p, sin_p,
                       cos_p[:, 0:8].T, -sin_p[:, 0:8].T, bp, t, tm)
    outs_s = _inproj(x_sample.reshape(n_s, D_MODEL), nmix, w_proj, *_rope_tables(pos_s, 0), n_s)
    (qraw_p, qrot_p, kvc_p, kvct_p, kvst_p, kvwt_p, ksat_p, kwat_p, vst_p, vwt_p, qd_p, kdt_p, kdtb_p, vd4_p, vdb_p,
     gx_p) = outs_p
    (qraw_s, qrot_s, kvc_s, kvs_s, kvw_s, _, _, _, _, qd_s, kd_s, _, vd_s, _, gx_s) = outs_s

    rt_p = t // CMP_STRIDE
    n_cmp_p = (t - CMP_BLOCK) // CMP_STRIDE + 1
    nb_p = t // SEL_BLOCK
    kvcmp_p = _compress(kvc_p.reshape(bp, rt_p, 4096), jnp.arange(bp, dtype=I32)[:, None], pea, peb, w1a, w1b, w2p, 1)
    tq = 128
    nq = t // tq
    ocmp_p, selb_p = _cmp_attend(qraw_p.reshape(bp * nq, tq, 1024), kvcmp_p, _imp_matrix(rt_p, n_cmp_p, nb_p, LANES),
                                 nq, nb_p, LANES, 0)
    qrot3 = qrot_p.reshape(bp * nq, tq, 1024)
    osel_p = _nsa_flash("sel", qrot3, selb_p, ksat_p, vst_p, bp, t, tq, 256)
    owin_p = _nsa_flash("win", qrot3, None, kwat_p, vwt_p, bp, t, tq, tq)
    odiff_p = _diff_flash(qd_p.reshape(bp * nq, tq, 512), kdtb_p, vdb_p, dl, subln, bp, t, tq, 256, lambda_init)

    n_cmp_s = (past_len + ts - CMP_BLOCK) // CMP_STRIDE + 1
    rt_s = past_len // CMP_STRIDE
    assert n_cmp_s <= rt_s
    nb_s = (past_len + ts + SEL_BLOCK - 1) // SEL_BLOCK
    assert nb_s <= 256
    ppg = 8 if pages % 8 == 0 else 1
    kvcmp_s = _compress(cache_cmp_kv[0].reshape(n_pool, page // CMP_STRIDE, 4096), page_table, pea, peb, w1a, w1b,
                        w2p, ppg)
    ocmp_s, selb_s = _cmp_attend(qraw_s.reshape(bs, ts, 1024), kvcmp_s, _imp_matrix(rt_s, n_cmp_s, nb_s, 256),
                                 1, nb_s, 256, past_len)
    qrot_s3 = qrot_s.reshape(bs, ts, 1024)
    sel_t = jnp.transpose(cache_sel_kv[0], (0, 2, 3, 4, 1))
    win_t = jnp.transpose(state_win_kv[0], (0, 2, 3, 4, 1))
    dk_t = jnp.transpose(cache_diff_k[0], (0, 2, 3, 4, 1))
    osel_s = _paged_attend("sel", qrot_s3, selb_s, sel_t, None, page_table,
                           kvs_s.reshape(bs, ts, 256), None, None, None, ppg, past_len, lambda_init)
    wbuf = state_win_kv.shape[2]
    assert wbuf == WINDOW
    owin_s = _paged_attend("win", qrot_s3, None, win_t, None, jnp.arange(bs, dtype=I32)[:, None],
                           kvw_s.reshape(bs, ts, 256), None, None, None, 1, past_len, lambda_init)
    odiff_s = _paged_attend("diff", qd_s.reshape(bs, ts, 512), None, dk_t, cache_diff_v[0], page_table,
                            kd_s.reshape(bs, ts, 512), vd_s.reshape(bs, ts, 512), dl, subln, ppg, past_len, lambda_init)

    x1_p, h_p, idx_p, gate_p = _merge(x_prompt.reshape(n_p, D_MODEL), ocmp_p.reshape(n_p, 512), osel_p.reshape(n_p, 512),
                                      owin_p.reshape(n_p, 512), odiff_p.reshape(n_p, 512), gx_p, nmix, nffn, w_gates,
                                      wbn, wbd, wo, wr, br, tm)
    x1_s, h_s, idx_s, gate_s = _merge(x_sample.reshape(n_s, D_MODEL), ocmp_s.reshape(n_s, 512), osel_s.reshape(n_s, 512),
                                      owin_s.reshape(n_s, 512), odiff_s.reshape(n_s, 512), gx_s, nmix, nffn, w_gates,
                                      wbn, wbd, wo, wr, br, n_s)

    n_tok = n_p + n_s
    n_assign = n_tok * TOP_K
    tmoe = 256
    assert n_p % n_s == 0 and (2 * tmoe) % TOP_K == 0
    h_all = jnp.concatenate([h_p, h_s], axis=0)
    flat_e = jnp.concatenate([idx_p[:, :TOP_K], idx_s[:, :TOP_K]], axis=0).reshape(-1)
    flat_g = jnp.concatenate([gate_p[:, :TOP_K], gate_s[:, :TOP_K]], axis=0).reshape(-1)
    order = jnp.argsort(flat_e).astype(I32)
    counts = jnp.bincount(flat_e, length=N_EXPERTS).astype(I32)
    padded = (counts + tmoe - 1) // tmoe * tmoe
    pad_end = jnp.cumsum(padded)
    pad_start = pad_end - padded
    start = jnp.cumsum(counts) - counts
    n_blocks = (n_assign + N_EXPERTS * (tmoe - 1) + tmoe - 1) // tmoe
    n_rows = n_blocks * tmoe
    block_expert = jnp.minimum(jnp.searchsorted(pad_end, jnp.arange(n_blocks, dtype=I32) * tmoe, side="right"),
                               N_EXPERTS - 1).astype(I32)
    n_used = (pad_end[-1] // tmoe).astype(I32)[None]
    prow = jnp.arange(n_rows, dtype=I32)
    blk = prow // tmoe
    e_row = block_expert[blk]
    off = prow - pad_start[e_row]
    valid = off < counts[e_row]
    assign = order[jnp.clip(start[e_row] + off, 0, n_assign - 1)]
    src_row = jnp.where(valid, assign // TOP_K, 0).astype(I32)
    dst_row = jnp.where(valid, assign, n_assign + (blk % 2) * tmoe + prow % tmoe).astype(I32)
    gate_rows = jnp.where(valid, flat_g[assign], 0.0)[:, None]
    yg = _moe(block_expert, n_used, src_row, dst_row, gate_rows, h_all, w_gate_up[0], b_gate_up[0][:, None, :],
              w_down[0], b_down[0][:, None, :], n_assign + 2 * tmoe, tmoe)
    yg = yg.reshape(n_tok + 2 * tmoe // TOP_K, TOP_K * D_MODEL)

    nf = norm_final[None, :]
    y_p = _final(x1_p, yg, nf, tm, 0)
    y_s = _final(x1_s, yg, nf, n_s, n_p // n_s)

    g, hd = NSA_GROUPS, HEAD_DIM
    def from_t(a, heads, length):
        return jnp.transpose(a.reshape(bp, heads, 2, hd, length), (0, 4, 1, 2, 3))[None]

    new_win_p = from_t(kvwt_p[:, :, t - wbuf:], 2, wbuf)
    new_win_s = jnp.concatenate([state_win_kv[0].reshape(bs, wbuf, 256)[:, ts:], kvw_s.reshape(bs, ts, 256)], axis=1)
    return (y_p.reshape(bp, t, D_MODEL), y_s.reshape(bs, ts, D_MODEL),
            from_t(kvct_p, 2, t), kvc_s.reshape(1, bs, ts, 2, g, hd),
            from_t(kvst_p, 2, t), kvs_s.reshape(1, bs, ts, 2, g, hd),
            new_win_p, new_win_s.reshape(1, bs, wbuf, 2, g, hd),
            from_t(kdt_p, DIFF_HEADS, t), kd_s.reshape(1, bs, ts, DIFF_HEADS, 2, hd),
            vd4_p[None], vd_s.reshape(1, bs, ts, DIFF_HEADS, DIFF_DV))
```

```python
import functools
import math

import numpy as np
import jax
import jax.numpy as jnp
from jax import lax
from jax.experimental import pallas as pl
from jax.experimental.pallas import tpu as pltpu

D_MODEL = 1024
HEAD_DIM = 64
NSA_HEADS = 8
NSA_GROUPS = 2
NSA_REP = 4
CMP_BLOCK = 32
CMP_STRIDE = 16
SEL_BLOCK = 64
N_SEL = 16
WINDOW = 512
DIFF_HEADS = 4
DIFF_DV = 128
ROT_DIM = 16
ROPE_THETA = 500000.0
N_EXPERTS = 32
TOP_K = 4
D_FF = 1024
SWIGLU_LIMIT = 7.0
SWIGLU_ALPHA = 1.702
RMS_EPS = 1e-5
FORCE_BONUS = 1e4
SCALE = HEAD_DIM ** -0.5
LANES = 128

BF = jnp.bfloat16
F32 = jnp.float32
I32 = jnp.int32
NEG = -1e30
VMEM_LIMIT = 56 * 1024 * 1024

OFF_Q, OFF_KVC, OFF_KVS, OFF_KVW, OFF_QD, OFF_KD, OFF_VD, OFF_GX = 0, 512, 768, 1024, 1280, 1792, 2304, 2816
IN_COLS_K = 4352
TOFF_Q, TOFF_KVC, TOFF_QD, TOFF_VD, TOFF_GX = 0, 512, 768, 1280, 1792
TIN_COLS = 3328
TIN_ROWS = 1280


def _dot(a, b):
    return jnp.dot(a, b, preferred_element_type=F32)


def _dot_nt(a, b):
    return lax.dot_general(a, b, (((1,), (1,)), ((), ())), preferred_element_type=F32)


def _dot_hi(a, b):
    return jnp.dot(a, b, preferred_element_type=F32, precision=lax.Precision.HIGHEST)


def _params(sem):
    return pltpu.CompilerParams(dimension_semantics=sem, vmem_limit_bytes=VMEM_LIMIT)


def _lane_iota(rows):
    return lax.broadcasted_iota(I32, (rows, LANES), 1)


def _pair(lo, a, b, g):
    if g == 0:
        return jnp.where(lo, a, pltpu.roll(b, 64, 1))
    return jnp.where(lo, pltpu.roll(a, 64, 1), b)


def _inproj_body(x_ref, nw_ref, w_ref, cos_ref, sin_ref, e_ref,
                 qraw_ref, qrot_ref, kvc_ref, kvs_ref, kvw_ref, ksa_ref, kwa_ref, vsb_ref, vwb_ref,
                 qd_ref, kd_ref, kdb_ref, vd_ref, vdb_ref, gx_ref):
    x = x_ref[...]
    tm = x.shape[0]
    xn = (x * lax.rsqrt(jnp.mean(x * x, axis=-1, keepdims=True) + RMS_EPS) * nw_ref[...]).astype(BF)
    cos = cos_ref[...]
    sin = sin_ref[...]
    epat = e_ref[...]
    lane = _lane_iota(tm)
    lo = lane < 64
    first8 = (lane & 63) < 8

    def mm(off):
        return _dot(xn, w_ref[:, off:off + LANES])

    def rope(y):
        sw = jnp.where(first8, pltpu.roll(y, LANES - 8, 1), pltpu.roll(y, 8, 1))
        return y * cos + sw * sin

    for c in range(4):
        y = mm(OFF_Q + c * LANES) * SCALE
        yr = rope(y)
        ys = pltpu.roll(y, 64, 1)
        yrs = pltpu.roll(yr, 64, 1)
        qraw_ref[:, (2 * c) * LANES:(2 * c + 1) * LANES] = jnp.where(lo, y, 0.0).astype(BF)
        qraw_ref[:, (2 * c + 1) * LANES:(2 * c + 2) * LANES] = jnp.where(lo, ys, 0.0).astype(BF)
        qrot_ref[:, (2 * c) * LANES:(2 * c + 1) * LANES] = jnp.where(lo, yr, 0.0).astype(BF)
        qrot_ref[:, (2 * c + 1) * LANES:(2 * c + 2) * LANES] = jnp.where(lo, yrs, 0.0).astype(BF)

    for c in range(2):
        kvc_ref[:, c * LANES:(c + 1) * LANES] = mm(OFF_KVC + c * LANES)

    for off, kv_ref, ka_ref, vb_ref in ((OFF_KVS, kvs_ref, ksa_ref, vsb_ref), (OFF_KVW, kvw_ref, kwa_ref, vwb_ref)):
        kr = rope(mm(off))
        v = mm(off + LANES)
        kv_ref[:, 0:LANES] = kr
        kv_ref[:, LANES:2 * LANES] = v
        ka_ref[:, 0:LANES] = jnp.where(lo, kr, epat).astype(BF)
        ka_ref[:, LANES:2 * LANES] = jnp.where(lo, pltpu.roll(kr, 64, 1), epat).astype(BF)
        vb_ref[...] = v.astype(BF)

    for c in range(4):
        sl = slice(c * LANES, (c + 1) * LANES)
        qd_ref[:, sl] = rope(mm(OFF_QD + c * LANES) * SCALE).astype(BF)
        kr = rope(mm(OFF_KD + c * LANES))
        kd_ref[:, sl] = kr
        kdb_ref[:, sl] = kr.astype(BF)
        v = mm(OFF_VD + c * LANES)
        vd_ref[:, sl] = v
        vdb_ref[:, sl] = v.astype(BF)

    for c in range(12):
        gx_ref[:, c * LANES:(c + 1) * LANES] = jax.nn.sigmoid(mm(OFF_GX + c * LANES))


def _inproj(x2d, norm_w, w_bf, cos_t, sin_t, e_t, tm):
    n = x2d.shape[0]
    nt = cos_t.shape[0] // tm
    row = lambda i: (i, 0)
    tab = lambda i: (i % nt, 0)
    const = lambda i: (0, 0)
    outs = [(1024, BF), (1024, BF), (256, F32), (256, F32), (256, F32), (256, BF), (256, BF), (128, BF), (128, BF),
            (512, BF), (512, F32), (512, BF), (512, F32), (512, BF), (1536, F32)]
    return pl.pallas_call(
        _inproj_body,
        grid=(n // tm,),
        in_specs=[pl.BlockSpec((tm, D_MODEL), row), pl.BlockSpec((1, D_MODEL), const),
                  pl.BlockSpec((D_MODEL, IN_COLS_K), const),
                  pl.BlockSpec((tm, LANES), tab), pl.BlockSpec((tm, LANES), tab), pl.BlockSpec((tm, LANES), tab)],
        out_specs=[pl.BlockSpec((tm, w), row) for w, _ in outs],
        out_shape=[jax.ShapeDtypeStruct((n, w), dt) for w, dt in outs],
        compiler_params=_params(("parallel",)),
        name="inproj",
    )(x2d, norm_w, w_bf, cos_t, sin_t, e_t)


def _inproj_t_body(x_ref, nw_ref, w_ref, wt_ref, cos_ref, sin_ref, cost_ref, sint_ref,
                   qraw_ref, qrot_ref, kvc_ref, kvct_ref, kvst_ref, kvwt_ref, ksat_ref, kwat_ref, vst_ref, vwt_ref,
                   qd_ref, kdt_ref, kdtb_ref, vd4_ref, vdb_ref, gx_ref):
    x = x_ref[...]
    tm = x.shape[0]
    xn = (x * lax.rsqrt(jnp.mean(x * x, axis=-1, keepdims=True) + RMS_EPS) * nw_ref[...]).astype(BF)
    cos = cos_ref[...]
    sin = sin_ref[...]
    cos_t = cost_ref[...]
    sin_t = sint_ref[...]
    lane = _lane_iota(tm)
    lo = lane < 64
    first8 = (lane & 63) < 8

    def mm(off):
        return _dot(xn, w_ref[:, off:off + LANES])

    def mm_t(off, n):
        return _dot_nt(wt_ref[off:off + n, :], xn)

    def rope(y):
        sw = jnp.where(first8, pltpu.roll(y, LANES - 8, 1), pltpu.roll(y, 8, 1))
        return y * cos + sw * sin

    def rope_t(y):
        parts = []
        for hb in range(0, y.shape[0], HEAD_DIM):
            x1 = y[hb:hb + 8]
            x2 = y[hb + 8:hb + 16]
            parts += [x1 * cos_t - x2 * sin_t, x1 * sin_t + x2 * cos_t, y[hb + 16:hb + HEAD_DIM]]
        return jnp.concatenate(parts, axis=0)

    for c in range(4):
        y = mm(TOFF_Q + c * LANES) * SCALE
        yr = rope(y)
        ys = pltpu.roll(y, 64, 1)
        yrs = pltpu.roll(yr, 64, 1)
        qraw_ref[:, (2 * c) * LANES:(2 * c + 1) * LANES] = jnp.where(lo, y, 0.0).astype(BF)
        qraw_ref[:, (2 * c + 1) * LANES:(2 * c + 2) * LANES] = jnp.where(lo, ys, 0.0).astype(BF)
        qrot_ref[:, (2 * c) * LANES:(2 * c + 1) * LANES] = jnp.where(lo, yr, 0.0).astype(BF)
        qrot_ref[:, (2 * c + 1) * LANES:(2 * c + 2) * LANES] = jnp.where(lo, yrs, 0.0).astype(BF)

    for c in range(2):
        kvc_ref[:, c * LANES:(c + 1) * LANES] = mm(TOFF_KVC + c * LANES)
    kvct_ref[0] = mm_t(0, 256)

    tpos = pl.program_id(1) * tm + lax.broadcasted_iota(I32, (32, tm), 1)
    e_t = jnp.where(jnp.right_shift(tpos, 6) == lax.broadcasted_iota(I32, (32, tm), 0), 1.0, 0.0)
    z_t = jnp.zeros((32, tm), F32)
    for off, kvt_ref, kat_ref, vt_ref in ((256, kvst_ref, ksat_ref, vst_ref), (512, kvwt_ref, kwat_ref, vwt_ref)):
        y = mm_t(off, 256)
        kr = rope_t(y[0:128])
        v = y[128:256]
        kvt_ref[0, 0:128, :] = kr
        kvt_ref[0, 128:256, :] = v
        for g in range(NSA_GROUPS):
            kat_ref[0, g] = jnp.concatenate([kr[g * 64:(g + 1) * 64], e_t, z_t], axis=0).astype(BF)
        vt_ref[0] = v.astype(BF)

    kd = rope_t(mm_t(768, 512))
    kdt_ref[0] = kd
    kdtb_ref[0] = kd.astype(BF)
    for c in range(4):
        sl = slice(c * LANES, (c + 1) * LANES)
        qd_ref[:, sl] = rope(mm(TOFF_QD + c * LANES) * SCALE).astype(BF)
        v = mm(TOFF_VD + c * LANES)
        vd4_ref[0, :, c, :] = v
        vdb_ref[:, sl] = v.astype(BF)

    for c in range(12):
        gx_ref[:, c * LANES:(c + 1) * LANES] = jax.nn.sigmoid(mm(TOFF_GX + c * LANES))


def _inproj_t(x2d, norm_w, w_bf, wt_bf, cos_t, sin_t, cos_tt, sin_tt, nbat, t, tm):
    n = x2d.shape[0]
    nt = t // tm
    row = lambda b, i: (b * nt + i, 0)
    tab = lambda b, i: (i, 0)
    tab_t = lambda b, i: (0, i)
    const = lambda b, i: (0, 0)
    tr = lambda b, i: (b, 0, i)
    tr4 = lambda b, i: (b, 0, 0, i)
    rm = lambda w, dt: (pl.BlockSpec((tm, w), row), jax.ShapeDtypeStruct((n, w), dt))
    tp = lambda r, dt: (pl.BlockSpec((1, r, tm), tr), jax.ShapeDtypeStruct((nbat, r, t), dt))
    aug = (pl.BlockSpec((1, NSA_GROUPS, LANES, tm), tr4), jax.ShapeDtypeStruct((nbat, NSA_GROUPS, LANES, t), BF))
    vd4 = (pl.BlockSpec((1, tm, DIFF_HEADS, DIFF_DV), lambda b, i: (b, i, 0, 0)),
           jax.ShapeDtypeStruct((nbat, t, DIFF_HEADS, DIFF_DV), F32))
    outs = [rm(1024, BF), rm(1024, BF), rm(256, F32), tp(256, F32), tp(256, F32), tp(256, F32), aug, aug,
            tp(128, BF), tp(128, BF), rm(512, BF), tp(512, F32), tp(512, BF), vd4, rm(512, BF), rm(1536, F32)]
    return pl.pallas_call(
        _inproj_t_body,
        grid=(nbat, nt),
        in_specs=[pl.BlockSpec((tm, D_MODEL), row), pl.BlockSpec((1, D_MODEL), const),
                  pl.BlockSpec((D_MODEL, TIN_COLS), const), pl.BlockSpec((TIN_ROWS, D_MODEL), const),
                  pl.BlockSpec((tm, LANES), tab), pl.BlockSpec((tm, LANES), tab),
                  pl.BlockSpec((8, tm), tab_t), pl.BlockSpec((8, tm), tab_t)],
        out_specs=[o[0] for o in outs],
        out_shape=[o[1] for o in outs],
        compiler_params=_params(("parallel", "parallel")),
        name="inproj_t",
    )(x2d, norm_w, w_bf, wt_bf, cos_t, sin_t, cos_tt, sin_tt)


def _compress_body(pt_ref, *refs, n_pages, rp, rt):
    page_refs = refs[:n_pages]
    pea_ref, peb_ref, w1a_ref, w1b_ref, w2_ref, out_ref, y_scr = refs[n_pages:]
    s = pl.program_id(1)
    for i in range(n_pages):
        x = page_refs[i][0]
        row0 = pl.multiple_of((s * n_pages + i) * rp, 8)
        for cg in range(4):
            ycg = jnp.concatenate(
                [x[:, l * 256 + cg * 64:l * 256 + cg * 64 + 64] for l in range(CMP_STRIDE)], axis=1)
            y_scr[cg, pl.ds(row0, rp), :] = ycg

    @pl.when(s == pl.num_programs(1) - 1)
    def _():
        for c in range(2):
            for g in range(2):
                cg = c * 2 + g
                y = y_scr[cg]
                za = _dot((y + pea_ref[c]).astype(BF), w1a_ref[c])
                zb = _dot((y + peb_ref[c]).astype(BF), w1b_ref[c])
                hid = jax.nn.gelu(za + pltpu.roll(zb, rt - 1, 0))
                out_ref[0, :, cg * LANES:(cg + 1) * LANES] = _dot(hid.astype(BF), w2_ref[c]).astype(BF)


def _compress(pool, page_table, pea, peb, w1a, w1b, w2p, n_pages):
    nb, pages = page_table.shape
    rp = pool.shape[1]
    rt = pages * rp
    steps = pages // n_pages
    page_specs = [pl.BlockSpec((1, rp, 4096), functools.partial(
        lambda b, s, pt, i: (pt[b, s * n_pages + i], 0, 0), i=i)) for i in range(n_pages)]
    c3 = lambda b, s, pt: (0, 0, 0)
    return pl.pallas_call(
        functools.partial(_compress_body, n_pages=n_pages, rp=rp, rt=rt),
        grid_spec=pltpu.PrefetchScalarGridSpec(
            num_scalar_prefetch=1,
            grid=(nb, steps),
            in_specs=page_specs + [pl.BlockSpec((2, 1, 1024), c3), pl.BlockSpec((2, 1, 1024), c3),
                                   pl.BlockSpec((2, 1024, LANES), c3), pl.BlockSpec((2, 1024, LANES), c3),
                                   pl.BlockSpec((2, LANES, LANES), c3)],
            out_specs=pl.BlockSpec((1, rt, 512), lambda b, s, pt: (b, 0, 0)),
            scratch_shapes=[pltpu.VMEM((4, rt, 1024), F32)]),
        out_shape=jax.ShapeDtypeStruct((nb, rt, 512), BF),
        compiler_params=_params(("parallel", "arbitrary")),
        name="compress",
    )(page_table, *([pool] * n_pages), pea, peb, w1a, w1b, w2p)


def _cmp_body(q_ref, kvc_ref, mimp_ref, ocmp_ref, selb_ref, *, tq, rt, nb, nbp, pos0):
    i = pl.program_id(1)
    pos_n = pos0 + i * tq + lax.broadcasted_iota(I32, (tq, rt), 0)
    n_i = lax.broadcasted_iota(I32, (tq, rt), 1)
    vis = (n_i * CMP_STRIDE + (CMP_BLOCK - 1)) <= pos_n
    lo = _lane_iota(tq) < 64
    pos_b = pos0 + i * tq + lax.broadcasted_iota(I32, (tq, nbp), 0)
    jb = lax.broadcasted_iota(I32, (tq, nbp), 1)
    cur = jnp.right_shift(pos_b, 6)
    valid = (jb * SEL_BLOCK <= pos_b) & (jb < nb)
    forced = (jb == 0) | (jb == cur) | (jb == cur - 1)
    mimp = mimp_ref[...]
    for g in range(NSA_GROUPS):
        kc = kvc_ref[0, :, g * LANES:(g + 1) * LANES]
        vc = kvc_ref[0, :, (2 + g) * LANES:(3 + g) * LANES]
        pg = jnp.zeros((tq, rt), F32)
        og = []
        for r in range(NSA_REP):
            h = g * NSA_REP + r
            q = q_ref[0, :, h * LANES:(h + 1) * LANES]
            s = jnp.where(vis, _dot_nt(q, kc), -jnp.inf)
            m = jnp.max(s, axis=-1, keepdims=True)
            m = jnp.where(m == -jnp.inf, 0.0, m)
            e = jnp.exp(s - m)
            d = jnp.sum(e, axis=-1, keepdims=True)
            p = e / jnp.where(d > 0, d, 1.0)
            og.append(_dot(p.astype(BF), vc))
            pg = pg + p
        imp = _dot_hi(pg, mimp)
        score = jnp.where(valid, jnp.where(forced, imp + FORCE_BONUS, imp), -jnp.inf)
        rank = jnp.zeros((tq, nbp), I32)
        for jj in range(nb):
            cj = score[:, jj:jj + 1]
            beats = (cj > score) | ((cj == score) & (jb > jj))
            rank = rank + beats.astype(I32)
        sel = (rank < N_SEL) & valid
        selb_ref[0, :, g * nbp:(g + 1) * nbp] = jnp.where(sel, 0.0, NEG)
        for c2 in range(2):
            ocmp_ref[0, :, (g * 2 + c2) * LANES:(g * 2 + c2 + 1) * LANES] = _pair(lo, og[2 * c2], og[2 * c2 + 1], 0)


def _cmp_attend(qraw3, kvc, mimp, nq, nb, nbp, pos0):
    nbat = kvc.shape[0]
    _, tq, _ = qraw3.shape
    rt = kvc.shape[1]
    return pl.pallas_call(
        functools.partial(_cmp_body, tq=tq, rt=rt, nb=nb, nbp=nbp, pos0=pos0),
        grid=(nbat, nq),
        in_specs=[pl.BlockSpec((1, tq, 1024), lambda b, i: (b * nq + i, 0, 0)),
                  pl.BlockSpec((1, rt, 512), lambda b, i: (b, 0, 0)),
                  pl.BlockSpec((rt, nbp), lambda b, i: (0, 0))],
        out_specs=[pl.BlockSpec((1, tq, 512), lambda b, i: (b * nq + i, 0, 0)),
                   pl.BlockSpec((1, tq, 2 * nbp), lambda b, i: (b * nq + i, 0, 0))],
        out_shape=[jax.ShapeDtypeStruct((nbat * nq, tq, 512), F32),
                   jax.ShapeDtypeStruct((nbat * nq, tq, 2 * nbp), F32)],
        compiler_params=_params(("parallel", "parallel")),
        name="cmp_attend",
    )(qraw3, kvc, mimp)


def _online(s, v, m_s, l_s, acc, idx, v_transposed=False):
    m_old = m_s[idx]
    m_new = jnp.maximum(m_old, jnp.max(s, axis=-1, keepdims=True))
    alpha = jnp.exp(m_old - m_new)
    p = jnp.exp(s - m_new)
    l_s[idx] = alpha * l_s[idx] + jnp.sum(p, axis=-1, keepdims=True)
    pv = _dot_nt(p.astype(BF), v) if v_transposed else _dot(p.astype(BF), v)
    acc[idx] = alpha * acc[idx] + pv
    m_s[idx] = m_new


STEP_FIRST, STEP_LAST, STEP_MASKED = 1, 2, 4


def _step_table(nq, tq, tk, window=None):
    qi, kt, fl = [], [], []
    for i in range(nq):
        first = 0 if window is None else max(0, (i * tq - window + 1) // tk)
        last = (i * tq + tq - 1) // tk
        for j in range(first, last + 1):
            masked = j * tk + tk - 1 > i * tq
            if window is not None:
                masked = masked or j * tk <= i * tq + tq - 1 - window
            qi.append(i)
            kt.append(j)
            fl.append((STEP_FIRST if j == first else 0) | (STEP_LAST if j == last else 0)
                      | (STEP_MASKED if masked else 0))
    return (jnp.asarray(np.array(qi, np.int32)), jnp.asarray(np.array(kt, np.int32)),
            jnp.asarray(np.array(fl, np.int32)))


def _nsa_flash_body(qi_ref, kt_ref, fl_ref, *refs, mode, tq, tk):
    if mode == "sel":
        q_ref, selb_ref, k_ref, v_ref, o_ref, qs, m_s, l_s, acc = refs
    else:
        q_ref, k_ref, v_ref, o_ref, qs, m_s, l_s, acc = refs
    st = pl.program_id(1)
    i = qi_ref[st]
    jt = kt_ref[st]
    flags = fl_ref[st]
    q0 = i * tq
    rows = NSA_REP * tq
    lane = _lane_iota(tq)
    lo = lane < 64

    @pl.when((flags & STEP_FIRST) != 0)
    def _():
        m_s[...] = jnp.full(m_s.shape, NEG, F32)
        l_s[...] = jnp.zeros(l_s.shape, F32)
        acc[...] = jnp.zeros(acc.shape, F32)
        for g in range(NSA_GROUPS):
            if mode == "sel":
                sb = pltpu.roll(selb_ref[0, :, g * LANES:(g + 1) * LANES], 64, 1)
                sb = jnp.where((lane >= 64) & (lane < 96), sb, 0.0)
            for r in range(NSA_REP):
                h = g * NSA_REP + r
                q = q_ref[0, :, h * LANES:(h + 1) * LANES]
                if mode == "sel":
                    q = (q.astype(F32) + sb).astype(BF)
                qs[g, r * tq:(r + 1) * tq, :] = q

    need_mask = (flags & STEP_MASKED) != 0

    def step(masked):
        for g in range(NSA_GROUPS):
            s = _dot(qs[g], k_ref[0, g])
            if masked:
                rpos = q0 + (lax.broadcasted_iota(I32, (rows, tk), 0) & (tq - 1))
                kpos = jt * tk + lax.broadcasted_iota(I32, (rows, tk), 1)
                vis = kpos <= rpos
                if mode == "win":
                    vis = vis & (kpos > rpos - WINDOW)
                s = jnp.where(vis, s, NEG)
            _online(s, v_ref[0], m_s, l_s, acc, g, v_transposed=True)

    @pl.when(need_mask)
    def _():
        step(True)

    @pl.when(jnp.logical_not(need_mask))
    def _():
        step(False)

    @pl.when((flags & STEP_LAST) != 0)
    def _():
        for g in range(NSA_GROUPS):
            on = acc[g] / l_s[g]
            for c2 in range(2):
                a = on[(2 * c2) * tq:(2 * c2 + 1) * tq]
                b = on[(2 * c2 + 1) * tq:(2 * c2 + 2) * tq]
                o_ref[0, :, (g * 2 + c2) * LANES:(g * 2 + c2 + 1) * LANES] = _pair(lo, a, b, g)


def _nsa_flash(mode, q3, selb3, k2, v2, nbat, t, tq, tk):
    nq = t // tq
    rows = NSA_REP * tq
    table = _step_table(nq, tq, tk, None if mode == "sel" else WINDOW)
    qmap = lambda b, s, qi, kt, fl: (b * nq + qi[s], 0, 0)
    in_specs = [pl.BlockSpec((1, tq, 1024), qmap)]
    args = [q3]
    if mode == "sel":
        in_specs.append(pl.BlockSpec((1, tq, 2 * LANES), qmap))
        args.append(selb3)
    in_specs += [pl.BlockSpec((1, NSA_GROUPS, LANES, tk), lambda b, s, qi, kt, fl: (b, 0, 0, kt[s])),
                 pl.BlockSpec((1, LANES, tk), lambda b, s, qi, kt, fl: (b, 0, kt[s]))]
    args += [k2, v2]
    return pl.pallas_call(
        functools.partial(_nsa_flash_body, mode=mode, tq=tq, tk=tk),
        grid_spec=pltpu.PrefetchScalarGridSpec(
            num_scalar_prefetch=3, grid=(nbat, int(table[0].shape[0])), in_specs=in_specs,
            out_specs=pl.BlockSpec((1, tq, 512), qmap),
            scratch_shapes=[pltpu.VMEM((NSA_GROUPS, rows, LANES), BF), pltpu.VMEM((NSA_GROUPS, rows, 1), F32),
                            pltpu.VMEM((NSA_GROUPS, rows, 1), F32), pltpu.VMEM((NSA_GROUPS, rows, LANES), F32)]),
        out_shape=jax.ShapeDtypeStruct((nbat * nq, tq, 512), F32),
        compiler_params=_params(("parallel", "arbitrary")),
        name="nsa_flash_" + mode,
    )(*table, *args)


def _diff_lambda(dl, lambda_init):
    a = jnp.sum(dl[0:1] * dl[1:2], axis=1, keepdims=True)
    b = jnp.sum(dl[2:3] * dl[3:4], axis=1, keepdims=True)
    return jnp.exp(a) - jnp.exp(b) + lambda_init


def _diff_finish(a0, a1, lam, subln, lambda_init):
    o = a0 - lam * a1
    o = o * lax.rsqrt(jnp.mean(o * o, axis=-1, keepdims=True) + RMS_EPS) * subln
    return o * (1.0 - lambda_init)


def _diff_flash_body(qi_ref, kt_ref, fl_ref, q_ref, k_ref, v_ref, dl_ref, sub_ref, o_ref, qs, m_s, l_s, acc, *,
                     tq, tk, lambda_init):
    st = pl.program_id(1)
    i = qi_ref[st]
    j = kt_ref[st]
    flags = fl_ref[st]
    q0 = i * tq
    rows = 2 * tq
    lo = _lane_iota(tq) < 64

    @pl.when((flags & STEP_FIRST) != 0)
    def _():
        m_s[...] = jnp.full(m_s.shape, NEG, F32)
        l_s[...] = jnp.zeros(l_s.shape, F32)
        acc[...] = jnp.zeros(acc.shape, F32)
        for h in range(DIFF_HEADS):
            q = q_ref[0, :, h * LANES:(h + 1) * LANES].astype(F32)
            qs[h, 0:tq, :] = jnp.where(lo, q, 0.0).astype(BF)
            qs[h, tq:2 * tq, :] = jnp.where(lo, 0.0, q).astype(BF)

    need_mask = (flags & STEP_MASKED) != 0

    def step(masked):
        for h in range(DIFF_HEADS):
            s = _dot(qs[h], k_ref[0, h * LANES:(h + 1) * LANES, :])
            if masked:
                rpos = q0 + (lax.broadcasted_iota(I32, (rows, tk), 0) & (tq - 1))
                kpos = j * tk + lax.broadcasted_iota(I32, (rows, tk), 1)
                s = jnp.where(kpos <= rpos, s, NEG)
            _online(s, v_ref[:, h * LANES:(h + 1) * LANES], m_s, l_s, acc, h)

    @pl.when(need_mask)
    def _():
        step(True)

    @pl.when(jnp.logical_not(need_mask))
    def _():
        step(False)

    @pl.when((flags & STEP_LAST) != 0)
    def _():
        lam = _diff_lambda(dl_ref[...], lambda_init)
        for h in range(DIFF_HEADS):
            on = acc[h] / l_s[h]
            o_ref[0, :, h * LANES:(h + 1) * LANES] = _diff_finish(on[0:tq], on[tq:2 * tq], lam, sub_ref[...], lambda_init)


def _diff_flash(q3, k2, v2, dl, subln, nbat, t, tq, tk, lambda_init):
    nq = t // tq
    nk = t // tk
    table = _step_table(nq, tq, tk)
    qmap = lambda b, s, qi, kt, fl: (b * nq + qi[s], 0, 0)
    kmap = lambda b, s, qi, kt, fl: (b, 0, kt[s])
    vmap = lambda b, s, qi, kt, fl: (b * nk + kt[s], 0)
    const = lambda b, s, qi, kt, fl: (0, 0)
    rows = 2 * tq
    return pl.pallas_call(
        functools.partial(_diff_flash_body, tq=tq, tk=tk, lambda_init=lambda_init),
        grid_spec=pltpu.PrefetchScalarGridSpec(
            num_scalar_prefetch=3, grid=(nbat, int(table[0].shape[0])),
            in_specs=[pl.BlockSpec((1, tq, 512), qmap), pl.BlockSpec((1, 512, tk), kmap),
                      pl.BlockSpec((tk, 512), vmap), pl.BlockSpec((4, HEAD_DIM), const),
                      pl.BlockSpec((1, DIFF_DV), const)],
            out_specs=pl.BlockSpec((1, tq, 512), qmap),
            scratch_shapes=[pltpu.VMEM((DIFF_HEADS, rows, LANES), BF), pltpu.VMEM((DIFF_HEADS, rows, 1), F32),
                            pltpu.VMEM((DIFF_HEADS, rows, 1), F32), pltpu.VMEM((DIFF_HEADS, rows, LANES), F32)]),
        out_shape=jax.ShapeDtypeStruct((nbat * nq, tq, 512), F32),
        compiler_params=_params(("parallel", "arbitrary")),
        name="diff_flash",
    )(*table, q3, k2, v2, dl, subln)


def _paged_body(pt_ref, *refs, mode, n_pages, past_len, lambda_init):
    q_ref = refs[0]
    k = 1
    if mode == "sel":
        selb_ref = refs[k]
        k += 1
    kpages = refs[k:k + n_pages]
    k += n_pages
    if mode == "diff":
        vpages = refs[k:k + n_pages]
        k += n_pages
    newk_ref = refs[k]
    k += 1
    if mode == "diff":
        newv_ref, dl_ref, sub_ref = refs[k:k + 3]
        k += 3
    o_ref = refs[k]
    k += 1
    if mode == "sel":
        qbd, bias, m_s, l_s, acc = refs[k:]
    else:
        qbd, m_s, l_s, acc = refs[k:]
    kw = WINDOW if mode == "win" else LANES
    st = pl.program_id(1)
    nq = 8
    rows = 64
    lane8 = _lane_iota(nq)
    lo8 = lane8 < 64
    t_row = lax.broadcasted_iota(I32, (rows, LANES), 0) & (nq - 1)
    col = lax.broadcasted_iota(I32, (rows, LANES), 1)

    def update(s, pv):
        m_old = m_s[0]
        m_new = jnp.maximum(m_old, jnp.max(s, axis=-1, keepdims=True))
        alpha = jnp.exp(m_old - m_new)
        p = jnp.exp(s - m_new)
        l_s[0] = alpha * l_s[0] + jnp.sum(p, axis=-1, keepdims=True)
        acc[0] = alpha * acc[0] + pv(p.astype(BF))
        m_s[0] = m_new

    def per_head(p, v_of_head):
        return jnp.concatenate([_dot(p[h * 16:(h + 1) * 16], v_of_head(h)) for h in range(DIFF_HEADS)], axis=0)

    @pl.when(st == 0)
    def _():
        m_s[0] = jnp.full((rows, 1), NEG, F32)
        l_s[0] = jnp.zeros((rows, 1), F32)
        acc[0] = jnp.zeros(acc.shape[1:], F32)
        if mode == "diff":
            qbd[...] = jnp.zeros(qbd.shape, F32)
            for h in range(DIFF_HEADS):
                q = q_ref[0, :, h * LANES:(h + 1) * LANES].astype(F32)
                qbd[(2 * h) * nq:(2 * h + 1) * nq, h * LANES:(h + 1) * LANES] = jnp.where(lo8, q, 0.0)
                qbd[(2 * h + 1) * nq:(2 * h + 2) * nq, h * LANES:(h + 1) * LANES] = jnp.where(lo8, 0.0, q)
        else:
            for h in range(NSA_HEADS):
                g = h // NSA_REP
                q = q_ref[0, :, h * LANES:(h + 1) * LANES].astype(F32)
                qbd[h * nq:(h + 1) * nq, :] = q if g == 0 else pltpu.roll(q, 64, 1)
                if mode == "sel":
                    bias[h * nq:(h + 1) * nq, :] = selb_ref[0, :, g * 256:(g + 1) * 256]
        kn = newk_ref[0]
        if mode == "diff":
            kk, vv = kn, newv_ref[0]
        else:
            kk, vv = kn[:, 0:LANES], kn[:, LANES:2 * LANES]
        kpad = jnp.concatenate([kk, jnp.zeros((LANES - nq, kk.shape[1]), F32)], axis=0).astype(BF)
        vpad = jnp.concatenate([vv, jnp.zeros((LANES - nq, vv.shape[1]), F32)], axis=0).astype(BF)
        s = _dot_nt(qbd[...].astype(BF), kpad)
        if mode == "sel":
            jnew = past_len // SEL_BLOCK
            s = s + bias[:, jnew:jnew + 1]
        s = jnp.where((col < nq) & (col <= t_row), s, NEG)
        if mode == "diff":
            update(s, lambda p: per_head(p, lambda h: vpad[:, h * LANES:(h + 1) * LANES]))
        else:
            update(s, lambda p: _dot(p, vpad))

    qb = qbd[...].astype(BF)
    for i in range(n_pages):
        pg = st * n_pages + i
        if mode == "diff":
            kt = kpages[i][0].reshape(4 * LANES, LANES).astype(BF)
        else:
            kt = kpages[i][0, 0].reshape(LANES, kw).astype(BF)
        s = _dot(qb, kt)
        if mode == "sel":
            krow = lax.broadcasted_iota(I32, (LANES, 256), 0)
            jbi = lax.broadcasted_iota(I32, (LANES, 256), 1)
            ep = jnp.where(jbi == 2 * pg + jnp.right_shift(krow, 6), 1.0, 0.0).astype(BF)
            s = s + _dot_nt(bias[...].astype(BF), ep)
        if mode == "win":
            t_w = lax.broadcasted_iota(I32, (rows, kw), 0) & (nq - 1)
            s = jnp.where(lax.broadcasted_iota(I32, (rows, kw), 1) > t_w, s, NEG)
        if mode == "diff":
            vpage = vpages[i]
            update(s, lambda p: per_head(p, lambda h: vpage[0, :, h, :].astype(BF)))
        else:
            vt = kpages[i][0, 1].reshape(LANES, kw).astype(BF)
            update(s, lambda p: _dot_nt(p, vt))

    @pl.when(st == pl.num_programs(1) - 1)
    def _():
        on = acc[0] / l_s[0]
        if mode == "diff":
            lam = _diff_lambda(dl_ref[...], lambda_init)
            for h in range(DIFF_HEADS):
                a0 = on[h * 16:h * 16 + nq]
                a1 = on[h * 16 + nq:h * 16 + 2 * nq]
                o_ref[0, :, h * LANES:(h + 1) * LANES] = _diff_finish(a0, a1, lam, sub_ref[...], lambda_init)
        else:
            for c in range(4):
                a = on[(2 * c) * nq:(2 * c + 1) * nq]
                b = on[(2 * c + 1) * nq:(2 * c + 2) * nq]
                o_ref[0, :, c * LANES:(c + 1) * LANES] = _pair(lo8, a, b, c // 2)


def _paged_attend(mode, q3, selb3, pool_k, pool_v, page_table, newk3, newv3, dl, subln, n_pages, past_len, lambda_init):
    nbat, pages = page_table.shape
    steps = pages // n_pages
    wq = q3.shape[2]
    wk = newk3.shape[2]
    bmap = lambda b, s, pt: (b, 0, 0)
    c2 = lambda b, s, pt: (0, 0)

    def pmap(i, rank):
        return lambda b, s, pt: (pt[b, s * n_pages + i],) + (0,) * (rank - 1)

    in_specs = [pl.BlockSpec((1, 8, wq), bmap)]
    args = [q3]
    if mode == "sel":
        in_specs.append(pl.BlockSpec((1, 8, 512), bmap))
        args.append(selb3)
    kblock = (1,) + pool_k.shape[1:]
    in_specs += [pl.BlockSpec(kblock, pmap(i, len(kblock))) for i in range(n_pages)]
    args += [pool_k] * n_pages
    if mode == "diff":
        in_specs += [pl.BlockSpec((1, LANES, DIFF_HEADS, DIFF_DV), pmap(i, 4)) for i in range(n_pages)]
        args += [pool_v] * n_pages
    in_specs.append(pl.BlockSpec((1, 8, wk), bmap))
    args.append(newk3)
    if mode == "diff":
        in_specs += [pl.BlockSpec((1, 8, 512), bmap), pl.BlockSpec((4, HEAD_DIM), c2), pl.BlockSpec((1, DIFF_DV), c2)]
        args += [newv3, dl, subln]
    ck = 512 if mode == "diff" else LANES
    scratch = [pltpu.VMEM((64, ck), F32)]
    if mode == "sel":
        scratch.append(pltpu.VMEM((64, 256), F32))
    scratch += [pltpu.VMEM((1, 64, 1), F32), pltpu.VMEM((1, 64, 1), F32), pltpu.VMEM((1, 64, LANES), F32)]
    return pl.pallas_call(
        functools.partial(_paged_body, mode=mode, n_pages=n_pages, past_len=past_len, lambda_init=lambda_init),
        grid_spec=pltpu.PrefetchScalarGridSpec(
            num_scalar_prefetch=1, grid=(nbat, steps), in_specs=in_specs,
            out_specs=pl.BlockSpec((1, 8, 512), bmap), scratch_shapes=scratch),
        out_shape=jax.ShapeDtypeStruct((nbat, 8, 512), F32),
        compiler_params=_params(("parallel", "arbitrary")),
        name="paged_" + mode,
    )(page_table, *args)


def _merge_body(x_ref, oc_ref, os_ref, ow_ref, od_ref, gx_ref, nmix_ref, nffn_ref, wg_ref, wbn_ref, wbd_ref, wo_ref,
                wr_ref, br_ref, x1_ref, h_ref, idx_ref, gate_ref):
    x = x_ref[...]
    tm = x.shape[0]
    xn = (x * lax.rsqrt(jnp.mean(x * x, axis=-1, keepdims=True) + RMS_EPS) * nmix_ref[...]).astype(BF)
    gx = gx_ref[...]
    o_nsa = gx[:, 0:512] * oc_ref[...] + gx[:, 512:1024] * os_ref[...] + gx[:, 1024:1536] * ow_ref[...]
    y_nsa = _dot(o_nsa.astype(BF), wbn_ref[...])
    y_diff = _dot(od_ref[...].astype(BF), wbd_ref[...])
    gates = jax.nn.sigmoid(_dot(xn, wg_ref[...]))
    mrg = gates[:, 0:D_MODEL] * y_nsa + gates[:, D_MODEL:2 * D_MODEL] * y_diff
    x1 = x + _dot(mrg.astype(BF), wo_ref[...])
    x1_ref[...] = x1
    h = x1 * lax.rsqrt(jnp.mean(x1 * x1, axis=-1, keepdims=True) + RMS_EPS) * nffn_ref[...]
    h_ref[...] = h
    logits = _dot_hi(h, wr_ref[...]) + br_ref[...]
    lane = _lane_iota(tm)
    vals, idxs = [], []
    for _ in range(TOP_K):
        m = jnp.max(logits, axis=-1, keepdims=True)
        idx = jnp.min(jnp.where(logits == m, lane, LANES), axis=-1, keepdims=True)
        vals.append(m)
        idxs.append(idx)
        logits = jnp.where(lane == idx, -jnp.inf, logits)
    es = [jnp.exp(v - vals[0]) for v in vals]
    den = es[0] + es[1] + es[2] + es[3]
    idx_out = jnp.zeros((tm, LANES), I32)
    gate_out = jnp.zeros((tm, LANES), F32)
    for k in range(TOP_K):
        idx_out = jnp.where(lane == k, idxs[k], idx_out)
        gate_out = jnp.where(lane == k, es[k] / den, gate_out)
    idx_ref[...] = idx_out
    gate_ref[...] = gate_out


def _merge(x2d, oc, osel, ow, od, gx, nmix, nffn, wg, wbn, wbd, wo, wr, br, tm):
    n = x2d.shape[0]
    row = lambda i: (i, 0)
    const = lambda i: (0, 0)
    return pl.pallas_call(
        _merge_body,
        grid=(n // tm,),
        in_specs=[pl.BlockSpec((tm, D_MODEL), row)] + [pl.BlockSpec((tm, 512), row)] * 4 +
                 [pl.BlockSpec((tm, 1536), row), pl.BlockSpec((1, D_MODEL), const), pl.BlockSpec((1, D_MODEL), const),
                  pl.BlockSpec((D_MODEL, 2 * D_MODEL), const), pl.BlockSpec((512, D_MODEL), const),
                  pl.BlockSpec((512, D_MODEL), const), pl.BlockSpec((D_MODEL, D_MODEL), const),
                  pl.BlockSpec((D_MODEL, LANES), const), pl.BlockSpec((1, LANES), const)],
        out_specs=[pl.BlockSpec((tm, D_MODEL), row), pl.BlockSpec((tm, D_MODEL), row),
                   pl.BlockSpec((tm, LANES), row), pl.BlockSpec((tm, LANES), row)],
        out_shape=[jax.ShapeDtypeStruct((n, D_MODEL), F32), jax.ShapeDtypeStruct((n, D_MODEL), F32),
                   jax.ShapeDtypeStruct((n, LANES), I32), jax.ShapeDtypeStruct((n, LANES), F32)],
        compiler_params=_params(("parallel",)),
        name="merge",
    )(x2d, oc, osel, ow, od, gx, nmix, nffn, wg, wbn, wbd, wo, wr, br)


def _route_body(idx_ref, dest_ref, meta_ref, cnt_s, base_s, *, tm, tmoe, nbp):
    ph = pl.program_id(0)
    i = pl.program_id(1)
    shift = int(math.log2(tmoe))
    idx_t = jnp.transpose(idx_ref[...])
    sub = lax.broadcasted_iota(I32, (LANES, tm), 0)
    ohs = [idx_t[k:k + 1, :] == sub for k in range(TOP_K)]
    tok = jnp.zeros((LANES, tm), F32)
    for oh in ohs:
        tok = tok + jnp.where(oh, 1.0, 0.0)
    tile_cnt = jnp.broadcast_to(jnp.sum(tok, axis=1, keepdims=True), (LANES, LANES))

    @pl.when((ph == 0) & (i == 0))
    def _():
        cnt_s[...] = jnp.zeros((LANES, LANES), F32)

    @pl.when(ph == 0)
    def _():
        cnt_s[...] = cnt_s[...] + tile_cnt

    @pl.when((ph == 1) & (i == 0))
    def _():
        cnt = cnt_s[...].astype(I32)
        padded = jnp.left_shift(jnp.right_shift(cnt + (tmoe - 1), shift), shift)
        er = lax.broadcasted_iota(I32, (LANES, LANES), 0)
        ec = lax.broadcasted_iota(I32, (LANES, LANES), 1)
        pad_start = _dot_hi(jnp.where(ec < er, 1.0, 0.0), padded.astype(F32))
        base_s[...] = pad_start
        pad_end = pad_start.astype(I32) + padded
        blk_start = lax.broadcasted_iota(I32, (LANES, nbp), 1) * tmoe
        ends = jnp.broadcast_to(pad_end[:, 0:1], (LANES, nbp))
        be = jnp.minimum(jnp.sum(jnp.where(ends <= blk_start, 1, 0), axis=0, keepdims=True), N_EXPERTS - 1)
        n_used = jnp.right_shift(jnp.max(pad_end, axis=0, keepdims=True)[:, 0:1], shift)
        last_blk = jnp.sum(jnp.where(er == ec, jnp.where(padded > 0, jnp.right_shift(pad_end, shift) - 1, -1), 0),
                           axis=0, keepdims=True)
        lb = jnp.concatenate([last_blk, jnp.full((1, nbp - LANES), -1, I32)], axis=1)
        row = lax.broadcasted_iota(I32, (8, nbp), 0)
        meta_ref[...] = jnp.where(row == 0, be, jnp.where(row == 1, n_used, jnp.where(row == 2, lb, 0)))

    @pl.when(ph == 1)
    def _():
        nr = lax.broadcasted_iota(I32, (tm, tm), 0)
        nc = lax.broadcasted_iota(I32, (tm, tm), 1)
        before = _dot(tok.astype(BF), jnp.where(nr < nc, 1.0, 0.0).astype(BF))
        pos = before + base_s[:, 0:1]
        rows = [jnp.sum(jnp.where(oh, pos, 0.0), axis=0, keepdims=True) for oh in ohs]
        dest_ref[...] = jnp.concatenate(rows + [jnp.zeros((8 - TOP_K, tm), F32)], axis=0).astype(I32)
        base_s[...] = base_s[...] + tile_cnt


def _route(idx_all, tm, tmoe, nbp):
    n_tok = idx_all.shape[0]
    return pl.pallas_call(
        functools.partial(_route_body, tm=tm, tmoe=tmoe, nbp=nbp),
        grid=(2, n_tok // tm),
        in_specs=[pl.BlockSpec((tm, LANES), lambda ph, i: (i, 0))],
        out_specs=[pl.BlockSpec((8, tm), lambda ph, i: (0, i * ph)), pl.BlockSpec((8, nbp), lambda ph, i: (0, 0))],
        out_shape=[jax.ShapeDtypeStruct((8, n_tok), I32), jax.ShapeDtypeStruct((8, nbp), I32)],
        scratch_shapes=[pltpu.VMEM((LANES, LANES), F32), pltpu.VMEM((LANES, LANES), F32)],
        compiler_params=_params(("arbitrary", "arbitrary")),
        name="route",
    )(idx_all)


def _dispatch_body(dest_ref, lb_ref, nu_ref, h_ref, xs_hbm, zbuf, zsem, sem, *, tm, tmoe, n_tok):
    i = pl.program_id(0)

    @pl.when(i == 0)
    def _():
        zbuf[...] = jnp.zeros(zbuf.shape, F32)

        def fill(e):
            return pltpu.make_async_copy(zbuf, xs_hbm.at[pl.ds(pl.multiple_of(lb_ref[e] * tmoe, tmoe), tmoe)], zsem)
        for e in range(N_EXPERTS):
            @pl.when(lb_ref[e] >= 0)
            def _():
                fill(e).start()
        for e in range(N_EXPERTS):
            @pl.when(lb_ref[e] >= 0)
            def _():
                fill(e).wait()

        def tail(b):
            return pltpu.make_async_copy(zbuf, xs_hbm.at[pl.ds(pl.multiple_of(b * tmoe, tmoe), tmoe)], zsem)
        n_blk = xs_hbm.shape[0] // tmoe
        lax.fori_loop(nu_ref[0], n_blk, lambda b, c: (tail(b).start(), c)[1], 0)
        lax.fori_loop(nu_ref[0], n_blk, lambda b, c: (tail(b).wait(), c)[1], 0)

    def body(n, c):
        for k in range(TOP_K):
            d = dest_ref[k * n_tok + i * tm + n]
            pltpu.make_async_copy(h_ref.at[pl.ds(n, 1)], xs_hbm.at[pl.ds(d, 1)], sem).start()
        return c
    lax.fori_loop(0, tm, body, 0)
    pltpu.make_async_copy(xs_hbm.at[pl.ds(0, TOP_K * tm)], xs_hbm.at[pl.ds(0, TOP_K * tm)], sem).wait()


def _dispatch(dest_flat, last_blk, n_used, h_all, n_rows, tm, tmoe):
    n_tok = h_all.shape[0]
    return pl.pallas_call(
        functools.partial(_dispatch_body, tm=tm, tmoe=tmoe, n_tok=n_tok),
        grid_spec=pltpu.PrefetchScalarGridSpec(
            num_scalar_prefetch=3, grid=(n_tok // tm,),
            in_specs=[pl.BlockSpec((tm, D_MODEL), lambda i, d, lb, nu: (i, 0))],
            out_specs=pl.BlockSpec(memory_space=pl.ANY),
            scratch_shapes=[pltpu.VMEM((tmoe, D_MODEL), F32), pltpu.SemaphoreType.DMA(()), pltpu.SemaphoreType.DMA(())]),
        out_shape=jax.ShapeDtypeStruct((n_rows, D_MODEL), F32),
        compiler_params=pltpu.CompilerParams(dimension_semantics=("arbitrary",), vmem_limit_bytes=VMEM_LIMIT,
                                             disable_bounds_checks=True),
        name="moe_dispatch",
    )(dest_flat, last_blk, n_used, h_all)


def _moe_body(be_ref, nused_ref, x_ref, wgu_ref, bgu_ref, wd_ref, bd_ref, o_ref, wgu_s, wd_s):
    i = pl.program_id(0)
    e = be_ref[i]
    prev = be_ref[jnp.maximum(i - 1, 0)]

    @pl.when((i == 0) | (e != prev))
    def _():
        def cast_rows(c, carry):
            r0 = pl.multiple_of(c * LANES, LANES)
            wgu_s[pl.ds(r0, LANES), :] = wgu_ref[0, pl.ds(r0, LANES), :].astype(BF)
            wd_s[pl.ds(r0, LANES), :] = wd_ref[0, pl.ds(r0, LANES), :].astype(BF)
            return carry
        lax.fori_loop(0, D_MODEL // LANES, cast_rows, 0)

    @pl.when(i < nused_ref[0])
    def _():
        gu = _dot(x_ref[...].astype(BF), wgu_s[...]) + bgu_ref[0]
        gate = jnp.minimum(gu[:, 0:D_FF], SWIGLU_LIMIT)
        up = jnp.clip(gu[:, D_FF:2 * D_FF], -SWIGLU_LIMIT, SWIGLU_LIMIT)
        act = (up + 1.0) * gate * jax.nn.sigmoid(SWIGLU_ALPHA * gate)
        o_ref[...] = _dot(act.astype(BF), wd_s[...]) + bd_ref[0]

    @pl.when(i >= nused_ref[0])
    def _():
        o_ref[...] = jnp.zeros(o_ref.shape, F32)


def _moe(block_expert, n_used, xs, wgu, bgu, wd, bd, tmoe):
    n_rows = xs.shape[0]
    nblk = n_rows // tmoe
    xmap = lambda i, be, nu: (jnp.minimum(i, jnp.maximum(nu[0] - 1, 0)), 0)
    emap = lambda i, be, nu: (be[i], 0, 0)
    return pl.pallas_call(
        _moe_body,
        grid_spec=pltpu.PrefetchScalarGridSpec(
            num_scalar_prefetch=2, grid=(nblk,),
            in_specs=[pl.BlockSpec((tmoe, D_MODEL), xmap),
                      pl.BlockSpec((1, D_MODEL, 2 * D_FF), emap), pl.BlockSpec((1, 1, 2 * D_FF), emap),
                      pl.BlockSpec((1, D_FF, D_MODEL), emap), pl.BlockSpec((1, 1, D_MODEL), emap)],
            out_specs=pl.BlockSpec((tmoe, D_MODEL), lambda i, be, nu: (i, 0)),
            scratch_shapes=[pltpu.VMEM((D_MODEL, 2 * D_FF), BF), pltpu.VMEM((D_FF, D_MODEL), BF)]),
        out_shape=jax.ShapeDtypeStruct((n_rows, D_MODEL), F32),
        compiler_params=_params(("arbitrary",)),
        name="moe_experts",
    )(block_expert, n_used, xs, wgu, bgu, wd, bd)


def _final_body(dest_ref, x1_ref, gate_ref, nf_ref, ys_hbm, y_ref, ybuf, sem, *, tm, n_tok, block0):
    i = pl.program_id(0)

    def gather(tile, sl):
        def body(n, c):
            for k in range(TOP_K):
                d = dest_ref[k * n_tok + (block0 + tile) * tm + n]
                pltpu.make_async_copy(ys_hbm.at[pl.ds(d, 1)], ybuf.at[sl, k, pl.ds(n, 1)], sem.at[sl]).start()
            return c
        lax.fori_loop(0, tm, body, 0)

    @pl.when(i == 0)
    def _():
        gather(0, 0)

    @pl.when(i + 1 < pl.num_programs(0))
    def _():
        gather(i + 1, (i + 1) % 2)

    slot = i % 2
    pltpu.make_async_copy(ybuf.at[slot], ybuf.at[slot], sem.at[slot]).wait()
    gate = gate_ref[...]
    x = x1_ref[...]
    for k in range(TOP_K):
        x = x + gate[:, k:k + 1] * ybuf[slot, k]
    y_ref[...] = x * lax.rsqrt(jnp.mean(x * x, axis=-1, keepdims=True) + RMS_EPS) * nf_ref[...]


def _final(dest_flat, x1, gates, nf, ys, tm, n_tok, block0):
    n = x1.shape[0]
    row = lambda i, d: (i, 0)
    return pl.pallas_call(
        functools.partial(_final_body, tm=tm, n_tok=n_tok, block0=block0),
        grid_spec=pltpu.PrefetchScalarGridSpec(
            num_scalar_prefetch=1, grid=(n // tm,),
            in_specs=[pl.BlockSpec((tm, D_MODEL), row), pl.BlockSpec((tm, LANES), row),
                      pl.BlockSpec((1, D_MODEL), lambda i, d: (0, 0)), pl.BlockSpec(memory_space=pl.ANY)],
            out_specs=pl.BlockSpec((tm, D_MODEL), row),
            scratch_shapes=[pltpu.VMEM((2, TOP_K, tm, D_MODEL), F32), pltpu.SemaphoreType.DMA((2,))]),
        out_shape=jax.ShapeDtypeStruct((n, D_MODEL), F32),
        compiler_params=pltpu.CompilerParams(dimension_semantics=("arbitrary",), vmem_limit_bytes=VMEM_LIMIT,
                                             disable_bounds_checks=True),
        name="final_norm",
    )(dest_flat, x1, gates, nf, ys)


def _rope_tables(pos, nb_lanes):
    half = ROT_DIM // 2
    inv = ROPE_THETA ** (-jnp.arange(half, dtype=F32) / half)
    ang = pos.astype(F32)[:, None] * inv[None, :]
    cos, sin = jnp.cos(ang), jnp.sin(ang)
    t = pos.shape[0]
    c64 = jnp.concatenate([cos, cos, jnp.ones((t, HEAD_DIM - ROT_DIM), F32)], axis=1)
    s64 = jnp.concatenate([-sin, sin, jnp.zeros((t, HEAD_DIM - ROT_DIM), F32)], axis=1)
    lane = jnp.arange(LANES)[None, :]
    e = ((lane >= 64) & (lane < 64 + nb_lanes) & ((pos // SEL_BLOCK)[:, None] == lane - 64)).astype(F32)
    return jnp.tile(c64, (1, 2)), jnp.tile(s64, (1, 2)), e


def _imp_matrix(rt, n_cmp, nb, nbp):
    m = np.zeros((rt, nbp), np.float32)
    sub = SEL_BLOCK // CMP_STRIDE
    for n in range(n_cmp):
        for shift in range(CMP_BLOCK // CMP_STRIDE):
            j = (n + shift) // sub
            if j < nb:
                m[n, j] += 1.0
    return jnp.asarray(m)


def kernel(x_prompt, x_sample, cache_cmp_kv, cache_sel_kv, state_win_kv, cache_diff_k, cache_diff_v, page_table, norm_mix, w_in, cmp_pe, w_cmp1, w_cmp2, diff_lambda, diff_subln, w_br_nsa, w_br_diff, w_out, norm_ffn, w_router, b_router, w_gate_up, b_gate_up, w_down, b_down, norm_final):
    bp, t, _ = x_prompt.shape
    bs, ts, _ = x_sample.shape
    assert ts == 8 and t % 256 == 0 and t <= 2048 and w_in.shape[0] == 1
    n_pool, page = cache_cmp_kv.shape[1], cache_cmp_kv.shape[2]
    pages = page_table.shape[1]
    past_len = pages * page
    assert page == LANES and past_len % SEL_BLOCK == 0
    lambda_init = 0.8 - 0.6 * math.exp(-0.3 * 0)
    n_p, n_s = bp * t, bs * ts

    w = w_in[0]
    gcols = w[:, 1280:1304].reshape(D_MODEL, NSA_HEADS, 3).transpose(0, 2, 1)
    gexp = jnp.broadcast_to(gcols[..., None], (D_MODEL, 3, NSA_HEADS, HEAD_DIM)).reshape(D_MODEL, 1536)
    w_proj = jnp.concatenate([w[:, 0:1280], w[:, 1304:2840], gexp], axis=1).astype(BF)
    w_gates = w[:, 2840:4888].astype(BF)
    nmix = norm_mix[0][None, :]
    nffn = norm_ffn[0][None, :]
    pe = cmp_pe[0]
    pea = pe[:, :CMP_STRIDE].reshape(2, 1, 1024)
    peb = pe[:, CMP_STRIDE:].reshape(2, 1, 1024)
    w1a = w_cmp1[0][:, :1024].astype(BF)
    w1b = w_cmp1[0][:, 1024:].astype(BF)
    w2p = jnp.pad(w_cmp2[0], ((0, 0), (0, 0), (0, LANES - HEAD_DIM))).astype(BF)
    wbn = w_br_nsa[0].astype(BF)
    wbd = w_br_diff[0].astype(BF)
    wo = w_out[0].astype(BF)
    wr = jnp.pad(w_router[0], ((0, 0), (0, LANES - N_EXPERTS)))
    br = jnp.concatenate([b_router[0], jnp.full((LANES - N_EXPERTS,), NEG, F32)])[None, :]
    dl = diff_lambda[0]
    subln = diff_subln[0][None, :]

    pos_p = jnp.arange(t, dtype=I32)
    pos_s = jnp.tile(past_len + jnp.arange(ts, dtype=I32), bs)
    tm = 256
    w_rows = jnp.concatenate([w[:, 0:768], w[:, 1304:1816], w[:, 2328:2840], gexp], axis=1).astype(BF)
    w_cols = jnp.concatenate([w[:, 512:1280], w[:, 1816:2328]], axis=1).T.astype(BF)
    cos_p, sin_p, _ = _rope_tables(pos_p, 0)
    outs_p = _inproj_t(x_prompt.reshape(n_p, D_MODEL), nmix, w_rows, w_cols, cos_p, sin_p,
                       cos_p[:, 0:8].T, -sin_p[:, 0:8].T, bp, t, tm)
    outs_s = _inproj(x_sample.reshape(n_s, D_MODEL), nmix, w_proj, *_rope_tables(pos_s, 0), n_s)
    (qraw_p, qrot_p, kvc_p, kvct_p, kvst_p, kvwt_p, ksat_p, kwat_p, vst_p, vwt_p, qd_p, kdt_p, kdtb_p, vd4_p, vdb_p,
     gx_p) = outs_p
    (qraw_s, qrot_s, kvc_s, kvs_s, kvw_s, _, _, _, _, qd_s, kd_s, _, vd_s, _, gx_s) = outs_s

    rt_p = t // CMP_STRIDE
    n_cmp_p = (t - CMP_BLOCK) // CMP_STRIDE + 1
    nb_p = t // SEL_BLOCK
    kvcmp_p = _compress(kvc_p.reshape(bp, rt_p, 4096), jnp.arange(bp, dtype=I32)[:, None], pea, peb, w1a, w1b, w2p, 1)
    tq = 128
    nq = t // tq
    ocmp_p, selb_p = _cmp_attend(qraw_p.reshape(bp * nq, tq, 1024), kvcmp_p, _imp_matrix(rt_p, n_cmp_p, nb_p, LANES),
                                 nq, nb_p, LANES, 0)
    qrot3 = qrot_p.reshape(bp * nq, tq, 1024)
    osel_p = _nsa_flash("sel", qrot3, selb_p, ksat_p, vst_p, bp, t, tq, 512)
    owin_p = _nsa_flash("win", qrot3, None, kwat_p, vwt_p, bp, t, tq, tq)
    odiff_p = _diff_flash(qd_p.reshape(bp * nq, tq, 512), kdtb_p, vdb_p, dl, subln, bp, t, tq, 512, lambda_init)

    n_cmp_s = (past_len + ts - CMP_BLOCK) // CMP_STRIDE + 1
    rt_s = past_len // CMP_STRIDE
    assert n_cmp_s <= rt_s
    nb_s = (past_len + ts + SEL_BLOCK - 1) // SEL_BLOCK
    assert nb_s <= 256
    ppg = 8 if pages % 8 == 0 else 1
    kvcmp_s = _compress(cache_cmp_kv[0].reshape(n_pool, page // CMP_STRIDE, 4096), page_table, pea, peb, w1a, w1b,
                        w2p, ppg)
    ocmp_s, selb_s = _cmp_attend(qraw_s.reshape(bs, ts, 1024), kvcmp_s, _imp_matrix(rt_s, n_cmp_s, nb_s, 256),
                                 1, nb_s, 256, past_len)
    qrot_s3 = qrot_s.reshape(bs, ts, 1024)
    sel_t = jnp.transpose(cache_sel_kv[0], (0, 2, 3, 4, 1))
    win_t = jnp.transpose(state_win_kv[0], (0, 2, 3, 4, 1))
    dk_t = jnp.transpose(cache_diff_k[0], (0, 2, 3, 4, 1))
    osel_s = _paged_attend("sel", qrot_s3, selb_s, sel_t, None, page_table,
                           kvs_s.reshape(bs, ts, 256), None, None, None, ppg, past_len, lambda_init)
    wbuf = state_win_kv.shape[2]
    assert wbuf == WINDOW
    owin_s = _paged_attend("win", qrot_s3, None, win_t, None, jnp.arange(bs, dtype=I32)[:, None],
                           kvw_s.reshape(bs, ts, 256), None, None, None, 1, past_len, lambda_init)
    odiff_s = _paged_attend("diff", qd_s.reshape(bs, ts, 512), None, dk_t, cache_diff_v[0], page_table,
                            kd_s.reshape(bs, ts, 512), vd_s.reshape(bs, ts, 512), dl, subln, ppg, past_len, lambda_init)

    x1_p, h_p, idx_p, gate_p = _merge(x_prompt.reshape(n_p, D_MODEL), ocmp_p.reshape(n_p, 512), osel_p.reshape(n_p, 512),
                                      owin_p.reshape(n_p, 512), odiff_p.reshape(n_p, 512), gx_p, nmix, nffn, w_gates,
                                      wbn, wbd, wo, wr, br, tm)
    x1_s, h_s, idx_s, gate_s = _merge(x_sample.reshape(n_s, D_MODEL), ocmp_s.reshape(n_s, 512), osel_s.reshape(n_s, 512),
                                      owin_s.reshape(n_s, 512), odiff_s.reshape(n_s, 512), gx_s, nmix, nffn, w_gates,
                                      wbn, wbd, wo, wr, br, n_s)

    n_tok = n_p + n_s
    n_assign = n_tok * TOP_K
    tmoe = 256
    assert n_p % n_s == 0 and n_tok % n_s == 0
    n_blocks = (n_assign + N_EXPERTS * (tmoe - 1) + tmoe - 1) // tmoe
    nbp = (n_blocks + LANES - 1) // LANES * LANES
    dest, meta = _route(jnp.concatenate([idx_p, idx_s], axis=0), n_s, tmoe, nbp)
    dest_flat = dest.reshape(-1)
    xs = _dispatch(dest_flat, meta[2, :N_EXPERTS], meta[1, :1], jnp.concatenate([h_p, h_s], axis=0), n_blocks * tmoe,
                   n_s, tmoe)
    ys = _moe(meta[0, :n_blocks], meta[1, :1], xs, w_gate_up[0], b_gate_up[0][:, None, :], w_down[0],
              b_down[0][:, None, :], tmoe)

    nf = norm_final[None, :]
    y_p = _final(dest_flat, x1_p, gate_p, nf, ys, n_s, n_tok, 0)
    y_s = _final(dest_flat, x1_s, gate_s, nf, ys, n_s, n_tok, n_p // n_s)

    g, hd = NSA_GROUPS, HEAD_DIM
    def from_t(a, heads, length):
        return jnp.transpose(a.reshape(bp, heads, 2, hd, length), (0, 4, 1, 2, 3))[None]

    new_win_p = from_t(kvwt_p[:, :, t - wbuf:], 2, wbuf)
    new_win_s = jnp.concatenate([state_win_kv[0].reshape(bs, wbuf, 256)[:, ts:], kvw_s.reshape(bs, ts, 256)], axis=1)
    return (y_p.reshape(bp, t, D_MODEL), y_s.reshape(bs, ts, D_MODEL),
            from_t(kvct_p, 2, t), kvc_s.reshape(1, bs, ts, 2, g, hd),
            from_t(kvst_p, 2, t), kvs_s.reshape(1, bs, ts, 2, g, hd),
            new_win_p, new_win_s.reshape(1, bs, wbuf, 2, g, hd),
            from_t(kdt_p, DIFF_HEADS, t), kd_s.reshape(1, bs, ts, DIFF_HEADS, 2, hd),
            vd4_p[None], vd_s.reshape(1, bs, ts, DIFF_HEADS, DIFF_DV))
```

```python
import functools
import math

import numpy as np
import jax
import jax.numpy as jnp
from jax import lax
from jax.experimental import pallas as pl
from jax.experimental.pallas import tpu as pltpu

D_MODEL = 1024
HEAD_DIM = 64
NSA_HEADS = 8
NSA_GROUPS = 2
NSA_REP = 4
CMP_BLOCK = 32
CMP_STRIDE = 16
SEL_BLOCK = 64
N_SEL = 16
WINDOW = 512
DIFF_HEADS = 4
DIFF_DV = 128
ROT_DIM = 16
ROPE_THETA = 500000.0
N_EXPERTS = 32
TOP_K = 4
D_FF = 1024
SWIGLU_LIMIT = 7.0
SWIGLU_ALPHA = 1.702
RMS_EPS = 1e-5
FORCE_BONUS = 1e4
SCALE = HEAD_DIM ** -0.5
LANES = 128

BF = jnp.bfloat16
F32 = jnp.float32
I32 = jnp.int32
NEG = -1e30
VMEM_LIMIT = 56 * 1024 * 1024

OFF_Q, OFF_KVC, OFF_KVS, OFF_KVW, OFF_QD, OFF_KD, OFF_VD, OFF_GX = 0, 512, 768, 1024, 1280, 1792, 2304, 2816
IN_COLS_K = 4352
TOFF_Q, TOFF_KVC, TOFF_QD, TOFF_VD, TOFF_GX = 0, 512, 768, 1280, 1792
TIN_COLS = 3328
TIN_ROWS = 1280


def _dot(a, b):
    return jnp.dot(a, b, preferred_element_type=F32)


def _dot_nt(a, b):
    return lax.dot_general(a, b, (((1,), (1,)), ((), ())), preferred_element_type=F32)


def _dot_hi(a, b):
    return jnp.dot(a, b, preferred_element_type=F32, precision=lax.Precision.HIGHEST)


def _params(sem):
    return pltpu.CompilerParams(dimension_semantics=sem, vmem_limit_bytes=VMEM_LIMIT)


def _lane_iota(rows):
    return lax.broadcasted_iota(I32, (rows, LANES), 1)


def _pair(lo, a, b, g):
    if g == 0:
        return jnp.where(lo, a, pltpu.roll(b, 64, 1))
    return jnp.where(lo, pltpu.roll(a, 64, 1), b)


def _inproj_body(x_ref, nw_ref, w_ref, cos_ref, sin_ref, e_ref,
                 qraw_ref, qrot_ref, kvc_ref, kvs_ref, kvw_ref, ksa_ref, kwa_ref, vsb_ref, vwb_ref,
                 qd_ref, kd_ref, kdb_ref, vd_ref, vdb_ref, gx_ref):
    x = x_ref[...]
    tm = x.shape[0]
    xn = (x * lax.rsqrt(jnp.mean(x * x, axis=-1, keepdims=True) + RMS_EPS) * nw_ref[...]).astype(BF)
    cos = cos_ref[...]
    sin = sin_ref[...]
    epat = e_ref[...]
    lane = _lane_iota(tm)
    lo = lane < 64
    first8 = (lane & 63) < 8

    def mm(off):
        return _dot(xn, w_ref[:, off:off + LANES])

    def rope(y):
        sw = jnp.where(first8, pltpu.roll(y, LANES - 8, 1), pltpu.roll(y, 8, 1))
        return y * cos + sw * sin

    for c in range(4):
        y = mm(OFF_Q + c * LANES) * SCALE
        yr = rope(y)
        ys = pltpu.roll(y, 64, 1)
        yrs = pltpu.roll(yr, 64, 1)
        qraw_ref[:, (2 * c) * LANES:(2 * c + 1) * LANES] = jnp.where(lo, y, 0.0).astype(BF)
        qraw_ref[:, (2 * c + 1) * LANES:(2 * c + 2) * LANES] = jnp.where(lo, ys, 0.0).astype(BF)
        qrot_ref[:, (2 * c) * LANES:(2 * c + 1) * LANES] = jnp.where(lo, yr, 0.0).astype(BF)
        qrot_ref[:, (2 * c + 1) * LANES:(2 * c + 2) * LANES] = jnp.where(lo, yrs, 0.0).astype(BF)

    for c in range(2):
        kvc_ref[:, c * LANES:(c + 1) * LANES] = mm(OFF_KVC + c * LANES)

    for off, kv_ref, ka_ref, vb_ref in ((OFF_KVS, kvs_ref, ksa_ref, vsb_ref), (OFF_KVW, kvw_ref, kwa_ref, vwb_ref)):
        kr = rope(mm(off))
        v = mm(off + LANES)
        kv_ref[:, 0:LANES] = kr
        kv_ref[:, LANES:2 * LANES] = v
        ka_ref[:, 0:LANES] = jnp.where(lo, kr, epat).astype(BF)
        ka_ref[:, LANES:2 * LANES] = jnp.where(lo, pltpu.roll(kr, 64, 1), epat).astype(BF)
        vb_ref[...] = v.astype(BF)

    for c in range(4):
        sl = slice(c * LANES, (c + 1) * LANES)
        qd_ref[:, sl] = rope(mm(OFF_QD + c * LANES) * SCALE).astype(BF)
        kr = rope(mm(OFF_KD + c * LANES))
        kd_ref[:, sl] = kr
        kdb_ref[:, sl] = kr.astype(BF)
        v = mm(OFF_VD + c * LANES)
        vd_ref[:, sl] = v
        vdb_ref[:, sl] = v.astype(BF)

    for c in range(12):
        gx_ref[:, c * LANES:(c + 1) * LANES] = jax.nn.sigmoid(mm(OFF_GX + c * LANES))


def _inproj(x2d, norm_w, w_bf, cos_t, sin_t, e_t, tm):
    n = x2d.shape[0]
    nt = cos_t.shape[0] // tm
    row = lambda i: (i, 0)
    tab = lambda i: (i % nt, 0)
    const = lambda i: (0, 0)
    outs = [(1024, BF), (1024, BF), (256, F32), (256, F32), (256, F32), (256, BF), (256, BF), (128, BF), (128, BF),
            (512, BF), (512, F32), (512, BF), (512, F32), (512, BF), (1536, F32)]
    return pl.pallas_call(
        _inproj_body,
        grid=(n // tm,),
        in_specs=[pl.BlockSpec((tm, D_MODEL), row), pl.BlockSpec((1, D_MODEL), const),
                  pl.BlockSpec((D_MODEL, IN_COLS_K), const),
                  pl.BlockSpec((tm, LANES), tab), pl.BlockSpec((tm, LANES), tab), pl.BlockSpec((tm, LANES), tab)],
        out_specs=[pl.BlockSpec((tm, w), row) for w, _ in outs],
        out_shape=[jax.ShapeDtypeStruct((n, w), dt) for w, dt in outs],
        compiler_params=_params(("parallel",)),
        name="inproj",
    )(x2d, norm_w, w_bf, cos_t, sin_t, e_t)


def _inproj_t_body(x_ref, nw_ref, w_ref, wt_ref, cos_ref, sin_ref, cost_ref, sint_ref,
                   qraw_ref, qrot_ref, kvc_ref, kvct_ref, kvst_ref, kvwt_ref, ksat_ref, kwat_ref, vst_ref, vwt_ref,
                   qd_ref, kdt_ref, kdtb_ref, vd4_ref, vdb_ref, gx_ref):
    x = x_ref[...]
    tm = x.shape[0]
    xn = (x * lax.rsqrt(jnp.mean(x * x, axis=-1, keepdims=True) + RMS_EPS) * nw_ref[...]).astype(BF)
    cos = cos_ref[...]
    sin = sin_ref[...]
    cos_t = cost_ref[...]
    sin_t = sint_ref[...]
    lane = _lane_iota(tm)
    lo = lane < 64
    first8 = (lane & 63) < 8

    def mm(off):
        return _dot(xn, w_ref[:, off:off + LANES])

    def mm_t(off, n):
        return _dot_nt(wt_ref[off:off + n, :], xn)

    def rope(y):
        sw = jnp.where(first8, pltpu.roll(y, LANES - 8, 1), pltpu.roll(y, 8, 1))
        return y * cos + sw * sin

    def rope_t(y):
        parts = []
        for hb in range(0, y.shape[0], HEAD_DIM):
            x1 = y[hb:hb + 8]
            x2 = y[hb + 8:hb + 16]
            parts += [x1 * cos_t - x2 * sin_t, x1 * sin_t + x2 * cos_t, y[hb + 16:hb + HEAD_DIM]]
        return jnp.concatenate(parts, axis=0)

    for c in range(4):
        y = mm(TOFF_Q + c * LANES) * SCALE
        yr = rope(y)
        ys = pltpu.roll(y, 64, 1)
        yrs = pltpu.roll(yr, 64, 1)
        qraw_ref[:, (2 * c) * LANES:(2 * c + 1) * LANES] = jnp.where(lo, y, 0.0).astype(BF)
        qraw_ref[:, (2 * c + 1) * LANES:(2 * c + 2) * LANES] = jnp.where(lo, ys, 0.0).astype(BF)
        qrot_ref[:, (2 * c) * LANES:(2 * c + 1) * LANES] = jnp.where(lo, yr, 0.0).astype(BF)
        qrot_ref[:, (2 * c + 1) * LANES:(2 * c + 2) * LANES] = jnp.where(lo, yrs, 0.0).astype(BF)

    for c in range(2):
        kvc_ref[:, c * LANES:(c + 1) * LANES] = mm(TOFF_KVC + c * LANES)
    kvct_ref[0] = mm_t(0, 256)

    tpos = pl.program_id(1) * tm + lax.broadcasted_iota(I32, (32, tm), 1)
    e_t = jnp.where(jnp.right_shift(tpos, 6) == lax.broadcasted_iota(I32, (32, tm), 0), 1.0, 0.0)
    z_t = jnp.zeros((32, tm), F32)
    for off, kvt_ref, kat_ref, vt_ref in ((256, kvst_ref, ksat_ref, vst_ref), (512, kvwt_ref, kwat_ref, vwt_ref)):
        y = mm_t(off, 256)
        kr = rope_t(y[0:128])
        v = y[128:256]
        kvt_ref[0, 0:128, :] = kr
        kvt_ref[0, 128:256, :] = v
        for g in range(NSA_GROUPS):
            kat_ref[0, g] = jnp.concatenate([kr[g * 64:(g + 1) * 64], e_t, z_t], axis=0).astype(BF)
        vt_ref[0] = v.astype(BF)

    kd = rope_t(mm_t(768, 512))
    kdt_ref[0] = kd
    kdtb_ref[0] = kd.astype(BF)
    for c in range(4):
        sl = slice(c * LANES, (c + 1) * LANES)
        qd_ref[:, sl] = rope(mm(TOFF_QD + c * LANES) * SCALE).astype(BF)
        v = mm(TOFF_VD + c * LANES)
        vd4_ref[0, :, c, :] = v
        vdb_ref[:, sl] = v.astype(BF)

    for c in range(12):
        gx_ref[:, c * LANES:(c + 1) * LANES] = jax.nn.sigmoid(mm(TOFF_GX + c * LANES))


def _inproj_t(x2d, norm_w, w_bf, wt_bf, cos_t, sin_t, cos_tt, sin_tt, nbat, t, tm):
    n = x2d.shape[0]
    nt = t // tm
    row = lambda b, i: (b * nt + i, 0)
    tab = lambda b, i: (i, 0)
    tab_t = lambda b, i: (0, i)
    const = lambda b, i: (0, 0)
    tr = lambda b, i: (b, 0, i)
    tr4 = lambda b, i: (b, 0, 0, i)
    rm = lambda w, dt: (pl.BlockSpec((tm, w), row), jax.ShapeDtypeStruct((n, w), dt))
    tp = lambda r, dt: (pl.BlockSpec((1, r, tm), tr), jax.ShapeDtypeStruct((nbat, r, t), dt))
    aug = (pl.BlockSpec((1, NSA_GROUPS, LANES, tm), tr4), jax.ShapeDtypeStruct((nbat, NSA_GROUPS, LANES, t), BF))
    vd4 = (pl.BlockSpec((1, tm, DIFF_HEADS, DIFF_DV), lambda b, i: (b, i, 0, 0)),
           jax.ShapeDtypeStruct((nbat, t, DIFF_HEADS, DIFF_DV), F32))
    outs = [rm(1024, BF), rm(1024, BF), rm(256, F32), tp(256, F32), tp(256, F32), tp(256, F32), aug, aug,
            tp(128, BF), tp(128, BF), rm(512, BF), tp(512, F32), tp(512, BF), vd4, rm(512, BF), rm(1536, F32)]
    return pl.pallas_call(
        _inproj_t_body,
        grid=(nbat, nt),
        in_specs=[pl.BlockSpec((tm, D_MODEL), row), pl.BlockSpec((1, D_MODEL), const),
                  pl.BlockSpec((D_MODEL, TIN_COLS), const), pl.BlockSpec((TIN_ROWS, D_MODEL), const),
                  pl.BlockSpec((tm, LANES), tab), pl.BlockSpec((tm, LANES), tab),
                  pl.BlockSpec((8, tm), tab_t), pl.BlockSpec((8, tm), tab_t)],
        out_specs=[o[0] for o in outs],
        out_shape=[o[1] for o in outs],
        compiler_params=_params(("parallel", "parallel")),
        name="inproj_t",
    )(x2d, norm_w, w_bf, wt_bf, cos_t, sin_t, cos_tt, sin_tt)


def _compress_body(pt_ref, *refs, n_pages, rp, rt):
    page_refs = refs[:n_pages]
    pea_ref, peb_ref, w1a_ref, w1b_ref, w2_ref, out_ref, y_scr = refs[n_pages:]
    s = pl.program_id(1)
    for i in range(n_pages):
        x = page_refs[i][0]
        row0 = pl.multiple_of((s * n_pages + i) * rp, 8)
        for cg in range(4):
            ycg = jnp.concatenate(
                [x[:, l * 256 + cg * 64:l * 256 + cg * 64 + 64] for l in range(CMP_STRIDE)], axis=1)
            y_scr[cg, pl.ds(row0, rp), :] = ycg

    @pl.when(s == pl.num_programs(1) - 1)
    def _():
        for c in range(2):
            for g in range(2):
                cg = c * 2 + g
                y = y_scr[cg]
                za = _dot((y + pea_ref[c]).astype(BF), w1a_ref[c])
                zb = _dot((y + peb_ref[c]).astype(BF), w1b_ref[c])
                hid = jax.nn.gelu(za + pltpu.roll(zb, rt - 1, 0))
                out_ref[0, :, cg * LANES:(cg + 1) * LANES] = _dot(hid.astype(BF), w2_ref[c]).astype(BF)


def _compress(pool, page_table, pea, peb, w1a, w1b, w2p, n_pages):
    nb, pages = page_table.shape
    rp = pool.shape[1]
    rt = pages * rp
    steps = pages // n_pages
    page_specs = [pl.BlockSpec((1, rp, 4096), functools.partial(
        lambda b, s, pt, i: (pt[b, s * n_pages + i], 0, 0), i=i)) for i in range(n_pages)]
    c3 = lambda b, s, pt: (0, 0, 0)
    return pl.pallas_call(
        functools.partial(_compress_body, n_pages=n_pages, rp=rp, rt=rt),
        grid_spec=pltpu.PrefetchScalarGridSpec(
            num_scalar_prefetch=1,
            grid=(nb, steps),
            in_specs=page_specs + [pl.BlockSpec((2, 1, 1024), c3), pl.BlockSpec((2, 1, 1024), c3),
                                   pl.BlockSpec((2, 1024, LANES), c3), pl.BlockSpec((2, 1024, LANES), c3),
                                   pl.BlockSpec((2, LANES, LANES), c3)],
            out_specs=pl.BlockSpec((1, rt, 512), lambda b, s, pt: (b, 0, 0)),
            scratch_shapes=[pltpu.VMEM((4, rt, 1024), F32)]),
        out_shape=jax.ShapeDtypeStruct((nb, rt, 512), BF),
        compiler_params=_params(("parallel", "arbitrary")),
        name="compress",
    )(page_table, *([pool] * n_pages), pea, peb, w1a, w1b, w2p)


def _cmp_body(q_ref, kvc_ref, mimp_ref, ocmp_ref, selb_ref, *, tq, rt, nb, nbp, pos0):
    i = pl.program_id(1)
    pos_n = pos0 + i * tq + lax.broadcasted_iota(I32, (tq, rt), 0)
    n_i = lax.broadcasted_iota(I32, (tq, rt), 1)
    vis = (n_i * CMP_STRIDE + (CMP_BLOCK - 1)) <= pos_n
    lo = _lane_iota(tq) < 64
    pos_b = pos0 + i * tq + lax.broadcasted_iota(I32, (tq, nbp), 0)
    jb = lax.broadcasted_iota(I32, (tq, nbp), 1)
    cur = jnp.right_shift(pos_b, 6)
    valid = (jb * SEL_BLOCK <= pos_b) & (jb < nb)
    forced = (jb == 0) | (jb == cur) | (jb == cur - 1)
    mimp = mimp_ref[...]
    for g in range(NSA_GROUPS):
        kc = kvc_ref[0, :, g * LANES:(g + 1) * LANES]
        vc = kvc_ref[0, :, (2 + g) * LANES:(3 + g) * LANES]
        pg = jnp.zeros((tq, rt), F32)
        og = []
        for r in range(NSA_REP):
            h = g * NSA_REP + r
            q = q_ref[0, :, h * LANES:(h + 1) * LANES]
            s = jnp.where(vis, _dot_nt(q, kc), -jnp.inf)
            m = jnp.max(s, axis=-1, keepdims=True)
            m = jnp.where(m == -jnp.inf, 0.0, m)
            e = jnp.exp(s - m)
            d = jnp.sum(e, axis=-1, keepdims=True)
            p = e / jnp.where(d > 0, d, 1.0)
            og.append(_dot(p.astype(BF), vc))
            pg = pg + p
        imp = _dot_hi(pg, mimp)
        score = jnp.where(valid, jnp.where(forced, imp + FORCE_BONUS, imp), -jnp.inf)
        rank = jnp.zeros((tq, nbp), I32)
        for jj in range(nb):
            cj = score[:, jj:jj + 1]
            beats = (cj > score) | ((cj == score) & (jb > jj))
            rank = rank + beats.astype(I32)
        sel = (rank < N_SEL) & valid
        selb_ref[0, :, g * nbp:(g + 1) * nbp] = jnp.where(sel, 0.0, NEG)
        for c2 in range(2):
            ocmp_ref[0, :, (g * 2 + c2) * LANES:(g * 2 + c2 + 1) * LANES] = _pair(lo, og[2 * c2], og[2 * c2 + 1], 0)


def _cmp_attend(qraw3, kvc, mimp, nq, nb, nbp, pos0):
    nbat = kvc.shape[0]
    _, tq, _ = qraw3.shape
    rt = kvc.shape[1]
    return pl.pallas_call(
        functools.partial(_cmp_body, tq=tq, rt=rt, nb=nb, nbp=nbp, pos0=pos0),
        grid=(nbat, nq),
        in_specs=[pl.BlockSpec((1, tq, 1024), lambda b, i: (b * nq + i, 0, 0)),
                  pl.BlockSpec((1, rt, 512), lambda b, i: (b, 0, 0)),
                  pl.BlockSpec((rt, nbp), lambda b, i: (0, 0))],
        out_specs=[pl.BlockSpec((1, tq, 512), lambda b, i: (b * nq + i, 0, 0)),
                   pl.BlockSpec((1, tq, 2 * nbp), lambda b, i: (b * nq + i, 0, 0))],
        out_shape=[jax.ShapeDtypeStruct((nbat * nq, tq, 512), F32),
                   jax.ShapeDtypeStruct((nbat * nq, tq, 2 * nbp), F32)],
        compiler_params=_params(("parallel", "parallel")),
        name="cmp_attend",
    )(qraw3, kvc, mimp)


def _online(s, v, m_s, l_s, acc, idx, v_transposed=False):
    m_old = m_s[idx]
    m_new = jnp.maximum(m_old, jnp.max(s, axis=-1, keepdims=True))
    alpha = jnp.exp(m_old - m_new)
    p = jnp.exp(s - m_new)
    l_s[idx] = alpha * l_s[idx] + jnp.sum(p, axis=-1, keepdims=True)
    pv = _dot_nt(p.astype(BF), v) if v_transposed else _dot(p.astype(BF), v)
    acc[idx] = alpha * acc[idx] + pv
    m_s[idx] = m_new


STEP_FIRST, STEP_LAST, STEP_MASKED = 1, 2, 4


def _step_table(nq, tq, tk, window=None):
    qi, kt, fl = [], [], []
    for i in range(nq):
        first = 0 if window is None else max(0, (i * tq - window + 1) // tk)
        last = (i * tq + tq - 1) // tk
        for j in range(first, last + 1):
            masked = j * tk + tk - 1 > i * tq
            if window is not None:
                masked = masked or j * tk <= i * tq + tq - 1 - window
            qi.append(i)
            kt.append(j)
            fl.append((STEP_FIRST if j == first else 0) | (STEP_LAST if j == last else 0)
                      | (STEP_MASKED if masked else 0))
    return (jnp.asarray(np.array(qi, np.int32)), jnp.asarray(np.array(kt, np.int32)),
            jnp.asarray(np.array(fl, np.int32)))


def _nsa_flash_body(qi_ref, kt_ref, fl_ref, *refs, mode, tq, tk):
    if mode == "sel":
        q_ref, selb_ref, k_ref, v_ref, o_ref, qs, m_s, l_s, acc = refs
    else:
        q_ref, k_ref, v_ref, o_ref, qs, m_s, l_s, acc = refs
    st = pl.program_id(1)
    i = qi_ref[st]
    jt = kt_ref[st]
    flags = fl_ref[st]
    q0 = i * tq
    rows = NSA_REP * tq
    lane = _lane_iota(tq)
    lo = lane < 64

    @pl.when((flags & STEP_FIRST) != 0)
    def _():
        m_s[...] = jnp.full(m_s.shape, NEG, F32)
        l_s[...] = jnp.zeros(l_s.shape, F32)
        acc[...] = jnp.zeros(acc.shape, F32)
        for g in range(NSA_GROUPS):
            if mode == "sel":
                sb = pltpu.roll(selb_ref[0, :, g * LANES:(g + 1) * LANES], 64, 1)
                sb = jnp.where((lane >= 64) & (lane < 96), sb, 0.0)
            for r in range(NSA_REP):
                h = g * NSA_REP + r
                q = q_ref[0, :, h * LANES:(h + 1) * LANES]
                if mode == "sel":
                    q = (q.astype(F32) + sb).astype(BF)
                qs[g, r * tq:(r + 1) * tq, :] = q

    need_mask = (flags & STEP_MASKED) != 0

    def step(masked):
        for g in range(NSA_GROUPS):
            s = _dot(qs[g], k_ref[0, g])
            if masked:
                rpos = q0 + (lax.broadcasted_iota(I32, (rows, tk), 0) & (tq - 1))
                kpos = jt * tk + lax.broadcasted_iota(I32, (rows, tk), 1)
                vis = kpos <= rpos
                if mode == "win":
                    vis = vis & (kpos > rpos - WINDOW)
                s = jnp.where(vis, s, NEG)
            _online(s, v_ref[0], m_s, l_s, acc, g, v_transposed=True)

    @pl.when(need_mask)
    def _():
        step(True)

    @pl.when(jnp.logical_not(need_mask))
    def _():
        step(False)

    @pl.when((flags & STEP_LAST) != 0)
    def _():
        for g in range(NSA_GROUPS):
            on = acc[g] / l_s[g]
            for c2 in range(2):
                a = on[(2 * c2) * tq:(2 * c2 + 1) * tq]
                b = on[(2 * c2 + 1) * tq:(2 * c2 + 2) * tq]
                o_ref[0, :, (g * 2 + c2) * LANES:(g * 2 + c2 + 1) * LANES] = _pair(lo, a, b, g)


def _nsa_flash(mode, q3, selb3, k2, v2, nbat, t, tq, tk):
    nq = t // tq
    rows = NSA_REP * tq
    table = _step_table(nq, tq, tk, None if mode == "sel" else WINDOW)
    qmap = lambda b, s, qi, kt, fl: (b * nq + qi[s], 0, 0)
    in_specs = [pl.BlockSpec((1, tq, 1024), qmap)]
    args = [q3]
    if mode == "sel":
        in_specs.append(pl.BlockSpec((1, tq, 2 * LANES), qmap))
        args.append(selb3)
    in_specs += [pl.BlockSpec((1, NSA_GROUPS, LANES, tk), lambda b, s, qi, kt, fl: (b, 0, 0, kt[s])),
                 pl.BlockSpec((1, LANES, tk), lambda b, s, qi, kt, fl: (b, 0, kt[s]))]
    args += [k2, v2]
    return pl.pallas_call(
        functools.partial(_nsa_flash_body, mode=mode, tq=tq, tk=tk),
        grid_spec=pltpu.PrefetchScalarGridSpec(
            num_scalar_prefetch=3, grid=(nbat, int(table[0].shape[0])), in_specs=in_specs,
            out_specs=pl.BlockSpec((1, tq, 512), qmap),
            scratch_shapes=[pltpu.VMEM((NSA_GROUPS, rows, LANES), BF), pltpu.VMEM((NSA_GROUPS, rows, 1), F32),
                            pltpu.VMEM((NSA_GROUPS, rows, 1), F32), pltpu.VMEM((NSA_GROUPS, rows, LANES), F32)]),
        out_shape=jax.ShapeDtypeStruct((nbat * nq, tq, 512), F32),
        compiler_params=_params(("parallel", "arbitrary")),
        name="nsa_flash_" + mode,
    )(*table, *args)


def _diff_lambda(dl, lambda_init):
    a = jnp.sum(dl[0:1] * dl[1:2], axis=1, keepdims=True)
    b = jnp.sum(dl[2:3] * dl[3:4], axis=1, keepdims=True)
    return jnp.exp(a) - jnp.exp(b) + lambda_init


def _diff_finish(a0, a1, lam, subln, lambda_init):
    o = a0 - lam * a1
    o = o * lax.rsqrt(jnp.mean(o * o, axis=-1, keepdims=True) + RMS_EPS) * subln
    return o * (1.0 - lambda_init)


def _diff_flash_body(qi_ref, kt_ref, fl_ref, q_ref, k_ref, v_ref, dl_ref, sub_ref, o_ref, qs, m_s, l_s, acc, *,
                     tq, tk, lambda_init):
    st = pl.program_id(1)
    i = qi_ref[st]
    j = kt_ref[st]
    flags = fl_ref[st]
    q0 = i * tq
    rows = 2 * tq
    lo = _lane_iota(tq) < 64

    @pl.when((flags & STEP_FIRST) != 0)
    def _():
        m_s[...] = jnp.full(m_s.shape, NEG, F32)
        l_s[...] = jnp.zeros(l_s.shape, F32)
        acc[...] = jnp.zeros(acc.shape, F32)
        for h in range(DIFF_HEADS):
            q = q_ref[0, :, h * LANES:(h + 1) * LANES].astype(F32)
            qs[h, 0:tq, :] = jnp.where(lo, q, 0.0).astype(BF)
            qs[h, tq:2 * tq, :] = jnp.where(lo, 0.0, q).astype(BF)

    need_mask = (flags & STEP_MASKED) != 0

    def step(masked):
        for h in range(DIFF_HEADS):
            s = _dot(qs[h], k_ref[0, h * LANES:(h + 1) * LANES, :])
            if masked:
                rpos = q0 + (lax.broadcasted_iota(I32, (rows, tk), 0) & (tq - 1))
                kpos = j * tk + lax.broadcasted_iota(I32, (rows, tk), 1)
                s = jnp.where(kpos <= rpos, s, NEG)
            _online(s, v_ref[:, h * LANES:(h + 1) * LANES], m_s, l_s, acc, h)

    @pl.when(need_mask)
    def _():
        step(True)

    @pl.when(jnp.logical_not(need_mask))
    def _():
        step(False)

    @pl.when((flags & STEP_LAST) != 0)
    def _():
        lam = _diff_lambda(dl_ref[...], lambda_init)
        for h in range(DIFF_HEADS):
            on = acc[h] / l_s[h]
            o_ref[0, :, h * LANES:(h + 1) * LANES] = _diff_finish(on[0:tq], on[tq:2 * tq], lam, sub_ref[...], lambda_init)


def _diff_flash(q3, k2, v2, dl, subln, nbat, t, tq, tk, lambda_init):
    nq = t // tq
    nk = t // tk
    table = _step_table(nq, tq, tk)
    qmap = lambda b, s, qi, kt, fl: (b * nq + qi[s], 0, 0)
    kmap = lambda b, s, qi, kt, fl: (b, 0, kt[s])
    vmap = lambda b, s, qi, kt, fl: (b * nk + kt[s], 0)
    const = lambda b, s, qi, kt, fl: (0, 0)
    rows = 2 * tq
    return pl.pallas_call(
        functools.partial(_diff_flash_body, tq=tq, tk=tk, lambda_init=lambda_init),
        grid_spec=pltpu.PrefetchScalarGridSpec(
            num_scalar_prefetch=3, grid=(nbat, int(table[0].shape[0])),
            in_specs=[pl.BlockSpec((1, tq, 512), qmap), pl.BlockSpec((1, 512, tk), kmap),
                      pl.BlockSpec((tk, 512), vmap), pl.BlockSpec((4, HEAD_DIM), const),
                      pl.BlockSpec((1, DIFF_DV), const)],
            out_specs=pl.BlockSpec((1, tq, 512), qmap),
            scratch_shapes=[pltpu.VMEM((DIFF_HEADS, rows, LANES), BF), pltpu.VMEM((DIFF_HEADS, rows, 1), F32),
                            pltpu.VMEM((DIFF_HEADS, rows, 1), F32), pltpu.VMEM((DIFF_HEADS, rows, LANES), F32)]),
        out_shape=jax.ShapeDtypeStruct((nbat * nq, tq, 512), F32),
        compiler_params=_params(("parallel", "arbitrary")),
        name="diff_flash",
    )(*table, q3, k2, v2, dl, subln)


def _paged_body(pt_ref, *refs, mode, n_pages, past_len, lambda_init):
    q_ref = refs[0]
    k = 1
    if mode == "sel":
        selb_ref = refs[k]
        k += 1
    kpages = refs[k:k + n_pages]
    k += n_pages
    if mode == "diff":
        vpages = refs[k:k + n_pages]
        k += n_pages
    newk_ref = refs[k]
    k += 1
    if mode == "diff":
        newv_ref, dl_ref, sub_ref = refs[k:k + 3]
        k += 3
    o_ref = refs[k]
    k += 1
    if mode == "sel":
        qbd, bias, m_s, l_s, acc = refs[k:]
    else:
        qbd, m_s, l_s, acc = refs[k:]
    kw = WINDOW if mode == "win" else LANES
    st = pl.program_id(1)
    nq = 8
    rows = 64
    lane8 = _lane_iota(nq)
    lo8 = lane8 < 64
    t_row = lax.broadcasted_iota(I32, (rows, LANES), 0) & (nq - 1)
    col = lax.broadcasted_iota(I32, (rows, LANES), 1)

    def update(s, pv):
        m_old = m_s[0]
        m_new = jnp.maximum(m_old, jnp.max(s, axis=-1, keepdims=True))
        alpha = jnp.exp(m_old - m_new)
        p = jnp.exp(s - m_new)
        l_s[0] = alpha * l_s[0] + jnp.sum(p, axis=-1, keepdims=True)
        acc[0] = alpha * acc[0] + pv(p.astype(BF))
        m_s[0] = m_new

    def per_head(p, v_of_head):
        return jnp.concatenate([_dot(p[h * 16:(h + 1) * 16], v_of_head(h)) for h in range(DIFF_HEADS)], axis=0)

    @pl.when(st == 0)
    def _():
        m_s[0] = jnp.full((rows, 1), NEG, F32)
        l_s[0] = jnp.zeros((rows, 1), F32)
        acc[0] = jnp.zeros(acc.shape[1:], F32)
        if mode == "diff":
            qbd[...] = jnp.zeros(qbd.shape, F32)
            for h in range(DIFF_HEADS):
                q = q_ref[0, :, h * LANES:(h + 1) * LANES].astype(F32)
                qbd[(2 * h) * nq:(2 * h + 1) * nq, h * LANES:(h + 1) * LANES] = jnp.where(lo8, q, 0.0)
                qbd[(2 * h + 1) * nq:(2 * h + 2) * nq, h * LANES:(h + 1) * LANES] = jnp.where(lo8, 0.0, q)
        else:
            for h in range(NSA_HEADS):
                g = h // NSA_REP
                q = q_ref[0, :, h * LANES:(h + 1) * LANES].astype(F32)
                qbd[h * nq:(h + 1) * nq, :] = q if g == 0 else pltpu.roll(q, 64, 1)
                if mode == "sel":
                    bias[h * nq:(h + 1) * nq, :] = selb_ref[0, :, g * 256:(g + 1) * 256]
        kn = newk_ref[0]
        if mode == "diff":
            kk, vv = kn, newv_ref[0]
        else:
            kk, vv = kn[:, 0:LANES], kn[:, LANES:2 * LANES]
        kpad = jnp.concatenate([kk, jnp.zeros((LANES - nq, kk.shape[1]), F32)], axis=0).astype(BF)
        vpad = jnp.concatenate([vv, jnp.zeros((LANES - nq, vv.shape[1]), F32)], axis=0).astype(BF)
        s = _dot_nt(qbd[...].astype(BF), kpad)
        if mode == "sel":
            jnew = past_len // SEL_BLOCK
            s = s + bias[:, jnew:jnew + 1]
        s = jnp.where((col < nq) & (col <= t_row), s, NEG)
        if mode == "diff":
            update(s, lambda p: per_head(p, lambda h: vpad[:, h * LANES:(h + 1) * LANES]))
        else:
            update(s, lambda p: _dot(p, vpad))

    qb = qbd[...].astype(BF)
    nkeys = n_pages * kw
    if mode == "diff":
        kt = jnp.concatenate([kpages[i][0].reshape(4 * LANES, LANES).astype(BF) for i in range(n_pages)], axis=1)
    else:
        kt = jnp.concatenate([kpages[i][0, 0].reshape(LANES, kw).astype(BF) for i in range(n_pages)], axis=1)
    s = _dot(qb, kt)
    if mode == "sel":
        krow = lax.broadcasted_iota(I32, (nkeys, 256), 0)
        jbi = lax.broadcasted_iota(I32, (nkeys, 256), 1)
        ep = jnp.where(jbi == 2 * n_pages * st + jnp.right_shift(krow, 6), 1.0, 0.0).astype(BF)
        s = s + _dot_nt(bias[...].astype(BF), ep)
    if mode == "win":
        t_w = lax.broadcasted_iota(I32, (rows, nkeys), 0) & (nq - 1)
        s = jnp.where(lax.broadcasted_iota(I32, (rows, nkeys), 1) > t_w, s, NEG)
    if mode == "diff":
        update(s, lambda p: per_head(p, lambda h: jnp.concatenate(
            [vpages[i][0, :, h, :].astype(BF) for i in range(n_pages)], axis=0)))
    else:
        vt = jnp.concatenate([kpages[i][0, 1].reshape(LANES, kw).astype(BF) for i in range(n_pages)], axis=1)
        update(s, lambda p: _dot_nt(p, vt))

    @pl.when(st == pl.num_programs(1) - 1)
    def _():
        on = acc[0] / l_s[0]
        if mode == "diff":
            lam = _diff_lambda(dl_ref[...], lambda_init)
            for h in range(DIFF_HEADS):
                a0 = on[h * 16:h * 16 + nq]
                a1 = on[h * 16 + nq:h * 16 + 2 * nq]
                o_ref[0, :, h * LANES:(h + 1) * LANES] = _diff_finish(a0, a1, lam, sub_ref[...], lambda_init)
        else:
            for c in range(4):
                a = on[(2 * c) * nq:(2 * c + 1) * nq]
                b = on[(2 * c + 1) * nq:(2 * c + 2) * nq]
                o_ref[0, :, c * LANES:(c + 1) * LANES] = _pair(lo8, a, b, c // 2)


def _paged_attend(mode, q3, selb3, pool_k, pool_v, page_table, newk3, newv3, dl, subln, n_pages, past_len, lambda_init):
    nbat, pages = page_table.shape
    steps = pages // n_pages
    wq = q3.shape[2]
    wk = newk3.shape[2]
    bmap = lambda b, s, pt: (b, 0, 0)
    c2 = lambda b, s, pt: (0, 0)

    def pmap(i, rank):
        return lambda b, s, pt: (pt[b, s * n_pages + i],) + (0,) * (rank - 1)

    in_specs = [pl.BlockSpec((1, 8, wq), bmap)]
    args = [q3]
    if mode == "sel":
        in_specs.append(pl.BlockSpec((1, 8, 512), bmap))
        args.append(selb3)
    kblock = (1,) + pool_k.shape[1:]
    in_specs += [pl.BlockSpec(kblock, pmap(i, len(kblock))) for i in range(n_pages)]
    args += [pool_k] * n_pages
    if mode == "diff":
        in_specs += [pl.BlockSpec((1, LANES, DIFF_HEADS, DIFF_DV), pmap(i, 4)) for i in range(n_pages)]
        args += [pool_v] * n_pages
    in_specs.append(pl.BlockSpec((1, 8, wk), bmap))
    args.append(newk3)
    if mode == "diff":
        in_specs += [pl.BlockSpec((1, 8, 512), bmap), pl.BlockSpec((4, HEAD_DIM), c2), pl.BlockSpec((1, DIFF_DV), c2)]
        args += [newv3, dl, subln]
    ck = 512 if mode == "diff" else LANES
    scratch = [pltpu.VMEM((64, ck), F32)]
    if mode == "sel":
        scratch.append(pltpu.VMEM((64, 256), F32))
    scratch += [pltpu.VMEM((1, 64, 1), F32), pltpu.VMEM((1, 64, 1), F32), pltpu.VMEM((1, 64, LANES), F32)]
    return pl.pallas_call(
        functools.partial(_paged_body, mode=mode, n_pages=n_pages, past_len=past_len, lambda_init=lambda_init),
        grid_spec=pltpu.PrefetchScalarGridSpec(
            num_scalar_prefetch=1, grid=(nbat, steps), in_specs=in_specs,
            out_specs=pl.BlockSpec((1, 8, 512), bmap), scratch_shapes=scratch),
        out_shape=jax.ShapeDtypeStruct((nbat, 8, 512), F32),
        compiler_params=_params(("parallel", "arbitrary")),
        name="paged_" + mode,
    )(page_table, *args)


def _merge_body(x_ref, oc_ref, os_ref, ow_ref, od_ref, gx_ref, nmix_ref, nffn_ref, wg_ref, wbn_ref, wbd_ref, wo_ref,
                wr_ref, br_ref, x1_ref, h_ref, idx_ref, gate_ref):
    x = x_ref[...]
    tm = x.shape[0]
    xn = (x * lax.rsqrt(jnp.mean(x * x, axis=-1, keepdims=True) + RMS_EPS) * nmix_ref[...]).astype(BF)
    gx = gx_ref[...]
    o_nsa = gx[:, 0:512] * oc_ref[...] + gx[:, 512:1024] * os_ref[...] + gx[:, 1024:1536] * ow_ref[...]
    y_nsa = _dot(o_nsa.astype(BF), wbn_ref[...])
    y_diff = _dot(od_ref[...].astype(BF), wbd_ref[...])
    gates = jax.nn.sigmoid(_dot(xn, wg_ref[...]))
    mrg = gates[:, 0:D_MODEL] * y_nsa + gates[:, D_MODEL:2 * D_MODEL] * y_diff
    x1 = x + _dot(mrg.astype(BF), wo_ref[...])
    x1_ref[...] = x1
    h = x1 * lax.rsqrt(jnp.mean(x1 * x1, axis=-1, keepdims=True) + RMS_EPS) * nffn_ref[...]
    h_ref[...] = h
    logits = _dot_hi(h, wr_ref[...]) + br_ref[...]
    lane = _lane_iota(tm)
    vals, idxs = [], []
    for _ in range(TOP_K):
        m = jnp.max(logits, axis=-1, keepdims=True)
        idx = jnp.min(jnp.where(logits == m, lane, LANES), axis=-1, keepdims=True)
        vals.append(m)
        idxs.append(idx)
        logits = jnp.where(lane == idx, -jnp.inf, logits)
    es = [jnp.exp(v - vals[0]) for v in vals]
    den = es[0] + es[1] + es[2] + es[3]
    idx_out = jnp.zeros((tm, LANES), I32)
    gate_out = jnp.zeros((tm, LANES), F32)
    for k in range(TOP_K):
        idx_out = jnp.where(lane == k, idxs[k], idx_out)
        gate_out = jnp.where(lane == k, es[k] / den, gate_out)
    idx_ref[...] = idx_out
    gate_ref[...] = gate_out


def _merge(x2d, oc, osel, ow, od, gx, nmix, nffn, wg, wbn, wbd, wo, wr, br, tm):
    n = x2d.shape[0]
    row = lambda i: (i, 0)
    const = lambda i: (0, 0)
    return pl.pallas_call(
        _merge_body,
        grid=(n // tm,),
        in_specs=[pl.BlockSpec((tm, D_MODEL), row)] + [pl.BlockSpec((tm, 512), row)] * 4 +
                 [pl.BlockSpec((tm, 1536), row), pl.BlockSpec((1, D_MODEL), const), pl.BlockSpec((1, D_MODEL), const),
                  pl.BlockSpec((D_MODEL, 2 * D_MODEL), const), pl.BlockSpec((512, D_MODEL), const),
                  pl.BlockSpec((512, D_MODEL), const), pl.BlockSpec((D_MODEL, D_MODEL), const),
                  pl.BlockSpec((D_MODEL, LANES), const), pl.BlockSpec((1, LANES), const)],
        out_specs=[pl.BlockSpec((tm, D_MODEL), row), pl.BlockSpec((tm, D_MODEL), row),
                   pl.BlockSpec((tm, LANES), row), pl.BlockSpec((tm, LANES), row)],
        out_shape=[jax.ShapeDtypeStruct((n, D_MODEL), F32), jax.ShapeDtypeStruct((n, D_MODEL), F32),
                   jax.ShapeDtypeStruct((n, LANES), I32), jax.ShapeDtypeStruct((n, LANES), F32)],
        compiler_params=_params(("parallel",)),
        name="merge",
    )(x2d, oc, osel, ow, od, gx, nmix, nffn, wg, wbn, wbd, wo, wr, br)


def _route_body(idx_ref, dest_ref, meta_ref, cnt_s, base_s, *, tm, tmoe, nbp):
    ph = pl.program_id(0)
    i = pl.program_id(1)
    shift = int(math.log2(tmoe))
    idx_t = jnp.transpose(idx_ref[...])
    sub = lax.broadcasted_iota(I32, (LANES, tm), 0)
    ohs = [idx_t[k:k + 1, :] == sub for k in range(TOP_K)]
    tok = jnp.zeros((LANES, tm), F32)
    for oh in ohs:
        tok = tok + jnp.where(oh, 1.0, 0.0)
    tile_cnt = jnp.broadcast_to(jnp.sum(tok, axis=1, keepdims=True), (LANES, LANES))

    @pl.when((ph == 0) & (i == 0))
    def _():
        cnt_s[...] = jnp.zeros((LANES, LANES), F32)

    @pl.when(ph == 0)
    def _():
        cnt_s[...] = cnt_s[...] + tile_cnt

    @pl.when((ph == 1) & (i == 0))
    def _():
        cnt = cnt_s[...].astype(I32)
        padded = jnp.left_shift(jnp.right_shift(cnt + (tmoe - 1), shift), shift)
        er = lax.broadcasted_iota(I32, (LANES, LANES), 0)
        ec = lax.broadcasted_iota(I32, (LANES, LANES), 1)
        pad_start = _dot_hi(jnp.where(ec < er, 1.0, 0.0), padded.astype(F32))
        base_s[...] = pad_start
        pad_end = pad_start.astype(I32) + padded
        blk_start = lax.broadcasted_iota(I32, (LANES, nbp), 1) * tmoe
        ends = jnp.broadcast_to(pad_end[:, 0:1], (LANES, nbp))
        be = jnp.minimum(jnp.sum(jnp.where(ends <= blk_start, 1, 0), axis=0, keepdims=True), N_EXPERTS - 1)
        n_used = jnp.right_shift(jnp.max(pad_end, axis=0, keepdims=True)[:, 0:1], shift)
        last_blk = jnp.sum(jnp.where(er == ec, jnp.where(padded > 0, jnp.right_shift(pad_end, shift) - 1, -1), 0),
                           axis=0, keepdims=True)
        lb = jnp.concatenate([last_blk, jnp.full((1, nbp - LANES), -1, I32)], axis=1)
        row = lax.broadcasted_iota(I32, (8, nbp), 0)
        meta_ref[...] = jnp.where(row == 0, be, jnp.where(row == 1, n_used, jnp.where(row == 2, lb, 0)))

    @pl.when(ph == 1)
    def _():
        nr = lax.broadcasted_iota(I32, (tm, tm), 0)
        nc = lax.broadcasted_iota(I32, (tm, tm), 1)
        before = _dot(tok.astype(BF), jnp.where(nr < nc, 1.0, 0.0).astype(BF))
        pos = before + base_s[:, 0:1]
        rows = [jnp.sum(jnp.where(oh, pos, 0.0), axis=0, keepdims=True) for oh in ohs]
        dest_ref[...] = jnp.concatenate(rows + [jnp.zeros((8 - TOP_K, tm), F32)], axis=0).astype(I32)
        base_s[...] = base_s[...] + tile_cnt


def _route(idx_all, tm, tmoe, nbp):
    n_tok = idx_all.shape[0]
    return pl.pallas_call(
        functools.partial(_route_body, tm=tm, tmoe=tmoe, nbp=nbp),
        grid=(2, n_tok // tm),
        in_specs=[pl.BlockSpec((tm, LANES), lambda ph, i: (i, 0))],
        out_specs=[pl.BlockSpec((8, tm), lambda ph, i: (0, i * ph)), pl.BlockSpec((8, nbp), lambda ph, i: (0, 0))],
        out_shape=[jax.ShapeDtypeStruct((8, n_tok), I32), jax.ShapeDtypeStruct((8, nbp), I32)],
        scratch_shapes=[pltpu.VMEM((LANES, LANES), F32), pltpu.VMEM((LANES, LANES), F32)],
        compiler_params=_params(("arbitrary", "arbitrary")),
        name="route",
    )(idx_all)


def _dispatch_body(dest_ref, lb_ref, nu_ref, h_ref, xs_hbm, zbuf, zsem, sem, *, tm, tmoe, n_tok):
    i = pl.program_id(0)

    @pl.when(i == 0)
    def _():
        zbuf[...] = jnp.zeros(zbuf.shape, F32)

        def fill(e):
            return pltpu.make_async_copy(zbuf, xs_hbm.at[pl.ds(pl.multiple_of(lb_ref[e] * tmoe, tmoe), tmoe)], zsem)
        for e in range(N_EXPERTS):
            @pl.when(lb_ref[e] >= 0)
            def _():
                fill(e).start()
        for e in range(N_EXPERTS):
            @pl.when(lb_ref[e] >= 0)
            def _():
                fill(e).wait()

        def tail(b):
            return pltpu.make_async_copy(zbuf, xs_hbm.at[pl.ds(pl.multiple_of(b * tmoe, tmoe), tmoe)], zsem)
        n_blk = xs_hbm.shape[0] // tmoe
        lax.fori_loop(nu_ref[0], n_blk, lambda b, c: (tail(b).start(), c)[1], 0)
        lax.fori_loop(nu_ref[0], n_blk, lambda b, c: (tail(b).wait(), c)[1], 0)

    def body(n, c):
        for k in range(TOP_K):
            d = dest_ref[k * n_tok + i * tm + n]
            pltpu.make_async_copy(h_ref.at[pl.ds(n, 1)], xs_hbm.at[pl.ds(d, 1)], sem).start(priority=k % 2)
        return c
    lax.fori_loop(0, tm, body, 0)
    pltpu.make_async_copy(xs_hbm.at[pl.ds(0, TOP_K * tm)], xs_hbm.at[pl.ds(0, TOP_K * tm)], sem).wait()


def _dispatch(dest_flat, last_blk, n_used, h_all, n_rows, tm, tmoe):
    n_tok = h_all.shape[0]
    return pl.pallas_call(
        functools.partial(_dispatch_body, tm=tm, tmoe=tmoe, n_tok=n_tok),
        grid_spec=pltpu.PrefetchScalarGridSpec(
            num_scalar_prefetch=3, grid=(n_tok // tm,),
            in_specs=[pl.BlockSpec((tm, D_MODEL), lambda i, d, lb, nu: (i, 0))],
            out_specs=pl.BlockSpec(memory_space=pl.ANY),
            scratch_shapes=[pltpu.VMEM((tmoe, D_MODEL), F32), pltpu.SemaphoreType.DMA(()), pltpu.SemaphoreType.DMA(())]),
        out_shape=jax.ShapeDtypeStruct((n_rows, D_MODEL), F32),
        compiler_params=pltpu.CompilerParams(dimension_semantics=("arbitrary",), vmem_limit_bytes=VMEM_LIMIT,
                                             disable_bounds_checks=True),
        name="moe_dispatch",
    )(dest_flat, last_blk, n_used, h_all)


def _moe_body(be_ref, nused_ref, x_ref, wgu_ref, bgu_ref, wd_ref, bd_ref, o_ref, wgu_s, wd_s):
    i = pl.program_id(0)
    e = be_ref[i]
    prev = be_ref[jnp.maximum(i - 1, 0)]

    @pl.when((i == 0) | (e != prev))
    def _():
        def cast_rows(c, carry):
            r0 = pl.multiple_of(c * LANES, LANES)
            wgu_s[pl.ds(r0, LANES), :] = wgu_ref[0, pl.ds(r0, LANES), :].astype(BF)
            wd_s[pl.ds(r0, LANES), :] = wd_ref[0, pl.ds(r0, LANES), :].astype(BF)
            return carry
        lax.fori_loop(0, D_MODEL // LANES, cast_rows, 0)

    @pl.when(i < nused_ref[0])
    def _():
        gu = _dot(x_ref[...].astype(BF), wgu_s[...]) + bgu_ref[0]
        gate = jnp.minimum(gu[:, 0:D_FF], SWIGLU_LIMIT)
        up = jnp.clip(gu[:, D_FF:2 * D_FF], -SWIGLU_LIMIT, SWIGLU_LIMIT)
        act = (up + 1.0) * gate * jax.nn.sigmoid(SWIGLU_ALPHA * gate)
        o_ref[...] = _dot(act.astype(BF), wd_s[...]) + bd_ref[0]

    @pl.when(i >= nused_ref[0])
    def _():
        o_ref[...] = jnp.zeros(o_ref.shape, F32)


def _moe(block_expert, n_used, xs, wgu, bgu, wd, bd, tmoe):
    n_rows = xs.shape[0]
    nblk = n_rows // tmoe
    xmap = lambda i, be, nu: (jnp.minimum(i, jnp.maximum(nu[0] - 1, 0)), 0)
    emap = lambda i, be, nu: (be[i], 0, 0)
    return pl.pallas_call(
        _moe_body,
        grid_spec=pltpu.PrefetchScalarGridSpec(
            num_scalar_prefetch=2, grid=(nblk,),
            in_specs=[pl.BlockSpec((tmoe, D_MODEL), xmap),
                      pl.BlockSpec((1, D_MODEL, 2 * D_FF), emap), pl.BlockSpec((1, 1, 2 * D_FF), emap),
                      pl.BlockSpec((1, D_FF, D_MODEL), emap), pl.BlockSpec((1, 1, D_MODEL), emap)],
            out_specs=pl.BlockSpec((tmoe, D_MODEL), lambda i, be, nu: (i, 0)),
            scratch_shapes=[pltpu.VMEM((D_MODEL, 2 * D_FF), BF), pltpu.VMEM((D_FF, D_MODEL), BF)]),
        out_shape=jax.ShapeDtypeStruct((n_rows, D_MODEL), F32),
        compiler_params=_params(("arbitrary",)),
        name="moe_experts",
    )(block_expert, n_used, xs, wgu, bgu, wd, bd)


def _final_body(dest_ref, x1_ref, gate_ref, nf_ref, ys_hbm, y_ref, ybuf, sem, *, tm, n_tok, block0):
    i = pl.program_id(0)

    def gather(tile, sl):
        def body(n, c):
            for k in range(TOP_K):
                d = dest_ref[k * n_tok + (block0 + tile) * tm + n]
                pltpu.make_async_copy(ys_hbm.at[pl.ds(d, 1)], ybuf.at[sl, k, pl.ds(n, 1)],
                                      sem.at[sl]).start(priority=k % 2)
            return c
        lax.fori_loop(0, tm, body, 0)

    @pl.when(i == 0)
    def _():
        gather(0, 0)

    @pl.when(i + 1 < pl.num_programs(0))
    def _():
        gather(i + 1, (i + 1) % 2)

    slot = i % 2
    pltpu.make_async_copy(ybuf.at[slot], ybuf.at[slot], sem.at[slot]).wait()
    gate = gate_ref[...]
    x = x1_ref[...]
    for k in range(TOP_K):
        x = x + gate[:, k:k + 1] * ybuf[slot, k]
    y_ref[...] = x * lax.rsqrt(jnp.mean(x * x, axis=-1, keepdims=True) + RMS_EPS) * nf_ref[...]


def _final(dest_flat, x1, gates, nf, ys, tm, n_tok, block0):
    n = x1.shape[0]
    row = lambda i, d: (i, 0)
    return pl.pallas_call(
        functools.partial(_final_body, tm=tm, n_tok=n_tok, block0=block0),
        grid_spec=pltpu.PrefetchScalarGridSpec(
            num_scalar_prefetch=1, grid=(n // tm,),
            in_specs=[pl.BlockSpec((tm, D_MODEL), row), pl.BlockSpec((tm, LANES), row),
                      pl.BlockSpec((1, D_MODEL), lambda i, d: (0, 0)), pl.BlockSpec(memory_space=pl.ANY)],
            out_specs=pl.BlockSpec((tm, D_MODEL), row),
            scratch_shapes=[pltpu.VMEM((2, TOP_K, tm, D_MODEL), F32), pltpu.SemaphoreType.DMA((2,))]),
        out_shape=jax.ShapeDtypeStruct((n, D_MODEL), F32),
        compiler_params=pltpu.CompilerParams(dimension_semantics=("arbitrary",), vmem_limit_bytes=VMEM_LIMIT,
                                             disable_bounds_checks=True),
        name="final_norm",
    )(dest_flat, x1, gates, nf, ys)


def _rope_tables(pos, nb_lanes):
    half = ROT_DIM // 2
    inv = ROPE_THETA ** (-jnp.arange(half, dtype=F32) / half)
    ang = pos.astype(F32)[:, None] * inv[None, :]
    cos, sin = jnp.cos(ang), jnp.sin(ang)
    t = pos.shape[0]
    c64 = jnp.concatenate([cos, cos, jnp.ones((t, HEAD_DIM - ROT_DIM), F32)], axis=1)
    s64 = jnp.concatenate([-sin, sin, jnp.zeros((t, HEAD_DIM - ROT_DIM), F32)], axis=1)
    lane = jnp.arange(LANES)[None, :]
    e = ((lane >= 64) & (lane < 64 + nb_lanes) & ((pos // SEL_BLOCK)[:, None] == lane - 64)).astype(F32)
    return jnp.tile(c64, (1, 2)), jnp.tile(s64, (1, 2)), e


def _imp_matrix(rt, n_cmp, nb, nbp):
    m = np.zeros((rt, nbp), np.float32)
    sub = SEL_BLOCK // CMP_STRIDE
    for n in range(n_cmp):
        for shift in range(CMP_BLOCK // CMP_STRIDE):
            j = (n + shift) // sub
            if j < nb:
                m[n, j] += 1.0
    return jnp.asarray(m)


def kernel(x_prompt, x_sample, cache_cmp_kv, cache_sel_kv, state_win_kv, cache_diff_k, cache_diff_v, page_table, norm_mix, w_in, cmp_pe, w_cmp1, w_cmp2, diff_lambda, diff_subln, w_br_nsa, w_br_diff, w_out, norm_ffn, w_router, b_router, w_gate_up, b_gate_up, w_down, b_down, norm_final):
    bp, t, _ = x_prompt.shape
    bs, ts, _ = x_sample.shape
    assert ts == 8 and t % 256 == 0 and t <= 2048 and w_in.shape[0] == 1
    n_pool, page = cache_cmp_kv.shape[1], cache_cmp_kv.shape[2]
    pages = page_table.shape[1]
    past_len = pages * page
    assert page == LANES and past_len % SEL_BLOCK == 0
    lambda_init = 0.8 - 0.6 * math.exp(-0.3 * 0)
    n_p, n_s = bp * t, bs * ts

    w = w_in[0]
    gcols = w[:, 1280:1304].reshape(D_MODEL, NSA_HEADS, 3).transpose(0, 2, 1)
    gexp = jnp.broadcast_to(gcols[..., None], (D_MODEL, 3, NSA_HEADS, HEAD_DIM)).reshape(D_MODEL, 1536)
    w_proj = jnp.concatenate([w[:, 0:1280], w[:, 1304:2840], gexp], axis=1).astype(BF)
    w_gates = w[:, 2840:4888].astype(BF)
    nmix = norm_mix[0][None, :]
    nffn = norm_ffn[0][None, :]
    pe = cmp_pe[0]
    pea = pe[:, :CMP_STRIDE].reshape(2, 1, 1024)
    peb = pe[:, CMP_STRIDE:].reshape(2, 1, 1024)
    w1a = w_cmp1[0][:, :1024].astype(BF)
    w1b = w_cmp1[0][:, 1024:].astype(BF)
    w2p = jnp.pad(w_cmp2[0], ((0, 0), (0, 0), (0, LANES - HEAD_DIM))).astype(BF)
    wbn = w_br_nsa[0].astype(BF)
    wbd = w_br_diff[0].astype(BF)
    wo = w_out[0].astype(BF)
    wr = jnp.pad(w_router[0], ((0, 0), (0, LANES - N_EXPERTS)))
    br = jnp.concatenate([b_router[0], jnp.full((LANES - N_EXPERTS,), NEG, F32)])[None, :]
    dl = diff_lambda[0]
    subln = diff_subln[0][None, :]

    pos_p = jnp.arange(t, dtype=I32)
    pos_s = jnp.tile(past_len + jnp.arange(ts, dtype=I32), bs)
    tm = 256
    w_rows = jnp.concatenate([w[:, 0:768], w[:, 1304:1816], w[:, 2328:2840], gexp], axis=1).astype(BF)
    w_cols = jnp.concatenate([w[:, 512:1280], w[:, 1816:2328]], axis=1).T.astype(BF)
    cos_p, sin_p, _ = _rope_tables(pos_p, 0)
    outs_p = _inproj_t(x_prompt.reshape(n_p, D_MODEL), nmix, w_rows, w_cols, cos_p, sin_p,
                       cos_p[:, 0:8].T, -sin_p[:, 0:8].T, bp, t, tm)
    outs_s = _inproj(x_sample.reshape(n_s, D_MODEL), nmix, w_proj, *_rope_tables(pos_s, 0), n_s)
    (qraw_p, qrot_p, kvc_p, kvct_p, kvst_p, kvwt_p, ksat_p, kwat_p, vst_p, vwt_p, qd_p, kdt_p, kdtb_p, vd4_p, vdb_p,
     gx_p) = outs_p
    (qraw_s, qrot_s, kvc_s, kvs_s, kvw_s, _, _, _, _, qd_s, kd_s, _, vd_s, _, gx_s) = outs_s

    rt_p = t // CMP_STRIDE
    n_cmp_p = (t - CMP_BLOCK) // CMP_STRIDE + 1
    nb_p = t // SEL_BLOCK
    kvcmp_p = _compress(kvc_p.reshape(bp, rt_p, 4096), jnp.arange(bp, dtype=I32)[:, None], pea, peb, w1a, w1b, w2p, 1)
    tq = 128
    nq = t // tq
    ocmp_p, selb_p = _cmp_attend(qraw_p.reshape(bp * nq, tq, 1024), kvcmp_p, _imp_matrix(rt_p, n_cmp_p, nb_p, LANES),
                                 nq, nb_p, LANES, 0)
    qrot3 = qrot_p.reshape(bp * nq, tq, 1024)
    osel_p = _nsa_flash("sel", qrot3, selb_p, ksat_p, vst_p, bp, t, tq, 512)
    owin_p = _nsa_flash("win", qrot3, None, kwat_p, vwt_p, bp, t, tq, 256)
    odiff_p = _diff_flash(qd_p.reshape(bp * nq, tq, 512), kdtb_p, vdb_p, dl, subln, bp, t, tq, 512, lambda_init)

    n_cmp_s = (past_len + ts - CMP_BLOCK) // CMP_STRIDE + 1
    rt_s = past_len // CMP_STRIDE
    assert n_cmp_s <= rt_s
    nb_s = (past_len + ts + SEL_BLOCK - 1) // SEL_BLOCK
    assert nb_s <= 256
    ppg = 16 if pages % 16 == 0 else (8 if pages % 8 == 0 else 1)
    kvcmp_s = _compress(cache_cmp_kv[0].reshape(n_pool, page // CMP_STRIDE, 4096), page_table, pea, peb, w1a, w1b,
                        w2p, ppg)
    ocmp_s, selb_s = _cmp_attend(qraw_s.reshape(bs, ts, 1024), kvcmp_s, _imp_matrix(rt_s, n_cmp_s, nb_s, 256),
                                 1, nb_s, 256, past_len)
    qrot_s3 = qrot_s.reshape(bs, ts, 1024)
    sel_t = jnp.transpose(cache_sel_kv[0], (0, 2, 3, 4, 1))
    win_t = jnp.transpose(state_win_kv[0], (0, 2, 3, 4, 1))
    dk_t = jnp.transpose(cache_diff_k[0], (0, 2, 3, 4, 1))
    osel_s = _paged_attend("sel", qrot_s3, selb_s, sel_t, None, page_table,
                           kvs_s.reshape(bs, ts, 256), None, None, None, ppg, past_len, lambda_init)
    wbuf = state_win_kv.shape[2]
    assert wbuf == WINDOW
    owin_s = _paged_attend("win", qrot_s3, None, win_t, None, jnp.arange(bs, dtype=I32)[:, None],
                           kvw_s.reshape(bs, ts, 256), None, None, None, 1, past_len, lambda_init)
    odiff_s = _paged_attend("diff", qd_s.reshape(bs, ts, 512), None, dk_t, cache_diff_v[0], page_table,
                            kd_s.reshape(bs, ts, 512), vd_s.reshape(bs, ts, 512), dl, subln, ppg, past_len, lambda_init)

    x1_p, h_p, idx_p, gate_p = _merge(x_prompt.reshape(n_p, D_MODEL), ocmp_p.reshape(n_p, 512), osel_p.reshape(n_p, 512),
                                      owin_p.reshape(n_p, 512), odiff_p.reshape(n_p, 512), gx_p, nmix, nffn, w_gates,
                                      wbn, wbd, wo, wr, br, tm)
    x1_s, h_s, idx_s, gate_s = _merge(x_sample.reshape(n_s, D_MODEL), ocmp_s.reshape(n_s, 512), osel_s.reshape(n_s, 512),
                                      owin_s.reshape(n_s, 512), odiff_s.reshape(n_s, 512), gx_s, nmix, nffn, w_gates,
                                      wbn, wbd, wo, wr, br, n_s)

    n_tok = n_p + n_s
    n_assign = n_tok * TOP_K
    tmoe = 256
    assert n_p % n_s == 0 and n_tok % n_s == 0
    n_blocks = (n_assign + N_EXPERTS * (tmoe - 1) + tmoe - 1) // tmoe
    nbp = (n_blocks + LANES - 1) // LANES * LANES
    dest, meta = _route(jnp.concatenate([idx_p, idx_s], axis=0), n_s, tmoe, nbp)
    dest_flat = dest.reshape(-1)
    xs = _dispatch(dest_flat, meta[2, :N_EXPERTS], meta[1, :1], jnp.concatenate([h_p, h_s], axis=0), n_blocks * tmoe,
                   n_s, tmoe)
    ys = _moe(meta[0, :n_blocks], meta[1, :1], xs, w_gate_up[0], b_gate_up[0][:, None, :], w_down[0],
              b_down[0][:, None, :], tmoe)

    nf = norm_final[None, :]
    y_p = _final(dest_flat, x1_p, gate_p, nf, ys, n_s, n_tok, 0)
    y_s = _final(dest_flat, x1_s, gate_s, nf, ys, n_s, n_tok, n_p // n_s)

    g, hd = NSA_GROUPS, HEAD_DIM
    def from_t(a, heads, length):
        return jnp.transpose(a.reshape(bp, heads, 2, hd, length), (0, 4, 1, 2, 3))[None]

    new_win_p = from_t(kvwt_p[:, :, t - wbuf:], 2, wbuf)
    new_win_s = jnp.concatenate([state_win_kv[0].reshape(bs, wbuf, 256)[:, ts:], kvw_s.reshape(bs, ts, 256)], axis=1)
    return (y_p.reshape(bp, t, D_MODEL), y_s.reshape(bs, ts, D_MODEL),
            from_t(kvct_p, 2, t), kvc_s.reshape(1, bs, ts, 2, g, hd),
            from_t(kvst_p, 2, t), kvs_s.reshape(1, bs, ts, 2, g, hd),
            new_win_p, new_win_s.reshape(1, bs, wbuf, 2, g, hd),
            from_t(kdt_p, DIFF_HEADS, t), kd_s.reshape(1, bs, ts, DIFF_HEADS, 2, hd),
            vd4_p[None], vd_s.reshape(1, bs, ts, DIFF_HEADS, DIFF_DV))
```

```python
import functools
import math

import numpy as np
import jax
import jax.numpy as jnp
from jax import lax
from jax.experimental import pallas as pl
from jax.experimental.pallas import tpu as pltpu

D_MODEL = 1024
HEAD_DIM = 64
NSA_HEADS = 8
NSA_GROUPS = 2
NSA_REP = 4
CMP_BLOCK = 32
CMP_STRIDE = 16
SEL_BLOCK = 64
N_SEL = 16
WINDOW = 512
DIFF_HEADS = 4
DIFF_DV = 128
ROT_DIM = 16
ROPE_THETA = 500000.0
N_EXPERTS = 32
TOP_K = 4
D_FF = 1024
SWIGLU_LIMIT = 7.0
SWIGLU_ALPHA = 1.702
RMS_EPS = 1e-5
FORCE_BONUS = 1e4
SCALE = HEAD_DIM ** -0.5
LANES = 128

BF = jnp.bfloat16
F32 = jnp.float32
I32 = jnp.int32
NEG = -1e30
VMEM_LIMIT = 56 * 1024 * 1024

OFF_Q, OFF_KVC, OFF_KVS, OFF_KVW, OFF_QD, OFF_KD, OFF_VD, OFF_GX = 0, 512, 768, 1024, 1280, 1792, 2304, 2816
IN_COLS_K = 4352
TOFF_Q, TOFF_KVC, TOFF_QD, TOFF_VD, TOFF_GX, TOFF_KS, TOFF_KW, TOFF_KD = 0, 512, 768, 1280, 1792, 3328, 3456, 3584
TIN_COLS = 4096
TIN_ROWS = 1792


def _dot(a, b):
    return jnp.dot(a, b, preferred_element_type=F32)


def _dot_nt(a, b):
    return lax.dot_general(a, b, (((1,), (1,)), ((), ())), preferred_element_type=F32)


def _dot_hi(a, b):
    return jnp.dot(a, b, preferred_element_type=F32, precision=lax.Precision.HIGHEST)


def _params(sem):
    return pltpu.CompilerParams(dimension_semantics=sem, vmem_limit_bytes=VMEM_LIMIT)


def _lane_iota(rows):
    return lax.broadcasted_iota(I32, (rows, LANES), 1)


def _pair(lo, a, b, g):
    if g == 0:
        return jnp.where(lo, a, pltpu.roll(b, 64, 1))
    return jnp.where(lo, pltpu.roll(a, 64, 1), b)


def _inproj_body(x_ref, nw_ref, w_ref, cos_ref, sin_ref, e_ref,
                 qraw_ref, qrot_ref, kvc_ref, kvs_ref, kvw_ref, ksa_ref, kwa_ref, vsb_ref, vwb_ref,
                 qd_ref, kd_ref, kdb_ref, vd_ref, vdb_ref, gx_ref):
    x = x_ref[...]
    tm = x.shape[0]
    xn = (x * lax.rsqrt(jnp.mean(x * x, axis=-1, keepdims=True) + RMS_EPS) * nw_ref[...]).astype(BF)
    cos = cos_ref[...]
    sin = sin_ref[...]
    epat = e_ref[...]
    lane = _lane_iota(tm)
    lo = lane < 64
    first8 = (lane & 63) < 8

    def mm(off):
        return _dot(xn, w_ref[:, off:off + LANES])

    def rope(y):
        sw = jnp.where(first8, pltpu.roll(y, LANES - 8, 1), pltpu.roll(y, 8, 1))
        return y * cos + sw * sin

    for c in range(4):
        y = mm(OFF_Q + c * LANES) * SCALE
        yr = rope(y)
        ys = pltpu.roll(y, 64, 1)
        yrs = pltpu.roll(yr, 64, 1)
        qraw_ref[:, (2 * c) * LANES:(2 * c + 1) * LANES] = jnp.where(lo, y, 0.0).astype(BF)
        qraw_ref[:, (2 * c + 1) * LANES:(2 * c + 2) * LANES] = jnp.where(lo, ys, 0.0).astype(BF)
        qrot_ref[:, (2 * c) * LANES:(2 * c + 1) * LANES] = jnp.where(lo, yr, 0.0).astype(BF)
        qrot_ref[:, (2 * c + 1) * LANES:(2 * c + 2) * LANES] = jnp.where(lo, yrs, 0.0).astype(BF)

    for c in range(2):
        kvc_ref[:, c * LANES:(c + 1) * LANES] = mm(OFF_KVC + c * LANES)

    for off, kv_ref, ka_ref, vb_ref in ((OFF_KVS, kvs_ref, ksa_ref, vsb_ref), (OFF_KVW, kvw_ref, kwa_ref, vwb_ref)):
        kr = rope(mm(off))
        v = mm(off + LANES)
        kv_ref[:, 0:LANES] = kr
        kv_ref[:, LANES:2 * LANES] = v
        ka_ref[:, 0:LANES] = jnp.where(lo, kr, epat).astype(BF)
        ka_ref[:, LANES:2 * LANES] = jnp.where(lo, pltpu.roll(kr, 64, 1), epat).astype(BF)
        vb_ref[...] = v.astype(BF)

    for c in range(4):
        sl = slice(c * LANES, (c + 1) * LANES)
        qd_ref[:, sl] = rope(mm(OFF_QD + c * LANES) * SCALE).astype(BF)
        kr = rope(mm(OFF_KD + c * LANES))
        kd_ref[:, sl] = kr
        kdb_ref[:, sl] = kr.astype(BF)
        v = mm(OFF_VD + c * LANES)
        vd_ref[:, sl] = v
        vdb_ref[:, sl] = v.astype(BF)

    for c in range(12):
        gx_ref[:, c * LANES:(c + 1) * LANES] = jax.nn.sigmoid(mm(OFF_GX + c * LANES))


def _inproj(x2d, norm_w, w_bf, cos_t, sin_t, e_t, tm):
    n = x2d.shape[0]
    nt = cos_t.shape[0] // tm
    row = lambda i: (i, 0)
    tab = lambda i: (i % nt, 0)
    const = lambda i: (0, 0)
    outs = [(1024, BF), (1024, BF), (256, F32), (256, F32), (256, F32), (256, BF), (256, BF), (128, BF), (128, BF),
            (512, BF), (512, F32), (512, BF), (512, F32), (512, BF), (1536, F32)]
    return pl.pallas_call(
        _inproj_body,
        grid=(n // tm,),
        in_specs=[pl.BlockSpec((tm, D_MODEL), row), pl.BlockSpec((1, D_MODEL), const),
                  pl.BlockSpec((D_MODEL, IN_COLS_K), const),
                  pl.BlockSpec((tm, LANES), tab), pl.BlockSpec((tm, LANES), tab), pl.BlockSpec((tm, LANES), tab)],
        out_specs=[pl.BlockSpec((tm, w), row) for w, _ in outs],
        out_shape=[jax.ShapeDtypeStruct((n, w), dt) for w, dt in outs],
        compiler_params=_params(("parallel",)),
        name="inproj",
    )(x2d, norm_w, w_bf, cos_t, sin_t, e_t)


def _inproj_t_body(x_ref, nw_ref, w_ref, wt_ref, cos_ref, sin_ref, cost_ref, sint_ref,
                   qraw_ref, qrot_ref, kvc_ref, kvct_ref, kvst_ref, kvwt_ref, ksa_ref, kwa_ref, vst_ref, vwt_ref,
                   qd_ref, kdt_ref, kdb_ref, vd4_ref, vdtb_ref, gx_ref):
    x = x_ref[...]
    tm = x.shape[0]
    xn = (x * lax.rsqrt(jnp.mean(x * x, axis=-1, keepdims=True) + RMS_EPS) * nw_ref[...]).astype(BF)
    cos = cos_ref[...]
    sin = sin_ref[...]
    cos_t = cost_ref[...]
    sin_t = sint_ref[...]
    lane = _lane_iota(tm)
    lo = lane < 64
    first8 = (lane & 63) < 8

    def mm(off):
        return _dot(xn, w_ref[:, off:off + LANES])

    def mm_t(off, n):
        return _dot_nt(wt_ref[off:off + n, :], xn)

    def rope(y):
        sw = jnp.where(first8, pltpu.roll(y, LANES - 8, 1), pltpu.roll(y, 8, 1))
        return y * cos + sw * sin

    def rope_t(y):
        parts = []
        for hb in range(0, y.shape[0], HEAD_DIM):
            x1 = y[hb:hb + 8]
            x2 = y[hb + 8:hb + 16]
            parts += [x1 * cos_t - x2 * sin_t, x1 * sin_t + x2 * cos_t, y[hb + 16:hb + HEAD_DIM]]
        return jnp.concatenate(parts, axis=0)

    for c in range(4):
        y = mm(TOFF_Q + c * LANES) * SCALE
        yr = rope(y)
        ys = pltpu.roll(y, 64, 1)
        yrs = pltpu.roll(yr, 64, 1)
        qraw_ref[:, (2 * c) * LANES:(2 * c + 1) * LANES] = jnp.where(lo, y, 0.0).astype(BF)
        qraw_ref[:, (2 * c + 1) * LANES:(2 * c + 2) * LANES] = jnp.where(lo, ys, 0.0).astype(BF)
        qrot_ref[:, (2 * c) * LANES:(2 * c + 1) * LANES] = jnp.where(lo, yr, 0.0).astype(BF)
        qrot_ref[:, (2 * c + 1) * LANES:(2 * c + 2) * LANES] = jnp.where(lo, yrs, 0.0).astype(BF)

    for c in range(2):
        kvc_ref[:, c * LANES:(c + 1) * LANES] = mm(TOFF_KVC + c * LANES)
    kvct_ref[0] = mm_t(0, 256)

    blk = jnp.right_shift(pl.program_id(1) * tm + lax.broadcasted_iota(I32, (tm, LANES), 0), 6)
    epat = jnp.where((lane >= 64) & (lane < 96) & (blk == lane - 64), 1.0, 0.0)
    for off, koff, kvt_ref, ka_ref, vt_ref in ((256, TOFF_KS, kvst_ref, ksa_ref, vst_ref),
                                               (512, TOFF_KW, kvwt_ref, kwa_ref, vwt_ref)):
        y = mm_t(off, 256)
        v = y[128:256]
        kvt_ref[0, 0:128, :] = rope_t(y[0:128])
        kvt_ref[0, 128:256, :] = v
        vt_ref[0] = v.astype(BF)
        kr = rope(mm(koff))
        ka_ref[:, 0:LANES] = jnp.where(lo, kr, epat).astype(BF)
        ka_ref[:, LANES:2 * LANES] = jnp.where(lo, pltpu.roll(kr, 64, 1), epat).astype(BF)

    kdt_ref[0] = rope_t(mm_t(768, 512))
    vdtb_ref[0] = mm_t(1280, 512).astype(BF)
    for c in range(4):
        sl = slice(c * LANES, (c + 1) * LANES)
        qd_ref[:, sl] = rope(mm(TOFF_QD + c * LANES) * SCALE).astype(BF)
        kdb_ref[:, sl] = rope(mm(TOFF_KD + c * LANES)).astype(BF)
        vd4_ref[0, :, c, :] = mm(TOFF_VD + c * LANES)

    for c in range(12):
        gx_ref[:, c * LANES:(c + 1) * LANES] = jax.nn.sigmoid(mm(TOFF_GX + c * LANES))


def _inproj_t(x2d, norm_w, w_bf, wt_bf, cos_t, sin_t, cos_tt, sin_tt, nbat, t, tm):
    n = x2d.shape[0]
    nt = t // tm
    row = lambda b, i: (b * nt + i, 0)
    tab = lambda b, i: (i, 0)
    tab_t = lambda b, i: (0, i)
    const = lambda b, i: (0, 0)
    tr = lambda b, i: (b, 0, i)
    tr4 = lambda b, i: (b, 0, 0, i)
    rm = lambda w, dt: (pl.BlockSpec((tm, w), row), jax.ShapeDtypeStruct((n, w), dt))
    tp = lambda r, dt: (pl.BlockSpec((1, r, tm), tr), jax.ShapeDtypeStruct((nbat, r, t), dt))
    vd4 = (pl.BlockSpec((1, tm, DIFF_HEADS, DIFF_DV), lambda b, i: (b, i, 0, 0)),
           jax.ShapeDtypeStruct((nbat, t, DIFF_HEADS, DIFF_DV), F32))
    outs = [rm(1024, BF), rm(1024, BF), rm(256, F32), tp(256, F32), tp(256, F32), tp(256, F32), rm(256, BF),
            rm(256, BF), tp(128, BF), tp(128, BF), rm(512, BF), tp(512, F32), rm(512, BF), vd4, tp(512, BF),
            rm(1536, F32)]
    return pl.pallas_call(
        _inproj_t_body,
        grid=(nbat, nt),
        in_specs=[pl.BlockSpec((tm, D_MODEL), row), pl.BlockSpec((1, D_MODEL), const),
                  pl.BlockSpec((D_MODEL, TIN_COLS), const), pl.BlockSpec((TIN_ROWS, D_MODEL), const),
                  pl.BlockSpec((tm, LANES), tab), pl.BlockSpec((tm, LANES), tab),
                  pl.BlockSpec((8, tm), tab_t), pl.BlockSpec((8, tm), tab_t)],
        out_specs=[o[0] for o in outs],
        out_shape=[o[1] for o in outs],
        compiler_params=_params(("parallel", "parallel")),
        name="inproj_t",
    )(x2d, norm_w, w_bf, wt_bf, cos_t, sin_t, cos_tt, sin_tt)


def _compress_body(pt_ref, *refs, n_pages, rp, rt):
    page_refs = refs[:n_pages]
    pea_ref, peb_ref, w1a_ref, w1b_ref, w2_ref, out_ref, y_scr = refs[n_pages:]
    s = pl.program_id(1)
    for i in range(n_pages):
        x = page_refs[i][0]
        row0 = pl.multiple_of((s * n_pages + i) * rp, 8)
        for cg in range(4):
            ycg = jnp.concatenate(
                [x[:, l * 256 + cg * 64:l * 256 + cg * 64 + 64] for l in range(CMP_STRIDE)], axis=1)
            y_scr[cg, pl.ds(row0, rp), :] = ycg

    @pl.when(s == pl.num_programs(1) - 1)
    def _():
        for c in range(2):
            for g in range(2):
                cg = c * 2 + g
                y = y_scr[cg]
                za = _dot((y + pea_ref[c]).astype(BF), w1a_ref[c])
                zb = _dot((y + peb_ref[c]).astype(BF), w1b_ref[c])
                hid = jax.nn.gelu(za + pltpu.roll(zb, rt - 1, 0))
                out_ref[0, :, cg * LANES:(cg + 1) * LANES] = _dot(hid.astype(BF), w2_ref[c]).astype(BF)


def _compress(pool, page_table, pea, peb, w1a, w1b, w2p, n_pages):
    nb, pages = page_table.shape
    rp = pool.shape[1]
    rt = pages * rp
    steps = pages // n_pages
    page_specs = [pl.BlockSpec((1, rp, 4096), functools.partial(
        lambda b, s, pt, i: (pt[b, s * n_pages + i], 0, 0), i=i)) for i in range(n_pages)]
    c3 = lambda b, s, pt: (0, 0, 0)
    return pl.pallas_call(
        functools.partial(_compress_body, n_pages=n_pages, rp=rp, rt=rt),
        grid_spec=pltpu.PrefetchScalarGridSpec(
            num_scalar_prefetch=1,
            grid=(nb, steps),
            in_specs=page_specs + [pl.BlockSpec((2, 1, 1024), c3), pl.BlockSpec((2, 1, 1024), c3),
                                   pl.BlockSpec((2, 1024, LANES), c3), pl.BlockSpec((2, 1024, LANES), c3),
                                   pl.BlockSpec((2, LANES, LANES), c3)],
            out_specs=pl.BlockSpec((1, rt, 512), lambda b, s, pt: (b, 0, 0)),
            scratch_shapes=[pltpu.VMEM((4, rt, 1024), F32)]),
        out_shape=jax.ShapeDtypeStruct((nb, rt, 512), BF),
        compiler_params=_params(("parallel", "arbitrary")),
        name="compress",
    )(page_table, *([pool] * n_pages), pea, peb, w1a, w1b, w2p)


def _cmp_body(q_ref, kvc_ref, mimp_ref, ocmp_ref, selb_ref, *, tq, rt, nb, nbp, pos0):
    i = pl.program_id(1)
    pos_n = pos0 + i * tq + lax.broadcasted_iota(I32, (tq, rt), 0)
    n_i = lax.broadcasted_iota(I32, (tq, rt), 1)
    vis = (n_i * CMP_STRIDE + (CMP_BLOCK - 1)) <= pos_n
    lo = _lane_iota(tq) < 64
    pos_b = pos0 + i * tq + lax.broadcasted_iota(I32, (tq, nbp), 0)
    jb = lax.broadcasted_iota(I32, (tq, nbp), 1)
    cur = jnp.right_shift(pos_b, 6)
    valid = (jb * SEL_BLOCK <= pos_b) & (jb < nb)
    forced = (jb == 0) | (jb == cur) | (jb == cur - 1)
    mimp = mimp_ref[...]
    for g in range(NSA_GROUPS):
        kc = kvc_ref[0, :, g * LANES:(g + 1) * LANES]
        vc = kvc_ref[0, :, (2 + g) * LANES:(3 + g) * LANES]
        pg = jnp.zeros((tq, rt), F32)
        og = []
        for r in range(NSA_REP):
            h = g * NSA_REP + r
            q = q_ref[0, :, h * LANES:(h + 1) * LANES]
            s = jnp.where(vis, _dot_nt(q, kc), -jnp.inf)
            m = jnp.max(s, axis=-1, keepdims=True)
            m = jnp.where(m == -jnp.inf, 0.0, m)
            e = jnp.exp(s - m)
            d = jnp.sum(e, axis=-1, keepdims=True)
            p = e / jnp.where(d > 0, d, 1.0)
            og.append(_dot(p.astype(BF), vc))
            pg = pg + p
        imp = _dot_hi(pg, mimp)
        score = jnp.where(valid, jnp.where(forced, imp + FORCE_BONUS, imp), -jnp.inf)
        rank = jnp.zeros((tq, nbp), I32)
        for jj in range(nb):
            cj = score[:, jj:jj + 1]
            beats = (cj > score) | ((cj == score) & (jb > jj))
            rank = rank + beats.astype(I32)
        sel = (rank < N_SEL) & valid
        selb_ref[0, :, g * nbp:(g + 1) * nbp] = jnp.where(sel, 0.0, NEG)
        for c2 in range(2):
            ocmp_ref[0, :, (g * 2 + c2) * LANES:(g * 2 + c2 + 1) * LANES] = _pair(lo, og[2 * c2], og[2 * c2 + 1], 0)


def _cmp_attend(qraw3, kvc, mimp, nq, nb, nbp, pos0):
    nbat = kvc.shape[0]
    _, tq, _ = qraw3.shape
    rt = kvc.shape[1]
    return pl.pallas_call(
        functools.partial(_cmp_body, tq=tq, rt=rt, nb=nb, nbp=nbp, pos0=pos0),
        grid=(nbat, nq),
        in_specs=[pl.BlockSpec((1, tq, 1024), lambda b, i: (b * nq + i, 0, 0)),
                  pl.BlockSpec((1, rt, 512), lambda b, i: (b, 0, 0)),
                  pl.BlockSpec((rt, nbp), lambda b, i: (0, 0))],
        out_specs=[pl.BlockSpec((1, tq, 512), lambda b, i: (b * nq + i, 0, 0)),
                   pl.BlockSpec((1, tq, 2 * nbp), lambda b, i: (b * nq + i, 0, 0))],
        out_shape=[jax.ShapeDtypeStruct((nbat * nq, tq, 512), F32),
                   jax.ShapeDtypeStruct((nbat * nq, tq, 2 * nbp), F32)],
        compiler_params=_params(("parallel", "parallel")),
        name="cmp_attend",
    )(qraw3, kvc, mimp)


def _online_t(s, vt, m_s, l_s, acc, idx):
    m_old = m_s[idx]
    m_new = jnp.maximum(m_old, jnp.max(s, axis=0, keepdims=True))
    alpha = jnp.exp(m_old - m_new)
    p = jnp.exp(s - m_new)
    l_s[idx] = alpha * l_s[idx] + jnp.sum(p, axis=0, keepdims=True)
    acc[idx] = alpha * acc[idx] + _dot(vt, p.astype(BF))
    m_s[idx] = m_new


STEP_FIRST, STEP_LAST, STEP_MASKED = 1, 2, 4


def _step_table(nq, tq, tk, window=None):
    qi, kt, fl = [], [], []
    for i in range(nq):
        first = 0 if window is None else max(0, (i * tq - window + 1) // tk)
        last = (i * tq + tq - 1) // tk
        for j in range(first, last + 1):
            masked = j * tk + tk - 1 > i * tq
            if window is not None:
                masked = masked or j * tk <= i * tq + tq - 1 - window
            qi.append(i)
            kt.append(j)
            fl.append((STEP_FIRST if j == first else 0) | (STEP_LAST if j == last else 0)
                      | (STEP_MASKED if masked else 0))
    return (jnp.asarray(np.array(qi, np.int32)), jnp.asarray(np.array(kt, np.int32)),
            jnp.asarray(np.array(fl, np.int32)))


def _nsa_flash_body(qi_ref, kt_ref, fl_ref, *refs, mode, tq, tk):
    if mode == "sel":
        q_ref, selb_ref, k_ref, v_ref, o_ref, qs, m_s, l_s, acc = refs
    else:
        q_ref, k_ref, v_ref, o_ref, qs, m_s, l_s, acc = refs
    st = pl.program_id(1)
    i = qi_ref[st]
    jt = kt_ref[st]
    flags = fl_ref[st]
    q0 = i * tq
    rows = NSA_REP * tq
    lane = _lane_iota(tq)
    lo = lane < 64

    @pl.when((flags & STEP_FIRST) != 0)
    def _():
        m_s[...] = jnp.full(m_s.shape, NEG, F32)
        l_s[...] = jnp.zeros(l_s.shape, F32)
        acc[...] = jnp.zeros(acc.shape, F32)
        for g in range(NSA_GROUPS):
            if mode == "sel":
                sb = pltpu.roll(selb_ref[0, :, g * LANES:(g + 1) * LANES], 64, 1)
                sb = jnp.where((lane >= 64) & (lane < 96), sb, 0.0)
            for r in range(NSA_REP):
                h = g * NSA_REP + r
                q = q_ref[0, :, h * LANES:(h + 1) * LANES]
                if mode == "sel":
                    q = (q.astype(F32) + sb).astype(BF)
                qs[g, r * tq:(r + 1) * tq, :] = q

    need_mask = (flags & STEP_MASKED) != 0

    def step(masked):
        for g in range(NSA_GROUPS):
            s = _dot_nt(k_ref[:, g * LANES:(g + 1) * LANES], qs[g])
            if masked:
                kpos = jt * tk + lax.broadcasted_iota(I32, (tk, rows), 0)
                rpos = q0 + (lax.broadcasted_iota(I32, (tk, rows), 1) & (tq - 1))
                vis = kpos <= rpos
                if mode == "win":
                    vis = vis & (kpos > rpos - WINDOW)
                s = jnp.where(vis, s, NEG)
            _online_t(s, v_ref[0, g * HEAD_DIM:(g + 1) * HEAD_DIM, :], m_s, l_s, acc, g)

    @pl.when(need_mask)
    def _():
        step(True)

    @pl.when(jnp.logical_not(need_mask))
    def _():
        step(False)

    @pl.when((flags & STEP_LAST) != 0)
    def _():
        for g in range(NSA_GROUPS):
            on = acc[g] / l_s[g]
            for c2 in range(2):
                pair = jnp.concatenate([on[:, (2 * c2) * tq:(2 * c2 + 1) * tq],
                                        on[:, (2 * c2 + 1) * tq:(2 * c2 + 2) * tq]], axis=0)
                o_ref[0, :, (g * 2 + c2) * LANES:(g * 2 + c2 + 1) * LANES] = jnp.transpose(pair)


def _nsa_flash(mode, q3, selb3, k2, v2, nbat, t, tq, tk):
    nq = t // tq
    rows = NSA_REP * tq
    table = _step_table(nq, tq, tk, None if mode == "sel" else WINDOW)
    qmap = lambda b, s, qi, kt, fl: (b * nq + qi[s], 0, 0)
    in_specs = [pl.BlockSpec((1, tq, 1024), qmap)]
    args = [q3]
    if mode == "sel":
        in_specs.append(pl.BlockSpec((1, tq, 2 * LANES), qmap))
        args.append(selb3)
    nkv = t // tk
    in_specs += [pl.BlockSpec((tk, 2 * LANES), lambda b, s, qi, kt, fl: (b * nkv + kt[s], 0)),
                 pl.BlockSpec((1, LANES, tk), lambda b, s, qi, kt, fl: (b, 0, kt[s]))]
    args += [k2, v2]
    return pl.pallas_call(
        functools.partial(_nsa_flash_body, mode=mode, tq=tq, tk=tk),
        grid_spec=pltpu.PrefetchScalarGridSpec(
            num_scalar_prefetch=3, grid=(nbat, int(table[0].shape[0])), in_specs=in_specs,
            out_specs=pl.BlockSpec((1, tq, 512), qmap),
            scratch_shapes=[pltpu.VMEM((NSA_GROUPS, rows, LANES), BF), pltpu.VMEM((NSA_GROUPS, 1, rows), F32),
                            pltpu.VMEM((NSA_GROUPS, 1, rows), F32), pltpu.VMEM((NSA_GROUPS, HEAD_DIM, rows), F32)]),
        out_shape=jax.ShapeDtypeStruct((nbat * nq, tq, 512), F32),
        compiler_params=_params(("parallel", "arbitrary")),
        name="nsa_flash_" + mode,
    )(*table, *args)


def _diff_lambda(dl, lambda_init):
    a = jnp.sum(dl[0:1] * dl[1:2], axis=1, keepdims=True)
    b = jnp.sum(dl[2:3] * dl[3:4], axis=1, keepdims=True)
    return jnp.exp(a) - jnp.exp(b) + lambda_init


def _diff_finish(a0, a1, lam, subln, lambda_init):
    o = a0 - lam * a1
    o = o * lax.rsqrt(jnp.mean(o * o, axis=-1, keepdims=True) + RMS_EPS) * subln
    return o * (1.0 - lambda_init)


def _diff_flash_body(qi_ref, kt_ref, fl_ref, q_ref, k_ref, v_ref, dl_ref, sub_ref, o_ref, qs, m_s, l_s, acc, *,
                     tq, tk, lambda_init):
    st = pl.program_id(1)
    i = qi_ref[st]
    j = kt_ref[st]
    flags = fl_ref[st]
    q0 = i * tq
    rows = 2 * tq
    lo = _lane_iota(tq) < 64

    @pl.when((flags & STEP_FIRST) != 0)
    def _():
        m_s[...] = jnp.full(m_s.shape, NEG, F32)
        l_s[...] = jnp.zeros(l_s.shape, F32)
        acc[...] = jnp.zeros(acc.shape, F32)
        for h in range(DIFF_HEADS):
            q = q_ref[0, :, h * LANES:(h + 1) * LANES].astype(F32)
            qs[h, 0:tq, :] = jnp.where(lo, q, 0.0).astype(BF)
            qs[h, tq:2 * tq, :] = jnp.where(lo, 0.0, q).astype(BF)

    need_mask = (flags & STEP_MASKED) != 0

    def step(masked):
        for h in range(DIFF_HEADS):
            s = _dot_nt(k_ref[:, h * LANES:(h + 1) * LANES], qs[h])
            if masked:
                kpos = j * tk + lax.broadcasted_iota(I32, (tk, rows), 0)
                rpos = q0 + (lax.broadcasted_iota(I32, (tk, rows), 1) & (tq - 1))
                s = jnp.where(kpos <= rpos, s, NEG)
            _online_t(s, v_ref[0, h * LANES:(h + 1) * LANES, :], m_s, l_s, acc, h)

    @pl.when(need_mask)
    def _():
        step(True)

    @pl.when(jnp.logical_not(need_mask))
    def _():
        step(False)

    @pl.when((flags & STEP_LAST) != 0)
    def _():
        lam = _diff_lambda(dl_ref[...], lambda_init)
        for h in range(DIFF_HEADS):
            on = acc[h] / l_s[h]
            a0 = jnp.transpose(on[:, 0:tq])
            a1 = jnp.transpose(on[:, tq:2 * tq])
            o_ref[0, :, h * LANES:(h + 1) * LANES] = _diff_finish(a0, a1, lam, sub_ref[...], lambda_init)


def _diff_flash(q3, k2, v2, dl, subln, nbat, t, tq, tk, lambda_init):
    nq = t // tq
    nk = t // tk
    table = _step_table(nq, tq, tk)
    qmap = lambda b, s, qi, kt, fl: (b * nq + qi[s], 0, 0)
    kmap = lambda b, s, qi, kt, fl: (b * nk + kt[s], 0)
    vmap = lambda b, s, qi, kt, fl: (b, 0, kt[s])
    const = lambda b, s, qi, kt, fl: (0, 0)
    rows = 2 * tq
    return pl.pallas_call(
        functools.partial(_diff_flash_body, tq=tq, tk=tk, lambda_init=lambda_init),
        grid_spec=pltpu.PrefetchScalarGridSpec(
            num_scalar_prefetch=3, grid=(nbat, int(table[0].shape[0])),
            in_specs=[pl.BlockSpec((1, tq, 512), qmap), pl.BlockSpec((tk, 512), kmap),
                      pl.BlockSpec((1, 512, tk), vmap), pl.BlockSpec((4, HEAD_DIM), const),
                      pl.BlockSpec((1, DIFF_DV), const)],
            out_specs=pl.BlockSpec((1, tq, 512), qmap),
            scratch_shapes=[pltpu.VMEM((DIFF_HEADS, rows, LANES), BF), pltpu.VMEM((DIFF_HEADS, 1, rows), F32),
                            pltpu.VMEM((DIFF_HEADS, 1, rows), F32), pltpu.VMEM((DIFF_HEADS, DIFF_DV, rows), F32)]),
        out_shape=jax.ShapeDtypeStruct((nbat * nq, tq, 512), F32),
        compiler_params=_params(("parallel", "arbitrary")),
        name="diff_flash",
    )(*table, q3, k2, v2, dl, subln)


def _paged_body(pt_ref, *refs, mode, n_pages, past_len, lambda_init):
    q_ref = refs[0]
    k = 1
    if mode == "sel":
        selb_ref = refs[k]
        k += 1
    kpages = refs[k:k + n_pages]
    k += n_pages
    if mode == "diff":
        vpages = refs[k:k + n_pages]
        k += n_pages
    newk_ref = refs[k]
    k += 1
    if mode == "diff":
        newv_ref, dl_ref, sub_ref = refs[k:k + 3]
        k += 3
    o_ref = refs[k]
    k += 1
    if mode == "sel":
        qbd, bias, m_s, l_s, acc = refs[k:]
    else:
        qbd, m_s, l_s, acc = refs[k:]
    kw = WINDOW if mode == "win" else LANES
    st = pl.program_id(1)
    nq = 8
    rows = 64
    lane8 = _lane_iota(nq)
    lo8 = lane8 < 64
    t_row = lax.broadcasted_iota(I32, (rows, LANES), 0) & (nq - 1)
    col = lax.broadcasted_iota(I32, (rows, LANES), 1)

    def update(s, pv):
        m_old = m_s[0]
        m_new = jnp.maximum(m_old, jnp.max(s, axis=-1, keepdims=True))
        alpha = jnp.exp(m_old - m_new)
        p = jnp.exp(s - m_new)
        l_s[0] = alpha * l_s[0] + jnp.sum(p, axis=-1, keepdims=True)
        acc[0] = alpha * acc[0] + pv(p.astype(BF))
        m_s[0] = m_new

    def per_head(p, v_of_head):
        return jnp.concatenate([_dot(p[h * 16:(h + 1) * 16], v_of_head(h)) for h in range(DIFF_HEADS)], axis=0)

    @pl.when(st == 0)
    def _():
        m_s[0] = jnp.full((rows, 1), NEG, F32)
        l_s[0] = jnp.zeros((rows, 1), F32)
        acc[0] = jnp.zeros(acc.shape[1:], F32)
        if mode == "diff":
            qbd[...] = jnp.zeros(qbd.shape, F32)
            for h in range(DIFF_HEADS):
                q = q_ref[0, :, h * LANES:(h + 1) * LANES].astype(F32)
                qbd[(2 * h) * nq:(2 * h + 1) * nq, h * LANES:(h + 1) * LANES] = jnp.where(lo8, q, 0.0)
                qbd[(2 * h + 1) * nq:(2 * h + 2) * nq, h * LANES:(h + 1) * LANES] = jnp.where(lo8, 0.0, q)
        else:
            for h in range(NSA_HEADS):
                g = h // NSA_REP
                q = q_ref[0, :, h * LANES:(h + 1) * LANES].astype(F32)
                qbd[h * nq:(h + 1) * nq, :] = q if g == 0 else pltpu.roll(q, 64, 1)
                if mode == "sel":
                    bias[h * nq:(h + 1) * nq, :] = selb_ref[0, :, g * 256:(g + 1) * 256]
        kn = newk_ref[0]
        if mode == "diff":
            kk, vv = kn, newv_ref[0]
        else:
            kk, vv = kn[:, 0:LANES], kn[:, LANES:2 * LANES]
        kpad = jnp.concatenate([kk, jnp.zeros((LANES - nq, kk.shape[1]), F32)], axis=0).astype(BF)
        vpad = jnp.concatenate([vv, jnp.zeros((LANES - nq, vv.shape[1]), F32)], axis=0).astype(BF)
        s = _dot_nt(qbd[...].astype(BF), kpad)
        if mode == "sel":
            jnew = past_len // SEL_BLOCK
            s = s + bias[:, jnew:jnew + 1]
        s = jnp.where((col < nq) & (col <= t_row), s, NEG)
        if mode == "diff":
            update(s, lambda p: per_head(p, lambda h: vpad[:, h * LANES:(h + 1) * LANES]))
        else:
            update(s, lambda p: _dot(p, vpad))

    qb = qbd[...].astype(BF)
    nkeys = n_pages * kw
    if mode == "diff":
        kt = jnp.concatenate([kpages[i][0].reshape(4 * LANES, LANES).astype(BF) for i in range(n_pages)], axis=1)
    else:
        kt = jnp.concatenate([kpages[i][0, 0].reshape(LANES, kw).astype(BF) for i in range(n_pages)], axis=1)
    s = _dot(qb, kt)
    if mode == "sel":
        krow = lax.broadcasted_iota(I32, (nkeys, 256), 0)
        jbi = lax.broadcasted_iota(I32, (nkeys, 256), 1)
        ep = jnp.where(jbi == 2 * n_pages * st + jnp.right_shift(krow, 6), 1.0, 0.0).astype(BF)
        s = s + _dot_nt(bias[...].astype(BF), ep)
    if mode == "win":
        t_w = lax.broadcasted_iota(I32, (rows, nkeys), 0) & (nq - 1)
        s = jnp.where(lax.broadcasted_iota(I32, (rows, nkeys), 1) > t_w, s, NEG)
    if mode == "diff":
        update(s, lambda p: per_head(p, lambda h: jnp.concatenate(
            [vpages[i][0, :, h, :].astype(BF) for i in range(n_pages)], axis=0)))
    else:
        vt = jnp.concatenate([kpages[i][0, 1].reshape(LANES, kw).astype(BF) for i in range(n_pages)], axis=1)
        update(s, lambda p: _dot_nt(p, vt))

    @pl.when(st == pl.num_programs(1) - 1)
    def _():
        on = acc[0] / l_s[0]
        if mode == "diff":
            lam = _diff_lambda(dl_ref[...], lambda_init)
            for h in range(DIFF_HEADS):
                a0 = on[h * 16:h * 16 + nq]
                a1 = on[h * 16 + nq:h * 16 + 2 * nq]
                o_ref[0, :, h * LANES:(h + 1) * LANES] = _diff_finish(a0, a1, lam, sub_ref[...], lambda_init)
        else:
            for c in range(4):
                a = on[(2 * c) * nq:(2 * c + 1) * nq]
                b = on[(2 * c + 1) * nq:(2 * c + 2) * nq]
                o_ref[0, :, c * LANES:(c + 1) * LANES] = _pair(lo8, a, b, c // 2)


def _paged_attend(mode, q3, selb3, pool_k, pool_v, page_table, newk3, newv3, dl, subln, n_pages, past_len, lambda_init):
    nbat, pages = page_table.shape
    steps = pages // n_pages
    wq = q3.shape[2]
    wk = newk3.shape[2]
    bmap = lambda b, s, pt: (b, 0, 0)
    c2 = lambda b, s, pt: (0, 0)

    def pmap(i, rank):
        return lambda b, s, pt: (pt[b, s * n_pages + i],) + (0,) * (rank - 1)

    in_specs = [pl.BlockSpec((1, 8, wq), bmap)]
    args = [q3]
    if mode == "sel":
        in_specs.append(pl.BlockSpec((1, 8, 512), bmap))
        args.append(selb3)
    kblock = (1,) + pool_k.shape[1:]
    in_specs += [pl.BlockSpec(kblock, pmap(i, len(kblock))) for i in range(n_pages)]
    args += [pool_k] * n_pages
    if mode == "diff":
        in_specs += [pl.BlockSpec((1, LANES, DIFF_HEADS, DIFF_DV), pmap(i, 4)) for i in range(n_pages)]
        args += [pool_v] * n_pages
    in_specs.append(pl.BlockSpec((1, 8, wk), bmap))
    args.append(newk3)
    if mode == "diff":
        in_specs += [pl.BlockSpec((1, 8, 512), bmap), pl.BlockSpec((4, HEAD_DIM), c2), pl.BlockSpec((1, DIFF_DV), c2)]
        args += [newv3, dl, subln]
    ck = 512 if mode == "diff" else LANES
    scratch = [pltpu.VMEM((64, ck), F32)]
    if mode == "sel":
        scratch.append(pltpu.VMEM((64, 256), F32))
    scratch += [pltpu.VMEM((1, 64, 1), F32), pltpu.VMEM((1, 64, 1), F32), pltpu.VMEM((1, 64, LANES), F32)]
    return pl.pallas_call(
        functools.partial(_paged_body, mode=mode, n_pages=n_pages, past_len=past_len, lambda_init=lambda_init),
        grid_spec=pltpu.PrefetchScalarGridSpec(
            num_scalar_prefetch=1, grid=(nbat, steps), in_specs=in_specs,
            out_specs=pl.BlockSpec((1, 8, 512), bmap), scratch_shapes=scratch),
        out_shape=jax.ShapeDtypeStruct((nbat, 8, 512), F32),
        compiler_params=_params(("parallel", "arbitrary")),
        name="paged_" + mode,
    )(page_table, *args)


def _merge_body(x_ref, oc_ref, os_ref, ow_ref, od_ref, gx_ref, nmix_ref, nffn_ref, wg_ref, wbn_ref, wbd_ref, wo_ref,
                wr_ref, br_ref, x1_ref, h_ref, idx_ref, gate_ref):
    x = x_ref[...]
    tm = x.shape[0]
    xn = (x * lax.rsqrt(jnp.mean(x * x, axis=-1, keepdims=True) + RMS_EPS) * nmix_ref[...]).astype(BF)
    gx = gx_ref[...]
    o_nsa = gx[:, 0:512] * oc_ref[...] + gx[:, 512:1024] * os_ref[...] + gx[:, 1024:1536] * ow_ref[...]
    y_nsa = _dot(o_nsa.astype(BF), wbn_ref[...])
    y_diff = _dot(od_ref[...].astype(BF), wbd_ref[...])
    gates = jax.nn.sigmoid(_dot(xn, wg_ref[...]))
    mrg = gates[:, 0:D_MODEL] * y_nsa + gates[:, D_MODEL:2 * D_MODEL] * y_diff
    x1 = x + _dot(mrg.astype(BF), wo_ref[...])
    x1_ref[...] = x1
    h = x1 * lax.rsqrt(jnp.mean(x1 * x1, axis=-1, keepdims=True) + RMS_EPS) * nffn_ref[...]
    h_ref[...] = h
    logits = _dot_hi(h, wr_ref[...]) + br_ref[...]
    lane = _lane_iota(tm)
    vals, idxs = [], []
    for _ in range(TOP_K):
        m = jnp.max(logits, axis=-1, keepdims=True)
        idx = jnp.min(jnp.where(logits == m, lane, LANES), axis=-1, keepdims=True)
        vals.append(m)
        idxs.append(idx)
        logits = jnp.where(lane == idx, -jnp.inf, logits)
    es = [jnp.exp(v - vals[0]) for v in vals]
    den = es[0] + es[1] + es[2] + es[3]
    idx_out = jnp.zeros((tm, LANES), I32)
    gate_out = jnp.zeros((tm, LANES), F32)
    for k in range(TOP_K):
        idx_out = jnp.where(lane == k, idxs[k], idx_out)
        gate_out = jnp.where(lane == k, es[k] / den, gate_out)
    idx_ref[...] = idx_out
    gate_ref[...] = gate_out


def _merge(x2d, oc, osel, ow, od, gx, nmix, nffn, wg, wbn, wbd, wo, wr, br, tm):
    n = x2d.shape[0]
    row = lambda i: (i, 0)
    const = lambda i: (0, 0)
    return pl.pallas_call(
        _merge_body,
        grid=(n // tm,),
        in_specs=[pl.BlockSpec((tm, D_MODEL), row)] + [pl.BlockSpec((tm, 512), row)] * 4 +
                 [pl.BlockSpec((tm, 1536), row), pl.BlockSpec((1, D_MODEL), const), pl.BlockSpec((1, D_MODEL), const),
                  pl.BlockSpec((D_MODEL, 2 * D_MODEL), const), pl.BlockSpec((512, D_MODEL), const),
                  pl.BlockSpec((512, D_MODEL), const), pl.BlockSpec((D_MODEL, D_MODEL), const),
                  pl.BlockSpec((D_MODEL, LANES), const), pl.BlockSpec((1, LANES), const)],
        out_specs=[pl.BlockSpec((tm, D_MODEL), row), pl.BlockSpec((tm, D_MODEL), row),
                   pl.BlockSpec((tm, LANES), row), pl.BlockSpec((tm, LANES), row)],
        out_shape=[jax.ShapeDtypeStruct((n, D_MODEL), F32), jax.ShapeDtypeStruct((n, D_MODEL), F32),
                   jax.ShapeDtypeStruct((n, LANES), I32), jax.ShapeDtypeStruct((n, LANES), F32)],
        compiler_params=_params(("parallel",)),
        name="merge",
    )(x2d, oc, osel, ow, od, gx, nmix, nffn, wg, wbn, wbd, wo, wr, br)


def _route_body(idx_ref, dest_ref, meta_ref, cnt_s, base_s, *, tm, tmoe, nbp):
    ph = pl.program_id(0)
    i = pl.program_id(1)
    shift = int(math.log2(tmoe))
    idx_t = jnp.transpose(idx_ref[...])
    sub = lax.broadcasted_iota(I32, (LANES, tm), 0)
    ohs = [idx_t[k:k + 1, :] == sub for k in range(TOP_K)]
    tok = jnp.zeros((LANES, tm), F32)
    for oh in ohs:
        tok = tok + jnp.where(oh, 1.0, 0.0)
    tile_cnt = jnp.broadcast_to(jnp.sum(tok, axis=1, keepdims=True), (LANES, LANES))

    @pl.when((ph == 0) & (i == 0))
    def _():
        cnt_s[...] = jnp.zeros((LANES, LANES), F32)

    @pl.when(ph == 0)
    def _():
        cnt_s[...] = cnt_s[...] + tile_cnt

    @pl.when((ph == 1) & (i == 0))
    def _():
        cnt = cnt_s[...].astype(I32)
        padded = jnp.left_shift(jnp.right_shift(cnt + (tmoe - 1), shift), shift)
        er = lax.broadcasted_iota(I32, (LANES, LANES), 0)
        ec = lax.broadcasted_iota(I32, (LANES, LANES), 1)
        pad_start = _dot_hi(jnp.where(ec < er, 1.0, 0.0), padded.astype(F32))
        base_s[...] = pad_start
        pad_end = pad_start.astype(I32) + padded
        blk_start = lax.broadcasted_iota(I32, (LANES, nbp), 1) * tmoe
        ends = jnp.broadcast_to(pad_end[:, 0:1], (LANES, nbp))
        be = jnp.minimum(jnp.sum(jnp.where(ends <= blk_start, 1, 0), axis=0, keepdims=True), N_EXPERTS - 1)
        n_used = jnp.right_shift(jnp.max(pad_end, axis=0, keepdims=True)[:, 0:1], shift)
        last_blk = jnp.sum(jnp.where(er == ec, jnp.where(padded > 0, jnp.right_shift(pad_end, shift) - 1, -1), 0),
                           axis=0, keepdims=True)
        lb = jnp.concatenate([last_blk, jnp.full((1, nbp - LANES), -1, I32)], axis=1)
        row = lax.broadcasted_iota(I32, (8, nbp), 0)
        meta_ref[...] = jnp.where(row == 0, be, jnp.where(row == 1, n_used, jnp.where(row == 2, lb, 0)))

    @pl.when(ph == 1)
    def _():
        nr = lax.broadcasted_iota(I32, (tm, tm), 0)
        nc = lax.broadcasted_iota(I32, (tm, tm), 1)
        before = _dot(tok.astype(BF), jnp.where(nr < nc, 1.0, 0.0).astype(BF))
        pos = before + base_s[:, 0:1]
        rows = [jnp.sum(jnp.where(oh, pos, 0.0), axis=0, keepdims=True) for oh in ohs]
        dest_ref[...] = jnp.concatenate(rows + [jnp.zeros((8 - TOP_K, tm), F32)], axis=0).astype(I32)
        base_s[...] = base_s[...] + tile_cnt


def _route(idx_all, tm, tmoe, nbp):
    n_tok = idx_all.shape[0]
    return pl.pallas_call(
        functools.partial(_route_body, tm=tm, tmoe=tmoe, nbp=nbp),
        grid=(2, n_tok // tm),
        in_specs=[pl.BlockSpec((tm, LANES), lambda ph, i: (i, 0))],
        out_specs=[pl.BlockSpec((8, tm), lambda ph, i: (0, i * ph)), pl.BlockSpec((8, nbp), lambda ph, i: (0, 0))],
        out_shape=[jax.ShapeDtypeStruct((8, n_tok), I32), jax.ShapeDtypeStruct((8, nbp), I32)],
        scratch_shapes=[pltpu.VMEM((LANES, LANES), F32), pltpu.VMEM((LANES, LANES), F32)],
        compiler_params=_params(("arbitrary", "arbitrary")),
        name="route",
    )(idx_all)


def _dispatch_body(dest_ref, lb_ref, nu_ref, h_ref, xs_hbm, zbuf, zsem, sem, *, tm, tmoe, n_tok):
    i = pl.program_id(0)

    @pl.when(i == 0)
    def _():
        zbuf[...] = jnp.zeros(zbuf.shape, F32)

        def fill(e):
            return pltpu.make_async_copy(zbuf, xs_hbm.at[pl.ds(pl.multiple_of(lb_ref[e] * tmoe, tmoe), tmoe)], zsem)
        for e in range(N_EXPERTS):
            @pl.when(lb_ref[e] >= 0)
            def _():
                fill(e).start()
        for e in range(N_EXPERTS):
            @pl.when(lb_ref[e] >= 0)
            def _():
                fill(e).wait()

        def tail(b):
            return pltpu.make_async_copy(zbuf, xs_hbm.at[pl.ds(pl.multiple_of(b * tmoe, tmoe), tmoe)], zsem)
        n_blk = xs_hbm.shape[0] // tmoe
        lax.fori_loop(nu_ref[0], n_blk, lambda b, c: (tail(b).start(), c)[1], 0)
        lax.fori_loop(nu_ref[0], n_blk, lambda b, c: (tail(b).wait(), c)[1], 0)

    def body(n, c):
        for k in range(TOP_K):
            d = dest_ref[k * n_tok + i * tm + n]
            pltpu.make_async_copy(h_ref.at[pl.ds(n, 1)], xs_hbm.at[pl.ds(d, 1)], sem).start(priority=k % 2)
        return c
    lax.fori_loop(0, tm, body, 0)
    pltpu.make_async_copy(xs_hbm.at[pl.ds(0, TOP_K * tm)], xs_hbm.at[pl.ds(0, TOP_K * tm)], sem).wait()


def _dispatch(dest_flat, last_blk, n_used, h_all, n_rows, tm, tmoe):
    n_tok = h_all.shape[0]
    return pl.pallas_call(
        functools.partial(_dispatch_body, tm=tm, tmoe=tmoe, n_tok=n_tok),
        grid_spec=pltpu.PrefetchScalarGridSpec(
            num_scalar_prefetch=3, grid=(n_tok // tm,),
            in_specs=[pl.BlockSpec((tm, D_MODEL), lambda i, d, lb, nu: (i, 0))],
            out_specs=pl.BlockSpec(memory_space=pl.ANY),
            scratch_shapes=[pltpu.VMEM((tmoe, D_MODEL), F32), pltpu.SemaphoreType.DMA(()), pltpu.SemaphoreType.DMA(())]),
        out_shape=jax.ShapeDtypeStruct((n_rows, D_MODEL), F32),
        compiler_params=pltpu.CompilerParams(dimension_semantics=("arbitrary",), vmem_limit_bytes=VMEM_LIMIT,
                                             disable_bounds_checks=True),
        name="moe_dispatch",
    )(dest_flat, last_blk, n_used, h_all)


def _moe_body(be_ref, nused_ref, x_ref, wgu_ref, bgu_ref, wd_ref, bd_ref, o_ref, wgu_s, wd_s):
    i = pl.program_id(0)
    e = be_ref[i]
    prev = be_ref[jnp.maximum(i - 1, 0)]

    @pl.when((i == 0) | (e != prev))
    def _():
        def cast_rows(c, carry):
            r0 = pl.multiple_of(c * LANES, LANES)
            wgu_s[pl.ds(r0, LANES), :] = wgu_ref[0, pl.ds(r0, LANES), :].astype(BF)
            wd_s[pl.ds(r0, LANES), :] = wd_ref[0, pl.ds(r0, LANES), :].astype(BF)
            return carry
        lax.fori_loop(0, D_MODEL // LANES, cast_rows, 0)

    @pl.when(i < nused_ref[0])
    def _():
        gu = _dot(x_ref[...].astype(BF), wgu_s[...]) + bgu_ref[0]
        gate = jnp.minimum(gu[:, 0:D_FF], SWIGLU_LIMIT)
        up = jnp.clip(gu[:, D_FF:2 * D_FF], -SWIGLU_LIMIT, SWIGLU_LIMIT)
        act = (up + 1.0) * gate * jax.nn.sigmoid(SWIGLU_ALPHA * gate)
        o_ref[...] = _dot(act.astype(BF), wd_s[...]) + bd_ref[0]

    @pl.when(i >= nused_ref[0])
    def _():
        o_ref[...] = jnp.zeros(o_ref.shape, F32)


def _moe(block_expert, n_used, xs, wgu, bgu, wd, bd, tmoe):
    n_rows = xs.shape[0]
    nblk = n_rows // tmoe
    xmap = lambda i, be, nu: (jnp.minimum(i, jnp.maximum(nu[0] - 1, 0)), 0)
    emap = lambda i, be, nu: (be[i], 0, 0)
    return pl.pallas_call(
        _moe_body,
        grid_spec=pltpu.PrefetchScalarGridSpec(
            num_scalar_prefetch=2, grid=(nblk,),
            in_specs=[pl.BlockSpec((tmoe, D_MODEL), xmap),
                      pl.BlockSpec((1, D_MODEL, 2 * D_FF), emap), pl.BlockSpec((1, 1, 2 * D_FF), emap),
                      pl.BlockSpec((1, D_FF, D_MODEL), emap), pl.BlockSpec((1, 1, D_MODEL), emap)],
            out_specs=pl.BlockSpec((tmoe, D_MODEL), lambda i, be, nu: (i, 0)),
            scratch_shapes=[pltpu.VMEM((D_MODEL, 2 * D_FF), BF), pltpu.VMEM((D_FF, D_MODEL), BF)]),
        out_shape=jax.ShapeDtypeStruct((n_rows, D_MODEL), F32),
        compiler_params=_params(("arbitrary",)),
        name="moe_experts",
    )(block_expert, n_used, xs, wgu, bgu, wd, bd)


def _final_body(dest_ref, x1_ref, gate_ref, nf_ref, ys_hbm, y_ref, ybuf, sem, *, tm, n_tok, block0):
    i = pl.program_id(0)

    def gather(tile, sl):
        def body(n, c):
            for k in range(TOP_K):
                d = dest_ref[k * n_tok + (block0 + tile) * tm + n]
                pltpu.make_async_copy(ys_hbm.at[pl.ds(d, 1)], ybuf.at[sl, k, pl.ds(n, 1)],
                                      sem.at[sl]).start(priority=k % 2)
            return c
        lax.fori_loop(0, tm, body, 0)

    @pl.when(i == 0)
    def _():
        gather(0, 0)

    @pl.when(i + 1 < pl.num_programs(0))
    def _():
        gather(i + 1, (i + 1) % 2)

    slot = i % 2
    pltpu.make_async_copy(ybuf.at[slot], ybuf.at[slot], sem.at[slot]).wait()
    gate = gate_ref[...]
    x = x1_ref[...]
    for k in range(TOP_K):
        x = x + gate[:, k:k + 1] * ybuf[slot, k]
    y_ref[...] = x * lax.rsqrt(jnp.mean(x * x, axis=-1, keepdims=True) + RMS_EPS) * nf_ref[...]


def _final(dest_flat, x1, gates, nf, ys, tm, n_tok, block0):
    n = x1.shape[0]
    row = lambda i, d: (i, 0)
    return pl.pallas_call(
        functools.partial(_final_body, tm=tm, n_tok=n_tok, block0=block0),
        grid_spec=pltpu.PrefetchScalarGridSpec(
            num_scalar_prefetch=1, grid=(n // tm,),
            in_specs=[pl.BlockSpec((tm, D_MODEL), row), pl.BlockSpec((tm, LANES), row),
                      pl.BlockSpec((1, D_MODEL), lambda i, d: (0, 0)), pl.BlockSpec(memory_space=pl.ANY)],
            out_specs=pl.BlockSpec((tm, D_MODEL), row),
            scratch_shapes=[pltpu.VMEM((2, TOP_K, tm, D_MODEL), F32), pltpu.SemaphoreType.DMA((2,))]),
        out_shape=jax.ShapeDtypeStruct((n, D_MODEL), F32),
        compiler_params=pltpu.CompilerParams(dimension_semantics=("arbitrary",), vmem_limit_bytes=VMEM_LIMIT,
                                             disable_bounds_checks=True),
        name="final_norm",
    )(dest_flat, x1, gates, nf, ys)


def _rope_tables(pos, nb_lanes):
    half = ROT_DIM // 2
    inv = ROPE_THETA ** (-jnp.arange(half, dtype=F32) / half)
    ang = pos.astype(F32)[:, None] * inv[None, :]
    cos, sin = jnp.cos(ang), jnp.sin(ang)
    t = pos.shape[0]
    c64 = jnp.concatenate([cos, cos, jnp.ones((t, HEAD_DIM - ROT_DIM), F32)], axis=1)
    s64 = jnp.concatenate([-sin, sin, jnp.zeros((t, HEAD_DIM - ROT_DIM), F32)], axis=1)
    lane = jnp.arange(LANES)[None, :]
    e = ((lane >= 64) & (lane < 64 + nb_lanes) & ((pos // SEL_BLOCK)[:, None] == lane - 64)).astype(F32)
    return jnp.tile(c64, (1, 2)), jnp.tile(s64, (1, 2)), e


def _imp_matrix(rt, n_cmp, nb, nbp):
    m = np.zeros((rt, nbp), np.float32)
    sub = SEL_BLOCK // CMP_STRIDE
    for n in range(n_cmp):
        for shift in range(CMP_BLOCK // CMP_STRIDE):
            j = (n + shift) // sub
            if j < nb:
                m[n, j] += 1.0
    return jnp.asarray(m)


def kernel(x_prompt, x_sample, cache_cmp_kv, cache_sel_kv, state_win_kv, cache_diff_k, cache_diff_v, page_table, norm_mix, w_in, cmp_pe, w_cmp1, w_cmp2, diff_lambda, diff_subln, w_br_nsa, w_br_diff, w_out, norm_ffn, w_router, b_router, w_gate_up, b_gate_up, w_down, b_down, norm_final):
    bp, t, _ = x_prompt.shape
    bs, ts, _ = x_sample.shape
    assert ts == 8 and t % 256 == 0 and t <= 2048 and w_in.shape[0] == 1
    n_pool, page = cache_cmp_kv.shape[1], cache_cmp_kv.shape[2]
    pages = page_table.shape[1]
    past_len = pages * page
    assert page == LANES and past_len % SEL_BLOCK == 0
    lambda_init = 0.8 - 0.6 * math.exp(-0.3 * 0)
    n_p, n_s = bp * t, bs * ts

    w = w_in[0]
    gcols = w[:, 1280:1304].reshape(D_MODEL, NSA_HEADS, 3).transpose(0, 2, 1)
    gexp = jnp.broadcast_to(gcols[..., None], (D_MODEL, 3, NSA_HEADS, HEAD_DIM)).reshape(D_MODEL, 1536)
    w_proj = jnp.concatenate([w[:, 0:1280], w[:, 1304:2840], gexp], axis=1).astype(BF)
    w_gates = w[:, 2840:4888].astype(BF)
    nmix = norm_mix[0][None, :]
    nffn = norm_ffn[0][None, :]
    pe = cmp_pe[0]
    pea = pe[:, :CMP_STRIDE].reshape(2, 1, 1024)
    peb = pe[:, CMP_STRIDE:].reshape(2, 1, 1024)
    w1a = w_cmp1[0][:, :1024].astype(BF)
    w1b = w_cmp1[0][:, 1024:].astype(BF)
    w2p = jnp.pad(w_cmp2[0], ((0, 0), (0, 0), (0, LANES - HEAD_DIM))).astype(BF)
    wbn = w_br_nsa[0].astype(BF)
    wbd = w_br_diff[0].astype(BF)
    wo = w_out[0].astype(BF)
    wr = jnp.pad(w_router[0], ((0, 0), (0, LANES - N_EXPERTS)))
    br = jnp.concatenate([b_router[0], jnp.full((LANES - N_EXPERTS,), NEG, F32)])[None, :]
    dl = diff_lambda[0]
    subln = diff_subln[0][None, :]

    pos_p = jnp.arange(t, dtype=I32)
    pos_s = jnp.tile(past_len + jnp.arange(ts, dtype=I32), bs)
    tm = 256
    w_rows = jnp.concatenate([w[:, 0:768], w[:, 1304:1816], w[:, 2328:2840], gexp, w[:, 768:896], w[:, 1024:1152],
                              w[:, 1816:2328]], axis=1).astype(BF)
    w_cols = jnp.concatenate([w[:, 512:1280], w[:, 1816:2840]], axis=1).T.astype(BF)
    cos_p, sin_p, _ = _rope_tables(pos_p, 0)
    outs_p = _inproj_t(x_prompt.reshape(n_p, D_MODEL), nmix, w_rows, w_cols, cos_p, sin_p,
                       cos_p[:, 0:8].T, -sin_p[:, 0:8].T, bp, t, tm)
    outs_s = _inproj(x_sample.reshape(n_s, D_MODEL), nmix, w_proj, *_rope_tables(pos_s, 0), n_s)
    (qraw_p, qrot_p, kvc_p, kvct_p, kvst_p, kvwt_p, ksa_p, kwa_p, vst_p, vwt_p, qd_p, kdt_p, kdb_p, vd4_p, vdtb_p,
     gx_p) = outs_p
    (qraw_s, qrot_s, kvc_s, kvs_s, kvw_s, _, _, _, _, qd_s, kd_s, _, vd_s, _, gx_s) = outs_s

    rt_p = t // CMP_STRIDE
    n_cmp_p = (t - CMP_BLOCK) // CMP_STRIDE + 1
    nb_p = t // SEL_BLOCK
    kvcmp_p = _compress(kvc_p.reshape(bp, rt_p, 4096), jnp.arange(bp, dtype=I32)[:, None], pea, peb, w1a, w1b, w2p, 1)
    tq = 128
    nq = t // tq
    ocmp_p, selb_p = _cmp_attend(qraw_p.reshape(bp * nq, tq, 1024), kvcmp_p, _imp_matrix(rt_p, n_cmp_p, nb_p, LANES),
                                 nq, nb_p, LANES, 0)
    qrot3 = qrot_p.reshape(bp * nq, tq, 1024)
    osel_p = _nsa_flash("sel", qrot3, selb_p, ksa_p, vst_p, bp, t, tq, 512)
    owin_p = _nsa_flash("win", qrot3, None, kwa_p, vwt_p, bp, t, tq, 256)
    odiff_p = _diff_flash(qd_p.reshape(bp * nq, tq, 512), kdb_p, vdtb_p, dl, subln, bp, t, tq, 512, lambda_init)

    n_cmp_s = (past_len + ts - CMP_BLOCK) // CMP_STRIDE + 1
    rt_s = past_len // CMP_STRIDE
    assert n_cmp_s <= rt_s
    nb_s = (past_len + ts + SEL_BLOCK - 1) // SEL_BLOCK
    assert nb_s <= 256
    ppg = 16 if pages % 16 == 0 else (8 if pages % 8 == 0 else 1)
    kvcmp_s = _compress(cache_cmp_kv[0].reshape(n_pool, page // CMP_STRIDE, 4096), page_table, pea, peb, w1a, w1b,
                        w2p, ppg)
    ocmp_s, selb_s = _cmp_attend(qraw_s.reshape(bs, ts, 1024), kvcmp_s, _imp_matrix(rt_s, n_cmp_s, nb_s, 256),
                                 1, nb_s, 256, past_len)
    qrot_s3 = qrot_s.reshape(bs, ts, 1024)
    sel_t = jnp.transpose(cache_sel_kv[0], (0, 2, 3, 4, 1))
    win_t = jnp.transpose(state_win_kv[0], (0, 2, 3, 4, 1))
    dk_t = jnp.transpose(cache_diff_k[0], (0, 2, 3, 4, 1))
    osel_s = _paged_attend("sel", qrot_s3, selb_s, sel_t, None, page_table,
                           kvs_s.reshape(bs, ts, 256), None, None, None, ppg, past_len, lambda_init)
    wbuf = state_win_kv.shape[2]
    assert wbuf == WINDOW
    owin_s = _paged_attend("win", qrot_s3, None, win_t, None, jnp.arange(bs, dtype=I32)[:, None],
                           kvw_s.reshape(bs, ts, 256), None, None, None, 1, past_len, lambda_init)
    odiff_s = _paged_attend("diff", qd_s.reshape(bs, ts, 512), None, dk_t, cache_diff_v[0], page_table,
                            kd_s.reshape(bs, ts, 512), vd_s.reshape(bs, ts, 512), dl, subln, ppg, past_len, lambda_init)

    x1_p, h_p, idx_p, gate_p = _merge(x_prompt.reshape(n_p, D_MODEL), ocmp_p.reshape(n_p, 512), osel_p.reshape(n_p, 512),
                                      owin_p.reshape(n_p, 512), odiff_p.reshape(n_p, 512), gx_p, nmix, nffn, w_gates,
                                      wbn, wbd, wo, wr, br, tm)
    x1_s, h_s, idx_s, gate_s = _merge(x_sample.reshape(n_s, D_MODEL), ocmp_s.reshape(n_s, 512), osel_s.reshape(n_s, 512),
                                      owin_s.reshape(n_s, 512), odiff_s.reshape(n_s, 512), gx_s, nmix, nffn, w_gates,
                                      wbn, wbd, wo, wr, br, n_s)

    n_tok = n_p + n_s
    n_assign = n_tok * TOP_K
    tmoe = 256
    assert n_p % n_s == 0 and n_tok % n_s == 0
    n_blocks = (n_assign + N_EXPERTS * (tmoe - 1) + tmoe - 1) // tmoe
    nbp = (n_blocks + LANES - 1) // LANES * LANES
    dest, meta = _route(jnp.concatenate([idx_p, idx_s], axis=0), n_s, tmoe, nbp)
    dest_flat = dest.reshape(-1)
    xs = _dispatch(dest_flat, meta[2, :N_EXPERTS], meta[1, :1], jnp.concatenate([h_p, h_s], axis=0), n_blocks * tmoe,
                   n_s, tmoe)
    ys = _moe(meta[0, :n_blocks], meta[1, :1], xs, w_gate_up[0], b_gate_up[0][:, None, :], w_down[0],
              b_down[0][:, None, :], tmoe)

    nf = norm_final[None, :]
    y_p = _final(dest_flat, x1_p, gate_p, nf, ys, n_s, n_tok, 0)
    y_s = _final(dest_flat, x1_s, gate_s, nf, ys, n_s, n_tok, n_p // n_s)

    g, hd = NSA_GROUPS, HEAD_DIM
    def from_t(a, heads, length):
        return jnp.transpose(a.reshape(bp, heads, 2, hd, length), (0, 4, 1, 2, 3))[None]

    new_win_p = from_t(kvwt_p[:, :, t - wbuf:], 2, wbuf)
    new_win_s = jnp.concatenate([state_win_kv[0].reshape(bs, wbuf, 256)[:, ts:], kvw_s.reshape(bs, ts, 256)], axis=1)
    return (y_p.reshape(bp, t, D_MODEL), y_s.reshape(bs, ts, D_MODEL),
            from_t(kvct_p, 2, t), kvc_s.reshape(1, bs, ts, 2, g, hd),
            from_t(kvst_p, 2, t), kvs_s.reshape(1, bs, ts, 2, g, hd),
            new_win_p, new_win_s.reshape(1, bs, wbuf, 2, g, hd),
            from_t(kdt_p, DIFF_HEADS, t), kd_s.reshape(1, bs, ts, DIFF_HEADS, 2, hd),
            vd4_p[None], vd_s.reshape(1, bs, ts, DIFF_HEADS, DIFF_DV))
```

```python
import functools
import math

import numpy as np
import jax
import jax.numpy as jnp
from jax import lax
from jax.experimental import pallas as pl
from jax.experimental.pallas import tpu as pltpu

D_MODEL = 1024
HEAD_DIM = 64
NSA_HEADS = 8
NSA_GROUPS = 2
NSA_REP = 4
CMP_BLOCK = 32
CMP_STRIDE = 16
SEL_BLOCK = 64
N_SEL = 16
WINDOW = 512
DIFF_HEADS = 4
DIFF_DV = 128
ROT_DIM = 16
ROPE_THETA = 500000.0
N_EXPERTS = 32
TOP_K = 4
D_FF = 1024
SWIGLU_LIMIT = 7.0
SWIGLU_ALPHA = 1.702
RMS_EPS = 1e-5
FORCE_BONUS = 1e4
SCALE = HEAD_DIM ** -0.5
LANES = 128

BF = jnp.bfloat16
F32 = jnp.float32
I32 = jnp.int32
NEG = -1e30
VMEM_LIMIT = 56 * 1024 * 1024

OFF_Q, OFF_KVC, OFF_KVS, OFF_KVW, OFF_QD, OFF_KD, OFF_VD, OFF_GX = 0, 512, 768, 1024, 1280, 1792, 2304, 2816
IN_COLS_K = 4352
TOFF_Q, TOFF_KVC, TOFF_QD, TOFF_VD, TOFF_GX, TOFF_KS, TOFF_KW, TOFF_KD = 0, 512, 768, 1280, 1792, 3328, 3456, 3584
TIN_COLS = 4096
TIN_ROWS = 1792


def _dot(a, b):
    return jnp.dot(a, b, preferred_element_type=F32)


def _dot_nt(a, b):
    return lax.dot_general(a, b, (((1,), (1,)), ((), ())), preferred_element_type=F32)


def _dot_hi(a, b):
    return jnp.dot(a, b, preferred_element_type=F32, precision=lax.Precision.HIGHEST)


def _params(sem):
    return pltpu.CompilerParams(dimension_semantics=sem, vmem_limit_bytes=VMEM_LIMIT)


def _lane_iota(rows):
    return lax.broadcasted_iota(I32, (rows, LANES), 1)


def _pair(lo, a, b, g):
    if g == 0:
        return jnp.where(lo, a, pltpu.roll(b, 64, 1))
    return jnp.where(lo, pltpu.roll(a, 64, 1), b)


def _inproj_body(x_ref, nw_ref, w_ref, cos_ref, sin_ref, e_ref,
                 qraw_ref, qrot_ref, kvc_ref, kvs_ref, kvw_ref, ksa_ref, kwa_ref, vsb_ref, vwb_ref,
                 qd_ref, kd_ref, kdb_ref, vd_ref, vdb_ref, gx_ref):
    x = x_ref[...]
    tm = x.shape[0]
    xn = (x * lax.rsqrt(jnp.mean(x * x, axis=-1, keepdims=True) + RMS_EPS) * nw_ref[...]).astype(BF)
    cos = cos_ref[...]
    sin = sin_ref[...]
    epat = e_ref[...]
    lane = _lane_iota(tm)
    lo = lane < 64
    first8 = (lane & 63) < 8

    def mm(off):
        return _dot(xn, w_ref[:, off:off + LANES])

    def rope(y):
        sw = jnp.where(first8, pltpu.roll(y, LANES - 8, 1), pltpu.roll(y, 8, 1))
        return y * cos + sw * sin

    for c in range(4):
        y = mm(OFF_Q + c * LANES) * SCALE
        yr = rope(y)
        ys = pltpu.roll(y, 64, 1)
        yrs = pltpu.roll(yr, 64, 1)
        qraw_ref[:, (2 * c) * LANES:(2 * c + 1) * LANES] = jnp.where(lo, y, 0.0).astype(BF)
        qraw_ref[:, (2 * c + 1) * LANES:(2 * c + 2) * LANES] = jnp.where(lo, ys, 0.0).astype(BF)
        qrot_ref[:, (2 * c) * LANES:(2 * c + 1) * LANES] = jnp.where(lo, yr, 0.0).astype(BF)
        qrot_ref[:, (2 * c + 1) * LANES:(2 * c + 2) * LANES] = jnp.where(lo, yrs, 0.0).astype(BF)

    for c in range(2):
        kvc_ref[:, c * LANES:(c + 1) * LANES] = mm(OFF_KVC + c * LANES)

    for off, kv_ref, ka_ref, vb_ref in ((OFF_KVS, kvs_ref, ksa_ref, vsb_ref), (OFF_KVW, kvw_ref, kwa_ref, vwb_ref)):
        kr = rope(mm(off))
        v = mm(off + LANES)
        kv_ref[:, 0:LANES] = kr
        kv_ref[:, LANES:2 * LANES] = v
        ka_ref[:, 0:LANES] = jnp.where(lo, kr, epat).astype(BF)
        ka_ref[:, LANES:2 * LANES] = jnp.where(lo, pltpu.roll(kr, 64, 1), epat).astype(BF)
        vb_ref[...] = v.astype(BF)

    for c in range(4):
        sl = slice(c * LANES, (c + 1) * LANES)
        qd_ref[:, sl] = rope(mm(OFF_QD + c * LANES) * SCALE).astype(BF)
        kr = rope(mm(OFF_KD + c * LANES))
        kd_ref[:, sl] = kr
        kdb_ref[:, sl] = kr.astype(BF)
        v = mm(OFF_VD + c * LANES)
        vd_ref[:, sl] = v
        vdb_ref[:, sl] = v.astype(BF)

    for c in range(12):
        gx_ref[:, c * LANES:(c + 1) * LANES] = jax.nn.sigmoid(mm(OFF_GX + c * LANES))


def _inproj(x2d, norm_w, w_bf, cos_t, sin_t, e_t, tm):
    n = x2d.shape[0]
    nt = cos_t.shape[0] // tm
    row = lambda i: (i, 0)
    tab = lambda i: (i % nt, 0)
    const = lambda i: (0, 0)
    outs = [(1024, BF), (1024, BF), (256, F32), (256, F32), (256, F32), (256, BF), (256, BF), (128, BF), (128, BF),
            (512, BF), (512, F32), (512, BF), (512, F32), (512, BF), (1536, F32)]
    return pl.pallas_call(
        _inproj_body,
        grid=(n // tm,),
        in_specs=[pl.BlockSpec((tm, D_MODEL), row), pl.BlockSpec((1, D_MODEL), const),
                  pl.BlockSpec((D_MODEL, IN_COLS_K), const),
                  pl.BlockSpec((tm, LANES), tab), pl.BlockSpec((tm, LANES), tab), pl.BlockSpec((tm, LANES), tab)],
        out_specs=[pl.BlockSpec((tm, w), row) for w, _ in outs],
        out_shape=[jax.ShapeDtypeStruct((n, w), dt) for w, dt in outs],
        compiler_params=_params(("parallel",)),
        name="inproj",
    )(x2d, norm_w, w_bf, cos_t, sin_t, e_t)


def _inproj_t_body(x_ref, nw_ref, w_ref, wt_ref, cos_ref, sin_ref, cost_ref, sint_ref,
                   qraw_ref, qrot_ref, kvc_ref, kvct_ref, kvst_ref, kvwt_ref, ksa_ref, kwa_ref, vst_ref, vwt_ref,
                   qd_ref, kdt_ref, kdb_ref, vd4_ref, vdtb_ref, gx_ref):
    x = x_ref[...]
    tm = x.shape[0]
    xn = (x * lax.rsqrt(jnp.mean(x * x, axis=-1, keepdims=True) + RMS_EPS) * nw_ref[...]).astype(BF)
    cos = cos_ref[...]
    sin = sin_ref[...]
    cos_t = cost_ref[...]
    sin_t = sint_ref[...]
    lane = _lane_iota(tm)
    lo = lane < 64
    first8 = (lane & 63) < 8

    def mm(off):
        return _dot(xn, w_ref[:, off:off + LANES])

    def mm_t(off, n):
        return _dot_nt(wt_ref[off:off + n, :], xn)

    def rope(y):
        sw = jnp.where(first8, pltpu.roll(y, LANES - 8, 1), pltpu.roll(y, 8, 1))
        return y * cos + sw * sin

    def rope_t(y):
        parts = []
        for hb in range(0, y.shape[0], HEAD_DIM):
            x1 = y[hb:hb + 8]
            x2 = y[hb + 8:hb + 16]
            parts += [x1 * cos_t - x2 * sin_t, x1 * sin_t + x2 * cos_t, y[hb + 16:hb + HEAD_DIM]]
        return jnp.concatenate(parts, axis=0)

    for c in range(4):
        y = mm(TOFF_Q + c * LANES) * SCALE
        yr = rope(y)
        ys = pltpu.roll(y, 64, 1)
        yrs = pltpu.roll(yr, 64, 1)
        qraw_ref[:, (2 * c) * LANES:(2 * c + 1) * LANES] = jnp.where(lo, y, 0.0).astype(BF)
        qraw_ref[:, (2 * c + 1) * LANES:(2 * c + 2) * LANES] = jnp.where(lo, ys, 0.0).astype(BF)
        qrot_ref[:, (2 * c) * LANES:(2 * c + 1) * LANES] = jnp.where(lo, yr, 0.0).astype(BF)
        qrot_ref[:, (2 * c + 1) * LANES:(2 * c + 2) * LANES] = jnp.where(lo, yrs, 0.0).astype(BF)

    for c in range(2):
        kvc_ref[:, c * LANES:(c + 1) * LANES] = mm(TOFF_KVC + c * LANES)
    kvct_ref[0] = mm_t(0, 256)

    blk = jnp.right_shift(pl.program_id(1) * tm + lax.broadcasted_iota(I32, (tm, LANES), 0), 6)
    epat = jnp.where((lane >= 64) & (lane < 96) & (blk == lane - 64), 1.0, 0.0)
    for off, koff, kvt_ref, ka_ref, vt_ref in ((256, TOFF_KS, kvst_ref, ksa_ref, vst_ref),
                                               (512, TOFF_KW, kvwt_ref, kwa_ref, vwt_ref)):
        y = mm_t(off, 256)
        v = y[128:256]
        kvt_ref[0, 0:128, :] = rope_t(y[0:128])
        kvt_ref[0, 128:256, :] = v
        vt_ref[0] = v.astype(BF)
        kr = rope(mm(koff))
        ka_ref[:, 0:LANES] = jnp.where(lo, kr, epat).astype(BF)
        ka_ref[:, LANES:2 * LANES] = jnp.where(lo, pltpu.roll(kr, 64, 1), epat).astype(BF)

    kdt_ref[0] = rope_t(mm_t(768, 512))
    vdtb_ref[0] = mm_t(1280, 512).astype(BF)
    for c in range(4):
        sl = slice(c * LANES, (c + 1) * LANES)
        qd_ref[:, sl] = rope(mm(TOFF_QD + c * LANES) * SCALE).astype(BF)
        kdb_ref[:, sl] = rope(mm(TOFF_KD + c * LANES)).astype(BF)
        vd4_ref[0, :, c, :] = mm(TOFF_VD + c * LANES)

    for c in range(12):
        gx_ref[:, c * LANES:(c + 1) * LANES] = jax.nn.sigmoid(mm(TOFF_GX + c * LANES))


def _inproj_t(x2d, norm_w, w_bf, wt_bf, cos_t, sin_t, cos_tt, sin_tt, nbat, t, tm):
    n = x2d.shape[0]
    nt = t // tm
    row = lambda b, i: (b * nt + i, 0)
    tab = lambda b, i: (i, 0)
    tab_t = lambda b, i: (0, i)
    const = lambda b, i: (0, 0)
    tr = lambda b, i: (b, 0, i)
    rm = lambda w, dt: (pl.BlockSpec((tm, w), row), jax.ShapeDtypeStruct((n, w), dt))
    tp = lambda r, dt: (pl.BlockSpec((1, r, tm), tr), jax.ShapeDtypeStruct((nbat, r, t), dt))
    vd4 = (pl.BlockSpec((1, tm, DIFF_HEADS, DIFF_DV), lambda b, i: (b, i, 0, 0)),
           jax.ShapeDtypeStruct((nbat, t, DIFF_HEADS, DIFF_DV), F32))
    outs = [rm(1024, BF), rm(1024, BF), rm(256, F32), tp(256, F32), tp(256, F32), tp(256, F32), rm(256, BF),
            rm(256, BF), tp(128, BF), tp(128, BF), rm(512, BF), tp(512, F32), rm(512, BF), vd4, tp(512, BF),
            rm(1536, F32)]
    return pl.pallas_call(
        _inproj_t_body,
        grid=(nbat, nt),
        in_specs=[pl.BlockSpec((tm, D_MODEL), row), pl.BlockSpec((1, D_MODEL), const),
                  pl.BlockSpec((D_MODEL, TIN_COLS), const), pl.BlockSpec((TIN_ROWS, D_MODEL), const),
                  pl.BlockSpec((tm, LANES), tab), pl.BlockSpec((tm, LANES), tab),
                  pl.BlockSpec((8, tm), tab_t), pl.BlockSpec((8, tm), tab_t)],
        out_specs=[o[0] for o in outs],
        out_shape=[o[1] for o in outs],
        compiler_params=_params(("parallel", "parallel")),
        name="inproj_t",
    )(x2d, norm_w, w_bf, wt_bf, cos_t, sin_t, cos_tt, sin_tt)


def _compress_body(pt_ref, *refs, n_pages, rp, rt):
    page_refs = refs[:n_pages]
    pea_ref, peb_ref, w1a_ref, w1b_ref, w2_ref, out_ref, y_scr = refs[n_pages:]
    s = pl.program_id(1)
    for i in range(n_pages):
        x = page_refs[i][0]
        row0 = pl.multiple_of((s * n_pages + i) * rp, 8)
        for cg in range(4):
            ycg = jnp.concatenate(
                [x[:, l * 256 + cg * 64:l * 256 + cg * 64 + 64] for l in range(CMP_STRIDE)], axis=1)
            y_scr[cg, pl.ds(row0, rp), :] = ycg

    @pl.when(s == pl.num_programs(1) - 1)
    def _():
        for c in range(2):
            for g in range(2):
                cg = c * 2 + g
                y = y_scr[cg]
                za = _dot((y + pea_ref[c]).astype(BF), w1a_ref[c])
                zb = _dot((y + peb_ref[c]).astype(BF), w1b_ref[c])
                hid = jax.nn.gelu(za + pltpu.roll(zb, rt - 1, 0))
                out_ref[0, :, cg * LANES:(cg + 1) * LANES] = _dot(hid.astype(BF), w2_ref[c]).astype(BF)


def _compress(pool, page_table, pea, peb, w1a, w1b, w2p, n_pages):
    nb, pages = page_table.shape
    rp = pool.shape[1]
    rt = pages * rp
    steps = pages // n_pages
    page_specs = [pl.BlockSpec((1, rp, 4096), functools.partial(
        lambda b, s, pt, i: (pt[b, s * n_pages + i], 0, 0), i=i)) for i in range(n_pages)]
    c3 = lambda b, s, pt: (0, 0, 0)
    return pl.pallas_call(
        functools.partial(_compress_body, n_pages=n_pages, rp=rp, rt=rt),
        grid_spec=pltpu.PrefetchScalarGridSpec(
            num_scalar_prefetch=1,
            grid=(nb, steps),
            in_specs=page_specs + [pl.BlockSpec((2, 1, 1024), c3), pl.BlockSpec((2, 1, 1024), c3),
                                   pl.BlockSpec((2, 1024, LANES), c3), pl.BlockSpec((2, 1024, LANES), c3),
                                   pl.BlockSpec((2, LANES, LANES), c3)],
            out_specs=pl.BlockSpec((1, rt, 512), lambda b, s, pt: (b, 0, 0)),
            scratch_shapes=[pltpu.VMEM((4, rt, 1024), F32)]),
        out_shape=jax.ShapeDtypeStruct((nb, rt, 512), BF),
        compiler_params=_params(("parallel", "arbitrary")),
        name="compress",
    )(page_table, *([pool] * n_pages), pea, peb, w1a, w1b, w2p)


def _cmp_body(q_ref, kvc_ref, mimp_ref, ocmp_ref, selb_ref, *, tq, rt, nb, nbp, pos0):
    i = pl.program_id(1)
    pos_n = pos0 + i * tq + lax.broadcasted_iota(I32, (tq, rt), 0)
    n_i = lax.broadcasted_iota(I32, (tq, rt), 1)
    vis = (n_i * CMP_STRIDE + (CMP_BLOCK - 1)) <= pos_n
    lo = _lane_iota(tq) < 64
    pos_b = pos0 + i * tq + lax.broadcasted_iota(I32, (tq, nbp), 0)
    jb = lax.broadcasted_iota(I32, (tq, nbp), 1)
    cur = jnp.right_shift(pos_b, 6)
    valid = (jb * SEL_BLOCK <= pos_b) & (jb < nb)
    forced = (jb == 0) | (jb == cur) | (jb == cur - 1)
    mimp = mimp_ref[...]
    for g in range(NSA_GROUPS):
        kc = kvc_ref[0, :, g * LANES:(g + 1) * LANES]
        vc = kvc_ref[0, :, (2 + g) * LANES:(3 + g) * LANES]
        pg = jnp.zeros((tq, rt), F32)
        og = []
        for r in range(NSA_REP):
            h = g * NSA_REP + r
            q = q_ref[0, :, h * LANES:(h + 1) * LANES]
            s = jnp.where(vis, _dot_nt(q, kc), -jnp.inf)
            m = jnp.max(s, axis=-1, keepdims=True)
            m = jnp.where(m == -jnp.inf, 0.0, m)
            e = jnp.exp(s - m)
            d = jnp.sum(e, axis=-1, keepdims=True)
            p = e / jnp.where(d > 0, d, 1.0)
            og.append(_dot(p.astype(BF), vc))
            pg = pg + p
        imp = _dot_hi(pg, mimp)
        score = jnp.where(valid, jnp.where(forced, imp + FORCE_BONUS, imp), -jnp.inf)
        if tq == nbp and nb % 8 == 0:
            st = jnp.transpose(score)[0:nb]
            jrow = lax.broadcasted_iota(I32, (nb, tq), 0)
            rank_t = jnp.zeros((nb, tq), F32)
            for jj in range(nb):
                cj = st[jj:jj + 1, :]
                rank_t = rank_t + jnp.where((cj > st) | ((cj == st) & (jrow > jj)), 1.0, 0.0)
            rank = jnp.transpose(jnp.concatenate([rank_t, jnp.full((nbp - nb, tq), float(nbp), F32)], axis=0))
            sel = (rank < float(N_SEL)) & valid
        else:
            rank = jnp.zeros((tq, nbp), I32)
            for jj in range(nb):
                cj = score[:, jj:jj + 1]
                beats = (cj > score) | ((cj == score) & (jb > jj))
                rank = rank + beats.astype(I32)
            sel = (rank < N_SEL) & valid
        selb_ref[0, :, g * nbp:(g + 1) * nbp] = jnp.where(sel, 0.0, NEG)
        for c2 in range(2):
            ocmp_ref[0, :, (g * 2 + c2) * LANES:(g * 2 + c2 + 1) * LANES] = _pair(lo, og[2 * c2], og[2 * c2 + 1], 0)


def _cmp_attend(qraw3, kvc, mimp, nq, nb, nbp, pos0):
    nbat = kvc.shape[0]
    _, tq, _ = qraw3.shape
    rt = kvc.shape[1]
    return pl.pallas_call(
        functools.partial(_cmp_body, tq=tq, rt=rt, nb=nb, nbp=nbp, pos0=pos0),
        grid=(nbat, nq),
        in_specs=[pl.BlockSpec((1, tq, 1024), lambda b, i: (b * nq + i, 0, 0)),
                  pl.BlockSpec((1, rt, 512), lambda b, i: (b, 0, 0)),
                  pl.BlockSpec((rt, nbp), lambda b, i: (0, 0))],
        out_specs=[pl.BlockSpec((1, tq, 512), lambda b, i: (b * nq + i, 0, 0)),
                   pl.BlockSpec((1, tq, 2 * nbp), lambda b, i: (b * nq + i, 0, 0))],
        out_shape=[jax.ShapeDtypeStruct((nbat * nq, tq, 512), F32),
                   jax.ShapeDtypeStruct((nbat * nq, tq, 2 * nbp), F32)],
        compiler_params=_params(("parallel", "parallel")),
        name="cmp_attend",
    )(qraw3, kvc, mimp)


def _online_t(s, vt, m_s, l_s, acc, idx):
    m_old = m_s[idx]
    m_new = jnp.maximum(m_old, jnp.max(s, axis=0, keepdims=True))
    alpha = jnp.exp(m_old - m_new)
    p = jnp.exp(s - m_new)
    l_s[idx] = alpha * l_s[idx] + jnp.sum(p, axis=0, keepdims=True)
    acc[idx] = alpha * acc[idx] + _dot(vt, p.astype(BF))
    m_s[idx] = m_new


STEP_FIRST, STEP_LAST, STEP_MASKED = 1, 2, 4


def _step_table(nq, tq, tk, window=None):
    qi, kt, fl = [], [], []
    for i in range(nq):
        first = 0 if window is None else max(0, (i * tq - window + 1) // tk)
        last = (i * tq + tq - 1) // tk
        for j in range(first, last + 1):
            masked = j * tk + tk - 1 > i * tq
            if window is not None:
                masked = masked or j * tk <= i * tq + tq - 1 - window
            qi.append(i)
            kt.append(j)
            fl.append((STEP_FIRST if j == first else 0) | (STEP_LAST if j == last else 0)
                      | (STEP_MASKED if masked else 0))
    return (jnp.asarray(np.array(qi, np.int32)), jnp.asarray(np.array(kt, np.int32)),
            jnp.asarray(np.array(fl, np.int32)))


def _nsa_flash_body(qi_ref, kt_ref, fl_ref, *refs, mode, tq, tk):
    if mode == "sel":
        q_ref, selb_ref, k_ref, v_ref, o_ref, qs, m_s, l_s, acc = refs
    else:
        q_ref, k_ref, v_ref, o_ref, qs, m_s, l_s, acc = refs
    st = pl.program_id(1)
    i = qi_ref[st]
    jt = kt_ref[st]
    flags = fl_ref[st]
    q0 = i * tq
    rows = NSA_REP * tq
    lane = _lane_iota(tq)
    lo = lane < 64

    @pl.when((flags & STEP_FIRST) != 0)
    def _():
        m_s[...] = jnp.full(m_s.shape, NEG, F32)
        l_s[...] = jnp.zeros(l_s.shape, F32)
        acc[...] = jnp.zeros(acc.shape, F32)
        for g in range(NSA_GROUPS):
            if mode == "sel":
                sb = pltpu.roll(selb_ref[0, :, g * LANES:(g + 1) * LANES], 64, 1)
                sb = jnp.where((lane >= 64) & (lane < 96), sb, 0.0)
            for r in range(NSA_REP):
                h = g * NSA_REP + r
                q = q_ref[0, :, h * LANES:(h + 1) * LANES]
                if mode == "sel":
                    q = (q.astype(F32) + sb).astype(BF)
                qs[g, r * tq:(r + 1) * tq, :] = q

    need_mask = (flags & STEP_MASKED) != 0

    def step(masked):
        for g in range(NSA_GROUPS):
            s = _dot_nt(k_ref[:, g * LANES:(g + 1) * LANES], qs[g])
            if masked:
                kpos = jt * tk + lax.broadcasted_iota(I32, (tk, rows), 0)
                rpos = q0 + (lax.broadcasted_iota(I32, (tk, rows), 1) & (tq - 1))
                vis = kpos <= rpos
                if mode == "win":
                    vis = vis & (kpos > rpos - WINDOW)
                s = jnp.where(vis, s, NEG)
            _online_t(s, v_ref[0, g * HEAD_DIM:(g + 1) * HEAD_DIM, :], m_s, l_s, acc, g)

    @pl.when(need_mask)
    def _():
        step(True)

    @pl.when(jnp.logical_not(need_mask))
    def _():
        step(False)

    @pl.when((flags & STEP_LAST) != 0)
    def _():
        for g in range(NSA_GROUPS):
            on = acc[g] / l_s[g]
            for c2 in range(2):
                pair = jnp.concatenate([on[:, (2 * c2) * tq:(2 * c2 + 1) * tq],
                                        on[:, (2 * c2 + 1) * tq:(2 * c2 + 2) * tq]], axis=0)
                o_ref[0, :, (g * 2 + c2) * LANES:(g * 2 + c2 + 1) * LANES] = jnp.transpose(pair)


def _nsa_flash(mode, q3, selb3, k2, v2, nbat, t, tq, tk):
    nq = t // tq
    rows = NSA_REP * tq
    table = _step_table(nq, tq, tk, None if mode == "sel" else WINDOW)
    qmap = lambda b, s, qi, kt, fl: (b * nq + qi[s], 0, 0)
    in_specs = [pl.BlockSpec((1, tq, 1024), qmap)]
    args = [q3]
    if mode == "sel":
        in_specs.append(pl.BlockSpec((1, tq, 2 * LANES), qmap))
        args.append(selb3)
    nkv = t // tk
    in_specs += [pl.BlockSpec((tk, 2 * LANES), lambda b, s, qi, kt, fl: (b * nkv + kt[s], 0)),
                 pl.BlockSpec((1, LANES, tk), lambda b, s, qi, kt, fl: (b, 0, kt[s]))]
    args += [k2, v2]
    return pl.pallas_call(
        functools.partial(_nsa_flash_body, mode=mode, tq=tq, tk=tk),
        grid_spec=pltpu.PrefetchScalarGridSpec(
            num_scalar_prefetch=3, grid=(nbat, int(table[0].shape[0])), in_specs=in_specs,
            out_specs=pl.BlockSpec((1, tq, 512), qmap),
            scratch_shapes=[pltpu.VMEM((NSA_GROUPS, rows, LANES), BF), pltpu.VMEM((NSA_GROUPS, 1, rows), F32),
                            pltpu.VMEM((NSA_GROUPS, 1, rows), F32), pltpu.VMEM((NSA_GROUPS, HEAD_DIM, rows), F32)]),
        out_shape=jax.ShapeDtypeStruct((nbat * nq, tq, 512), F32),
        compiler_params=_params(("parallel", "arbitrary")),
        name="nsa_flash_" + mode,
    )(*table, *args)


def _diff_lambda(dl, lambda_init):
    a = jnp.sum(dl[0:1] * dl[1:2], axis=1, keepdims=True)
    b = jnp.sum(dl[2:3] * dl[3:4], axis=1, keepdims=True)
    return jnp.exp(a) - jnp.exp(b) + lambda_init


def _diff_finish(a0, a1, lam, subln, lambda_init):
    o = a0 - lam * a1
    o = o * lax.rsqrt(jnp.mean(o * o, axis=-1, keepdims=True) + RMS_EPS) * subln
    return o * (1.0 - lambda_init)


def _diff_flash_body(qi_ref, kt_ref, fl_ref, q_ref, k_ref, v_ref, dl_ref, sub_ref, o_ref, qs, m_s, l_s, acc, *,
                     tq, tk, lambda_init):
    st = pl.program_id(1)
    i = qi_ref[st]
    j = kt_ref[st]
    flags = fl_ref[st]
    q0 = i * tq
    rows = 2 * tq
    lo = _lane_iota(tq) < 64

    @pl.when((flags & STEP_FIRST) != 0)
    def _():
        m_s[...] = jnp.full(m_s.shape, NEG, F32)
        l_s[...] = jnp.zeros(l_s.shape, F32)
        acc[...] = jnp.zeros(acc.shape, F32)
        for h in range(DIFF_HEADS):
            q = q_ref[0, :, h * LANES:(h + 1) * LANES].astype(F32)
            qs[h, 0:tq, :] = jnp.where(lo, q, 0.0).astype(BF)
            qs[h, tq:2 * tq, :] = jnp.where(lo, 0.0, q).astype(BF)

    need_mask = (flags & STEP_MASKED) != 0

    def step(masked):
        for h in range(DIFF_HEADS):
            s = _dot_nt(k_ref[:, h * LANES:(h + 1) * LANES], qs[h])
            if masked:
                kpos = j * tk + lax.broadcasted_iota(I32, (tk, rows), 0)
                rpos = q0 + (lax.broadcasted_iota(I32, (tk, rows), 1) & (tq - 1))
                s = jnp.where(kpos <= rpos, s, NEG)
            _online_t(s, v_ref[0, h * LANES:(h + 1) * LANES, :], m_s, l_s, acc, h)

    @pl.when(need_mask)
    def _():
        step(True)

    @pl.when(jnp.logical_not(need_mask))
    def _():
        step(False)

    @pl.when((flags & STEP_LAST) != 0)
    def _():
        lam = _diff_lambda(dl_ref[...], lambda_init)
        for h in range(DIFF_HEADS):
            on = acc[h] / l_s[h]
            a0, a1 = [jnp.concatenate([jnp.transpose(on[:, c0 + c * LANES:c0 + (c + 1) * LANES])
                                       for c in range(tq // LANES)], axis=0) for c0 in (0, tq)]
            o_ref[0, :, h * LANES:(h + 1) * LANES] = _diff_finish(a0, a1, lam, sub_ref[...], lambda_init)


def _diff_flash(q3, k2, v2, dl, subln, nbat, t, tq, tk, lambda_init):
    nq = t // tq
    nk = t // tk
    table = _step_table(nq, tq, tk)
    qmap = lambda b, s, qi, kt, fl: (b * nq + qi[s], 0, 0)
    kmap = lambda b, s, qi, kt, fl: (b * nk + kt[s], 0)
    vmap = lambda b, s, qi, kt, fl: (b, 0, kt[s])
    const = lambda b, s, qi, kt, fl: (0, 0)
    rows = 2 * tq
    return pl.pallas_call(
        functools.partial(_diff_flash_body, tq=tq, tk=tk, lambda_init=lambda_init),
        grid_spec=pltpu.PrefetchScalarGridSpec(
            num_scalar_prefetch=3, grid=(nbat, int(table[0].shape[0])),
            in_specs=[pl.BlockSpec((1, tq, 512), qmap), pl.BlockSpec((tk, 512), kmap),
                      pl.BlockSpec((1, 512, tk), vmap), pl.BlockSpec((4, HEAD_DIM), const),
                      pl.BlockSpec((1, DIFF_DV), const)],
            out_specs=pl.BlockSpec((1, tq, 512), qmap),
            scratch_shapes=[pltpu.VMEM((DIFF_HEADS, rows, LANES), BF), pltpu.VMEM((DIFF_HEADS, 1, rows), F32),
                            pltpu.VMEM((DIFF_HEADS, 1, rows), F32), pltpu.VMEM((DIFF_HEADS, DIFF_DV, rows), F32)]),
        out_shape=jax.ShapeDtypeStruct((nbat * nq, tq, 512), F32),
        compiler_params=_params(("parallel", "arbitrary")),
        name="diff_flash",
    )(*table, q3, k2, v2, dl, subln)


def _paged_body(pt_ref, *refs, mode, n_pages, past_len, lambda_init):
    q_ref = refs[0]
    k = 1
    if mode == "sel":
        selb_ref = refs[k]
        k += 1
    kpages = refs[k:k + n_pages]
    k += n_pages
    if mode == "diff":
        vpages = refs[k:k + n_pages]
        k += n_pages
    newk_ref = refs[k]
    k += 1
    if mode == "diff":
        newv_ref, dl_ref, sub_ref = refs[k:k + 3]
        k += 3
    o_ref = refs[k]
    k += 1
    if mode == "sel":
        qbd, bias, m_s, l_s, acc = refs[k:]
    else:
        qbd, m_s, l_s, acc = refs[k:]
    kw = WINDOW if mode == "win" else LANES
    st = pl.program_id(1)
    nq = 8
    rows = 64
    lane8 = _lane_iota(nq)
    lo8 = lane8 < 64
    t_row = lax.broadcasted_iota(I32, (rows, LANES), 0) & (nq - 1)
    col = lax.broadcasted_iota(I32, (rows, LANES), 1)

    def update(s, pv):
        m_old = m_s[0]
        m_new = jnp.maximum(m_old, jnp.max(s, axis=-1, keepdims=True))
        alpha = jnp.exp(m_old - m_new)
        p = jnp.exp(s - m_new)
        l_s[0] = alpha * l_s[0] + jnp.sum(p, axis=-1, keepdims=True)
        acc[0] = alpha * acc[0] + pv(p.astype(BF))
        m_s[0] = m_new

    def per_head(p, v_of_head):
        return jnp.concatenate([_dot(p[h * 16:(h + 1) * 16], v_of_head(h)) for h in range(DIFF_HEADS)], axis=0)

    @pl.when(st == 0)
    def _():
        m_s[0] = jnp.full((rows, 1), NEG, F32)
        l_s[0] = jnp.zeros((rows, 1), F32)
        acc[0] = jnp.zeros(acc.shape[1:], F32)
        if mode == "diff":
            qbd[...] = jnp.zeros(qbd.shape, F32)
            for h in range(DIFF_HEADS):
                q = q_ref[0, :, h * LANES:(h + 1) * LANES].astype(F32)
                qbd[(2 * h) * nq:(2 * h + 1) * nq, h * LANES:(h + 1) * LANES] = jnp.where(lo8, q, 0.0)
                qbd[(2 * h + 1) * nq:(2 * h + 2) * nq, h * LANES:(h + 1) * LANES] = jnp.where(lo8, 0.0, q)
        else:
            for h in range(NSA_HEADS):
                g = h // NSA_REP
                q = q_ref[0, :, h * LANES:(h + 1) * LANES].astype(F32)
                qbd[h * nq:(h + 1) * nq, :] = q if g == 0 else pltpu.roll(q, 64, 1)
                if mode == "sel":
                    bias[h * nq:(h + 1) * nq, :] = selb_ref[0, :, g * 256:(g + 1) * 256]
        kn = newk_ref[0]
        if mode == "diff":
            kk, vv = kn, newv_ref[0]
        else:
            kk, vv = kn[:, 0:LANES], kn[:, LANES:2 * LANES]
        kpad = jnp.concatenate([kk, jnp.zeros((LANES - nq, kk.shape[1]), F32)], axis=0).astype(BF)
        vpad = jnp.concatenate([vv, jnp.zeros((LANES - nq, vv.shape[1]), F32)], axis=0).astype(BF)
        s = _dot_nt(qbd[...].astype(BF), kpad)
        if mode == "sel":
            jnew = past_len // SEL_BLOCK
            s = s + bias[:, jnew:jnew + 1]
        s = jnp.where((col < nq) & (col <= t_row), s, NEG)
        if mode == "diff":
            update(s, lambda p: per_head(p, lambda h: vpad[:, h * LANES:(h + 1) * LANES]))
        else:
            update(s, lambda p: _dot(p, vpad))

    qb = qbd[...].astype(BF)
    nkeys = n_pages * kw
    if mode == "diff":
        kt = jnp.concatenate([kpages[i][0].reshape(4 * LANES, LANES).astype(BF) for i in range(n_pages)], axis=1)
    else:
        kt = jnp.concatenate([kpages[i][0, 0].reshape(LANES, kw).astype(BF) for i in range(n_pages)], axis=1)
    s = _dot(qb, kt)
    if mode == "sel":
        krow = lax.broadcasted_iota(I32, (nkeys, 256), 0)
        jbi = lax.broadcasted_iota(I32, (nkeys, 256), 1)
        ep = jnp.where(jbi == 2 * n_pages * st + jnp.right_shift(krow, 6), 1.0, 0.0).astype(BF)
        s = s + _dot_nt(bias[...].astype(BF), ep)
    if mode == "win":
        t_w = lax.broadcasted_iota(I32, (rows, nkeys), 0) & (nq - 1)
        s = jnp.where(lax.broadcasted_iota(I32, (rows, nkeys), 1) > t_w, s, NEG)
    if mode == "diff":
        update(s, lambda p: per_head(p, lambda h: jnp.concatenate(
            [vpages[i][0, :, h, :].astype(BF) for i in range(n_pages)], axis=0)))
    else:
        vt = jnp.concatenate([kpages[i][0, 1].reshape(LANES, kw).astype(BF) for i in range(n_pages)], axis=1)
        update(s, lambda p: _dot_nt(p, vt))

    @pl.when(st == pl.num_programs(1) - 1)
    def _():
        on = acc[0] / l_s[0]
        if mode == "diff":
            lam = _diff_lambda(dl_ref[...], lambda_init)
            for h in range(DIFF_HEADS):
                a0 = on[h * 16:h * 16 + nq]
                a1 = on[h * 16 + nq:h * 16 + 2 * nq]
                o_ref[0, :, h * LANES:(h + 1) * LANES] = _diff_finish(a0, a1, lam, sub_ref[...], lambda_init)
        else:
            for c in range(4):
                a = on[(2 * c) * nq:(2 * c + 1) * nq]
                b = on[(2 * c + 1) * nq:(2 * c + 2) * nq]
                o_ref[0, :, c * LANES:(c + 1) * LANES] = _pair(lo8, a, b, c // 2)


def _paged_attend(mode, q3, selb3, pool_k, pool_v, page_table, newk3, newv3, dl, subln, n_pages, past_len, lambda_init):
    nbat, pages = page_table.shape
    steps = pages // n_pages
    wq = q3.shape[2]
    wk = newk3.shape[2]
    bmap = lambda b, s, pt: (b, 0, 0)
    c2 = lambda b, s, pt: (0, 0)

    def pmap(i, rank):
        return lambda b, s, pt: (pt[b, s * n_pages + i],) + (0,) * (rank - 1)

    in_specs = [pl.BlockSpec((1, 8, wq), bmap)]
    args = [q3]
    if mode == "sel":
        in_specs.append(pl.BlockSpec((1, 8, 512), bmap))
        args.append(selb3)
    kblock = (1,) + pool_k.shape[1:]
    in_specs += [pl.BlockSpec(kblock, pmap(i, len(kblock))) for i in range(n_pages)]
    args += [pool_k] * n_pages
    if mode == "diff":
        in_specs += [pl.BlockSpec((1, LANES, DIFF_HEADS, DIFF_DV), pmap(i, 4)) for i in range(n_pages)]
        args += [pool_v] * n_pages
    in_specs.append(pl.BlockSpec((1, 8, wk), bmap))
    args.append(newk3)
    if mode == "diff":
        in_specs += [pl.BlockSpec((1, 8, 512), bmap), pl.BlockSpec((4, HEAD_DIM), c2), pl.BlockSpec((1, DIFF_DV), c2)]
        args += [newv3, dl, subln]
    ck = 512 if mode == "diff" else LANES
    scratch = [pltpu.VMEM((64, ck), F32)]
    if mode == "sel":
        scratch.append(pltpu.VMEM((64, 256), F32))
    scratch += [pltpu.VMEM((1, 64, 1), F32), pltpu.VMEM((1, 64, 1), F32), pltpu.VMEM((1, 64, LANES), F32)]
    return pl.pallas_call(
        functools.partial(_paged_body, mode=mode, n_pages=n_pages, past_len=past_len, lambda_init=lambda_init),
        grid_spec=pltpu.PrefetchScalarGridSpec(
            num_scalar_prefetch=1, grid=(nbat, steps), in_specs=in_specs,
            out_specs=pl.BlockSpec((1, 8, 512), bmap), scratch_shapes=scratch),
        out_shape=jax.ShapeDtypeStruct((nbat, 8, 512), F32),
        compiler_params=_params(("parallel", "arbitrary")),
        name="paged_" + mode,
    )(page_table, *args)


def _merge_body(x_ref, oc_ref, os_ref, ow_ref, od_ref, gx_ref, nmix_ref, nffn_ref, wg_ref, wbn_ref, wbd_ref, wo_ref,
                wr_ref, br_ref, x1_ref, h_ref, idx_ref, gate_ref):
    x = x_ref[...]
    tm = x.shape[0]
    xn = (x * lax.rsqrt(jnp.mean(x * x, axis=-1, keepdims=True) + RMS_EPS) * nmix_ref[...]).astype(BF)
    gx = gx_ref[...]
    o_nsa = gx[:, 0:512] * oc_ref[...] + gx[:, 512:1024] * os_ref[...] + gx[:, 1024:1536] * ow_ref[...]
    y_nsa = _dot(o_nsa.astype(BF), wbn_ref[...])
    y_diff = _dot(od_ref[...].astype(BF), wbd_ref[...])
    gates = jax.nn.sigmoid(_dot(xn, wg_ref[...]))
    mrg = gates[:, 0:D_MODEL] * y_nsa + gates[:, D_MODEL:2 * D_MODEL] * y_diff
    x1 = x + _dot(mrg.astype(BF), wo_ref[...])
    x1_ref[...] = x1
    h = x1 * lax.rsqrt(jnp.mean(x1 * x1, axis=-1, keepdims=True) + RMS_EPS) * nffn_ref[...]
    h_ref[...] = h
    logits = _dot_hi(h, wr_ref[...]) + br_ref[...]
    lane = _lane_iota(tm)
    vals, idxs = [], []
    for _ in range(TOP_K):
        m = jnp.max(logits, axis=-1, keepdims=True)
        idx = jnp.min(jnp.where(logits == m, lane, LANES), axis=-1, keepdims=True)
        vals.append(m)
        idxs.append(idx)
        logits = jnp.where(lane == idx, -jnp.inf, logits)
    es = [jnp.exp(v - vals[0]) for v in vals]
    den = es[0] + es[1] + es[2] + es[3]
    idx_out = jnp.zeros((tm, LANES), I32)
    gate_out = jnp.zeros((tm, LANES), F32)
    for k in range(TOP_K):
        idx_out = jnp.where(lane == k, idxs[k], idx_out)
        gate_out = jnp.where(lane == k, es[k] / den, gate_out)
    idx_ref[...] = idx_out
    gate_ref[...] = gate_out


def _merge(x2d, oc, osel, ow, od, gx, nmix, nffn, wg, wbn, wbd, wo, wr, br, tm):
    n = x2d.shape[0]
    row = lambda i: (i, 0)
    const = lambda i: (0, 0)
    return pl.pallas_call(
        _merge_body,
        grid=(n // tm,),
        in_specs=[pl.BlockSpec((tm, D_MODEL), row)] + [pl.BlockSpec((tm, 512), row)] * 4 +
                 [pl.BlockSpec((tm, 1536), row), pl.BlockSpec((1, D_MODEL), const), pl.BlockSpec((1, D_MODEL), const),
                  pl.BlockSpec((D_MODEL, 2 * D_MODEL), const), pl.BlockSpec((512, D_MODEL), const),
                  pl.BlockSpec((512, D_MODEL), const), pl.BlockSpec((D_MODEL, D_MODEL), const),
                  pl.BlockSpec((D_MODEL, LANES), const), pl.BlockSpec((1, LANES), const)],
        out_specs=[pl.BlockSpec((tm, D_MODEL), row), pl.BlockSpec((tm, D_MODEL), row),
                   pl.BlockSpec((tm, LANES), row), pl.BlockSpec((tm, LANES), row)],
        out_shape=[jax.ShapeDtypeStruct((n, D_MODEL), F32), jax.ShapeDtypeStruct((n, D_MODEL), F32),
                   jax.ShapeDtypeStruct((n, LANES), I32), jax.ShapeDtypeStruct((n, LANES), F32)],
        compiler_params=_params(("parallel",)),
        name="merge",
    )(x2d, oc, osel, ow, od, gx, nmix, nffn, wg, wbn, wbd, wo, wr, br)


def _route_body(idx_ref, dest_ref, meta_ref, cnt_s, base_s, *, tm, tmoe, nbp):
    ph = pl.program_id(0)
    i = pl.program_id(1)
    shift = int(math.log2(tmoe))
    idx_t = jnp.transpose(idx_ref[...])
    sub = lax.broadcasted_iota(I32, (LANES, tm), 0)
    ohs = [idx_t[k:k + 1, :] == sub for k in range(TOP_K)]
    tok = jnp.zeros((LANES, tm), F32)
    for oh in ohs:
        tok = tok + jnp.where(oh, 1.0, 0.0)
    tile_cnt = jnp.broadcast_to(jnp.sum(tok, axis=1, keepdims=True), (LANES, LANES))

    @pl.when((ph == 0) & (i == 0))
    def _():
        cnt_s[...] = jnp.zeros((LANES, LANES), F32)

    @pl.when(ph == 0)
    def _():
        cnt_s[...] = cnt_s[...] + tile_cnt

    @pl.when((ph == 1) & (i == 0))
    def _():
        cnt = cnt_s[...].astype(I32)
        padded = jnp.left_shift(jnp.right_shift(cnt + (tmoe - 1), shift), shift)
        er = lax.broadcasted_iota(I32, (LANES, LANES), 0)
        ec = lax.broadcasted_iota(I32, (LANES, LANES), 1)
        pad_start = _dot_hi(jnp.where(ec < er, 1.0, 0.0), padded.astype(F32))
        base_s[...] = pad_start
        pad_end = pad_start.astype(I32) + padded
        blk_start = lax.broadcasted_iota(I32, (LANES, nbp), 1) * tmoe
        ends = jnp.broadcast_to(pad_end[:, 0:1], (LANES, nbp))
        be = jnp.minimum(jnp.sum(jnp.where(ends <= blk_start, 1, 0), axis=0, keepdims=True), N_EXPERTS - 1)
        n_used = jnp.right_shift(jnp.max(pad_end, axis=0, keepdims=True)[:, 0:1], shift)
        last_blk = jnp.sum(jnp.where(er == ec, jnp.where(padded > 0, jnp.right_shift(pad_end, shift) - 1, -1), 0),
                           axis=0, keepdims=True)
        lb = jnp.concatenate([last_blk, jnp.full((1, nbp - LANES), -1, I32)], axis=1)
        row = lax.broadcasted_iota(I32, (8, nbp), 0)
        meta_ref[...] = jnp.where(row == 0, be, jnp.where(row == 1, n_used, jnp.where(row == 2, lb, 0)))

    @pl.when(ph == 1)
    def _():
        nr = lax.broadcasted_iota(I32, (tm, tm), 0)
        nc = lax.broadcasted_iota(I32, (tm, tm), 1)
        before = _dot(tok.astype(BF), jnp.where(nr < nc, 1.0, 0.0).astype(BF))
        pos = before + base_s[:, 0:1]
        rows = [jnp.sum(jnp.where(oh, pos, 0.0), axis=0, keepdims=True) for oh in ohs]
        dest_ref[...] = jnp.concatenate(rows + [jnp.zeros((8 - TOP_K, tm), F32)], axis=0).astype(I32)
        base_s[...] = base_s[...] + tile_cnt


def _route(idx_all, tm, tmoe, nbp):
    n_tok = idx_all.shape[0]
    return pl.pallas_call(
        functools.partial(_route_body, tm=tm, tmoe=tmoe, nbp=nbp),
        grid=(2, n_tok // tm),
        in_specs=[pl.BlockSpec((tm, LANES), lambda ph, i: (i, 0))],
        out_specs=[pl.BlockSpec((8, tm), lambda ph, i: (0, i * ph)), pl.BlockSpec((8, nbp), lambda ph, i: (0, 0))],
        out_shape=[jax.ShapeDtypeStruct((8, n_tok), I32), jax.ShapeDtypeStruct((8, nbp), I32)],
        scratch_shapes=[pltpu.VMEM((LANES, LANES), F32), pltpu.VMEM((LANES, LANES), F32)],
        compiler_params=_params(("arbitrary", "arbitrary")),
        name="route",
    )(idx_all)


def _dispatch_body(dest_ref, lb_ref, nu_ref, h_ref, xs_hbm, zbuf, zsem, sem, *, tm, tmoe, n_tok):
    i = pl.program_id(0)

    @pl.when(i == 0)
    def _():
        zbuf[...] = jnp.zeros(zbuf.shape, F32)

        def fill(e):
            return pltpu.make_async_copy(zbuf, xs_hbm.at[pl.ds(pl.multiple_of(lb_ref[e] * tmoe, tmoe), tmoe)], zsem)
        for e in range(N_EXPERTS):
            @pl.when(lb_ref[e] >= 0)
            def _():
                fill(e).start()
        for e in range(N_EXPERTS):
            @pl.when(lb_ref[e] >= 0)
            def _():
                fill(e).wait()

        def tail(b):
            return pltpu.make_async_copy(zbuf, xs_hbm.at[pl.ds(pl.multiple_of(b * tmoe, tmoe), tmoe)], zsem)
        n_blk = xs_hbm.shape[0] // tmoe
        lax.fori_loop(nu_ref[0], n_blk, lambda b, c: (tail(b).start(), c)[1], 0)
        lax.fori_loop(nu_ref[0], n_blk, lambda b, c: (tail(b).wait(), c)[1], 0)

    def body(n, c):
        for k in range(TOP_K):
            d = dest_ref[k * n_tok + i * tm + n]
            pltpu.make_async_copy(h_ref.at[pl.ds(n, 1)], xs_hbm.at[pl.ds(d, 1)], sem).start(priority=k % 2)
        return c
    lax.fori_loop(0, tm, body, 0, unroll=8)
    pltpu.make_async_copy(xs_hbm.at[pl.ds(0, TOP_K * tm)], xs_hbm.at[pl.ds(0, TOP_K * tm)], sem).wait()


def _dispatch(dest_flat, last_blk, n_used, h_all, n_rows, tm, tmoe):
    n_tok = h_all.shape[0]
    return pl.pallas_call(
        functools.partial(_dispatch_body, tm=tm, tmoe=tmoe, n_tok=n_tok),
        grid_spec=pltpu.PrefetchScalarGridSpec(
            num_scalar_prefetch=3, grid=(n_tok // tm,),
            in_specs=[pl.BlockSpec((tm, D_MODEL), lambda i, d, lb, nu: (i, 0))],
            out_specs=pl.BlockSpec(memory_space=pl.ANY),
            scratch_shapes=[pltpu.VMEM((tmoe, D_MODEL), F32), pltpu.SemaphoreType.DMA(()), pltpu.SemaphoreType.DMA(())]),
        out_shape=jax.ShapeDtypeStruct((n_rows, D_MODEL), F32),
        compiler_params=pltpu.CompilerParams(dimension_semantics=("arbitrary",), vmem_limit_bytes=VMEM_LIMIT,
                                             disable_bounds_checks=True),
        name="moe_dispatch",
    )(dest_flat, last_blk, n_used, h_all)


def _moe_body(be_ref, nused_ref, x_ref, wgu_ref, bgu_ref, wd_ref, bd_ref, o_ref, wgu_s, wd_s):
    i = pl.program_id(0)
    e = be_ref[i]
    prev = be_ref[jnp.maximum(i - 1, 0)]

    @pl.when((i == 0) | (e != prev))
    def _():
        def cast_rows(c, carry):
            r0 = pl.multiple_of(c * LANES, LANES)
            wgu_s[pl.ds(r0, LANES), :] = wgu_ref[0, pl.ds(r0, LANES), :].astype(BF)
            wd_s[pl.ds(r0, LANES), :] = wd_ref[0, pl.ds(r0, LANES), :].astype(BF)
            return carry
        lax.fori_loop(0, D_MODEL // LANES, cast_rows, 0)

    @pl.when(i < nused_ref[0])
    def _():
        gu = _dot(x_ref[...].astype(BF), wgu_s[...]) + bgu_ref[0]
        gate = jnp.minimum(gu[:, 0:D_FF], SWIGLU_LIMIT)
        up = jnp.clip(gu[:, D_FF:2 * D_FF], -SWIGLU_LIMIT, SWIGLU_LIMIT)
        act = (up + 1.0) * gate * jax.nn.sigmoid(SWIGLU_ALPHA * gate)
        o_ref[...] = _dot(act.astype(BF), wd_s[...]) + bd_ref[0]

    @pl.when(i >= nused_ref[0])
    def _():
        o_ref[...] = jnp.zeros(o_ref.shape, F32)


def _moe(block_expert, n_used, xs, wgu, bgu, wd, bd, tmoe):
    n_rows = xs.shape[0]
    nblk = n_rows // tmoe
    xmap = lambda i, be, nu: (jnp.minimum(i, jnp.maximum(nu[0] - 1, 0)), 0)
    emap = lambda i, be, nu: (be[i], 0, 0)
    return pl.pallas_call(
        _moe_body,
        grid_spec=pltpu.PrefetchScalarGridSpec(
            num_scalar_prefetch=2, grid=(nblk,),
            in_specs=[pl.BlockSpec((tmoe, D_MODEL), xmap),
                      pl.BlockSpec((1, D_MODEL, 2 * D_FF), emap), pl.BlockSpec((1, 1, 2 * D_FF), emap),
                      pl.BlockSpec((1, D_FF, D_MODEL), emap), pl.BlockSpec((1, 1, D_MODEL), emap)],
            out_specs=pl.BlockSpec((tmoe, D_MODEL), lambda i, be, nu: (i, 0)),
            scratch_shapes=[pltpu.VMEM((D_MODEL, 2 * D_FF), BF), pltpu.VMEM((D_FF, D_MODEL), BF)]),
        out_shape=jax.ShapeDtypeStruct((n_rows, D_MODEL), F32),
        compiler_params=_params(("arbitrary",)),
        name="moe_experts",
    )(block_expert, n_used, xs, wgu, bgu, wd, bd)


def _final_body(dest_ref, x1_ref, gate_ref, nf_ref, ys_hbm, y_ref, ybuf, sem, *, tm, n_tok, block0):
    i = pl.program_id(0)

    def gather(tile, sl):
        def body(n, c):
            for k in range(TOP_K):
                d = dest_ref[k * n_tok + (block0 + tile) * tm + n]
                pltpu.make_async_copy(ys_hbm.at[pl.ds(d, 1)], ybuf.at[sl, k, pl.ds(n, 1)],
                                      sem.at[sl]).start(priority=k % 2)
            return c
        lax.fori_loop(0, tm, body, 0, unroll=8)

    @pl.when(i == 0)
    def _():
        gather(0, 0)

    @pl.when(i + 1 < pl.num_programs(0))
    def _():
        gather(i + 1, (i + 1) % 2)

    slot = i % 2
    pltpu.make_async_copy(ybuf.at[slot], ybuf.at[slot], sem.at[slot]).wait()
    gate = gate_ref[...]
    x = x1_ref[...]
    for k in range(TOP_K):
        x = x + gate[:, k:k + 1] * ybuf[slot, k]
    y_ref[...] = x * lax.rsqrt(jnp.mean(x * x, axis=-1, keepdims=True) + RMS_EPS) * nf_ref[...]


def _final(dest_flat, x1, gates, nf, ys, tm, n_tok, block0):
    n = x1.shape[0]
    row = lambda i, d: (i, 0)
    return pl.pallas_call(
        functools.partial(_final_body, tm=tm, n_tok=n_tok, block0=block0),
        grid_spec=pltpu.PrefetchScalarGridSpec(
            num_scalar_prefetch=1, grid=(n // tm,),
            in_specs=[pl.BlockSpec((tm, D_MODEL), row), pl.BlockSpec((tm, LANES), row),
                      pl.BlockSpec((1, D_MODEL), lambda i, d: (0, 0)), pl.BlockSpec(memory_space=pl.ANY)],
            out_specs=pl.BlockSpec((tm, D_MODEL), row),
            scratch_shapes=[pltpu.VMEM((2, TOP_K, tm, D_MODEL), F32), pltpu.SemaphoreType.DMA((2,))]),
        out_shape=jax.ShapeDtypeStruct((n, D_MODEL), F32),
        compiler_params=pltpu.CompilerParams(dimension_semantics=("arbitrary",), vmem_limit_bytes=VMEM_LIMIT,
                                             disable_bounds_checks=True),
        name="final_norm",
    )(dest_flat, x1, gates, nf, ys)


def _rope_tables(pos, nb_lanes):
    half = ROT_DIM // 2
    inv = ROPE_THETA ** (-jnp.arange(half, dtype=F32) / half)
    ang = pos.astype(F32)[:, None] * inv[None, :]
    cos, sin = jnp.cos(ang), jnp.sin(ang)
    t = pos.shape[0]
    c64 = jnp.concatenate([cos, cos, jnp.ones((t, HEAD_DIM - ROT_DIM), F32)], axis=1)
    s64 = jnp.concatenate([-sin, sin, jnp.zeros((t, HEAD_DIM - ROT_DIM), F32)], axis=1)
    lane = jnp.arange(LANES)[None, :]
    e = ((lane >= 64) & (lane < 64 + nb_lanes) & ((pos // SEL_BLOCK)[:, None] == lane - 64)).astype(F32)
    return jnp.tile(c64, (1, 2)), jnp.tile(s64, (1, 2)), e


def _imp_matrix(rt, n_cmp, nb, nbp):
    m = np.zeros((rt, nbp), np.float32)
    sub = SEL_BLOCK // CMP_STRIDE
    for n in range(n_cmp):
        for shift in range(CMP_BLOCK // CMP_STRIDE):
            j = (n + shift) // sub
            if j < nb:
                m[n, j] += 1.0
    return jnp.asarray(m)


def kernel(x_prompt, x_sample, cache_cmp_kv, cache_sel_kv, state_win_kv, cache_diff_k, cache_diff_v, page_table, norm_mix, w_in, cmp_pe, w_cmp1, w_cmp2, diff_lambda, diff_subln, w_br_nsa, w_br_diff, w_out, norm_ffn, w_router, b_router, w_gate_up, b_gate_up, w_down, b_down, norm_final):
    bp, t, _ = x_prompt.shape
    bs, ts, _ = x_sample.shape
    assert ts == 8 and t % 256 == 0 and t <= 2048 and w_in.shape[0] == 1
    n_pool, page = cache_cmp_kv.shape[1], cache_cmp_kv.shape[2]
    pages = page_table.shape[1]
    past_len = pages * page
    assert page == LANES and past_len % SEL_BLOCK == 0
    lambda_init = 0.8 - 0.6 * math.exp(-0.3 * 0)
    n_p, n_s = bp * t, bs * ts

    w = w_in[0]
    gcols = w[:, 1280:1304].reshape(D_MODEL, NSA_HEADS, 3).transpose(0, 2, 1)
    gexp = jnp.broadcast_to(gcols[..., None], (D_MODEL, 3, NSA_HEADS, HEAD_DIM)).reshape(D_MODEL, 1536)
    w_proj = jnp.concatenate([w[:, 0:1280], w[:, 1304:2840], gexp], axis=1).astype(BF)
    w_gates = w[:, 2840:4888].astype(BF)
    nmix = norm_mix[0][None, :]
    nffn = norm_ffn[0][None, :]
    pe = cmp_pe[0]
    pea = pe[:, :CMP_STRIDE].reshape(2, 1, 1024)
    peb = pe[:, CMP_STRIDE:].reshape(2, 1, 1024)
    w1a = w_cmp1[0][:, :1024].astype(BF)
    w1b = w_cmp1[0][:, 1024:].astype(BF)
    w2p = jnp.pad(w_cmp2[0], ((0, 0), (0, 0), (0, LANES - HEAD_DIM))).astype(BF)
    wbn = w_br_nsa[0].astype(BF)
    wbd = w_br_diff[0].astype(BF)
    wo = w_out[0].astype(BF)
    wr = jnp.pad(w_router[0], ((0, 0), (0, LANES - N_EXPERTS)))
    br = jnp.concatenate([b_router[0], jnp.full((LANES - N_EXPERTS,), NEG, F32)])[None, :]
    dl = diff_lambda[0]
    subln = diff_subln[0][None, :]

    pos_p = jnp.arange(t, dtype=I32)
    pos_s = jnp.tile(past_len + jnp.arange(ts, dtype=I32), bs)
    tm = 256
    w_rows = jnp.concatenate([w[:, 0:768], w[:, 1304:1816], w[:, 2328:2840], gexp, w[:, 768:896], w[:, 1024:1152],
                              w[:, 1816:2328]], axis=1).astype(BF)
    w_cols = jnp.concatenate([w[:, 512:1280], w[:, 1816:2840]], axis=1).T.astype(BF)
    cos_p, sin_p, _ = _rope_tables(pos_p, 0)
    outs_p = _inproj_t(x_prompt.reshape(n_p, D_MODEL), nmix, w_rows, w_cols, cos_p, sin_p,
                       cos_p[:, 0:8].T, -sin_p[:, 0:8].T, bp, t, tm)
    outs_s = _inproj(x_sample.reshape(n_s, D_MODEL), nmix, w_proj, *_rope_tables(pos_s, 0), n_s)
    (qraw_p, qrot_p, kvc_p, kvct_p, kvst_p, kvwt_p, ksa_p, kwa_p, vst_p, vwt_p, qd_p, kdt_p, kdb_p, vd4_p, vdtb_p,
     gx_p) = outs_p
    (qraw_s, qrot_s, kvc_s, kvs_s, kvw_s, _, _, _, _, qd_s, kd_s, _, vd_s, _, gx_s) = outs_s

    rt_p = t // CMP_STRIDE
    n_cmp_p = (t - CMP_BLOCK) // CMP_STRIDE + 1
    nb_p = t // SEL_BLOCK
    kvcmp_p = _compress(kvc_p.reshape(bp, rt_p, 4096), jnp.arange(bp, dtype=I32)[:, None], pea, peb, w1a, w1b, w2p, 1)
    tq = 128
    nq = t // tq
    ocmp_p, selb_p = _cmp_attend(qraw_p.reshape(bp * nq, tq, 1024), kvcmp_p, _imp_matrix(rt_p, n_cmp_p, nb_p, LANES),
                                 nq, nb_p, LANES, 0)
    qrot3 = qrot_p.reshape(bp * nq, tq, 1024)
    osel_p = _nsa_flash("sel", qrot3, selb_p, ksa_p, vst_p, bp, t, tq, 512)
    owin_p = _nsa_flash("win", qrot3, None, kwa_p, vwt_p, bp, t, tq, 256)
    tq_d = 256
    odiff_p = _diff_flash(qd_p.reshape(n_p // tq_d, tq_d, 512), kdb_p, vdtb_p, dl, subln, bp, t, tq_d, 512, lambda_init)

    n_cmp_s = (past_len + ts - CMP_BLOCK) // CMP_STRIDE + 1
    rt_s = past_len // CMP_STRIDE
    assert n_cmp_s <= rt_s
    nb_s = (past_len + ts + SEL_BLOCK - 1) // SEL_BLOCK
    assert nb_s <= 256
    ppg = 16 if pages % 16 == 0 else (8 if pages % 8 == 0 else 1)
    kvcmp_s = _compress(cache_cmp_kv[0].reshape(n_pool, page // CMP_STRIDE, 4096), page_table, pea, peb, w1a, w1b,
                        w2p, ppg)
    ocmp_s, selb_s = _cmp_attend(qraw_s.reshape(bs, ts, 1024), kvcmp_s, _imp_matrix(rt_s, n_cmp_s, nb_s, 256),
                                 1, nb_s, 256, past_len)
    qrot_s3 = qrot_s.reshape(bs, ts, 1024)
    sel_t = jnp.transpose(cache_sel_kv[0], (0, 2, 3, 4, 1))
    win_t = jnp.transpose(state_win_kv[0], (0, 2, 3, 4, 1))
    dk_t = jnp.transpose(cache_diff_k[0], (0, 2, 3, 4, 1))
    osel_s = _paged_attend("sel", qrot_s3, selb_s, sel_t, None, page_table,
                           kvs_s.reshape(bs, ts, 256), None, None, None, ppg, past_len, lambda_init)
    wbuf = state_win_kv.shape[2]
    assert wbuf == WINDOW
    owin_s = _paged_attend("win", qrot_s3, None, win_t, None, jnp.arange(bs, dtype=I32)[:, None],
                           kvw_s.reshape(bs, ts, 256), None, None, None, 1, past_len, lambda_init)
    odiff_s = _paged_attend("diff", qd_s.reshape(bs, ts, 512), None, dk_t, cache_diff_v[0], page_table,
                            kd_s.reshape(bs, ts, 512), vd_s.reshape(bs, ts, 512), dl, subln, ppg, past_len, lambda_init)

    x1_p, h_p, idx_p, gate_p = _merge(x_prompt.reshape(n_p, D_MODEL), ocmp_p.reshape(n_p, 512), osel_p.reshape(n_p, 512),
                                      owin_p.reshape(n_p, 512), odiff_p.reshape(n_p, 512), gx_p, nmix, nffn, w_gates,
                                      wbn, wbd, wo, wr, br, tm)
    x1_s, h_s, idx_s, gate_s = _merge(x_sample.reshape(n_s, D_MODEL), ocmp_s.reshape(n_s, 512), osel_s.reshape(n_s, 512),
                                      owin_s.reshape(n_s, 512), odiff_s.reshape(n_s, 512), gx_s, nmix, nffn, w_gates,
                                      wbn, wbd, wo, wr, br, n_s)

    n_tok = n_p + n_s
    n_assign = n_tok * TOP_K
    tmoe = 512
    assert n_p % n_s == 0 and n_tok % n_s == 0
    n_blocks = (n_assign + N_EXPERTS * (tmoe - 1) + tmoe - 1) // tmoe
    nbp = (n_blocks + LANES - 1) // LANES * LANES
    dest, meta = _route(jnp.concatenate([idx_p, idx_s], axis=0), n_s, tmoe, nbp)
    dest_flat = dest.reshape(-1)
    xs = _dispatch(dest_flat, meta[2, :N_EXPERTS], meta[1, :1], jnp.concatenate([h_p, h_s], axis=0), n_blocks * tmoe,
                   n_s, tmoe)
    ys = _moe(meta[0, :n_blocks], meta[1, :1], xs, w_gate_up[0], b_gate_up[0][:, None, :], w_down[0],
              b_down[0][:, None, :], tmoe)

    nf = norm_final[None, :]
    y_p = _final(dest_flat, x1_p, gate_p, nf, ys, n_s, n_tok, 0)
    y_s = _final(dest_flat, x1_s, gate_s, nf, ys, n_s, n_tok, n_p // n_s)

    g, hd = NSA_GROUPS, HEAD_DIM
    def from_t(a, heads, length):
        return jnp.transpose(a.reshape(bp, heads, 2, hd, length), (0, 4, 1, 2, 3))[None]

    new_win_p = from_t(kvwt_p[:, :, t - wbuf:], 2, wbuf)
    new_win_s = jnp.concatenate([state_win_kv[0].reshape(bs, wbuf, 256)[:, ts:], kvw_s.reshape(bs, ts, 256)], axis=1)
    return (y_p.reshape(bp, t, D_MODEL), y_s.reshape(bs, ts, D_MODEL),
            from_t(kvct_p, 2, t), kvc_s.reshape(1, bs, ts, 2, g, hd),
            from_t(kvst_p, 2, t), kvs_s.reshape(1, bs, ts, 2, g, hd),
            new_win_p, new_win_s.reshape(1, bs, wbuf, 2, g, hd),
            from_t(kdt_p, DIFF_HEADS, t), kd_s.reshape(1, bs, ts, DIFF_HEADS, 2, hd),
            vd4_p[None], vd_s.reshape(1, bs, ts, DIFF_HEADS, DIFF_DV))
```

```python
import functools
import math

import numpy as np
import jax
import jax.numpy as jnp
from jax import lax
from jax.experimental import pallas as pl
from jax.experimental.pallas import tpu as pltpu

D_MODEL = 1024
HEAD_DIM = 64
NSA_HEADS = 8
NSA_GROUPS = 2
NSA_REP = 4
CMP_BLOCK = 32
CMP_STRIDE = 16
SEL_BLOCK = 64
N_SEL = 16
WINDOW = 512
DIFF_HEADS = 4
DIFF_DV = 128
ROT_DIM = 16
ROPE_THETA = 500000.0
N_EXPERTS = 32
TOP_K = 4
D_FF = 1024
SWIGLU_LIMIT = 7.0
SWIGLU_ALPHA = 1.702
RMS_EPS = 1e-5
FORCE_BONUS = 1e4
SCALE = HEAD_DIM ** -0.5
LANES = 128

BF = jnp.bfloat16
F32 = jnp.float32
I32 = jnp.int32
NEG = -1e30
VMEM_LIMIT = 56 * 1024 * 1024

OFF_Q, OFF_KVC, OFF_KVS, OFF_KVW, OFF_QD, OFF_KD, OFF_VD, OFF_GX = 0, 512, 768, 1024, 1280, 1792, 2304, 2816
IN_COLS_K = 4352
TOFF_Q, TOFF_KVC, TOFF_QD, TOFF_VD, TOFF_GX, TOFF_KS, TOFF_KW, TOFF_KD = 0, 512, 768, 1280, 1792, 1920, 2048, 2176
TIN_COLS = 2688
TIN_ROWS = 1792


def _dot(a, b):
    return jnp.dot(a, b, preferred_element_type=F32)


def _dot_nt(a, b):
    return lax.dot_general(a, b, (((1,), (1,)), ((), ())), preferred_element_type=F32)


def _dot_hi(a, b):
    return jnp.dot(a, b, preferred_element_type=F32, precision=lax.Precision.HIGHEST)


def _params(sem):
    return pltpu.CompilerParams(dimension_semantics=sem, vmem_limit_bytes=VMEM_LIMIT)


def _lane_iota(rows):
    return lax.broadcasted_iota(I32, (rows, LANES), 1)


def _pair(lo, a, b, g):
    if g == 0:
        return jnp.where(lo, a, pltpu.roll(b, 64, 1))
    return jnp.where(lo, pltpu.roll(a, 64, 1), b)


def _inproj_body(x_ref, nw_ref, w_ref, cos_ref, sin_ref, e_ref,
                 qraw_ref, qrot_ref, kvc_ref, kvs_ref, kvw_ref, ksa_ref, kwa_ref, vsb_ref, vwb_ref,
                 qd_ref, kd_ref, kdb_ref, vd_ref, vdb_ref, gx_ref):
    x = x_ref[...]
    tm = x.shape[0]
    xn = (x * lax.rsqrt(jnp.mean(x * x, axis=-1, keepdims=True) + RMS_EPS) * nw_ref[...]).astype(BF)
    cos = cos_ref[...]
    sin = sin_ref[...]
    epat = e_ref[...]
    lane = _lane_iota(tm)
    lo = lane < 64
    first8 = (lane & 63) < 8

    def mm(off):
        return _dot(xn, w_ref[:, off:off + LANES])

    def rope(y):
        sw = jnp.where(first8, pltpu.roll(y, LANES - 8, 1), pltpu.roll(y, 8, 1))
        return y * cos + sw * sin

    for c in range(4):
        y = mm(OFF_Q + c * LANES) * SCALE
        yr = rope(y)
        ys = pltpu.roll(y, 64, 1)
        yrs = pltpu.roll(yr, 64, 1)
        qraw_ref[:, (2 * c) * LANES:(2 * c + 1) * LANES] = jnp.where(lo, y, 0.0).astype(BF)
        qraw_ref[:, (2 * c + 1) * LANES:(2 * c + 2) * LANES] = jnp.where(lo, ys, 0.0).astype(BF)
        qrot_ref[:, (2 * c) * LANES:(2 * c + 1) * LANES] = jnp.where(lo, yr, 0.0).astype(BF)
        qrot_ref[:, (2 * c + 1) * LANES:(2 * c + 2) * LANES] = jnp.where(lo, yrs, 0.0).astype(BF)

    for c in range(2):
        kvc_ref[:, c * LANES:(c + 1) * LANES] = mm(OFF_KVC + c * LANES)

    for off, kv_ref, ka_ref, vb_ref in ((OFF_KVS, kvs_ref, ksa_ref, vsb_ref), (OFF_KVW, kvw_ref, kwa_ref, vwb_ref)):
        kr = rope(mm(off))
        v = mm(off + LANES)
        kv_ref[:, 0:LANES] = kr
        kv_ref[:, LANES:2 * LANES] = v
        ka_ref[:, 0:LANES] = jnp.where(lo, kr, epat).astype(BF)
        ka_ref[:, LANES:2 * LANES] = jnp.where(lo, pltpu.roll(kr, 64, 1), epat).astype(BF)
        vb_ref[...] = v.astype(BF)

    for c in range(4):
        sl = slice(c * LANES, (c + 1) * LANES)
        qd_ref[:, sl] = rope(mm(OFF_QD + c * LANES) * SCALE).astype(BF)
        kr = rope(mm(OFF_KD + c * LANES))
        kd_ref[:, sl] = kr
        kdb_ref[:, sl] = kr.astype(BF)
        v = mm(OFF_VD + c * LANES)
        vd_ref[:, sl] = v
        vdb_ref[:, sl] = v.astype(BF)

    for c in range(12):
        gx_ref[:, c * LANES:(c + 1) * LANES] = jax.nn.sigmoid(mm(OFF_GX + c * LANES))


def _inproj(x2d, norm_w, w_bf, cos_t, sin_t, e_t, tm):
    n = x2d.shape[0]
    nt = cos_t.shape[0] // tm
    row = lambda i: (i, 0)
    tab = lambda i: (i % nt, 0)
    const = lambda i: (0, 0)
    outs = [(1024, BF), (1024, BF), (256, F32), (256, F32), (256, F32), (256, BF), (256, BF), (128, BF), (128, BF),
            (512, BF), (512, F32), (512, BF), (512, F32), (512, BF), (1536, F32)]
    return pl.pallas_call(
        _inproj_body,
        grid=(n // tm,),
        in_specs=[pl.BlockSpec((tm, D_MODEL), row), pl.BlockSpec((1, D_MODEL), const),
                  pl.BlockSpec((D_MODEL, IN_COLS_K), const),
                  pl.BlockSpec((tm, LANES), tab), pl.BlockSpec((tm, LANES), tab), pl.BlockSpec((tm, LANES), tab)],
        out_specs=[pl.BlockSpec((tm, w), row) for w, _ in outs],
        out_shape=[jax.ShapeDtypeStruct((n, w), dt) for w, dt in outs],
        compiler_params=_params(("parallel",)),
        name="inproj",
    )(x2d, norm_w, w_bf, cos_t, sin_t, e_t)


def _inproj_t_body(x_ref, nw_ref, w_ref, wt_ref, cos_ref, sin_ref, cost_ref, sint_ref,
                   qraw_ref, qrot_ref, kvc_ref, kvct_ref, kvst_ref, kvwt_ref, ksa_ref, kwa_ref, vst_ref, vwt_ref,
                   qd_ref, kdt_ref, kdb_ref, vd4_ref, vdtb_ref, gx_ref):
    x = x_ref[...]
    tm = x.shape[0]
    xn = (x * lax.rsqrt(jnp.mean(x * x, axis=-1, keepdims=True) + RMS_EPS) * nw_ref[...]).astype(BF)
    cos = cos_ref[...]
    sin = sin_ref[...]
    cos_t = cost_ref[...]
    sin_t = sint_ref[...]
    lane = _lane_iota(tm)
    lo = lane < 64
    first8 = (lane & 63) < 8

    def mm(off):
        return _dot(xn, w_ref[:, off:off + LANES])

    def mm_t(off, n):
        return _dot_nt(wt_ref[off:off + n, :], xn)

    def rope(y):
        sw = jnp.where(first8, pltpu.roll(y, LANES - 8, 1), pltpu.roll(y, 8, 1))
        return y * cos + sw * sin

    def rope_t(y):
        parts = []
        for hb in range(0, y.shape[0], HEAD_DIM):
            x1 = y[hb:hb + 8]
            x2 = y[hb + 8:hb + 16]
            parts += [x1 * cos_t - x2 * sin_t, x1 * sin_t + x2 * cos_t, y[hb + 16:hb + HEAD_DIM]]
        return jnp.concatenate(parts, axis=0)

    for c in range(4):
        y = mm(TOFF_Q + c * LANES) * SCALE
        yr = rope(y)
        ys = pltpu.roll(y, 64, 1)
        yrs = pltpu.roll(yr, 64, 1)
        qraw_ref[:, (2 * c) * LANES:(2 * c + 1) * LANES] = jnp.where(lo, y, 0.0).astype(BF)
        qraw_ref[:, (2 * c + 1) * LANES:(2 * c + 2) * LANES] = jnp.where(lo, ys, 0.0).astype(BF)
        qrot_ref[:, (2 * c) * LANES:(2 * c + 1) * LANES] = jnp.where(lo, yr, 0.0).astype(BF)
        qrot_ref[:, (2 * c + 1) * LANES:(2 * c + 2) * LANES] = jnp.where(lo, yrs, 0.0).astype(BF)

    for c in range(2):
        kvc_ref[:, c * LANES:(c + 1) * LANES] = mm(TOFF_KVC + c * LANES)
    kvct_ref[0] = mm_t(0, 256)

    blk = jnp.right_shift(pl.program_id(1) * tm + lax.broadcasted_iota(I32, (tm, LANES), 0), 6)
    epat = jnp.where((lane >= 64) & (lane < 96) & (blk == lane - 64), 1.0, 0.0)
    for off, koff, kvt_ref, ka_ref, vt_ref in ((256, TOFF_KS, kvst_ref, ksa_ref, vst_ref),
                                               (512, TOFF_KW, kvwt_ref, kwa_ref, vwt_ref)):
        y = mm_t(off, 256)
        v = y[128:256]
        kvt_ref[0, 0:128, :] = rope_t(y[0:128])
        kvt_ref[0, 128:256, :] = v
        vt_ref[0] = v.astype(BF)
        kr = rope(mm(koff))
        ka_ref[:, 0:LANES] = jnp.where(lo, kr, epat).astype(BF)
        ka_ref[:, LANES:2 * LANES] = jnp.where(lo, pltpu.roll(kr, 64, 1), epat).astype(BF)

    kdt_ref[0] = rope_t(mm_t(768, 512))
    vdtb_ref[0] = mm_t(1280, 512).astype(BF)
    for c in range(4):
        sl = slice(c * LANES, (c + 1) * LANES)
        qd_ref[:, sl] = rope(mm(TOFF_QD + c * LANES) * SCALE).astype(BF)
        kdb_ref[:, sl] = rope(mm(TOFF_KD + c * LANES)).astype(BF)
        vd4_ref[0, :, c, :] = mm(TOFF_VD + c * LANES)

    gx_ref[...] = jax.nn.sigmoid(mm(TOFF_GX))


def _inproj_t(x2d, norm_w, w_bf, wt_bf, cos_t, sin_t, cos_tt, sin_tt, nbat, t, tm):
    n = x2d.shape[0]
    nt = t // tm
    row = lambda b, i: (b * nt + i, 0)
    tab = lambda b, i: (i, 0)
    tab_t = lambda b, i: (0, i)
    const = lambda b, i: (0, 0)
    tr = lambda b, i: (b, 0, i)
    rm = lambda w, dt: (pl.BlockSpec((tm, w), row), jax.ShapeDtypeStruct((n, w), dt))
    tp = lambda r, dt: (pl.BlockSpec((1, r, tm), tr), jax.ShapeDtypeStruct((nbat, r, t), dt))
    vd4 = (pl.BlockSpec((1, tm, DIFF_HEADS, DIFF_DV), lambda b, i: (b, i, 0, 0)),
           jax.ShapeDtypeStruct((nbat, t, DIFF_HEADS, DIFF_DV), F32))
    outs = [rm(1024, BF), rm(1024, BF), rm(256, F32), tp(256, F32), tp(256, F32), tp(256, F32), rm(256, BF),
            rm(256, BF), tp(128, BF), tp(128, BF), rm(512, BF), tp(512, F32), rm(512, BF), vd4, tp(512, BF),
            rm(LANES, F32)]
    return pl.pallas_call(
        _inproj_t_body,
        grid=(nbat, nt),
        in_specs=[pl.BlockSpec((tm, D_MODEL), row), pl.BlockSpec((1, D_MODEL), const),
                  pl.BlockSpec((D_MODEL, TIN_COLS), const), pl.BlockSpec((TIN_ROWS, D_MODEL), const),
                  pl.BlockSpec((tm, LANES), tab), pl.BlockSpec((tm, LANES), tab),
                  pl.BlockSpec((8, tm), tab_t), pl.BlockSpec((8, tm), tab_t)],
        out_specs=[o[0] for o in outs],
        out_shape=[o[1] for o in outs],
        compiler_params=_params(("parallel", "parallel")),
        name="inproj_t",
    )(x2d, norm_w, w_bf, wt_bf, cos_t, sin_t, cos_tt, sin_tt)


def _compress_body(pt_ref, *refs, n_pages, rp, rt):
    page_refs = refs[:n_pages]
    pea_ref, peb_ref, w1a_ref, w1b_ref, w2_ref, out_ref, y_scr = refs[n_pages:]
    s = pl.program_id(1)
    for i in range(n_pages):
        x = page_refs[i][0]
        row0 = pl.multiple_of((s * n_pages + i) * rp, 8)
        for cg in range(4):
            ycg = jnp.concatenate(
                [x[:, l * 256 + cg * 64:l * 256 + cg * 64 + 64] for l in range(CMP_STRIDE)], axis=1)
            y_scr[cg, pl.ds(row0, rp), :] = ycg

    @pl.when(s == pl.num_programs(1) - 1)
    def _():
        for c in range(2):
            for g in range(2):
                cg = c * 2 + g
                y = y_scr[cg]
                za = _dot((y + pea_ref[c]).astype(BF), w1a_ref[c])
                zb = _dot((y + peb_ref[c]).astype(BF), w1b_ref[c])
                hid = jax.nn.gelu(za + pltpu.roll(zb, rt - 1, 0))
                out_ref[0, :, cg * LANES:(cg + 1) * LANES] = _dot(hid.astype(BF), w2_ref[c]).astype(BF)


def _compress(pool, page_table, pea, peb, w1a, w1b, w2p, n_pages):
    nb, pages = page_table.shape
    rp = pool.shape[1]
    rt = pages * rp
    steps = pages // n_pages
    page_specs = [pl.BlockSpec((1, rp, 4096), functools.partial(
        lambda b, s, pt, i: (pt[b, s * n_pages + i], 0, 0), i=i)) for i in range(n_pages)]
    c3 = lambda b, s, pt: (0, 0, 0)
    return pl.pallas_call(
        functools.partial(_compress_body, n_pages=n_pages, rp=rp, rt=rt),
        grid_spec=pltpu.PrefetchScalarGridSpec(
            num_scalar_prefetch=1,
            grid=(nb, steps),
            in_specs=page_specs + [pl.BlockSpec((2, 1, 1024), c3), pl.BlockSpec((2, 1, 1024), c3),
                                   pl.BlockSpec((2, 1024, LANES), c3), pl.BlockSpec((2, 1024, LANES), c3),
                                   pl.BlockSpec((2, LANES, LANES), c3)],
            out_specs=pl.BlockSpec((1, rt, 512), lambda b, s, pt: (b, 0, 0)),
            scratch_shapes=[pltpu.VMEM((4, rt, 1024), F32)]),
        out_shape=jax.ShapeDtypeStruct((nb, rt, 512), BF),
        compiler_params=_params(("parallel", "arbitrary")),
        name="compress",
    )(page_table, *([pool] * n_pages), pea, peb, w1a, w1b, w2p)


def _cmp_body(q_ref, kvc_ref, mimp_ref, ocmp_ref, selb_ref, *, tq, rt, nb, nbp, pos0):
    i = pl.program_id(1)
    pos_n = pos0 + i * tq + lax.broadcasted_iota(I32, (tq, rt), 0)
    n_i = lax.broadcasted_iota(I32, (tq, rt), 1)
    vis = (n_i * CMP_STRIDE + (CMP_BLOCK - 1)) <= pos_n
    lo = _lane_iota(tq) < 64
    pos_b = pos0 + i * tq + lax.broadcasted_iota(I32, (tq, nbp), 0)
    jb = lax.broadcasted_iota(I32, (tq, nbp), 1)
    cur = jnp.right_shift(pos_b, 6)
    valid = (jb * SEL_BLOCK <= pos_b) & (jb < nb)
    forced = (jb == 0) | (jb == cur) | (jb == cur - 1)
    mimp = mimp_ref[...]
    for g in range(NSA_GROUPS):
        kc = kvc_ref[0, :, g * LANES:(g + 1) * LANES]
        vc = kvc_ref[0, :, (2 + g) * LANES:(3 + g) * LANES]
        pg = jnp.zeros((tq, rt), F32)
        og = []
        for r in range(NSA_REP):
            h = g * NSA_REP + r
            q = q_ref[0, :, h * LANES:(h + 1) * LANES]
            s = jnp.where(vis, _dot_nt(q, kc), -jnp.inf)
            m = jnp.max(s, axis=-1, keepdims=True)
            m = jnp.where(m == -jnp.inf, 0.0, m)
            e = jnp.exp(s - m)
            d = jnp.sum(e, axis=-1, keepdims=True)
            p = e / jnp.where(d > 0, d, 1.0)
            og.append(_dot(p.astype(BF), vc))
            pg = pg + p
        imp = _dot_hi(pg, mimp)
        score = jnp.where(valid, jnp.where(forced, imp + FORCE_BONUS, imp), -jnp.inf)
        if tq == nbp and nb % 8 == 0:
            st = jnp.transpose(score)[0:nb]
            jrow = lax.broadcasted_iota(I32, (nb, tq), 0)
            rank_t = jnp.zeros((nb, tq), F32)
            for jj in range(nb):
                cj = st[jj:jj + 1, :]
                rank_t = rank_t + jnp.where((cj > st) | ((cj == st) & (jrow > jj)), 1.0, 0.0)
            rank = jnp.transpose(jnp.concatenate([rank_t, jnp.full((nbp - nb, tq), float(nbp), F32)], axis=0))
            sel = (rank < float(N_SEL)) & valid
        else:
            rank = jnp.zeros((tq, nbp), I32)
            for jj in range(nb):
                cj = score[:, jj:jj + 1]
                beats = (cj > score) | ((cj == score) & (jb > jj))
                rank = rank + beats.astype(I32)
            sel = (rank < N_SEL) & valid
        selb_ref[0, :, g * nbp:(g + 1) * nbp] = jnp.where(sel, 0.0, NEG)
        for c2 in range(2):
            ocmp_ref[0, :, (g * 2 + c2) * LANES:(g * 2 + c2 + 1) * LANES] = _pair(lo, og[2 * c2], og[2 * c2 + 1], 0)


def _cmp_attend(qraw3, kvc, mimp, nq, nb, nbp, pos0):
    nbat = kvc.shape[0]
    _, tq, _ = qraw3.shape
    rt = kvc.shape[1]
    return pl.pallas_call(
        functools.partial(_cmp_body, tq=tq, rt=rt, nb=nb, nbp=nbp, pos0=pos0),
        grid=(nbat, nq),
        in_specs=[pl.BlockSpec((1, tq, 1024), lambda b, i: (b * nq + i, 0, 0)),
                  pl.BlockSpec((1, rt, 512), lambda b, i: (b, 0, 0)),
                  pl.BlockSpec((rt, nbp), lambda b, i: (0, 0))],
        out_specs=[pl.BlockSpec((1, tq, 512), lambda b, i: (b * nq + i, 0, 0)),
                   pl.BlockSpec((1, tq, 2 * nbp), lambda b, i: (b * nq + i, 0, 0))],
        out_shape=[jax.ShapeDtypeStruct((nbat * nq, tq, 512), F32),
                   jax.ShapeDtypeStruct((nbat * nq, tq, 2 * nbp), F32)],
        compiler_params=_params(("parallel", "parallel")),
        name="cmp_attend",
    )(qraw3, kvc, mimp)


def _online_t(s, vt, m_s, l_s, acc, idx):
    m_old = m_s[idx]
    m_new = jnp.maximum(m_old, jnp.max(s, axis=0, keepdims=True))
    alpha = jnp.exp(m_old - m_new)
    p = jnp.exp(s - m_new)
    l_s[idx] = alpha * l_s[idx] + jnp.sum(p, axis=0, keepdims=True)
    acc[idx] = alpha * acc[idx] + _dot(vt, p.astype(BF))
    m_s[idx] = m_new


STEP_FIRST, STEP_LAST, STEP_MASKED = 1, 2, 4


def _step_table(nq, tq, tk, window=None):
    qi, kt, fl = [], [], []
    for i in range(nq):
        first = 0 if window is None else max(0, (i * tq - window + 1) // tk)
        last = (i * tq + tq - 1) // tk
        for j in range(first, last + 1):
            masked = j * tk + tk - 1 > i * tq
            if window is not None:
                masked = masked or j * tk <= i * tq + tq - 1 - window
            qi.append(i)
            kt.append(j)
            fl.append((STEP_FIRST if j == first else 0) | (STEP_LAST if j == last else 0)
                      | (STEP_MASKED if masked else 0))
    return (jnp.asarray(np.array(qi, np.int32)), jnp.asarray(np.array(kt, np.int32)),
            jnp.asarray(np.array(fl, np.int32)))


def _nsa_flash_body(qi_ref, kt_ref, fl_ref, *refs, mode, tq, tk):
    if mode == "sel":
        q_ref, selb_ref, k_ref, v_ref, o_ref, qs, m_s, l_s, acc = refs
    else:
        q_ref, k_ref, v_ref, o_ref, qs, m_s, l_s, acc = refs
    st = pl.program_id(1)
    i = qi_ref[st]
    jt = kt_ref[st]
    flags = fl_ref[st]
    q0 = i * tq
    rows = NSA_REP * tq
    lane = _lane_iota(tq)
    lo = lane < 64

    @pl.when((flags & STEP_FIRST) != 0)
    def _():
        m_s[...] = jnp.full(m_s.shape, NEG, F32)
        l_s[...] = jnp.zeros(l_s.shape, F32)
        acc[...] = jnp.zeros(acc.shape, F32)
        for g in range(NSA_GROUPS):
            if mode == "sel":
                sb = pltpu.roll(selb_ref[0, :, g * LANES:(g + 1) * LANES], 64, 1)
                sb = jnp.where((lane >= 64) & (lane < 96), sb, 0.0)
            for r in range(NSA_REP):
                h = g * NSA_REP + r
                q = q_ref[0, :, h * LANES:(h + 1) * LANES]
                if mode == "sel":
                    q = (q.astype(F32) + sb).astype(BF)
                qs[g, r * tq:(r + 1) * tq, :] = q

    need_mask = (flags & STEP_MASKED) != 0

    def step(masked):
        for g in range(NSA_GROUPS):
            s = _dot_nt(k_ref[:, g * LANES:(g + 1) * LANES], qs[g])
            if masked:
                kpos = jt * tk + lax.broadcasted_iota(I32, (tk, rows), 0)
                rpos = q0 + (lax.broadcasted_iota(I32, (tk, rows), 1) & (tq - 1))
                vis = kpos <= rpos
                if mode == "win":
                    vis = vis & (kpos > rpos - WINDOW)
                s = jnp.where(vis, s, NEG)
            _online_t(s, v_ref[0, g * HEAD_DIM:(g + 1) * HEAD_DIM, :], m_s, l_s, acc, g)

    @pl.when(need_mask)
    def _():
        step(True)

    @pl.when(jnp.logical_not(need_mask))
    def _():
        step(False)

    @pl.when((flags & STEP_LAST) != 0)
    def _():
        for g in range(NSA_GROUPS):
            on = acc[g] / l_s[g]
            for c2 in range(2):
                pair = jnp.concatenate([on[:, (2 * c2) * tq:(2 * c2 + 1) * tq],
                                        on[:, (2 * c2 + 1) * tq:(2 * c2 + 2) * tq]], axis=0)
                o_ref[0, :, (g * 2 + c2) * LANES:(g * 2 + c2 + 1) * LANES] = jnp.concatenate(
                    [jnp.transpose(pair[:, c * LANES:(c + 1) * LANES]) for c in range(tq // LANES)], axis=0)


def _nsa_flash(mode, q3, selb3, k2, v2, nbat, t, tq, tk):
    nq = t // tq
    rows = NSA_REP * tq
    table = _step_table(nq, tq, tk, None if mode == "sel" else WINDOW)
    qmap = lambda b, s, qi, kt, fl: (b * nq + qi[s], 0, 0)
    in_specs = [pl.BlockSpec((1, tq, 1024), qmap)]
    args = [q3]
    if mode == "sel":
        in_specs.append(pl.BlockSpec((1, tq, 2 * LANES), qmap))
        args.append(selb3)
    nkv = t // tk
    in_specs += [pl.BlockSpec((tk, 2 * LANES), lambda b, s, qi, kt, fl: (b * nkv + kt[s], 0)),
                 pl.BlockSpec((1, LANES, tk), lambda b, s, qi, kt, fl: (b, 0, kt[s]))]
    args += [k2, v2]
    return pl.pallas_call(
        functools.partial(_nsa_flash_body, mode=mode, tq=tq, tk=tk),
        grid_spec=pltpu.PrefetchScalarGridSpec(
            num_scalar_prefetch=3, grid=(nbat, int(table[0].shape[0])), in_specs=in_specs,
            out_specs=pl.BlockSpec((1, tq, 512), qmap),
            scratch_shapes=[pltpu.VMEM((NSA_GROUPS, rows, LANES), BF), pltpu.VMEM((NSA_GROUPS, 1, rows), F32),
                            pltpu.VMEM((NSA_GROUPS, 1, rows), F32), pltpu.VMEM((NSA_GROUPS, HEAD_DIM, rows), F32)]),
        out_shape=jax.ShapeDtypeStruct((nbat * nq, tq, 512), F32),
        compiler_params=_params(("parallel", "arbitrary")),
        name="nsa_flash_" + mode,
    )(*table, *args)


def _diff_lambda(dl, lambda_init):
    a = jnp.sum(dl[0:1] * dl[1:2], axis=1, keepdims=True)
    b = jnp.sum(dl[2:3] * dl[3:4], axis=1, keepdims=True)
    return jnp.exp(a) - jnp.exp(b) + lambda_init


def _diff_finish(a0, a1, lam, subln, lambda_init):
    o = a0 - lam * a1
    o = o * lax.rsqrt(jnp.mean(o * o, axis=-1, keepdims=True) + RMS_EPS) * subln
    return o * (1.0 - lambda_init)


def _diff_flash_body(qi_ref, kt_ref, fl_ref, q_ref, k_ref, v_ref, dl_ref, sub_ref, o_ref, qs, m_s, l_s, acc, *,
                     tq, tk, lambda_init):
    st = pl.program_id(1)
    i = qi_ref[st]
    j = kt_ref[st]
    flags = fl_ref[st]
    q0 = i * tq
    rows = 2 * tq
    lo = _lane_iota(tq) < 64

    @pl.when((flags & STEP_FIRST) != 0)
    def _():
        m_s[...] = jnp.full(m_s.shape, NEG, F32)
        l_s[...] = jnp.zeros(l_s.shape, F32)
        acc[...] = jnp.zeros(acc.shape, F32)
        for h in range(DIFF_HEADS):
            q = q_ref[0, :, h * LANES:(h + 1) * LANES].astype(F32)
            qs[h, 0:tq, :] = jnp.where(lo, q, 0.0).astype(BF)
            qs[h, tq:2 * tq, :] = jnp.where(lo, 0.0, q).astype(BF)

    need_mask = (flags & STEP_MASKED) != 0

    def step(masked):
        for h in range(DIFF_HEADS):
            s = _dot_nt(k_ref[:, h * LANES:(h + 1) * LANES], qs[h])
            if masked:
                kpos = j * tk + lax.broadcasted_iota(I32, (tk, rows), 0)
                rpos = q0 + (lax.broadcasted_iota(I32, (tk, rows), 1) & (tq - 1))
                s = jnp.where(kpos <= rpos, s, NEG)
            _online_t(s, v_ref[0, h * LANES:(h + 1) * LANES, :], m_s, l_s, acc, h)

    @pl.when(need_mask)
    def _():
        step(True)

    @pl.when(jnp.logical_not(need_mask))
    def _():
        step(False)

    @pl.when((flags & STEP_LAST) != 0)
    def _():
        lam = _diff_lambda(dl_ref[...], lambda_init)
        for h in range(DIFF_HEADS):
            on = acc[h] / l_s[h]
            a0, a1 = [jnp.concatenate([jnp.transpose(on[:, c0 + c * LANES:c0 + (c + 1) * LANES])
                                       for c in range(tq // LANES)], axis=0) for c0 in (0, tq)]
            o_ref[0, :, h * LANES:(h + 1) * LANES] = _diff_finish(a0, a1, lam, sub_ref[...], lambda_init)


def _diff_flash(q3, k2, v2, dl, subln, nbat, t, tq, tk, lambda_init):
    nq = t // tq
    nk = t // tk
    table = _step_table(nq, tq, tk)
    qmap = lambda b, s, qi, kt, fl: (b * nq + qi[s], 0, 0)
    kmap = lambda b, s, qi, kt, fl: (b * nk + kt[s], 0)
    vmap = lambda b, s, qi, kt, fl: (b, 0, kt[s])
    const = lambda b, s, qi, kt, fl: (0, 0)
    rows = 2 * tq
    return pl.pallas_call(
        functools.partial(_diff_flash_body, tq=tq, tk=tk, lambda_init=lambda_init),
        grid_spec=pltpu.PrefetchScalarGridSpec(
            num_scalar_prefetch=3, grid=(nbat, int(table[0].shape[0])),
            in_specs=[pl.BlockSpec((1, tq, 512), qmap), pl.BlockSpec((tk, 512), kmap),
                      pl.BlockSpec((1, 512, tk), vmap), pl.BlockSpec((4, HEAD_DIM), const),
                      pl.BlockSpec((1, DIFF_DV), const)],
            out_specs=pl.BlockSpec((1, tq, 512), qmap),
            scratch_shapes=[pltpu.VMEM((DIFF_HEADS, rows, LANES), BF), pltpu.VMEM((DIFF_HEADS, 1, rows), F32),
                            pltpu.VMEM((DIFF_HEADS, 1, rows), F32), pltpu.VMEM((DIFF_HEADS, DIFF_DV, rows), F32)]),
        out_shape=jax.ShapeDtypeStruct((nbat * nq, tq, 512), F32),
        compiler_params=_params(("parallel", "arbitrary")),
        name="diff_flash",
    )(*table, q3, k2, v2, dl, subln)


def _paged_body(pt_ref, *refs, mode, n_pages, past_len, lambda_init):
    q_ref = refs[0]
    k = 1
    if mode == "sel":
        selb_ref = refs[k]
        k += 1
    kpages = refs[k:k + n_pages]
    k += n_pages
    if mode == "diff":
        vpages = refs[k:k + n_pages]
        k += n_pages
    newk_ref = refs[k]
    k += 1
    if mode == "diff":
        newv_ref, dl_ref, sub_ref = refs[k:k + 3]
        k += 3
    o_ref = refs[k]
    k += 1
    if mode == "sel":
        qbd, bias, m_s, l_s, acc = refs[k:]
    else:
        qbd, m_s, l_s, acc = refs[k:]
    kw = WINDOW if mode == "win" else LANES
    st = pl.program_id(1)
    nq = 8
    rows = 64
    lane8 = _lane_iota(nq)
    lo8 = lane8 < 64
    t_row = lax.broadcasted_iota(I32, (rows, LANES), 0) & (nq - 1)
    col = lax.broadcasted_iota(I32, (rows, LANES), 1)

    def update(s, pv):
        m_old = m_s[0]
        m_new = jnp.maximum(m_old, jnp.max(s, axis=-1, keepdims=True))
        alpha = jnp.exp(m_old - m_new)
        p = jnp.exp(s - m_new)
        l_s[0] = alpha * l_s[0] + jnp.sum(p, axis=-1, keepdims=True)
        acc[0] = alpha * acc[0] + pv(p.astype(BF))
        m_s[0] = m_new

    def per_head(p, v_of_head):
        return jnp.concatenate([_dot(p[h * 16:(h + 1) * 16], v_of_head(h)) for h in range(DIFF_HEADS)], axis=0)

    @pl.when(st == 0)
    def _():
        m_s[0] = jnp.full((rows, 1), NEG, F32)
        l_s[0] = jnp.zeros((rows, 1), F32)
        acc[0] = jnp.zeros(acc.shape[1:], F32)
        if mode == "diff":
            qbd[...] = jnp.zeros(qbd.shape, F32)
            for h in range(DIFF_HEADS):
                q = q_ref[0, :, h * LANES:(h + 1) * LANES].astype(F32)
                qbd[(2 * h) * nq:(2 * h + 1) * nq, h * LANES:(h + 1) * LANES] = jnp.where(lo8, q, 0.0)
                qbd[(2 * h + 1) * nq:(2 * h + 2) * nq, h * LANES:(h + 1) * LANES] = jnp.where(lo8, 0.0, q)
        else:
            for h in range(NSA_HEADS):
                g = h // NSA_REP
                q = q_ref[0, :, h * LANES:(h + 1) * LANES].astype(F32)
                qbd[h * nq:(h + 1) * nq, :] = q if g == 0 else pltpu.roll(q, 64, 1)
                if mode == "sel":
                    bias[h * nq:(h + 1) * nq, :] = selb_ref[0, :, g * 256:(g + 1) * 256]
        kn = newk_ref[0]
        if mode == "diff":
            kk, vv = kn, newv_ref[0]
        else:
            kk, vv = kn[:, 0:LANES], kn[:, LANES:2 * LANES]
        kpad = jnp.concatenate([kk, jnp.zeros((LANES - nq, kk.shape[1]), F32)], axis=0).astype(BF)
        vpad = jnp.concatenate([vv, jnp.zeros((LANES - nq, vv.shape[1]), F32)], axis=0).astype(BF)
        s = _dot_nt(qbd[...].astype(BF), kpad)
        if mode == "sel":
            jnew = past_len // SEL_BLOCK
            s = s + bias[:, jnew:jnew + 1]
        s = jnp.where((col < nq) & (col <= t_row), s, NEG)
        if mode == "diff":
            update(s, lambda p: per_head(p, lambda h: vpad[:, h * LANES:(h + 1) * LANES]))
        else:
            update(s, lambda p: _dot(p, vpad))

    qb = qbd[...].astype(BF)
    nkeys = n_pages * kw
    if mode == "diff":
        kt = jnp.concatenate([kpages[i][0].reshape(4 * LANES, LANES).astype(BF) for i in range(n_pages)], axis=1)
    else:
        kt = jnp.concatenate([kpages[i][0, 0].reshape(LANES, kw).astype(BF) for i in range(n_pages)], axis=1)
    s = _dot(qb, kt)
    if mode == "sel":
        krow = lax.broadcasted_iota(I32, (nkeys, 256), 0)
        jbi = lax.broadcasted_iota(I32, (nkeys, 256), 1)
        ep = jnp.where(jbi == 2 * n_pages * st + jnp.right_shift(krow, 6), 1.0, 0.0).astype(BF)
        s = s + _dot_nt(bias[...].astype(BF), ep)
    if mode == "win":
        t_w = lax.broadcasted_iota(I32, (rows, nkeys), 0) & (nq - 1)
        s = jnp.where(lax.broadcasted_iota(I32, (rows, nkeys), 1) > t_w, s, NEG)
    if mode == "diff":
        update(s, lambda p: per_head(p, lambda h: jnp.concatenate(
            [vpages[i][0, :, h, :].astype(BF) for i in range(n_pages)], axis=0)))
    else:
        vt = jnp.concatenate([kpages[i][0, 1].reshape(LANES, kw).astype(BF) for i in range(n_pages)], axis=1)
        update(s, lambda p: _dot_nt(p, vt))

    @pl.when(st == pl.num_programs(1) - 1)
    def _():
        on = acc[0] / l_s[0]
        if mode == "diff":
            lam = _diff_lambda(dl_ref[...], lambda_init)
            for h in range(DIFF_HEADS):
                a0 = on[h * 16:h * 16 + nq]
                a1 = on[h * 16 + nq:h * 16 + 2 * nq]
                o_ref[0, :, h * LANES:(h + 1) * LANES] = _diff_finish(a0, a1, lam, sub_ref[...], lambda_init)
        else:
            for c in range(4):
                a = on[(2 * c) * nq:(2 * c + 1) * nq]
                b = on[(2 * c + 1) * nq:(2 * c + 2) * nq]
                o_ref[0, :, c * LANES:(c + 1) * LANES] = _pair(lo8, a, b, c // 2)


def _paged_attend(mode, q3, selb3, pool_k, pool_v, page_table, newk3, newv3, dl, subln, n_pages, past_len, lambda_init):
    nbat, pages = page_table.shape
    steps = pages // n_pages
    wq = q3.shape[2]
    wk = newk3.shape[2]
    bmap = lambda b, s, pt: (b, 0, 0)
    c2 = lambda b, s, pt: (0, 0)

    def pmap(i, rank):
        return lambda b, s, pt: (pt[b, s * n_pages + i],) + (0,) * (rank - 1)

    in_specs = [pl.BlockSpec((1, 8, wq), bmap)]
    args = [q3]
    if mode == "sel":
        in_specs.append(pl.BlockSpec((1, 8, 512), bmap))
        args.append(selb3)
    kblock = (1,) + pool_k.shape[1:]
    in_specs += [pl.BlockSpec(kblock, pmap(i, len(kblock))) for i in range(n_pages)]
    args += [pool_k] * n_pages
    if mode == "diff":
        in_specs += [pl.BlockSpec((1, LANES, DIFF_HEADS, DIFF_DV), pmap(i, 4)) for i in range(n_pages)]
        args += [pool_v] * n_pages
    in_specs.append(pl.BlockSpec((1, 8, wk), bmap))
    args.append(newk3)
    if mode == "diff":
        in_specs += [pl.BlockSpec((1, 8, 512), bmap), pl.BlockSpec((4, HEAD_DIM), c2), pl.BlockSpec((1, DIFF_DV), c2)]
        args += [newv3, dl, subln]
    ck = 512 if mode == "diff" else LANES
    scratch = [pltpu.VMEM((64, ck), F32)]
    if mode == "sel":
        scratch.append(pltpu.VMEM((64, 256), F32))
    scratch += [pltpu.VMEM((1, 64, 1), F32), pltpu.VMEM((1, 64, 1), F32), pltpu.VMEM((1, 64, LANES), F32)]
    return pl.pallas_call(
        functools.partial(_paged_body, mode=mode, n_pages=n_pages, past_len=past_len, lambda_init=lambda_init),
        grid_spec=pltpu.PrefetchScalarGridSpec(
            num_scalar_prefetch=1, grid=(nbat, steps), in_specs=in_specs,
            out_specs=pl.BlockSpec((1, 8, 512), bmap), scratch_shapes=scratch),
        out_shape=jax.ShapeDtypeStruct((nbat, 8, 512), F32),
        compiler_params=_params(("parallel", "arbitrary")),
        name="paged_" + mode,
    )(page_table, *args)


def _merge_body(x_ref, oc_ref, os_ref, ow_ref, od_ref, gx_ref, nmix_ref, nffn_ref, wg_ref, wbn_ref, wbd_ref, wo_ref,
                wr_ref, br_ref, eexp_ref, x1_ref, h_ref, idx_ref, gate_ref):
    x = x_ref[...]
    tm = x.shape[0]
    xn = (x * lax.rsqrt(jnp.mean(x * x, axis=-1, keepdims=True) + RMS_EPS) * nmix_ref[...]).astype(BF)
    gx = gx_ref[...]
    if gx.shape[1] == LANES:
        hi = gx.astype(BF)
        r1 = gx - hi.astype(F32)
        mid = r1.astype(BF)
        lo = (r1 - mid.astype(F32)).astype(BF)
        gx = _dot(hi, eexp_ref[...]) + _dot(mid, eexp_ref[...]) + _dot(lo, eexp_ref[...])
    o_nsa = gx[:, 0:512] * oc_ref[...] + gx[:, 512:1024] * os_ref[...] + gx[:, 1024:1536] * ow_ref[...]
    y_nsa = _dot(o_nsa.astype(BF), wbn_ref[...])
    y_diff = _dot(od_ref[...].astype(BF), wbd_ref[...])
    gates = jax.nn.sigmoid(_dot(xn, wg_ref[...]))
    mrg = gates[:, 0:D_MODEL] * y_nsa + gates[:, D_MODEL:2 * D_MODEL] * y_diff
    x1 = x + _dot(mrg.astype(BF), wo_ref[...])
    x1_ref[...] = x1
    h = x1 * lax.rsqrt(jnp.mean(x1 * x1, axis=-1, keepdims=True) + RMS_EPS) * nffn_ref[...]
    h_ref[...] = h
    logits = _dot_hi(h, wr_ref[...]) + br_ref[...]
    lane = _lane_iota(tm)
    vals, idxs = [], []
    for _ in range(TOP_K):
        m = jnp.max(logits, axis=-1, keepdims=True)
        idx = jnp.min(jnp.where(logits == m, lane, LANES), axis=-1, keepdims=True)
        vals.append(m)
        idxs.append(idx)
        logits = jnp.where(lane == idx, -jnp.inf, logits)
    es = [jnp.exp(v - vals[0]) for v in vals]
    den = es[0] + es[1] + es[2] + es[3]
    idx_out = jnp.zeros((tm, LANES), I32)
    gate_out = jnp.zeros((tm, LANES), F32)
    for k in range(TOP_K):
        idx_out = jnp.where(lane == k, idxs[k], idx_out)
        gate_out = jnp.where(lane == k, es[k] / den, gate_out)
    idx_ref[...] = idx_out
    gate_ref[...] = gate_out


def _merge(x2d, oc, osel, ow, od, gx, nmix, nffn, wg, wbn, wbd, wo, wr, br, eexp, tm):
    n = x2d.shape[0]
    row = lambda i: (i, 0)
    const = lambda i: (0, 0)
    return pl.pallas_call(
        _merge_body,
        grid=(n // tm,),
        in_specs=[pl.BlockSpec((tm, D_MODEL), row)] + [pl.BlockSpec((tm, 512), row)] * 4 +
                 [pl.BlockSpec((tm, gx.shape[1]), row), pl.BlockSpec((1, D_MODEL), const),
                  pl.BlockSpec((1, D_MODEL), const),
                  pl.BlockSpec((D_MODEL, 2 * D_MODEL), const), pl.BlockSpec((512, D_MODEL), const),
                  pl.BlockSpec((512, D_MODEL), const), pl.BlockSpec((D_MODEL, D_MODEL), const),
                  pl.BlockSpec((D_MODEL, LANES), const), pl.BlockSpec((1, LANES), const),
                  pl.BlockSpec((LANES, 1536), const)],
        out_specs=[pl.BlockSpec((tm, D_MODEL), row), pl.BlockSpec((tm, D_MODEL), row),
                   pl.BlockSpec((tm, LANES), row), pl.BlockSpec((tm, LANES), row)],
        out_shape=[jax.ShapeDtypeStruct((n, D_MODEL), F32), jax.ShapeDtypeStruct((n, D_MODEL), F32),
                   jax.ShapeDtypeStruct((n, LANES), I32), jax.ShapeDtypeStruct((n, LANES), F32)],
        compiler_params=_params(("parallel",)),
        name="merge",
    )(x2d, oc, osel, ow, od, gx, nmix, nffn, wg, wbn, wbd, wo, wr, br, eexp)


def _route_body(idx_ref, dest_ref, meta_ref, cnt_s, base_s, *, tm, tmoe, nbp):
    ph = pl.program_id(0)
    i = pl.program_id(1)
    shift = int(math.log2(tmoe))
    idx_t = jnp.transpose(idx_ref[...])
    sub = lax.broadcasted_iota(I32, (LANES, tm), 0)
    ohs = [idx_t[k:k + 1, :] == sub for k in range(TOP_K)]
    tok = jnp.zeros((LANES, tm), F32)
    for oh in ohs:
        tok = tok + jnp.where(oh, 1.0, 0.0)
    tile_cnt = jnp.broadcast_to(jnp.sum(tok, axis=1, keepdims=True), (LANES, LANES))

    @pl.when((ph == 0) & (i == 0))
    def _():
        cnt_s[...] = jnp.zeros((LANES, LANES), F32)

    @pl.when(ph == 0)
    def _():
        cnt_s[...] = cnt_s[...] + tile_cnt

    @pl.when((ph == 1) & (i == 0))
    def _():
        cnt = cnt_s[...].astype(I32)
        padded = jnp.left_shift(jnp.right_shift(cnt + (tmoe - 1), shift), shift)
        er = lax.broadcasted_iota(I32, (LANES, LANES), 0)
        ec = lax.broadcasted_iota(I32, (LANES, LANES), 1)
        pad_start = _dot_hi(jnp.where(ec < er, 1.0, 0.0), padded.astype(F32))
        base_s[...] = pad_start
        pad_end = pad_start.astype(I32) + padded
        blk_start = lax.broadcasted_iota(I32, (LANES, nbp), 1) * tmoe
        ends = jnp.broadcast_to(pad_end[:, 0:1], (LANES, nbp))
        be = jnp.minimum(jnp.sum(jnp.where(ends <= blk_start, 1, 0), axis=0, keepdims=True), N_EXPERTS - 1)
        n_used = jnp.right_shift(jnp.max(pad_end, axis=0, keepdims=True)[:, 0:1], shift)
        last_blk = jnp.sum(jnp.where(er == ec, jnp.where(padded > 0, jnp.right_shift(pad_end, shift) - 1, -1), 0),
                           axis=0, keepdims=True)
        lb = jnp.concatenate([last_blk, jnp.full((1, nbp - LANES), -1, I32)], axis=1)
        row = lax.broadcasted_iota(I32, (8, nbp), 0)
        meta_ref[...] = jnp.where(row == 0, be, jnp.where(row == 1, n_used, jnp.where(row == 2, lb, 0)))

    @pl.when(ph == 1)
    def _():
        nr = lax.broadcasted_iota(I32, (tm, tm), 0)
        nc = lax.broadcasted_iota(I32, (tm, tm), 1)
        before = _dot(tok.astype(BF), jnp.where(nr < nc, 1.0, 0.0).astype(BF))
        pos = before + base_s[:, 0:1]
        rows = [jnp.sum(jnp.where(oh, pos, 0.0), axis=0, keepdims=True) for oh in ohs]
        dest_ref[...] = jnp.concatenate(rows + [jnp.zeros((8 - TOP_K, tm), F32)], axis=0).astype(I32)
        base_s[...] = base_s[...] + tile_cnt


def _route(idx_all, tm, tmoe, nbp):
    n_tok = idx_all.shape[0]
    return pl.pallas_call(
        functools.partial(_route_body, tm=tm, tmoe=tmoe, nbp=nbp),
        grid=(2, n_tok // tm),
        in_specs=[pl.BlockSpec((tm, LANES), lambda ph, i: (i, 0))],
        out_specs=[pl.BlockSpec((8, tm), lambda ph, i: (0, i * ph)), pl.BlockSpec((8, nbp), lambda ph, i: (0, 0))],
        out_shape=[jax.ShapeDtypeStruct((8, n_tok), I32), jax.ShapeDtypeStruct((8, nbp), I32)],
        scratch_shapes=[pltpu.VMEM((LANES, LANES), F32), pltpu.VMEM((LANES, LANES), F32)],
        compiler_params=_params(("arbitrary", "arbitrary")),
        name="route",
    )(idx_all)


def _dispatch_body(dest_ref, lb_ref, nu_ref, h_ref, xs_hbm, zbuf, zsem, sem, *, tm, tmoe, n_tok):
    i = pl.program_id(0)

    @pl.when(i == 0)
    def _():
        zbuf[...] = jnp.zeros(zbuf.shape, F32)

        def fill(e):
            return pltpu.make_async_copy(zbuf, xs_hbm.at[pl.ds(pl.multiple_of(lb_ref[e] * tmoe, tmoe), tmoe)], zsem)
        for e in range(N_EXPERTS):
            @pl.when(lb_ref[e] >= 0)
            def _():
                fill(e).start()
        for e in range(N_EXPERTS):
            @pl.when(lb_ref[e] >= 0)
            def _():
                fill(e).wait()

        def tail(b):
            return pltpu.make_async_copy(zbuf, xs_hbm.at[pl.ds(pl.multiple_of(b * tmoe, tmoe), tmoe)], zsem)
        n_blk = xs_hbm.shape[0] // tmoe
        lax.fori_loop(nu_ref[0], n_blk, lambda b, c: (tail(b).start(), c)[1], 0)
        lax.fori_loop(nu_ref[0], n_blk, lambda b, c: (tail(b).wait(), c)[1], 0)

    def body(n, c):
        for k in range(TOP_K):
            d = dest_ref[k * n_tok + i * tm + n]
            pltpu.make_async_copy(h_ref.at[pl.ds(n, 1)], xs_hbm.at[pl.ds(d, 1)], sem).start(priority=k % 2)
        return c
    lax.fori_loop(0, tm, body, 0, unroll=8)
    pltpu.make_async_copy(xs_hbm.at[pl.ds(0, TOP_K * tm)], xs_hbm.at[pl.ds(0, TOP_K * tm)], sem).wait()


def _dispatch(dest_flat, last_blk, n_used, h_all, n_rows, tm, tmoe):
    n_tok = h_all.shape[0]
    return pl.pallas_call(
        functools.partial(_dispatch_body, tm=tm, tmoe=tmoe, n_tok=n_tok),
        grid_spec=pltpu.PrefetchScalarGridSpec(
            num_scalar_prefetch=3, grid=(n_tok // tm,),
            in_specs=[pl.BlockSpec((tm, D_MODEL), lambda i, d, lb, nu: (i, 0))],
            out_specs=pl.BlockSpec(memory_space=pl.ANY),
            scratch_shapes=[pltpu.VMEM((tmoe, D_MODEL), F32), pltpu.SemaphoreType.DMA(()), pltpu.SemaphoreType.DMA(())]),
        out_shape=jax.ShapeDtypeStruct((n_rows, D_MODEL), F32),
        compiler_params=pltpu.CompilerParams(dimension_semantics=("arbitrary",), vmem_limit_bytes=VMEM_LIMIT,
                                             disable_bounds_checks=True),
        name="moe_dispatch",
    )(dest_flat, last_blk, n_used, h_all)


def _moe_body(be_ref, nused_ref, x_ref, wgu_ref, bgu_ref, wd_ref, bd_ref, o_ref, wgu_s, wd_s):
    i = pl.program_id(0)
    e = be_ref[i]
    prev = be_ref[jnp.maximum(i - 1, 0)]

    @pl.when((i == 0) | (e != prev))
    def _():
        def cast_rows(c, carry):
            r0 = pl.multiple_of(c * LANES, LANES)
            wgu_s[pl.ds(r0, LANES), :] = wgu_ref[0, pl.ds(r0, LANES), :].astype(BF)
            wd_s[pl.ds(r0, LANES), :] = wd_ref[0, pl.ds(r0, LANES), :].astype(BF)
            return carry
        lax.fori_loop(0, D_MODEL // LANES, cast_rows, 0)

    @pl.when(i < nused_ref[0])
    def _():
        gu = _dot(x_ref[...].astype(BF), wgu_s[...]) + bgu_ref[0]
        gate = jnp.minimum(gu[:, 0:D_FF], SWIGLU_LIMIT)
        up = jnp.clip(gu[:, D_FF:2 * D_FF], -SWIGLU_LIMIT, SWIGLU_LIMIT)
        act = (up + 1.0) * gate * jax.nn.sigmoid(SWIGLU_ALPHA * gate)
        o_ref[...] = _dot(act.astype(BF), wd_s[...]) + bd_ref[0]

    @pl.when(i >= nused_ref[0])
    def _():
        o_ref[...] = jnp.zeros(o_ref.shape, F32)


def _moe(block_expert, n_used, xs, wgu, bgu, wd, bd, tmoe):
    n_rows = xs.shape[0]
    nblk = n_rows // tmoe
    xmap = lambda i, be, nu: (jnp.minimum(i, jnp.maximum(nu[0] - 1, 0)), 0)
    emap = lambda i, be, nu: (be[i], 0, 0)
    return pl.pallas_call(
        _moe_body,
        grid_spec=pltpu.PrefetchScalarGridSpec(
            num_scalar_prefetch=2, grid=(nblk,),
            in_specs=[pl.BlockSpec((tmoe, D_MODEL), xmap),
                      pl.BlockSpec((1, D_MODEL, 2 * D_FF), emap), pl.BlockSpec((1, 1, 2 * D_FF), emap),
                      pl.BlockSpec((1, D_FF, D_MODEL), emap), pl.BlockSpec((1, 1, D_MODEL), emap)],
            out_specs=pl.BlockSpec((tmoe, D_MODEL), lambda i, be, nu: (i, 0)),
            scratch_shapes=[pltpu.VMEM((D_MODEL, 2 * D_FF), BF), pltpu.VMEM((D_FF, D_MODEL), BF)]),
        out_shape=jax.ShapeDtypeStruct((n_rows, D_MODEL), F32),
        compiler_params=_params(("arbitrary",)),
        name="moe_experts",
    )(block_expert, n_used, xs, wgu, bgu, wd, bd)


def _final_body(dest_ref, x1_ref, gate_ref, nf_ref, ys_hbm, y_ref, ybuf, sem, *, tm, n_tok, block0):
    i = pl.program_id(0)

    def gather(tile, sl):
        def body(n, c):
            for k in range(TOP_K):
                d = dest_ref[k * n_tok + (block0 + tile) * tm + n]
                pltpu.make_async_copy(ys_hbm.at[pl.ds(d, 1)], ybuf.at[sl, k, pl.ds(n, 1)],
                                      sem.at[sl]).start(priority=k % 2)
            return c
        lax.fori_loop(0, tm, body, 0, unroll=8)

    @pl.when(i == 0)
    def _():
        gather(0, 0)

    @pl.when(i + 1 < pl.num_programs(0))
    def _():
        gather(i + 1, (i + 1) % 2)

    slot = i % 2
    pltpu.make_async_copy(ybuf.at[slot], ybuf.at[slot], sem.at[slot]).wait()
    gate = gate_ref[...]
    x = x1_ref[...]
    for k in range(TOP_K):
        x = x + gate[:, k:k + 1] * ybuf[slot, k]
    y_ref[...] = x * lax.rsqrt(jnp.mean(x * x, axis=-1, keepdims=True) + RMS_EPS) * nf_ref[...]


def _final(dest_flat, x1, gates, nf, ys, tm, n_tok, block0):
    n = x1.shape[0]
    row = lambda i, d: (i, 0)
    return pl.pallas_call(
        functools.partial(_final_body, tm=tm, n_tok=n_tok, block0=block0),
        grid_spec=pltpu.PrefetchScalarGridSpec(
            num_scalar_prefetch=1, grid=(n // tm,),
            in_specs=[pl.BlockSpec((tm, D_MODEL), row), pl.BlockSpec((tm, LANES), row),
                      pl.BlockSpec((1, D_MODEL), lambda i, d: (0, 0)), pl.BlockSpec(memory_space=pl.ANY)],
            out_specs=pl.BlockSpec((tm, D_MODEL), row),
            scratch_shapes=[pltpu.VMEM((2, TOP_K, tm, D_MODEL), F32), pltpu.SemaphoreType.DMA((2,))]),
        out_shape=jax.ShapeDtypeStruct((n, D_MODEL), F32),
        compiler_params=pltpu.CompilerParams(dimension_semantics=("arbitrary",), vmem_limit_bytes=VMEM_LIMIT,
                                             disable_bounds_checks=True),
        name="final_norm",
    )(dest_flat, x1, gates, nf, ys)


def _rope_tables(pos, nb_lanes):
    half = ROT_DIM // 2
    inv = ROPE_THETA ** (-jnp.arange(half, dtype=F32) / half)
    ang = pos.astype(F32)[:, None] * inv[None, :]
    cos, sin = jnp.cos(ang), jnp.sin(ang)
    t = pos.shape[0]
    c64 = jnp.concatenate([cos, cos, jnp.ones((t, HEAD_DIM - ROT_DIM), F32)], axis=1)
    s64 = jnp.concatenate([-sin, sin, jnp.zeros((t, HEAD_DIM - ROT_DIM), F32)], axis=1)
    lane = jnp.arange(LANES)[None, :]
    e = ((lane >= 64) & (lane < 64 + nb_lanes) & ((pos // SEL_BLOCK)[:, None] == lane - 64)).astype(F32)
    return jnp.tile(c64, (1, 2)), jnp.tile(s64, (1, 2)), e


def _imp_matrix(rt, n_cmp, nb, nbp):
    m = np.zeros((rt, nbp), np.float32)
    sub = SEL_BLOCK // CMP_STRIDE
    for n in range(n_cmp):
        for shift in range(CMP_BLOCK // CMP_STRIDE):
            j = (n + shift) // sub
            if j < nb:
                m[n, j] += 1.0
    return jnp.asarray(m)


def kernel(x_prompt, x_sample, cache_cmp_kv, cache_sel_kv, state_win_kv, cache_diff_k, cache_diff_v, page_table, norm_mix, w_in, cmp_pe, w_cmp1, w_cmp2, diff_lambda, diff_subln, w_br_nsa, w_br_diff, w_out, norm_ffn, w_router, b_router, w_gate_up, b_gate_up, w_down, b_down, norm_final):
    bp, t, _ = x_prompt.shape
    bs, ts, _ = x_sample.shape
    assert ts == 8 and t % 256 == 0 and t <= 2048 and w_in.shape[0] == 1
    n_pool, page = cache_cmp_kv.shape[1], cache_cmp_kv.shape[2]
    pages = page_table.shape[1]
    past_len = pages * page
    assert page == LANES and past_len % SEL_BLOCK == 0
    lambda_init = 0.8 - 0.6 * math.exp(-0.3 * 0)
    n_p, n_s = bp * t, bs * ts

    w = w_in[0]
    gcols = w[:, 1280:1304].reshape(D_MODEL, NSA_HEADS, 3).transpose(0, 2, 1)
    gexp = jnp.broadcast_to(gcols[..., None], (D_MODEL, 3, NSA_HEADS, HEAD_DIM)).reshape(D_MODEL, 1536)
    w_proj = jnp.concatenate([w[:, 0:1280], w[:, 1304:2840], gexp], axis=1).astype(BF)
    w_gates = w[:, 2840:4888].astype(BF)
    nmix = norm_mix[0][None, :]
    nffn = norm_ffn[0][None, :]
    pe = cmp_pe[0]
    pea = pe[:, :CMP_STRIDE].reshape(2, 1, 1024)
    peb = pe[:, CMP_STRIDE:].reshape(2, 1, 1024)
    w1a = w_cmp1[0][:, :1024].astype(BF)
    w1b = w_cmp1[0][:, 1024:].astype(BF)
    w2p = jnp.pad(w_cmp2[0], ((0, 0), (0, 0), (0, LANES - HEAD_DIM))).astype(BF)
    wbn = w_br_nsa[0].astype(BF)
    wbd = w_br_diff[0].astype(BF)
    wo = w_out[0].astype(BF)
    wr = jnp.pad(w_router[0], ((0, 0), (0, LANES - N_EXPERTS)))
    br = jnp.concatenate([b_router[0], jnp.full((LANES - N_EXPERTS,), NEG, F32)])[None, :]
    dl = diff_lambda[0]
    subln = diff_subln[0][None, :]

    pos_p = jnp.arange(t, dtype=I32)
    pos_s = jnp.tile(past_len + jnp.arange(ts, dtype=I32), bs)
    tm = 256
    gpad = jnp.pad(w[:, 1280:1304], ((0, 0), (0, LANES - 3 * NSA_HEADS)))
    w_rows = jnp.concatenate([w[:, 0:768], w[:, 1304:1816], w[:, 2328:2840], gpad, w[:, 768:896], w[:, 1024:1152],
                              w[:, 1816:2328]], axis=1).astype(BF)
    eexp_np = np.zeros((LANES, 1536), np.float32)
    for hh in range(NSA_HEADS):
        for brn in range(3):
            eexp_np[hh * 3 + brn, brn * 512 + hh * HEAD_DIM:brn * 512 + (hh + 1) * HEAD_DIM] = 1.0
    eexp = jnp.asarray(eexp_np).astype(BF)
    w_cols = jnp.concatenate([w[:, 512:1280], w[:, 1816:2840]], axis=1).T.astype(BF)
    cos_p, sin_p, _ = _rope_tables(pos_p, 0)
    outs_p = _inproj_t(x_prompt.reshape(n_p, D_MODEL), nmix, w_rows, w_cols, cos_p, sin_p,
                       cos_p[:, 0:8].T, -sin_p[:, 0:8].T, bp, t, tm)
    outs_s = _inproj(x_sample.reshape(n_s, D_MODEL), nmix, w_proj, *_rope_tables(pos_s, 0), n_s)
    (qraw_p, qrot_p, kvc_p, kvct_p, kvst_p, kvwt_p, ksa_p, kwa_p, vst_p, vwt_p, qd_p, kdt_p, kdb_p, vd4_p, vdtb_p,
     gx_p) = outs_p
    (qraw_s, qrot_s, kvc_s, kvs_s, kvw_s, _, _, _, _, qd_s, kd_s, _, vd_s, _, gx_s) = outs_s

    rt_p = t // CMP_STRIDE
    n_cmp_p = (t - CMP_BLOCK) // CMP_STRIDE + 1
    nb_p = t // SEL_BLOCK
    kvcmp_p = _compress(kvc_p.reshape(bp, rt_p, 4096), jnp.arange(bp, dtype=I32)[:, None], pea, peb, w1a, w1b, w2p, 1)
    tq = 128
    nq = t // tq
    ocmp_p, selb_p = _cmp_attend(qraw_p.reshape(bp * nq, tq, 1024), kvcmp_p, _imp_matrix(rt_p, n_cmp_p, nb_p, LANES),
                                 nq, nb_p, LANES, 0)
    tq_d = 256
    qrot3 = qrot_p.reshape(n_p // tq_d, tq_d, 1024)
    osel_p = _nsa_flash("sel", qrot3, selb_p.reshape(n_p // tq_d, tq_d, 2 * LANES), ksa_p, vst_p, bp, t, tq_d, 512)
    owin_p = _nsa_flash("win", qrot3, None, kwa_p, vwt_p, bp, t, tq_d, 256)
    odiff_p = _diff_flash(qd_p.reshape(n_p // tq_d, tq_d, 512), kdb_p, vdtb_p, dl, subln, bp, t, tq_d, 512, lambda_init)

    n_cmp_s = (past_len + ts - CMP_BLOCK) // CMP_STRIDE + 1
    rt_s = past_len // CMP_STRIDE
    assert n_cmp_s <= rt_s
    nb_s = (past_len + ts + SEL_BLOCK - 1) // SEL_BLOCK
    assert nb_s <= 256
    ppg = 16 if pages % 16 == 0 else (8 if pages % 8 == 0 else 1)
    kvcmp_s = _compress(cache_cmp_kv[0].reshape(n_pool, page // CMP_STRIDE, 4096), page_table, pea, peb, w1a, w1b,
                        w2p, ppg)
    ocmp_s, selb_s = _cmp_attend(qraw_s.reshape(bs, ts, 1024), kvcmp_s, _imp_matrix(rt_s, n_cmp_s, nb_s, 256),
                                 1, nb_s, 256, past_len)
    qrot_s3 = qrot_s.reshape(bs, ts, 1024)
    sel_t = jnp.transpose(cache_sel_kv[0], (0, 2, 3, 4, 1))
    win_t = jnp.transpose(state_win_kv[0], (0, 2, 3, 4, 1))
    dk_t = jnp.transpose(cache_diff_k[0], (0, 2, 3, 4, 1))
    osel_s = _paged_attend("sel", qrot_s3, selb_s, sel_t, None, page_table,
                           kvs_s.reshape(bs, ts, 256), None, None, None, ppg, past_len, lambda_init)
    wbuf = state_win_kv.shape[2]
    assert wbuf == WINDOW
    owin_s = _paged_attend("win", qrot_s3, None, win_t, None, jnp.arange(bs, dtype=I32)[:, None],
                           kvw_s.reshape(bs, ts, 256), None, None, None, 1, past_len, lambda_init)
    odiff_s = _paged_attend("diff", qd_s.reshape(bs, ts, 512), None, dk_t, cache_diff_v[0], page_table,
                            kd_s.reshape(bs, ts, 512), vd_s.reshape(bs, ts, 512), dl, subln, ppg, past_len, lambda_init)

    x1_p, h_p, idx_p, gate_p = _merge(x_prompt.reshape(n_p, D_MODEL), ocmp_p.reshape(n_p, 512), osel_p.reshape(n_p, 512),
                                      owin_p.reshape(n_p, 512), odiff_p.reshape(n_p, 512), gx_p, nmix, nffn, w_gates,
                                      wbn, wbd, wo, wr, br, eexp, tm)
    x1_s, h_s, idx_s, gate_s = _merge(x_sample.reshape(n_s, D_MODEL), ocmp_s.reshape(n_s, 512), osel_s.reshape(n_s, 512),
                                      owin_s.reshape(n_s, 512), odiff_s.reshape(n_s, 512), gx_s, nmix, nffn, w_gates,
                                      wbn, wbd, wo, wr, br, eexp, n_s)

    n_tok = n_p + n_s
    n_assign = n_tok * TOP_K
    tmoe = 512
    assert n_p % n_s == 0 and n_tok % n_s == 0
    n_blocks = (n_assign + N_EXPERTS * (tmoe - 1) + tmoe - 1) // tmoe
    nbp = (n_blocks + LANES - 1) // LANES * LANES
    dest, meta = _route(jnp.concatenate([idx_p, idx_s], axis=0), n_s, tmoe, nbp)
    dest_flat = dest.reshape(-1)
    xs = _dispatch(dest_flat, meta[2, :N_EXPERTS], meta[1, :1], jnp.concatenate([h_p, h_s], axis=0), n_blocks * tmoe,
                   n_s, tmoe)
    ys = _moe(meta[0, :n_blocks], meta[1, :1], xs, w_gate_up[0], b_gate_up[0][:, None, :], w_down[0],
              b_down[0][:, None, :], tmoe)

    nf = norm_final[None, :]
    y_p = _final(dest_flat, x1_p, gate_p, nf, ys, n_s, n_tok, 0)
    y_s = _final(dest_flat, x1_s, gate_s, nf, ys, n_s, n_tok, n_p // n_s)

    g, hd = NSA_GROUPS, HEAD_DIM
    def from_t(a, heads, length):
        return jnp.transpose(a.reshape(bp, heads, 2, hd, length), (0, 4, 1, 2, 3))[None]

    new_win_p = from_t(kvwt_p[:, :, t - wbuf:], 2, wbuf)
    new_win_s = jnp.concatenate([state_win_kv[0].reshape(bs, wbuf, 256)[:, ts:], kvw_s.reshape(bs, ts, 256)], axis=1)
    return (y_p.reshape(bp, t, D_MODEL), y_s.reshape(bs, ts, D_MODEL),
            from_t(kvct_p, 2, t), kvc_s.reshape(1, bs, ts, 2, g, hd),
            from_t(kvst_p, 2, t), kvs_s.reshape(1, bs, ts, 2, g, hd),
            new_win_p, new_win_s.reshape(1, bs, wbuf, 2, g, hd),
            from_t(kdt_p, DIFF_HEADS, t), kd_s.reshape(1, bs, ts, DIFF_HEADS, 2, hd),
            vd4_p[None], vd_s.reshape(1, bs, ts, DIFF_HEADS, DIFF_DV))
```

```python
import functools
import math

import numpy as np
import jax
import jax.numpy as jnp
from jax import lax
from jax.experimental import pallas as pl
from jax.experimental.pallas import tpu as pltpu

D_MODEL = 1024
HEAD_DIM = 64
NSA_HEADS = 8
NSA_GROUPS = 2
NSA_REP = 4
CMP_BLOCK = 32
CMP_STRIDE = 16
SEL_BLOCK = 64
N_SEL = 16
WINDOW = 512
DIFF_HEADS = 4
DIFF_DV = 128
ROT_DIM = 16
ROPE_THETA = 500000.0
N_EXPERTS = 32
TOP_K = 4
D_FF = 1024
SWIGLU_LIMIT = 7.0
SWIGLU_ALPHA = 1.702
RMS_EPS = 1e-5
FORCE_BONUS = 1e4
SCALE = HEAD_DIM ** -0.5
LANES = 128

BF = jnp.bfloat16
F32 = jnp.float32
I32 = jnp.int32
NEG = -1e30
VMEM_LIMIT = 56 * 1024 * 1024

OFF_Q, OFF_KVC, OFF_KVS, OFF_KVW, OFF_QD, OFF_KD, OFF_VD, OFF_GX = 0, 512, 768, 1024, 1280, 1792, 2304, 2816
IN_COLS_K = 4352
TOFF_Q, TOFF_KVC, TOFF_QD, TOFF_VD, TOFF_GX, TOFF_KS, TOFF_KW, TOFF_KD = 0, 512, 768, 1280, 1792, 1920, 2048, 2176
TIN_COLS = 2688
TIN_ROWS = 1792


def _dot(a, b):
    return jnp.dot(a, b, preferred_element_type=F32)


def _dot_nt(a, b):
    return lax.dot_general(a, b, (((1,), (1,)), ((), ())), preferred_element_type=F32)


def _dot_hi(a, b):
    return jnp.dot(a, b, preferred_element_type=F32, precision=lax.Precision.HIGHEST)


def _params(sem):
    return pltpu.CompilerParams(dimension_semantics=sem, vmem_limit_bytes=VMEM_LIMIT)


def _lane_iota(rows):
    return lax.broadcasted_iota(I32, (rows, LANES), 1)


def _pair(lo, a, b, g):
    if g == 0:
        return jnp.where(lo, a, pltpu.roll(b, 64, 1))
    return jnp.where(lo, pltpu.roll(a, 64, 1), b)


def _inproj_body(x_ref, nw_ref, w_ref, cos_ref, sin_ref, e_ref,
                 qraw_ref, qrot_ref, kvc_ref, kvs_ref, kvw_ref, ksa_ref, kwa_ref, vsb_ref, vwb_ref,
                 qd_ref, kd_ref, kdb_ref, vd_ref, vdb_ref, gx_ref):
    x = x_ref[...]
    tm = x.shape[0]
    xn = (x * lax.rsqrt(jnp.mean(x * x, axis=-1, keepdims=True) + RMS_EPS) * nw_ref[...]).astype(BF)
    cos = cos_ref[...]
    sin = sin_ref[...]
    epat = e_ref[...]
    lane = _lane_iota(tm)
    lo = lane < 64
    first8 = (lane & 63) < 8

    def mm(off):
        return _dot(xn, w_ref[:, off:off + LANES])

    def rope(y):
        sw = jnp.where(first8, pltpu.roll(y, LANES - 8, 1), pltpu.roll(y, 8, 1))
        return y * cos + sw * sin

    for c in range(4):
        y = mm(OFF_Q + c * LANES) * SCALE
        yr = rope(y)
        ys = pltpu.roll(y, 64, 1)
        yrs = pltpu.roll(yr, 64, 1)
        qraw_ref[:, (2 * c) * LANES:(2 * c + 1) * LANES] = jnp.where(lo, y, 0.0).astype(BF)
        qraw_ref[:, (2 * c + 1) * LANES:(2 * c + 2) * LANES] = jnp.where(lo, ys, 0.0).astype(BF)
        qrot_ref[:, (2 * c) * LANES:(2 * c + 1) * LANES] = jnp.where(lo, yr, 0.0).astype(BF)
        qrot_ref[:, (2 * c + 1) * LANES:(2 * c + 2) * LANES] = jnp.where(lo, yrs, 0.0).astype(BF)

    for c in range(2):
        kvc_ref[:, c * LANES:(c + 1) * LANES] = mm(OFF_KVC + c * LANES)

    for off, kv_ref, ka_ref, vb_ref in ((OFF_KVS, kvs_ref, ksa_ref, vsb_ref), (OFF_KVW, kvw_ref, kwa_ref, vwb_ref)):
        kr = rope(mm(off))
        v = mm(off + LANES)
        kv_ref[:, 0:LANES] = kr
        kv_ref[:, LANES:2 * LANES] = v
        ka_ref[:, 0:LANES] = jnp.where(lo, kr, epat).astype(BF)
        ka_ref[:, LANES:2 * LANES] = jnp.where(lo, pltpu.roll(kr, 64, 1), epat).astype(BF)
        vb_ref[...] = v.astype(BF)

    for c in range(4):
        sl = slice(c * LANES, (c + 1) * LANES)
        qd_ref[:, sl] = rope(mm(OFF_QD + c * LANES) * SCALE).astype(BF)
        kr = rope(mm(OFF_KD + c * LANES))
        kd_ref[:, sl] = kr
        kdb_ref[:, sl] = kr.astype(BF)
        v = mm(OFF_VD + c * LANES)
        vd_ref[:, sl] = v
        vdb_ref[:, sl] = v.astype(BF)

    for c in range(12):
        gx_ref[:, c * LANES:(c + 1) * LANES] = jax.nn.sigmoid(mm(OFF_GX + c * LANES))


def _inproj(x2d, norm_w, w_bf, cos_t, sin_t, e_t, tm):
    n = x2d.shape[0]
    nt = cos_t.shape[0] // tm
    row = lambda i: (i, 0)
    tab = lambda i: (i % nt, 0)
    const = lambda i: (0, 0)
    outs = [(1024, BF), (1024, BF), (256, F32), (256, F32), (256, F32), (256, BF), (256, BF), (128, BF), (128, BF),
            (512, BF), (512, F32), (512, BF), (512, F32), (512, BF), (1536, F32)]
    return pl.pallas_call(
        _inproj_body,
        grid=(n // tm,),
        in_specs=[pl.BlockSpec((tm, D_MODEL), row), pl.BlockSpec((1, D_MODEL), const),
                  pl.BlockSpec((D_MODEL, IN_COLS_K), const),
                  pl.BlockSpec((tm, LANES), tab), pl.BlockSpec((tm, LANES), tab), pl.BlockSpec((tm, LANES), tab)],
        out_specs=[pl.BlockSpec((tm, w), row) for w, _ in outs],
        out_shape=[jax.ShapeDtypeStruct((n, w), dt) for w, dt in outs],
        compiler_params=_params(("parallel",)),
        name="inproj",
    )(x2d, norm_w, w_bf, cos_t, sin_t, e_t)


def _inproj_t_body(x_ref, nw_ref, w_ref, wt_ref, cos_ref, sin_ref, cost_ref, sint_ref,
                   qraw_ref, qrot_ref, kvc_ref, kvct_ref, kvst_ref, kvwt_ref, ksa_ref, kwa_ref, vst_ref, vwt_ref,
                   qd_ref, kdt_ref, kdb_ref, vd4_ref, vdtb_ref, gx_ref):
    x = x_ref[...]
    tm = x.shape[0]
    xn = (x * lax.rsqrt(jnp.mean(x * x, axis=-1, keepdims=True) + RMS_EPS) * nw_ref[...]).astype(BF)
    cos = cos_ref[...]
    sin = sin_ref[...]
    cos_t = cost_ref[...]
    sin_t = sint_ref[...]
    lane = _lane_iota(tm)
    lo = lane < 64
    first8 = (lane & 63) < 8

    def mm(off):
        return _dot(xn, w_ref[:, off:off + LANES])

    def mm_t(off, n):
        return _dot_nt(wt_ref[off:off + n, :], xn)

    def rope(y):
        sw = jnp.where(first8, pltpu.roll(y, LANES - 8, 1), pltpu.roll(y, 8, 1))
        return y * cos + sw * sin

    def rope_t(y):
        parts = []
        for hb in range(0, y.shape[0], HEAD_DIM):
            x1 = y[hb:hb + 8]
            x2 = y[hb + 8:hb + 16]
            parts += [x1 * cos_t - x2 * sin_t, x1 * sin_t + x2 * cos_t, y[hb + 16:hb + HEAD_DIM]]
        return jnp.concatenate(parts, axis=0)

    for c in range(4):
        y = mm(TOFF_Q + c * LANES) * SCALE
        yr = rope(y)
        ys = pltpu.roll(y, 64, 1)
        yrs = pltpu.roll(yr, 64, 1)
        qraw_ref[:, (2 * c) * LANES:(2 * c + 1) * LANES] = jnp.where(lo, y, 0.0).astype(BF)
        qraw_ref[:, (2 * c + 1) * LANES:(2 * c + 2) * LANES] = jnp.where(lo, ys, 0.0).astype(BF)
        qrot_ref[:, (2 * c) * LANES:(2 * c + 1) * LANES] = jnp.where(lo, yr, 0.0).astype(BF)
        qrot_ref[:, (2 * c + 1) * LANES:(2 * c + 2) * LANES] = jnp.where(lo, yrs, 0.0).astype(BF)

    for c in range(2):
        kvc_ref[:, c * LANES:(c + 1) * LANES] = mm(TOFF_KVC + c * LANES)
    kvct_ref[0] = mm_t(0, 256)

    blk = jnp.right_shift(pl.program_id(1) * tm + lax.broadcasted_iota(I32, (tm, LANES), 0), 6)
    epat = jnp.where((lane >= 64) & (lane < 96) & (blk == lane - 64), 1.0, 0.0)
    for off, koff, kvt_ref, ka_ref, vt_ref in ((256, TOFF_KS, kvst_ref, ksa_ref, vst_ref),
                                               (512, TOFF_KW, kvwt_ref, kwa_ref, vwt_ref)):
        y = mm_t(off, 256)
        v = y[128:256]
        kvt_ref[0, 0:128, :] = rope_t(y[0:128])
        kvt_ref[0, 128:256, :] = v
        vt_ref[0] = v.astype(BF)
        kr = rope(mm(koff))
        ka_ref[:, 0:LANES] = jnp.where(lo, kr, epat).astype(BF)
        ka_ref[:, LANES:2 * LANES] = jnp.where(lo, pltpu.roll(kr, 64, 1), epat).astype(BF)

    kdt_ref[0] = rope_t(mm_t(768, 512))
    vdtb_ref[0] = mm_t(1280, 512).astype(BF)
    for c in range(4):
        sl = slice(c * LANES, (c + 1) * LANES)
        qd_ref[:, sl] = rope(mm(TOFF_QD + c * LANES) * SCALE).astype(BF)
        kdb_ref[:, sl] = rope(mm(TOFF_KD + c * LANES)).astype(BF)
        vd4_ref[0, :, c, :] = mm(TOFF_VD + c * LANES)

    gx_ref[...] = jax.nn.sigmoid(mm(TOFF_GX))


def _inproj_t(x2d, norm_w, w_bf, wt_bf, cos_t, sin_t, cos_tt, sin_tt, nbat, t, tm):
    n = x2d.shape[0]
    nt = t // tm
    row = lambda b, i: (b * nt + i, 0)
    tab = lambda b, i: (i, 0)
    tab_t = lambda b, i: (0, i)
    const = lambda b, i: (0, 0)
    tr = lambda b, i: (b, 0, i)
    rm = lambda w, dt: (pl.BlockSpec((tm, w), row), jax.ShapeDtypeStruct((n, w), dt))
    tp = lambda r, dt: (pl.BlockSpec((1, r, tm), tr), jax.ShapeDtypeStruct((nbat, r, t), dt))
    vd4 = (pl.BlockSpec((1, tm, DIFF_HEADS, DIFF_DV), lambda b, i: (b, i, 0, 0)),
           jax.ShapeDtypeStruct((nbat, t, DIFF_HEADS, DIFF_DV), F32))
    outs = [rm(1024, BF), rm(1024, BF), rm(256, F32), tp(256, F32), tp(256, F32), tp(256, F32), rm(256, BF),
            rm(256, BF), tp(128, BF), tp(128, BF), rm(512, BF), tp(512, F32), rm(512, BF), vd4, tp(512, BF),
            rm(LANES, F32)]
    return pl.pallas_call(
        _inproj_t_body,
        grid=(nbat, nt),
        in_specs=[pl.BlockSpec((tm, D_MODEL), row), pl.BlockSpec((1, D_MODEL), const),
                  pl.BlockSpec((D_MODEL, TIN_COLS), const), pl.BlockSpec((TIN_ROWS, D_MODEL), const),
                  pl.BlockSpec((tm, LANES), tab), pl.BlockSpec((tm, LANES), tab),
                  pl.BlockSpec((8, tm), tab_t), pl.BlockSpec((8, tm), tab_t)],
        out_specs=[o[0] for o in outs],
        out_shape=[o[1] for o in outs],
        compiler_params=_params(("parallel", "parallel")),
        name="inproj_t",
    )(x2d, norm_w, w_bf, wt_bf, cos_t, sin_t, cos_tt, sin_tt)


def _compress_body(pt_ref, *refs, n_pages, rp, rt, native):
    page_refs = refs[:n_pages]
    pea_ref, peb_ref, w1a_ref, w1b_ref, w2_ref, out_ref, y_scr = refs[n_pages:n_pages + 7]
    s = pl.program_id(1)
    for i in range(n_pages):
        row0 = pl.multiple_of((s * n_pages + i) * rp, 8)
        if native:
            xbuf = refs[n_pages + 7]
            lo = _lane_iota(rp) < 64
            for c in range(2):
                xbuf[c] = jnp.transpose(page_refs[i][0, c].reshape(LANES, LANES))
            for c in range(2):
                pieces = [xbuf[c, pl.ds(l, rp, stride=CMP_STRIDE), :] for l in range(CMP_STRIDE)]
                for g in range(2):
                    y_scr[c * 2 + g, pl.ds(row0, rp), :] = jnp.concatenate(
                        [_pair(lo, pieces[2 * j], pieces[2 * j + 1], g) for j in range(CMP_STRIDE // 2)], axis=1)
            continue
        x = page_refs[i][0]
        for cg in range(4):
            ycg = jnp.concatenate(
                [x[:, l * 256 + cg * 64:l * 256 + cg * 64 + 64] for l in range(CMP_STRIDE)], axis=1)
            y_scr[cg, pl.ds(row0, rp), :] = ycg

    @pl.when(s == pl.num_programs(1) - 1)
    def _():
        for c in range(2):
            for g in range(2):
                cg = c * 2 + g
                y = y_scr[cg]
                za = _dot((y + pea_ref[c]).astype(BF), w1a_ref[c])
                zb = _dot((y + peb_ref[c]).astype(BF), w1b_ref[c])
                hid = jax.nn.gelu(za + pltpu.roll(zb, rt - 1, 0))
                out_ref[0, :, cg * LANES:(cg + 1) * LANES] = _dot(hid.astype(BF), w2_ref[c]).astype(BF)


def _compress(pool, page_table, pea, peb, w1a, w1b, w2p, n_pages):
    nb, pages = page_table.shape
    native = pool.ndim == 5
    rp = LANES // CMP_STRIDE if native else pool.shape[1]
    rt = pages * rp
    steps = pages // n_pages
    pblock = (1,) + pool.shape[1:]
    page_specs = [pl.BlockSpec(pblock, functools.partial(
        lambda b, s, pt, i: (pt[b, s * n_pages + i],) + (0,) * (len(pblock) - 1), i=i)) for i in range(n_pages)]
    c3 = lambda b, s, pt: (0, 0, 0)
    scratch = [pltpu.VMEM((4, rt, 1024), F32)] + ([pltpu.VMEM((2, LANES, LANES), F32)] if native else [])
    return pl.pallas_call(
        functools.partial(_compress_body, n_pages=n_pages, rp=rp, rt=rt, native=native),
        grid_spec=pltpu.PrefetchScalarGridSpec(
            num_scalar_prefetch=1,
            grid=(nb, steps),
            in_specs=page_specs + [pl.BlockSpec((2, 1, 1024), c3), pl.BlockSpec((2, 1, 1024), c3),
                                   pl.BlockSpec((2, 1024, LANES), c3), pl.BlockSpec((2, 1024, LANES), c3),
                                   pl.BlockSpec((2, LANES, LANES), c3)],
            out_specs=pl.BlockSpec((1, rt, 512), lambda b, s, pt: (b, 0, 0)),
            scratch_shapes=scratch),
        out_shape=jax.ShapeDtypeStruct((nb, rt, 512), BF),
        compiler_params=_params(("parallel", "arbitrary")),
        name="compress",
    )(page_table, *([pool] * n_pages), pea, peb, w1a, w1b, w2p)


def _cmp_body(q_ref, kvc_ref, mimp_ref, ocmp_ref, selb_ref, *, tq, rt, nb, nbp, pos0):
    i = pl.program_id(1)
    pos_n = pos0 + i * tq + lax.broadcasted_iota(I32, (tq, rt), 0)
    n_i = lax.broadcasted_iota(I32, (tq, rt), 1)
    vis = (n_i * CMP_STRIDE + (CMP_BLOCK - 1)) <= pos_n
    lo = _lane_iota(tq) < 64
    pos_b = pos0 + i * tq + lax.broadcasted_iota(I32, (tq, nbp), 0)
    jb = lax.broadcasted_iota(I32, (tq, nbp), 1)
    cur = jnp.right_shift(pos_b, 6)
    valid = (jb * SEL_BLOCK <= pos_b) & (jb < nb)
    forced = (jb == 0) | (jb == cur) | (jb == cur - 1)
    mimp = mimp_ref[...]
    for g in range(NSA_GROUPS):
        kc = kvc_ref[0, :, g * LANES:(g + 1) * LANES]
        vc = kvc_ref[0, :, (2 + g) * LANES:(3 + g) * LANES]
        pg = jnp.zeros((tq, rt), F32)
        og = []
        for r in range(NSA_REP):
            h = g * NSA_REP + r
            q = q_ref[0, :, h * LANES:(h + 1) * LANES]
            s = jnp.where(vis, _dot_nt(q, kc), -jnp.inf)
            m = jnp.max(s, axis=-1, keepdims=True)
            m = jnp.where(m == -jnp.inf, 0.0, m)
            e = jnp.exp(s - m)
            d = jnp.sum(e, axis=-1, keepdims=True)
            p = e / jnp.where(d > 0, d, 1.0)
            og.append(_dot(p.astype(BF), vc))
            pg = pg + p
        imp = _dot_hi(pg, mimp)
        score = jnp.where(valid, jnp.where(forced, imp + FORCE_BONUS, imp), -jnp.inf)
        if tq == nbp and nb % 8 == 0:
            st = jnp.transpose(score)[0:nb]
            jrow = lax.broadcasted_iota(I32, (nb, tq), 0)
            rank_t = jnp.zeros((nb, tq), F32)
            for jj in range(nb):
                cj = st[jj:jj + 1, :]
                rank_t = rank_t + jnp.where((cj > st) | ((cj == st) & (jrow > jj)), 1.0, 0.0)
            rank = jnp.transpose(jnp.concatenate([rank_t, jnp.full((nbp - nb, tq), float(nbp), F32)], axis=0))
            sel = (rank < float(N_SEL)) & valid
        else:
            rank = jnp.zeros((tq, nbp), I32)
            for jj in range(nb):
                cj = score[:, jj:jj + 1]
                beats = (cj > score) | ((cj == score) & (jb > jj))
                rank = rank + beats.astype(I32)
            sel = (rank < N_SEL) & valid
        selb_ref[0, :, g * nbp:(g + 1) * nbp] = jnp.where(sel, 0.0, NEG)
        for c2 in range(2):
            ocmp_ref[0, :, (g * 2 + c2) * LANES:(g * 2 + c2 + 1) * LANES] = _pair(lo, og[2 * c2], og[2 * c2 + 1], 0)


def _cmp_attend(qraw3, kvc, mimp, nq, nb, nbp, pos0):
    nbat = kvc.shape[0]
    _, tq, _ = qraw3.shape
    rt = kvc.shape[1]
    return pl.pallas_call(
        functools.partial(_cmp_body, tq=tq, rt=rt, nb=nb, nbp=nbp, pos0=pos0),
        grid=(nbat, nq),
        in_specs=[pl.BlockSpec((1, tq, 1024), lambda b, i: (b * nq + i, 0, 0)),
                  pl.BlockSpec((1, rt, 512), lambda b, i: (b, 0, 0)),
                  pl.BlockSpec((rt, nbp), lambda b, i: (0, 0))],
        out_specs=[pl.BlockSpec((1, tq, 512), lambda b, i: (b * nq + i, 0, 0)),
                   pl.BlockSpec((1, tq, 2 * nbp), lambda b, i: (b * nq + i, 0, 0))],
        out_shape=[jax.ShapeDtypeStruct((nbat * nq, tq, 512), F32),
                   jax.ShapeDtypeStruct((nbat * nq, tq, 2 * nbp), F32)],
        compiler_params=_params(("parallel", "parallel")),
        name="cmp_attend",
    )(qraw3, kvc, mimp)


def _online_t(s, vt, m_s, l_s, acc, idx):
    m_old = m_s[idx]
    m_new = jnp.maximum(m_old, jnp.max(s, axis=0, keepdims=True))
    alpha = jnp.exp(m_old - m_new)
    p = jnp.exp(s - m_new)
    l_s[idx] = alpha * l_s[idx] + jnp.sum(p, axis=0, keepdims=True)
    acc[idx] = alpha * acc[idx] + _dot(vt, p.astype(BF))
    m_s[idx] = m_new


STEP_FIRST, STEP_LAST, STEP_MASKED = 1, 2, 4


def _step_table(nq, tq, tk, window=None):
    qi, kt, fl = [], [], []
    for i in range(nq):
        first = 0 if window is None else max(0, (i * tq - window + 1) // tk)
        last = (i * tq + tq - 1) // tk
        for j in range(first, last + 1):
            masked = j * tk + tk - 1 > i * tq
            if window is not None:
                masked = masked or j * tk <= i * tq + tq - 1 - window
            qi.append(i)
            kt.append(j)
            fl.append((STEP_FIRST if j == first else 0) | (STEP_LAST if j == last else 0)
                      | (STEP_MASKED if masked else 0))
    return (jnp.asarray(np.array(qi, np.int32)), jnp.asarray(np.array(kt, np.int32)),
            jnp.asarray(np.array(fl, np.int32)))


def _nsa_flash_body(qi_ref, kt_ref, fl_ref, *refs, mode, tq, tk):
    if mode == "sel":
        q_ref, selb_ref, k_ref, v_ref, o_ref, qs, m_s, l_s, acc = refs
    else:
        q_ref, k_ref, v_ref, o_ref, qs, m_s, l_s, acc = refs
    st = pl.program_id(1)
    i = qi_ref[st]
    jt = kt_ref[st]
    flags = fl_ref[st]
    q0 = i * tq
    rows = NSA_REP * tq
    lane = _lane_iota(tq)
    lo = lane < 64

    @pl.when((flags & STEP_FIRST) != 0)
    def _():
        m_s[...] = jnp.full(m_s.shape, NEG, F32)
        l_s[...] = jnp.zeros(l_s.shape, F32)
        acc[...] = jnp.zeros(acc.shape, F32)
        for g in range(NSA_GROUPS):
            if mode == "sel":
                sb = pltpu.roll(selb_ref[0, :, g * LANES:(g + 1) * LANES], 64, 1)
                sb = jnp.where((lane >= 64) & (lane < 96), sb, 0.0)
            for r in range(NSA_REP):
                h = g * NSA_REP + r
                q = q_ref[0, :, h * LANES:(h + 1) * LANES]
                if mode == "sel":
                    q = (q.astype(F32) + sb).astype(BF)
                qs[g, r * tq:(r + 1) * tq, :] = q

    need_mask = (flags & STEP_MASKED) != 0

    def step(masked):
        for g in range(NSA_GROUPS):
            s = _dot_nt(k_ref[:, g * LANES:(g + 1) * LANES], qs[g])
            if masked:
                kpos = jt * tk + lax.broadcasted_iota(I32, (tk, rows), 0)
                rpos = q0 + (lax.broadcasted_iota(I32, (tk, rows), 1) & (tq - 1))
                vis = kpos <= rpos
                if mode == "win":
                    vis = vis & (kpos > rpos - WINDOW)
                s = jnp.where(vis, s, NEG)
            _online_t(s, v_ref[0, g * HEAD_DIM:(g + 1) * HEAD_DIM, :], m_s, l_s, acc, g)

    @pl.when(need_mask)
    def _():
        step(True)

    @pl.when(jnp.logical_not(need_mask))
    def _():
        step(False)

    @pl.when((flags & STEP_LAST) != 0)
    def _():
        for g in range(NSA_GROUPS):
            on = acc[g] / l_s[g]
            for c2 in range(2):
                pair = jnp.concatenate([on[:, (2 * c2) * tq:(2 * c2 + 1) * tq],
                                        on[:, (2 * c2 + 1) * tq:(2 * c2 + 2) * tq]], axis=0)
                o_ref[0, :, (g * 2 + c2) * LANES:(g * 2 + c2 + 1) * LANES] = jnp.concatenate(
                    [jnp.transpose(pair[:, c * LANES:(c + 1) * LANES]) for c in range(tq // LANES)], axis=0)


def _nsa_flash(mode, q3, selb3, k2, v2, nbat, t, tq, tk):
    nq = t // tq
    rows = NSA_REP * tq
    table = _step_table(nq, tq, tk, None if mode == "sel" else WINDOW)
    qmap = lambda b, s, qi, kt, fl: (b * nq + qi[s], 0, 0)
    in_specs = [pl.BlockSpec((1, tq, 1024), qmap)]
    args = [q3]
    if mode == "sel":
        in_specs.append(pl.BlockSpec((1, tq, 2 * LANES), qmap))
        args.append(selb3)
    nkv = t // tk
    in_specs += [pl.BlockSpec((tk, 2 * LANES), lambda b, s, qi, kt, fl: (b * nkv + kt[s], 0)),
                 pl.BlockSpec((1, LANES, tk), lambda b, s, qi, kt, fl: (b, 0, kt[s]))]
    args += [k2, v2]
    return pl.pallas_call(
        functools.partial(_nsa_flash_body, mode=mode, tq=tq, tk=tk),
        grid_spec=pltpu.PrefetchScalarGridSpec(
            num_scalar_prefetch=3, grid=(nbat, int(table[0].shape[0])), in_specs=in_specs,
            out_specs=pl.BlockSpec((1, tq, 512), qmap),
            scratch_shapes=[pltpu.VMEM((NSA_GROUPS, rows, LANES), BF), pltpu.VMEM((NSA_GROUPS, 1, rows), F32),
                            pltpu.VMEM((NSA_GROUPS, 1, rows), F32), pltpu.VMEM((NSA_GROUPS, HEAD_DIM, rows), F32)]),
        out_shape=jax.ShapeDtypeStruct((nbat * nq, tq, 512), F32),
        compiler_params=_params(("parallel", "arbitrary")),
        name="nsa_flash_" + mode,
    )(*table, *args)


def _diff_lambda(dl, lambda_init):
    a = jnp.sum(dl[0:1] * dl[1:2], axis=1, keepdims=True)
    b = jnp.sum(dl[2:3] * dl[3:4], axis=1, keepdims=True)
    return jnp.exp(a) - jnp.exp(b) + lambda_init


def _diff_finish(a0, a1, lam, subln, lambda_init):
    o = a0 - lam * a1
    o = o * lax.rsqrt(jnp.mean(o * o, axis=-1, keepdims=True) + RMS_EPS) * subln
    return o * (1.0 - lambda_init)


def _diff_flash_body(qi_ref, kt_ref, fl_ref, q_ref, k_ref, v_ref, dl_ref, sub_ref, o_ref, qs, m_s, l_s, acc, *,
                     tq, tk, lambda_init):
    st = pl.program_id(1)
    i = qi_ref[st]
    j = kt_ref[st]
    flags = fl_ref[st]
    q0 = i * tq
    rows = 2 * tq
    lo = _lane_iota(tq) < 64

    @pl.when((flags & STEP_FIRST) != 0)
    def _():
        m_s[...] = jnp.full(m_s.shape, NEG, F32)
        l_s[...] = jnp.zeros(l_s.shape, F32)
        acc[...] = jnp.zeros(acc.shape, F32)
        for h in range(DIFF_HEADS):
            q = q_ref[0, :, h * LANES:(h + 1) * LANES].astype(F32)
            qs[h, 0:tq, :] = jnp.where(lo, q, 0.0).astype(BF)
            qs[h, tq:2 * tq, :] = jnp.where(lo, 0.0, q).astype(BF)

    need_mask = (flags & STEP_MASKED) != 0

    def step(masked):
        for h in range(DIFF_HEADS):
            s = _dot_nt(k_ref[:, h * LANES:(h + 1) * LANES], qs[h])
            if masked:
                kpos = j * tk + lax.broadcasted_iota(I32, (tk, rows), 0)
                rpos = q0 + (lax.broadcasted_iota(I32, (tk, rows), 1) & (tq - 1))
                s = jnp.where(kpos <= rpos, s, NEG)
            _online_t(s, v_ref[0, h * LANES:(h + 1) * LANES, :], m_s, l_s, acc, h)

    @pl.when(need_mask)
    def _():
        step(True)

    @pl.when(jnp.logical_not(need_mask))
    def _():
        step(False)

    @pl.when((flags & STEP_LAST) != 0)
    def _():
        lam = _diff_lambda(dl_ref[...], lambda_init)
        for h in range(DIFF_HEADS):
            on = acc[h] / l_s[h]
            a0, a1 = [jnp.concatenate([jnp.transpose(on[:, c0 + c * LANES:c0 + (c + 1) * LANES])
                                       for c in range(tq // LANES)], axis=0) for c0 in (0, tq)]
            o_ref[0, :, h * LANES:(h + 1) * LANES] = _diff_finish(a0, a1, lam, sub_ref[...], lambda_init)


def _diff_flash(q3, k2, v2, dl, subln, nbat, t, tq, tk, lambda_init):
    nq = t // tq
    nk = t // tk
    table = _step_table(nq, tq, tk)
    qmap = lambda b, s, qi, kt, fl: (b * nq + qi[s], 0, 0)
    kmap = lambda b, s, qi, kt, fl: (b * nk + kt[s], 0)
    vmap = lambda b, s, qi, kt, fl: (b, 0, kt[s])
    const = lambda b, s, qi, kt, fl: (0, 0)
    rows = 2 * tq
    return pl.pallas_call(
        functools.partial(_diff_flash_body, tq=tq, tk=tk, lambda_init=lambda_init),
        grid_spec=pltpu.PrefetchScalarGridSpec(
            num_scalar_prefetch=3, grid=(nbat, int(table[0].shape[0])),
            in_specs=[pl.BlockSpec((1, tq, 512), qmap), pl.BlockSpec((tk, 512), kmap),
                      pl.BlockSpec((1, 512, tk), vmap), pl.BlockSpec((4, HEAD_DIM), const),
                      pl.BlockSpec((1, DIFF_DV), const)],
            out_specs=pl.BlockSpec((1, tq, 512), qmap),
            scratch_shapes=[pltpu.VMEM((DIFF_HEADS, rows, LANES), BF), pltpu.VMEM((DIFF_HEADS, 1, rows), F32),
                            pltpu.VMEM((DIFF_HEADS, 1, rows), F32), pltpu.VMEM((DIFF_HEADS, DIFF_DV, rows), F32)]),
        out_shape=jax.ShapeDtypeStruct((nbat * nq, tq, 512), F32),
        compiler_params=_params(("parallel", "arbitrary")),
        name="diff_flash",
    )(*table, q3, k2, v2, dl, subln)


def _paged_body(pt_ref, *refs, mode, n_pages, past_len, lambda_init):
    q_ref = refs[0]
    k = 1
    if mode == "sel":
        selb_ref = refs[k]
        k += 1
    kpages = refs[k:k + n_pages]
    k += n_pages
    if mode == "diff":
        vpages = refs[k:k + n_pages]
        k += n_pages
    newk_ref = refs[k]
    k += 1
    if mode == "diff":
        newv_ref, dl_ref, sub_ref = refs[k:k + 3]
        k += 3
    o_ref = refs[k]
    k += 1
    if mode == "sel":
        qbd, bias, m_s, l_s, acc = refs[k:]
    else:
        qbd, m_s, l_s, acc = refs[k:]
    kw = WINDOW if mode == "win" else LANES
    st = pl.program_id(1)
    nq = 8
    rows = 64
    lane8 = _lane_iota(nq)
    lo8 = lane8 < 64
    t_row = lax.broadcasted_iota(I32, (rows, LANES), 0) & (nq - 1)
    col = lax.broadcasted_iota(I32, (rows, LANES), 1)

    def update(s, pv):
        m_old = m_s[0]
        m_new = jnp.maximum(m_old, jnp.max(s, axis=-1, keepdims=True))
        alpha = jnp.exp(m_old - m_new)
        p = jnp.exp(s - m_new)
        l_s[0] = alpha * l_s[0] + jnp.sum(p, axis=-1, keepdims=True)
        acc[0] = alpha * acc[0] + pv(p.astype(BF))
        m_s[0] = m_new

    def per_head(p, v_of_head):
        return jnp.concatenate([_dot(p[h * 16:(h + 1) * 16], v_of_head(h)) for h in range(DIFF_HEADS)], axis=0)

    @pl.when(st == 0)
    def _():
        m_s[0] = jnp.full((rows, 1), NEG, F32)
        l_s[0] = jnp.zeros((rows, 1), F32)
        acc[0] = jnp.zeros(acc.shape[1:], F32)
        if mode == "diff":
            qbd[...] = jnp.zeros(qbd.shape, F32)
            for h in range(DIFF_HEADS):
                q = q_ref[0, :, h * LANES:(h + 1) * LANES].astype(F32)
                qbd[(2 * h) * nq:(2 * h + 1) * nq, h * LANES:(h + 1) * LANES] = jnp.where(lo8, q, 0.0)
                qbd[(2 * h + 1) * nq:(2 * h + 2) * nq, h * LANES:(h + 1) * LANES] = jnp.where(lo8, 0.0, q)
        else:
            for h in range(NSA_HEADS):
                g = h // NSA_REP
                q = q_ref[0, :, h * LANES:(h + 1) * LANES].astype(F32)
                qbd[h * nq:(h + 1) * nq, :] = q if g == 0 else pltpu.roll(q, 64, 1)
                if mode == "sel":
                    bias[h * nq:(h + 1) * nq, :] = selb_ref[0, :, g * 256:(g + 1) * 256]
        kn = newk_ref[0]
        if mode == "diff":
            kk, vv = kn, newv_ref[0]
        else:
            kk, vv = kn[:, 0:LANES], kn[:, LANES:2 * LANES]
        kpad = jnp.concatenate([kk, jnp.zeros((LANES - nq, kk.shape[1]), F32)], axis=0).astype(BF)
        vpad = jnp.concatenate([vv, jnp.zeros((LANES - nq, vv.shape[1]), F32)], axis=0).astype(BF)
        s = _dot_nt(qbd[...].astype(BF), kpad)
        if mode == "sel":
            jnew = past_len // SEL_BLOCK
            s = s + bias[:, jnew:jnew + 1]
        s = jnp.where((col < nq) & (col <= t_row), s, NEG)
        if mode == "diff":
            update(s, lambda p: per_head(p, lambda h: vpad[:, h * LANES:(h + 1) * LANES]))
        else:
            update(s, lambda p: _dot(p, vpad))

    qb = qbd[...].astype(BF)
    nkeys = n_pages * kw
    if mode == "diff":
        kt = jnp.concatenate([kpages[i][0].reshape(4 * LANES, LANES).astype(BF) for i in range(n_pages)], axis=1)
    else:
        kt = jnp.concatenate([kpages[i][0, 0].reshape(LANES, kw).astype(BF) for i in range(n_pages)], axis=1)
    s = _dot(qb, kt)
    if mode == "sel":
        krow = lax.broadcasted_iota(I32, (nkeys, 256), 0)
        jbi = lax.broadcasted_iota(I32, (nkeys, 256), 1)
        ep = jnp.where(jbi == 2 * n_pages * st + jnp.right_shift(krow, 6), 1.0, 0.0).astype(BF)
        s = s + _dot_nt(bias[...].astype(BF), ep)
    if mode == "win":
        t_w = lax.broadcasted_iota(I32, (rows, nkeys), 0) & (nq - 1)
        s = jnp.where(lax.broadcasted_iota(I32, (rows, nkeys), 1) > t_w, s, NEG)
    if mode == "diff":
        update(s, lambda p: per_head(p, lambda h: jnp.concatenate(
            [vpages[i][0, :, h, :].astype(BF) for i in range(n_pages)], axis=0)))
    else:
        vt = jnp.concatenate([kpages[i][0, 1].reshape(LANES, kw).astype(BF) for i in range(n_pages)], axis=1)
        update(s, lambda p: _dot_nt(p, vt))

    @pl.when(st == pl.num_programs(1) - 1)
    def _():
        on = acc[0] / l_s[0]
        if mode == "diff":
            lam = _diff_lambda(dl_ref[...], lambda_init)
            for h in range(DIFF_HEADS):
                a0 = on[h * 16:h * 16 + nq]
                a1 = on[h * 16 + nq:h * 16 + 2 * nq]
                o_ref[0, :, h * LANES:(h + 1) * LANES] = _diff_finish(a0, a1, lam, sub_ref[...], lambda_init)
        else:
            for c in range(4):
                a = on[(2 * c) * nq:(2 * c + 1) * nq]
                b = on[(2 * c + 1) * nq:(2 * c + 2) * nq]
                o_ref[0, :, c * LANES:(c + 1) * LANES] = _pair(lo8, a, b, c // 2)


def _paged_attend(mode, q3, selb3, pool_k, pool_v, page_table, newk3, newv3, dl, subln, n_pages, past_len, lambda_init):
    nbat, pages = page_table.shape
    steps = pages // n_pages
    wq = q3.shape[2]
    wk = newk3.shape[2]
    bmap = lambda b, s, pt: (b, 0, 0)
    c2 = lambda b, s, pt: (0, 0)

    def pmap(i, rank):
        return lambda b, s, pt: (pt[b, s * n_pages + i],) + (0,) * (rank - 1)

    in_specs = [pl.BlockSpec((1, 8, wq), bmap)]
    args = [q3]
    if mode == "sel":
        in_specs.append(pl.BlockSpec((1, 8, 512), bmap))
        args.append(selb3)
    kblock = (1,) + pool_k.shape[1:]
    in_specs += [pl.BlockSpec(kblock, pmap(i, len(kblock))) for i in range(n_pages)]
    args += [pool_k] * n_pages
    if mode == "diff":
        in_specs += [pl.BlockSpec((1, LANES, DIFF_HEADS, DIFF_DV), pmap(i, 4)) for i in range(n_pages)]
        args += [pool_v] * n_pages
    in_specs.append(pl.BlockSpec((1, 8, wk), bmap))
    args.append(newk3)
    if mode == "diff":
        in_specs += [pl.BlockSpec((1, 8, 512), bmap), pl.BlockSpec((4, HEAD_DIM), c2), pl.BlockSpec((1, DIFF_DV), c2)]
        args += [newv3, dl, subln]
    ck = 512 if mode == "diff" else LANES
    scratch = [pltpu.VMEM((64, ck), F32)]
    if mode == "sel":
        scratch.append(pltpu.VMEM((64, 256), F32))
    scratch += [pltpu.VMEM((1, 64, 1), F32), pltpu.VMEM((1, 64, 1), F32), pltpu.VMEM((1, 64, LANES), F32)]
    return pl.pallas_call(
        functools.partial(_paged_body, mode=mode, n_pages=n_pages, past_len=past_len, lambda_init=lambda_init),
        grid_spec=pltpu.PrefetchScalarGridSpec(
            num_scalar_prefetch=1, grid=(nbat, steps), in_specs=in_specs,
            out_specs=pl.BlockSpec((1, 8, 512), bmap), scratch_shapes=scratch),
        out_shape=jax.ShapeDtypeStruct((nbat, 8, 512), F32),
        compiler_params=_params(("parallel", "arbitrary")),
        name="paged_" + mode,
    )(page_table, *args)


def _merge_body(x_ref, oc_ref, os_ref, ow_ref, od_ref, gx_ref, nmix_ref, nffn_ref, wg_ref, wbn_ref, wbd_ref, wo_ref,
                wr_ref, br_ref, eexp_ref, x1_ref, h_ref, idx_ref, gate_ref):
    x = x_ref[...]
    tm = x.shape[0]
    xn = (x * lax.rsqrt(jnp.mean(x * x, axis=-1, keepdims=True) + RMS_EPS) * nmix_ref[...]).astype(BF)
    gx = gx_ref[...]
    if gx.shape[1] == LANES:
        hi = gx.astype(BF)
        r1 = gx - hi.astype(F32)
        mid = r1.astype(BF)
        lo = (r1 - mid.astype(F32)).astype(BF)
        gx = _dot(hi, eexp_ref[...]) + _dot(mid, eexp_ref[...]) + _dot(lo, eexp_ref[...])
    o_nsa = gx[:, 0:512] * oc_ref[...] + gx[:, 512:1024] * os_ref[...] + gx[:, 1024:1536] * ow_ref[...]
    y_nsa = _dot(o_nsa.astype(BF), wbn_ref[...])
    y_diff = _dot(od_ref[...].astype(BF), wbd_ref[...])
    gates = jax.nn.sigmoid(_dot(xn, wg_ref[...]))
    mrg = gates[:, 0:D_MODEL] * y_nsa + gates[:, D_MODEL:2 * D_MODEL] * y_diff
    x1 = x + _dot(mrg.astype(BF), wo_ref[...])
    x1_ref[...] = x1
    h = x1 * lax.rsqrt(jnp.mean(x1 * x1, axis=-1, keepdims=True) + RMS_EPS) * nffn_ref[...]
    h_ref[...] = h
    logits = _dot_hi(h, wr_ref[...]) + br_ref[...]
    lane = _lane_iota(tm)
    vals, idxs = [], []
    for _ in range(TOP_K):
        m = jnp.max(logits, axis=-1, keepdims=True)
        idx = jnp.min(jnp.where(logits == m, lane, LANES), axis=-1, keepdims=True)
        vals.append(m)
        idxs.append(idx)
        logits = jnp.where(lane == idx, -jnp.inf, logits)
    es = [jnp.exp(v - vals[0]) for v in vals]
    den = es[0] + es[1] + es[2] + es[3]
    idx_out = jnp.zeros((tm, LANES), I32)
    gate_out = jnp.zeros((tm, LANES), F32)
    for k in range(TOP_K):
        idx_out = jnp.where(lane == k, idxs[k], idx_out)
        gate_out = jnp.where(lane == k, es[k] / den, gate_out)
    idx_ref[...] = idx_out
    gate_ref[...] = gate_out


def _merge(x2d, oc, osel, ow, od, gx, nmix, nffn, wg, wbn, wbd, wo, wr, br, eexp, tm):
    n = x2d.shape[0]
    row = lambda i: (i, 0)
    const = lambda i: (0, 0)
    return pl.pallas_call(
        _merge_body,
        grid=(n // tm,),
        in_specs=[pl.BlockSpec((tm, D_MODEL), row)] + [pl.BlockSpec((tm, 512), row)] * 4 +
                 [pl.BlockSpec((tm, gx.shape[1]), row), pl.BlockSpec((1, D_MODEL), const),
                  pl.BlockSpec((1, D_MODEL), const),
                  pl.BlockSpec((D_MODEL, 2 * D_MODEL), const), pl.BlockSpec((512, D_MODEL), const),
                  pl.BlockSpec((512, D_MODEL), const), pl.BlockSpec((D_MODEL, D_MODEL), const),
                  pl.BlockSpec((D_MODEL, LANES), const), pl.BlockSpec((1, LANES), const),
                  pl.BlockSpec((LANES, 1536), const)],
        out_specs=[pl.BlockSpec((tm, D_MODEL), row), pl.BlockSpec((tm, D_MODEL), row),
                   pl.BlockSpec((tm, LANES), row), pl.BlockSpec((tm, LANES), row)],
        out_shape=[jax.ShapeDtypeStruct((n, D_MODEL), F32), jax.ShapeDtypeStruct((n, D_MODEL), F32),
                   jax.ShapeDtypeStruct((n, LANES), I32), jax.ShapeDtypeStruct((n, LANES), F32)],
        compiler_params=_params(("parallel",)),
        name="merge",
    )(x2d, oc, osel, ow, od, gx, nmix, nffn, wg, wbn, wbd, wo, wr, br, eexp)


def _route_body(idx_ref, dest_ref, meta_ref, cnt_s, base_s, *, tm, tmoe, nbp):
    ph = pl.program_id(0)
    i = pl.program_id(1)
    shift = int(math.log2(tmoe))
    idx_t = jnp.transpose(idx_ref[...])
    sub = lax.broadcasted_iota(I32, (LANES, tm), 0)
    ohs = [idx_t[k:k + 1, :] == sub for k in range(TOP_K)]
    tok = jnp.zeros((LANES, tm), F32)
    for oh in ohs:
        tok = tok + jnp.where(oh, 1.0, 0.0)
    tile_cnt = jnp.broadcast_to(jnp.sum(tok, axis=1, keepdims=True), (LANES, LANES))

    @pl.when((ph == 0) & (i == 0))
    def _():
        cnt_s[...] = jnp.zeros((LANES, LANES), F32)

    @pl.when(ph == 0)
    def _():
        cnt_s[...] = cnt_s[...] + tile_cnt

    @pl.when((ph == 1) & (i == 0))
    def _():
        cnt = cnt_s[...].astype(I32)
        padded = jnp.left_shift(jnp.right_shift(cnt + (tmoe - 1), shift), shift)
        er = lax.broadcasted_iota(I32, (LANES, LANES), 0)
        ec = lax.broadcasted_iota(I32, (LANES, LANES), 1)
        pad_start = _dot_hi(jnp.where(ec < er, 1.0, 0.0), padded.astype(F32))
        base_s[...] = pad_start
        pad_end = pad_start.astype(I32) + padded
        blk_start = lax.broadcasted_iota(I32, (LANES, nbp), 1) * tmoe
        ends = jnp.broadcast_to(pad_end[:, 0:1], (LANES, nbp))
        be = jnp.minimum(jnp.sum(jnp.where(ends <= blk_start, 1, 0), axis=0, keepdims=True), N_EXPERTS - 1)
        n_used = jnp.right_shift(jnp.max(pad_end, axis=0, keepdims=True)[:, 0:1], shift)
        last_blk = jnp.sum(jnp.where(er == ec, jnp.where(padded > 0, jnp.right_shift(pad_end, shift) - 1, -1), 0),
                           axis=0, keepdims=True)
        lb = jnp.concatenate([last_blk, jnp.full((1, nbp - LANES), -1, I32)], axis=1)
        row = lax.broadcasted_iota(I32, (8, nbp), 0)
        meta_ref[...] = jnp.where(row == 0, be, jnp.where(row == 1, n_used, jnp.where(row == 2, lb, 0)))

    @pl.when(ph == 1)
    def _():
        nr = lax.broadcasted_iota(I32, (tm, tm), 0)
        nc = lax.broadcasted_iota(I32, (tm, tm), 1)
        before = _dot(tok.astype(BF), jnp.where(nr < nc, 1.0, 0.0).astype(BF))
        pos = before + base_s[:, 0:1]
        rows = [jnp.sum(jnp.where(oh, pos, 0.0), axis=0, keepdims=True) for oh in ohs]
        dest_ref[...] = jnp.concatenate(rows + [jnp.zeros((8 - TOP_K, tm), F32)], axis=0).astype(I32)
        base_s[...] = base_s[...] + tile_cnt


def _route(idx_all, tm, tmoe, nbp):
    n_tok = idx_all.shape[0]
    return pl.pallas_call(
        functools.partial(_route_body, tm=tm, tmoe=tmoe, nbp=nbp),
        grid=(2, n_tok // tm),
        in_specs=[pl.BlockSpec((tm, LANES), lambda ph, i: (i, 0))],
        out_specs=[pl.BlockSpec((8, tm), lambda ph, i: (0, i * ph)), pl.BlockSpec((8, nbp), lambda ph, i: (0, 0))],
        out_shape=[jax.ShapeDtypeStruct((8, n_tok), I32), jax.ShapeDtypeStruct((8, nbp), I32)],
        scratch_shapes=[pltpu.VMEM((LANES, LANES), F32), pltpu.VMEM((LANES, LANES), F32)],
        compiler_params=_params(("arbitrary", "arbitrary")),
        name="route",
    )(idx_all)


def _dispatch_body(dest_ref, lb_ref, nu_ref, h_ref, xs_hbm, zbuf, zsem, sem, *, tm, tmoe, n_tok):
    i = pl.program_id(0)

    @pl.when(i == 0)
    def _():
        zbuf[...] = jnp.zeros(zbuf.shape, F32)

        def fill(e):
            return pltpu.make_async_copy(zbuf, xs_hbm.at[pl.ds(pl.multiple_of(lb_ref[e] * tmoe, tmoe), tmoe)], zsem)
        for e in range(N_EXPERTS):
            @pl.when(lb_ref[e] >= 0)
            def _():
                fill(e).start()
        for e in range(N_EXPERTS):
            @pl.when(lb_ref[e] >= 0)
            def _():
                fill(e).wait()

        def tail(b):
            return pltpu.make_async_copy(zbuf, xs_hbm.at[pl.ds(pl.multiple_of(b * tmoe, tmoe), tmoe)], zsem)
        n_blk = xs_hbm.shape[0] // tmoe
        lax.fori_loop(nu_ref[0], n_blk, lambda b, c: (tail(b).start(), c)[1], 0)
        lax.fori_loop(nu_ref[0], n_blk, lambda b, c: (tail(b).wait(), c)[1], 0)

    def body(n, c):
        for k in range(TOP_K):
            d = dest_ref[k * n_tok + i * tm + n]
            pltpu.make_async_copy(h_ref.at[pl.ds(n, 1)], xs_hbm.at[pl.ds(d, 1)], sem).start(priority=k % 2)
        return c
    lax.fori_loop(0, tm, body, 0, unroll=8)
    pltpu.make_async_copy(xs_hbm.at[pl.ds(0, TOP_K * tm)], xs_hbm.at[pl.ds(0, TOP_K * tm)], sem).wait()


def _dispatch(dest_flat, last_blk, n_used, h_all, n_rows, tm, tmoe):
    n_tok = h_all.shape[0]
    return pl.pallas_call(
        functools.partial(_dispatch_body, tm=tm, tmoe=tmoe, n_tok=n_tok),
        grid_spec=pltpu.PrefetchScalarGridSpec(
            num_scalar_prefetch=3, grid=(n_tok // tm,),
            in_specs=[pl.BlockSpec((tm, D_MODEL), lambda i, d, lb, nu: (i, 0))],
            out_specs=pl.BlockSpec(memory_space=pl.ANY),
            scratch_shapes=[pltpu.VMEM((tmoe, D_MODEL), F32), pltpu.SemaphoreType.DMA(()), pltpu.SemaphoreType.DMA(())]),
        out_shape=jax.ShapeDtypeStruct((n_rows, D_MODEL), F32),
        compiler_params=pltpu.CompilerParams(dimension_semantics=("arbitrary",), vmem_limit_bytes=VMEM_LIMIT,
                                             disable_bounds_checks=True),
        name="moe_dispatch",
    )(dest_flat, last_blk, n_used, h_all)


def _moe_body(be_ref, nused_ref, x_ref, wgu_ref, bgu_ref, wd_ref, bd_ref, o_ref, wgu_s, wd_s):
    i = pl.program_id(0)
    e = be_ref[i]
    prev = be_ref[jnp.maximum(i - 1, 0)]

    @pl.when((i == 0) | (e != prev))
    def _():
        def cast_rows(c, carry):
            r0 = pl.multiple_of(c * LANES, LANES)
            wgu_s[pl.ds(r0, LANES), :] = wgu_ref[0, pl.ds(r0, LANES), :].astype(BF)
            wd_s[pl.ds(r0, LANES), :] = wd_ref[0, pl.ds(r0, LANES), :].astype(BF)
            return carry
        lax.fori_loop(0, D_MODEL // LANES, cast_rows, 0)

    @pl.when(i < nused_ref[0])
    def _():
        gu = _dot(x_ref[...].astype(BF), wgu_s[...]) + bgu_ref[0]
        gate = jnp.minimum(gu[:, 0:D_FF], SWIGLU_LIMIT)
        up = jnp.clip(gu[:, D_FF:2 * D_FF], -SWIGLU_LIMIT, SWIGLU_LIMIT)
        act = (up + 1.0) * gate * jax.nn.sigmoid(SWIGLU_ALPHA * gate)
        o_ref[...] = _dot(act.astype(BF), wd_s[...]) + bd_ref[0]

    @pl.when(i >= nused_ref[0])
    def _():
        o_ref[...] = jnp.zeros(o_ref.shape, F32)


def _moe(block_expert, n_used, xs, wgu, bgu, wd, bd, tmoe):
    n_rows = xs.shape[0]
    nblk = n_rows // tmoe
    xmap = lambda i, be, nu: (jnp.minimum(i, jnp.maximum(nu[0] - 1, 0)), 0)
    emap = lambda i, be, nu: (be[i], 0, 0)
    return pl.pallas_call(
        _moe_body,
        grid_spec=pltpu.PrefetchScalarGridSpec(
            num_scalar_prefetch=2, grid=(nblk,),
            in_specs=[pl.BlockSpec((tmoe, D_MODEL), xmap),
                      pl.BlockSpec((1, D_MODEL, 2 * D_FF), emap), pl.BlockSpec((1, 1, 2 * D_FF), emap),
                      pl.BlockSpec((1, D_FF, D_MODEL), emap), pl.BlockSpec((1, 1, D_MODEL), emap)],
            out_specs=pl.BlockSpec((tmoe, D_MODEL), lambda i, be, nu: (i, 0)),
            scratch_shapes=[pltpu.VMEM((D_MODEL, 2 * D_FF), BF), pltpu.VMEM((D_FF, D_MODEL), BF)]),
        out_shape=jax.ShapeDtypeStruct((n_rows, D_MODEL), F32),
        compiler_params=_params(("arbitrary",)),
        name="moe_experts",
    )(block_expert, n_used, xs, wgu, bgu, wd, bd)


def _final_body(dest_ref, x1_ref, gate_ref, nf_ref, ys_hbm, y_ref, ybuf, sem, *, tm, n_tok, block0):
    i = pl.program_id(0)

    def gather(tile, sl):
        def body(n, c):
            for k in range(TOP_K):
                d = dest_ref[k * n_tok + (block0 + tile) * tm + n]
                pltpu.make_async_copy(ys_hbm.at[pl.ds(d, 1)], ybuf.at[sl, k, pl.ds(n, 1)],
                                      sem.at[sl]).start(priority=k % 2)
            return c
        lax.fori_loop(0, tm, body, 0, unroll=8)

    @pl.when(i == 0)
    def _():
        gather(0, 0)

    @pl.when(i + 1 < pl.num_programs(0))
    def _():
        gather(i + 1, (i + 1) % 2)

    slot = i % 2
    pltpu.make_async_copy(ybuf.at[slot], ybuf.at[slot], sem.at[slot]).wait()
    gate = gate_ref[...]
    x = x1_ref[...]
    for k in range(TOP_K):
        x = x + gate[:, k:k + 1] * ybuf[slot, k]
    y_ref[...] = x * lax.rsqrt(jnp.mean(x * x, axis=-1, keepdims=True) + RMS_EPS) * nf_ref[...]


def _final(dest_flat, x1, gates, nf, ys, tm, n_tok, block0):
    n = x1.shape[0]
    row = lambda i, d: (i, 0)
    return pl.pallas_call(
        functools.partial(_final_body, tm=tm, n_tok=n_tok, block0=block0),
        grid_spec=pltpu.PrefetchScalarGridSpec(
            num_scalar_prefetch=1, grid=(n // tm,),
            in_specs=[pl.BlockSpec((tm, D_MODEL), row), pl.BlockSpec((tm, LANES), row),
                      pl.BlockSpec((1, D_MODEL), lambda i, d: (0, 0)), pl.BlockSpec(memory_space=pl.ANY)],
            out_specs=pl.BlockSpec((tm, D_MODEL), row),
            scratch_shapes=[pltpu.VMEM((2, TOP_K, tm, D_MODEL), F32), pltpu.SemaphoreType.DMA((2,))]),
        out_shape=jax.ShapeDtypeStruct((n, D_MODEL), F32),
        compiler_params=pltpu.CompilerParams(dimension_semantics=("arbitrary",), vmem_limit_bytes=VMEM_LIMIT,
                                             disable_bounds_checks=True),
        name="final_norm",
    )(dest_flat, x1, gates, nf, ys)


def _rope_tables(pos, nb_lanes):
    half = ROT_DIM // 2
    inv = ROPE_THETA ** (-jnp.arange(half, dtype=F32) / half)
    ang = pos.astype(F32)[:, None] * inv[None, :]
    cos, sin = jnp.cos(ang), jnp.sin(ang)
    t = pos.shape[0]
    c64 = jnp.concatenate([cos, cos, jnp.ones((t, HEAD_DIM - ROT_DIM), F32)], axis=1)
    s64 = jnp.concatenate([-sin, sin, jnp.zeros((t, HEAD_DIM - ROT_DIM), F32)], axis=1)
    lane = jnp.arange(LANES)[None, :]
    e = ((lane >= 64) & (lane < 64 + nb_lanes) & ((pos // SEL_BLOCK)[:, None] == lane - 64)).astype(F32)
    return jnp.tile(c64, (1, 2)), jnp.tile(s64, (1, 2)), e


def _imp_matrix(rt, n_cmp, nb, nbp):
    m = np.zeros((rt, nbp), np.float32)
    sub = SEL_BLOCK // CMP_STRIDE
    for n in range(n_cmp):
        for shift in range(CMP_BLOCK // CMP_STRIDE):
            j = (n + shift) // sub
            if j < nb:
                m[n, j] += 1.0
    return jnp.asarray(m)


def kernel(x_prompt, x_sample, cache_cmp_kv, cache_sel_kv, state_win_kv, cache_diff_k, cache_diff_v, page_table, norm_mix, w_in, cmp_pe, w_cmp1, w_cmp2, diff_lambda, diff_subln, w_br_nsa, w_br_diff, w_out, norm_ffn, w_router, b_router, w_gate_up, b_gate_up, w_down, b_down, norm_final):
    bp, t, _ = x_prompt.shape
    bs, ts, _ = x_sample.shape
    assert ts == 8 and t % 256 == 0 and t <= 2048 and w_in.shape[0] == 1
    n_pool, page = cache_cmp_kv.shape[1], cache_cmp_kv.shape[2]
    pages = page_table.shape[1]
    past_len = pages * page
    assert page == LANES and past_len % SEL_BLOCK == 0
    lambda_init = 0.8 - 0.6 * math.exp(-0.3 * 0)
    n_p, n_s = bp * t, bs * ts

    w = w_in[0]
    gcols = w[:, 1280:1304].reshape(D_MODEL, NSA_HEADS, 3).transpose(0, 2, 1)
    gexp = jnp.broadcast_to(gcols[..., None], (D_MODEL, 3, NSA_HEADS, HEAD_DIM)).reshape(D_MODEL, 1536)
    w_proj = jnp.concatenate([w[:, 0:1280], w[:, 1304:2840], gexp], axis=1).astype(BF)
    w_gates = w[:, 2840:4888].astype(BF)
    nmix = norm_mix[0][None, :]
    nffn = norm_ffn[0][None, :]
    pe = cmp_pe[0]
    pea = pe[:, :CMP_STRIDE].reshape(2, 1, 1024)
    peb = pe[:, CMP_STRIDE:].reshape(2, 1, 1024)
    w1a = w_cmp1[0][:, :1024].astype(BF)
    w1b = w_cmp1[0][:, 1024:].astype(BF)
    w2p = jnp.pad(w_cmp2[0], ((0, 0), (0, 0), (0, LANES - HEAD_DIM))).astype(BF)
    wbn = w_br_nsa[0].astype(BF)
    wbd = w_br_diff[0].astype(BF)
    wo = w_out[0].astype(BF)
    wr = jnp.pad(w_router[0], ((0, 0), (0, LANES - N_EXPERTS)))
    br = jnp.concatenate([b_router[0], jnp.full((LANES - N_EXPERTS,), NEG, F32)])[None, :]
    dl = diff_lambda[0]
    subln = diff_subln[0][None, :]

    pos_p = jnp.arange(t, dtype=I32)
    pos_s = jnp.tile(past_len + jnp.arange(ts, dtype=I32), bs)
    tm = 256
    gpad = jnp.pad(w[:, 1280:1304], ((0, 0), (0, LANES - 3 * NSA_HEADS)))
    w_rows = jnp.concatenate([w[:, 0:768], w[:, 1304:1816], w[:, 2328:2840], gpad, w[:, 768:896], w[:, 1024:1152],
                              w[:, 1816:2328]], axis=1).astype(BF)
    eexp_np = np.zeros((LANES, 1536), np.float32)
    for hh in range(NSA_HEADS):
        for brn in range(3):
            eexp_np[hh * 3 + brn, brn * 512 + hh * HEAD_DIM:brn * 512 + (hh + 1) * HEAD_DIM] = 1.0
    eexp = jnp.asarray(eexp_np).astype(BF)
    w_cols = jnp.concatenate([w[:, 512:1280], w[:, 1816:2840]], axis=1).T.astype(BF)
    cos_p, sin_p, _ = _rope_tables(pos_p, 0)
    outs_p = _inproj_t(x_prompt.reshape(n_p, D_MODEL), nmix, w_rows, w_cols, cos_p, sin_p,
                       cos_p[:, 0:8].T, -sin_p[:, 0:8].T, bp, t, tm)
    outs_s = _inproj(x_sample.reshape(n_s, D_MODEL), nmix, w_proj, *_rope_tables(pos_s, 0), n_s)
    (qraw_p, qrot_p, kvc_p, kvct_p, kvst_p, kvwt_p, ksa_p, kwa_p, vst_p, vwt_p, qd_p, kdt_p, kdb_p, vd4_p, vdtb_p,
     gx_p) = outs_p
    (qraw_s, qrot_s, kvc_s, kvs_s, kvw_s, _, _, _, _, qd_s, kd_s, _, vd_s, _, gx_s) = outs_s

    rt_p = t // CMP_STRIDE
    n_cmp_p = (t - CMP_BLOCK) // CMP_STRIDE + 1
    nb_p = t // SEL_BLOCK
    kvcmp_p = _compress(kvc_p.reshape(bp, rt_p, 4096), jnp.arange(bp, dtype=I32)[:, None], pea, peb, w1a, w1b, w2p, 1)
    tq = 128
    nq = t // tq
    ocmp_p, selb_p = _cmp_attend(qraw_p.reshape(bp * nq, tq, 1024), kvcmp_p, _imp_matrix(rt_p, n_cmp_p, nb_p, LANES),
                                 nq, nb_p, LANES, 0)
    tq_d = 256
    qrot3 = qrot_p.reshape(n_p // tq_d, tq_d, 1024)
    osel_p = _nsa_flash("sel", qrot3, selb_p.reshape(n_p // tq_d, tq_d, 2 * LANES), ksa_p, vst_p, bp, t, tq_d, 512)
    owin_p = _nsa_flash("win", qrot3, None, kwa_p, vwt_p, bp, t, tq_d, 256)
    odiff_p = _diff_flash(qd_p.reshape(n_p // tq_d, tq_d, 512), kdb_p, vdtb_p, dl, subln, bp, t, tq_d, 512, lambda_init)

    n_cmp_s = (past_len + ts - CMP_BLOCK) // CMP_STRIDE + 1
    rt_s = past_len // CMP_STRIDE
    assert n_cmp_s <= rt_s
    nb_s = (past_len + ts + SEL_BLOCK - 1) // SEL_BLOCK
    assert nb_s <= 256
    ppg = 16 if pages % 16 == 0 else (8 if pages % 8 == 0 else 1)
    kvcmp_s = _compress(jnp.transpose(cache_cmp_kv[0], (0, 2, 3, 4, 1)), page_table, pea, peb, w1a, w1b, w2p, ppg)
    ocmp_s, selb_s = _cmp_attend(qraw_s.reshape(bs, ts, 1024), kvcmp_s, _imp_matrix(rt_s, n_cmp_s, nb_s, 256),
                                 1, nb_s, 256, past_len)
    qrot_s3 = qrot_s.reshape(bs, ts, 1024)
    sel_t = jnp.transpose(cache_sel_kv[0], (0, 2, 3, 4, 1))
    win_t = jnp.transpose(state_win_kv[0], (0, 2, 3, 4, 1))
    dk_t = jnp.transpose(cache_diff_k[0], (0, 2, 3, 4, 1))
    osel_s = _paged_attend("sel", qrot_s3, selb_s, sel_t, None, page_table,
                           kvs_s.reshape(bs, ts, 256), None, None, None, ppg, past_len, lambda_init)
    wbuf = state_win_kv.shape[2]
    assert wbuf == WINDOW
    owin_s = _paged_attend("win", qrot_s3, None, win_t, None, jnp.arange(bs, dtype=I32)[:, None],
                           kvw_s.reshape(bs, ts, 256), None, None, None, 1, past_len, lambda_init)
    odiff_s = _paged_attend("diff", qd_s.reshape(bs, ts, 512), None, dk_t, cache_diff_v[0], page_table,
                            kd_s.reshape(bs, ts, 512), vd_s.reshape(bs, ts, 512), dl, subln, ppg, past_len, lambda_init)

    x1_p, h_p, idx_p, gate_p = _merge(x_prompt.reshape(n_p, D_MODEL), ocmp_p.reshape(n_p, 512), osel_p.reshape(n_p, 512),
                                      owin_p.reshape(n_p, 512), odiff_p.reshape(n_p, 512), gx_p, nmix, nffn, w_gates,
                                      wbn, wbd, wo, wr, br, eexp, tm)
    x1_s, h_s, idx_s, gate_s = _merge(x_sample.reshape(n_s, D_MODEL), ocmp_s.reshape(n_s, 512), osel_s.reshape(n_s, 512),
                                      owin_s.reshape(n_s, 512), odiff_s.reshape(n_s, 512), gx_s, nmix, nffn, w_gates,
                                      wbn, wbd, wo, wr, br, eexp, n_s)

    n_tok = n_p + n_s
    n_assign = n_tok * TOP_K
    tmoe = 512
    assert n_p % n_s == 0 and n_tok % n_s == 0
    n_blocks = (n_assign + N_EXPERTS * (tmoe - 1) + tmoe - 1) // tmoe
    nbp = (n_blocks + LANES - 1) // LANES * LANES
    dest, meta = _route(jnp.concatenate([idx_p, idx_s], axis=0), n_s, tmoe, nbp)
    dest_flat = dest.reshape(-1)
    xs = _dispatch(dest_flat, meta[2, :N_EXPERTS], meta[1, :1], jnp.concatenate([h_p, h_s], axis=0), n_blocks * tmoe,
                   n_s, tmoe)
    ys = _moe(meta[0, :n_blocks], meta[1, :1], xs, w_gate_up[0], b_gate_up[0][:, None, :], w_down[0],
              b_down[0][:, None, :], tmoe)

    nf = norm_final[None, :]
    y_p = _final(dest_flat, x1_p, gate_p, nf, ys, n_s, n_tok, 0)
    y_s = _final(dest_flat, x1_s, gate_s, nf, ys, n_s, n_tok, n_p // n_s)

    g, hd = NSA_GROUPS, HEAD_DIM
    def from_t(a, heads, length):
        return jnp.transpose(a.reshape(bp, heads, 2, hd, length), (0, 4, 1, 2, 3))[None]

    new_win_p = from_t(kvwt_p[:, :, t - wbuf:], 2, wbuf)
    new_win_s = jnp.concatenate([state_win_kv[0].reshape(bs, wbuf, 256)[:, ts:], kvw_s.reshape(bs, ts, 256)], axis=1)
    return (y_p.reshape(bp, t, D_MODEL), y_s.reshape(bs, ts, D_MODEL),
            from_t(kvct_p, 2, t), kvc_s.reshape(1, bs, ts, 2, g, hd),
            from_t(kvst_p, 2, t), kvs_s.reshape(1, bs, ts, 2, g, hd),
            new_win_p, new_win_s.reshape(1, bs, wbuf, 2, g, hd),
            from_t(kdt_p, DIFF_HEADS, t), kd_s.reshape(1, bs, ts, DIFF_HEADS, 2, hd),
            vd4_p[None], vd_s.reshape(1, bs, ts, DIFF_HEADS, DIFF_DV))
```

```python
import functools
import math

import numpy as np
import jax
import jax.numpy as jnp
from jax import lax
from jax.experimental import pallas as pl
from jax.experimental.pallas import tpu as pltpu

D_MODEL = 1024
HEAD_DIM = 64
NSA_HEADS = 8
NSA_GROUPS = 2
NSA_REP = 4
CMP_BLOCK = 32
CMP_STRIDE = 16
SEL_BLOCK = 64
N_SEL = 16
WINDOW = 512
DIFF_HEADS = 4
DIFF_DV = 128
ROT_DIM = 16
ROPE_THETA = 500000.0
N_EXPERTS = 32
TOP_K = 4
D_FF = 1024
SWIGLU_LIMIT = 7.0
SWIGLU_ALPHA = 1.702
RMS_EPS = 1e-5
FORCE_BONUS = 1e4
SCALE = HEAD_DIM ** -0.5
LANES = 128

BF = jnp.bfloat16
F32 = jnp.float32
I32 = jnp.int32
NEG = -1e30
VMEM_LIMIT = 56 * 1024 * 1024

OFF_Q, OFF_KVC, OFF_KVS, OFF_KVW, OFF_QD, OFF_KD, OFF_VD, OFF_GX = 0, 512, 768, 1024, 1280, 1792, 2304, 2816
IN_COLS_K = 4352
TOFF_Q, TOFF_KVC, TOFF_QD, TOFF_VD, TOFF_GX, TOFF_KS, TOFF_KW, TOFF_KD = 0, 512, 768, 1280, 1792, 1920, 2048, 2176
TIN_COLS = 2688
TIN_ROWS = 1792


def _dot(a, b):
    return jnp.dot(a, b, preferred_element_type=F32)


def _dot_nt(a, b):
    return lax.dot_general(a, b, (((1,), (1,)), ((), ())), preferred_element_type=F32)


def _dot_hi(a, b):
    return jnp.dot(a, b, preferred_element_type=F32, precision=lax.Precision.HIGHEST)


def _params(sem):
    return pltpu.CompilerParams(dimension_semantics=sem, vmem_limit_bytes=VMEM_LIMIT)


def _lane_iota(rows):
    return lax.broadcasted_iota(I32, (rows, LANES), 1)


def _pair(lo, a, b, g):
    if g == 0:
        return jnp.where(lo, a, pltpu.roll(b, 64, 1))
    return jnp.where(lo, pltpu.roll(a, 64, 1), b)


def _inproj_body(x_ref, nw_ref, w_ref, cos_ref, sin_ref, e_ref,
                 qraw_ref, qrot_ref, kvc_ref, kvs_ref, kvw_ref, ksa_ref, kwa_ref, vsb_ref, vwb_ref,
                 qd_ref, kd_ref, kdb_ref, vd_ref, vdb_ref, gx_ref):
    x = x_ref[...]
    tm = x.shape[0]
    xn = (x * lax.rsqrt(jnp.mean(x * x, axis=-1, keepdims=True) + RMS_EPS) * nw_ref[...]).astype(BF)
    cos = cos_ref[...]
    sin = sin_ref[...]
    epat = e_ref[...]
    lane = _lane_iota(tm)
    lo = lane < 64
    first8 = (lane & 63) < 8

    def mm(off):
        return _dot(xn, w_ref[:, off:off + LANES])

    def rope(y):
        sw = jnp.where(first8, pltpu.roll(y, LANES - 8, 1), pltpu.roll(y, 8, 1))
        return y * cos + sw * sin

    for c in range(4):
        y = mm(OFF_Q + c * LANES) * SCALE
        yr = rope(y)
        ys = pltpu.roll(y, 64, 1)
        yrs = pltpu.roll(yr, 64, 1)
        qraw_ref[:, (2 * c) * LANES:(2 * c + 1) * LANES] = jnp.where(lo, y, 0.0).astype(BF)
        qraw_ref[:, (2 * c + 1) * LANES:(2 * c + 2) * LANES] = jnp.where(lo, ys, 0.0).astype(BF)
        qrot_ref[:, (2 * c) * LANES:(2 * c + 1) * LANES] = jnp.where(lo, yr, 0.0).astype(BF)
        qrot_ref[:, (2 * c + 1) * LANES:(2 * c + 2) * LANES] = jnp.where(lo, yrs, 0.0).astype(BF)

    for c in range(2):
        kvc_ref[:, c * LANES:(c + 1) * LANES] = mm(OFF_KVC + c * LANES)

    for off, kv_ref, ka_ref, vb_ref in ((OFF_KVS, kvs_ref, ksa_ref, vsb_ref), (OFF_KVW, kvw_ref, kwa_ref, vwb_ref)):
        kr = rope(mm(off))
        v = mm(off + LANES)
        kv_ref[:, 0:LANES] = kr
        kv_ref[:, LANES:2 * LANES] = v
        ka_ref[:, 0:LANES] = jnp.where(lo, kr, epat).astype(BF)
        ka_ref[:, LANES:2 * LANES] = jnp.where(lo, pltpu.roll(kr, 64, 1), epat).astype(BF)
        vb_ref[...] = v.astype(BF)

    for c in range(4):
        sl = slice(c * LANES, (c + 1) * LANES)
        qd_ref[:, sl] = rope(mm(OFF_QD + c * LANES) * SCALE).astype(BF)
        kr = rope(mm(OFF_KD + c * LANES))
        kd_ref[:, sl] = kr
        kdb_ref[:, sl] = kr.astype(BF)
        v = mm(OFF_VD + c * LANES)
        vd_ref[:, sl] = v
        vdb_ref[:, sl] = v.astype(BF)

    for c in range(12):
        gx_ref[:, c * LANES:(c + 1) * LANES] = jax.nn.sigmoid(mm(OFF_GX + c * LANES))


def _inproj(x2d, norm_w, w_bf, cos_t, sin_t, e_t, tm):
    n = x2d.shape[0]
    nt = cos_t.shape[0] // tm
    row = lambda i: (i, 0)
    tab = lambda i: (i % nt, 0)
    const = lambda i: (0, 0)
    outs = [(1024, BF), (1024, BF), (256, F32), (256, F32), (256, F32), (256, BF), (256, BF), (128, BF), (128, BF),
            (512, BF), (512, F32), (512, BF), (512, F32), (512, BF), (1536, F32)]
    return pl.pallas_call(
        _inproj_body,
        grid=(n // tm,),
        in_specs=[pl.BlockSpec((tm, D_MODEL), row), pl.BlockSpec((1, D_MODEL), const),
                  pl.BlockSpec((D_MODEL, IN_COLS_K), const),
                  pl.BlockSpec((tm, LANES), tab), pl.BlockSpec((tm, LANES), tab), pl.BlockSpec((tm, LANES), tab)],
        out_specs=[pl.BlockSpec((tm, w), row) for w, _ in outs],
        out_shape=[jax.ShapeDtypeStruct((n, w), dt) for w, dt in outs],
        compiler_params=_params(("parallel",)),
        name="inproj",
    )(x2d, norm_w, w_bf, cos_t, sin_t, e_t)


def _inproj_t_body(x_ref, nw_ref, w_ref, wt_ref, cos_ref, sin_ref, cost_ref, sint_ref,
                   qraw_ref, qrot_ref, kvc_ref, kvct_ref, kvst_ref, kvwt_ref, ksa_ref, kwa_ref, vst_ref, vwt_ref,
                   qd_ref, kdt_ref, kdb_ref, vd4_ref, vdtb_ref, gx_ref):
    x = x_ref[...]
    tm = x.shape[0]
    xn = (x * lax.rsqrt(jnp.mean(x * x, axis=-1, keepdims=True) + RMS_EPS) * nw_ref[...]).astype(BF)
    cos = cos_ref[...]
    sin = sin_ref[...]
    cos_t = cost_ref[...]
    sin_t = sint_ref[...]
    lane = _lane_iota(tm)
    lo = lane < 64
    first8 = (lane & 63) < 8

    def mm(off):
        return _dot(xn, w_ref[:, off:off + LANES])

    def mm_t(off, n):
        return _dot_nt(wt_ref[off:off + n, :], xn)

    def rope(y):
        sw = jnp.where(first8, pltpu.roll(y, LANES - 8, 1), pltpu.roll(y, 8, 1))
        return y * cos + sw * sin

    def rope_t(y):
        parts = []
        for hb in range(0, y.shape[0], HEAD_DIM):
            x1 = y[hb:hb + 8]
            x2 = y[hb + 8:hb + 16]
            parts += [x1 * cos_t - x2 * sin_t, x1 * sin_t + x2 * cos_t, y[hb + 16:hb + HEAD_DIM]]
        return jnp.concatenate(parts, axis=0)

    for c in range(4):
        y = mm(TOFF_Q + c * LANES) * SCALE
        yr = rope(y)
        ys = pltpu.roll(y, 64, 1)
        yrs = pltpu.roll(yr, 64, 1)
        qraw_ref[:, (2 * c) * LANES:(2 * c + 1) * LANES] = jnp.where(lo, y, 0.0).astype(BF)
        qraw_ref[:, (2 * c + 1) * LANES:(2 * c + 2) * LANES] = jnp.where(lo, ys, 0.0).astype(BF)
        qrot_ref[:, (2 * c) * LANES:(2 * c + 1) * LANES] = jnp.where(lo, yr, 0.0).astype(BF)
        qrot_ref[:, (2 * c + 1) * LANES:(2 * c + 2) * LANES] = jnp.where(lo, yrs, 0.0).astype(BF)

    for c in range(2):
        kvc_ref[:, c * LANES:(c + 1) * LANES] = mm(TOFF_KVC + c * LANES)
    kvct_ref[0] = mm_t(0, 256)

    blk = jnp.right_shift(pl.program_id(1) * tm + lax.broadcasted_iota(I32, (tm, LANES), 0), 6)
    epat = jnp.where((lane >= 64) & (lane < 96) & (blk == lane - 64), 1.0, 0.0)
    for off, koff, kvt_ref, ka_ref, vt_ref in ((256, TOFF_KS, kvst_ref, ksa_ref, vst_ref),
                                               (512, TOFF_KW, kvwt_ref, kwa_ref, vwt_ref)):
        y = mm_t(off, 256)
        v = y[128:256]
        kvt_ref[0, 0:128, :] = rope_t(y[0:128])
        kvt_ref[0, 128:256, :] = v
        vt_ref[0] = v.astype(BF)
        kr = rope(mm(koff))
        ka_ref[:, 0:LANES] = jnp.where(lo, kr, epat).astype(BF)
        ka_ref[:, LANES:2 * LANES] = jnp.where(lo, pltpu.roll(kr, 64, 1), epat).astype(BF)

    kdt_ref[0] = rope_t(mm_t(768, 512))
    vdtb_ref[0] = mm_t(1280, 512).astype(BF)
    for c in range(4):
        sl = slice(c * LANES, (c + 1) * LANES)
        qd_ref[:, sl] = rope(mm(TOFF_QD + c * LANES) * SCALE).astype(BF)
        kdb_ref[:, sl] = rope(mm(TOFF_KD + c * LANES)).astype(BF)
        vd4_ref[0, :, c, :] = mm(TOFF_VD + c * LANES)

    gx_ref[...] = jax.nn.sigmoid(mm(TOFF_GX))


def _inproj_t(x2d, norm_w, w_bf, wt_bf, cos_t, sin_t, cos_tt, sin_tt, nbat, t, tm):
    n = x2d.shape[0]
    nt = t // tm
    row = lambda b, i: (b * nt + i, 0)
    tab = lambda b, i: (i, 0)
    tab_t = lambda b, i: (0, i)
    const = lambda b, i: (0, 0)
    tr = lambda b, i: (b, 0, i)
    rm = lambda w, dt: (pl.BlockSpec((tm, w), row), jax.ShapeDtypeStruct((n, w), dt))
    tp = lambda r, dt: (pl.BlockSpec((1, r, tm), tr), jax.ShapeDtypeStruct((nbat, r, t), dt))
    vd4 = (pl.BlockSpec((1, tm, DIFF_HEADS, DIFF_DV), lambda b, i: (b, i, 0, 0)),
           jax.ShapeDtypeStruct((nbat, t, DIFF_HEADS, DIFF_DV), F32))
    outs = [rm(1024, BF), rm(1024, BF), rm(256, F32), tp(256, F32), tp(256, F32), tp(256, F32), rm(256, BF),
            rm(256, BF), tp(128, BF), tp(128, BF), rm(512, BF), tp(512, F32), rm(512, BF), vd4, tp(512, BF),
            rm(LANES, F32)]
    return pl.pallas_call(
        _inproj_t_body,
        grid=(nbat, nt),
        in_specs=[pl.BlockSpec((tm, D_MODEL), row), pl.BlockSpec((1, D_MODEL), const),
                  pl.BlockSpec((D_MODEL, TIN_COLS), const), pl.BlockSpec((TIN_ROWS, D_MODEL), const),
                  pl.BlockSpec((tm, LANES), tab), pl.BlockSpec((tm, LANES), tab),
                  pl.BlockSpec((8, tm), tab_t), pl.BlockSpec((8, tm), tab_t)],
        out_specs=[o[0] for o in outs],
        out_shape=[o[1] for o in outs],
        compiler_params=_params(("parallel", "parallel")),
        name="inproj_t",
    )(x2d, norm_w, w_bf, wt_bf, cos_t, sin_t, cos_tt, sin_tt)


def _compress_body(pt_ref, *refs, n_pages, rp, rt, native):
    page_refs = refs[:n_pages]
    pea_ref, peb_ref, w1a_ref, w1b_ref, w2_ref, out_ref, y_scr = refs[n_pages:n_pages + 7]
    s = pl.program_id(1)
    for i in range(n_pages):
        row0 = pl.multiple_of((s * n_pages + i) * rp, 8)
        if native:
            xbuf = refs[n_pages + 7]
            lo = _lane_iota(rp) < 64
            for c in range(2):
                xbuf[c] = jnp.transpose(page_refs[i][0, c].reshape(LANES, LANES))
            for c in range(2):
                pieces = [xbuf[c, pl.ds(l, rp, stride=CMP_STRIDE), :] for l in range(CMP_STRIDE)]
                for g in range(2):
                    y_scr[c * 2 + g, pl.ds(row0, rp), :] = jnp.concatenate(
                        [_pair(lo, pieces[2 * j], pieces[2 * j + 1], g) for j in range(CMP_STRIDE // 2)], axis=1)
            continue
        x = page_refs[i][0]
        for cg in range(4):
            ycg = jnp.concatenate(
                [x[:, l * 256 + cg * 64:l * 256 + cg * 64 + 64] for l in range(CMP_STRIDE)], axis=1)
            y_scr[cg, pl.ds(row0, rp), :] = ycg

    @pl.when(s == pl.num_programs(1) - 1)
    def _():
        for c in range(2):
            for g in range(2):
                cg = c * 2 + g
                y = y_scr[cg]
                za = _dot((y + pea_ref[c]).astype(BF), w1a_ref[c])
                zb = _dot((y + peb_ref[c]).astype(BF), w1b_ref[c])
                hid = jax.nn.gelu(za + pltpu.roll(zb, rt - 1, 0))
                out_ref[0, :, cg * LANES:(cg + 1) * LANES] = _dot(hid.astype(BF), w2_ref[c]).astype(BF)


def _compress(pool, page_table, pea, peb, w1a, w1b, w2p, n_pages):
    nb, pages = page_table.shape
    native = pool.ndim == 5
    rp = LANES // CMP_STRIDE if native else pool.shape[1]
    rt = pages * rp
    steps = pages // n_pages
    pblock = (1,) + pool.shape[1:]
    page_specs = [pl.BlockSpec(pblock, functools.partial(
        lambda b, s, pt, i: (pt[b, s * n_pages + i],) + (0,) * (len(pblock) - 1), i=i)) for i in range(n_pages)]
    c3 = lambda b, s, pt: (0, 0, 0)
    scratch = [pltpu.VMEM((4, rt, 1024), F32)] + ([pltpu.VMEM((2, LANES, LANES), F32)] if native else [])
    return pl.pallas_call(
        functools.partial(_compress_body, n_pages=n_pages, rp=rp, rt=rt, native=native),
        grid_spec=pltpu.PrefetchScalarGridSpec(
            num_scalar_prefetch=1,
            grid=(nb, steps),
            in_specs=page_specs + [pl.BlockSpec((2, 1, 1024), c3), pl.BlockSpec((2, 1, 1024), c3),
                                   pl.BlockSpec((2, 1024, LANES), c3), pl.BlockSpec((2, 1024, LANES), c3),
                                   pl.BlockSpec((2, LANES, LANES), c3)],
            out_specs=pl.BlockSpec((1, rt, 512), lambda b, s, pt: (b, 0, 0)),
            scratch_shapes=scratch),
        out_shape=jax.ShapeDtypeStruct((nb, rt, 512), BF),
        compiler_params=_params(("parallel", "arbitrary")),
        name="compress",
    )(page_table, *([pool] * n_pages), pea, peb, w1a, w1b, w2p)


def _cmp_body(q_ref, kvc_ref, mimp_ref, ocmp_ref, selb_ref, *, tq, rt, nb, nbp, pos0):
    i = pl.program_id(1)
    pos_n = pos0 + i * tq + lax.broadcasted_iota(I32, (tq, rt), 0)
    n_i = lax.broadcasted_iota(I32, (tq, rt), 1)
    vis = (n_i * CMP_STRIDE + (CMP_BLOCK - 1)) <= pos_n
    lo = _lane_iota(tq) < 64
    pos_b = pos0 + i * tq + lax.broadcasted_iota(I32, (tq, nbp), 0)
    jb = lax.broadcasted_iota(I32, (tq, nbp), 1)
    cur = jnp.right_shift(pos_b, 6)
    valid = (jb * SEL_BLOCK <= pos_b) & (jb < nb)
    forced = (jb == 0) | (jb == cur) | (jb == cur - 1)
    mimp = mimp_ref[...]
    for g in range(NSA_GROUPS):
        kc = kvc_ref[0, :, g * LANES:(g + 1) * LANES]
        vc = kvc_ref[0, :, (2 + g) * LANES:(3 + g) * LANES]
        pg = jnp.zeros((tq, rt), F32)
        og = []
        for r in range(NSA_REP):
            h = g * NSA_REP + r
            q = q_ref[0, :, h * LANES:(h + 1) * LANES]
            s = jnp.where(vis, _dot_nt(q, kc), -jnp.inf)
            m = jnp.max(s, axis=-1, keepdims=True)
            m = jnp.where(m == -jnp.inf, 0.0, m)
            e = jnp.exp(s - m)
            d = jnp.sum(e, axis=-1, keepdims=True)
            p = e / jnp.where(d > 0, d, 1.0)
            og.append(_dot(p.astype(BF), vc))
            pg = pg + p
        imp = _dot_hi(pg, mimp)
        score = jnp.where(valid, jnp.where(forced, imp + FORCE_BONUS, imp), -jnp.inf)
        if tq == nbp and nb % 8 == 0:
            st = jnp.transpose(score)[0:nb]
            jrow = lax.broadcasted_iota(I32, (nb, tq), 0)
            rank_t = jnp.zeros((nb, tq), F32)
            for jj in range(nb):
                cj = st[jj:jj + 1, :]
                rank_t = rank_t + jnp.where((cj > st) | ((cj == st) & (jrow > jj)), 1.0, 0.0)
            rank = jnp.transpose(jnp.concatenate([rank_t, jnp.full((nbp - nb, tq), float(nbp), F32)], axis=0))
            sel = (rank < float(N_SEL)) & valid
        else:
            rank = jnp.zeros((tq, nbp), I32)
            for jj in range(nb):
                cj = score[:, jj:jj + 1]
                beats = (cj > score) | ((cj == score) & (jb > jj))
                rank = rank + beats.astype(I32)
            sel = (rank < N_SEL) & valid
        selb_ref[0, :, g * nbp:(g + 1) * nbp] = jnp.where(sel, 0.0, NEG)
        for c2 in range(2):
            ocmp_ref[0, :, (g * 2 + c2) * LANES:(g * 2 + c2 + 1) * LANES] = _pair(lo, og[2 * c2], og[2 * c2 + 1], 0)


def _cmp_attend(qraw3, kvc, mimp, nq, nb, nbp, pos0):
    nbat = kvc.shape[0]
    _, tq, _ = qraw3.shape
    rt = kvc.shape[1]
    return pl.pallas_call(
        functools.partial(_cmp_body, tq=tq, rt=rt, nb=nb, nbp=nbp, pos0=pos0),
        grid=(nbat, nq),
        in_specs=[pl.BlockSpec((1, tq, 1024), lambda b, i: (b * nq + i, 0, 0)),
                  pl.BlockSpec((1, rt, 512), lambda b, i: (b, 0, 0)),
                  pl.BlockSpec((rt, nbp), lambda b, i: (0, 0))],
        out_specs=[pl.BlockSpec((1, tq, 512), lambda b, i: (b * nq + i, 0, 0)),
                   pl.BlockSpec((1, tq, 2 * nbp), lambda b, i: (b * nq + i, 0, 0))],
        out_shape=[jax.ShapeDtypeStruct((nbat * nq, tq, 512), F32),
                   jax.ShapeDtypeStruct((nbat * nq, tq, 2 * nbp), F32)],
        compiler_params=_params(("parallel", "parallel")),
        name="cmp_attend",
    )(qraw3, kvc, mimp)


def _online_t(s, vt, m_s, l_s, acc, idx):
    m_old = m_s[idx]
    m_new = jnp.maximum(m_old, jnp.max(s, axis=0, keepdims=True))
    alpha = jnp.exp(m_old - m_new)
    p = jnp.exp(s - m_new)
    l_s[idx] = alpha * l_s[idx] + jnp.sum(p, axis=0, keepdims=True)
    acc[idx] = alpha * acc[idx] + _dot(vt, p.astype(BF))
    m_s[idx] = m_new


STEP_FIRST, STEP_LAST, STEP_MASKED = 1, 2, 4


def _step_table(nq, tq, tk, window=None):
    qi, kt, fl = [], [], []
    for i in range(nq):
        first = 0 if window is None else max(0, (i * tq - window + 1) // tk)
        last = (i * tq + tq - 1) // tk
        for j in range(first, last + 1):
            masked = j * tk + tk - 1 > i * tq
            if window is not None:
                masked = masked or j * tk <= i * tq + tq - 1 - window
            qi.append(i)
            kt.append(j)
            fl.append((STEP_FIRST if j == first else 0) | (STEP_LAST if j == last else 0)
                      | (STEP_MASKED if masked else 0))
    return (jnp.asarray(np.array(qi, np.int32)), jnp.asarray(np.array(kt, np.int32)),
            jnp.asarray(np.array(fl, np.int32)))


def _nsa_flash_body(qi_ref, kt_ref, fl_ref, *refs, mode, tq, tk):
    if mode == "sel":
        q_ref, selb_ref, k_ref, v_ref, o_ref, qs, m_s, l_s, acc = refs
    else:
        q_ref, k_ref, v_ref, o_ref, qs, m_s, l_s, acc = refs
    st = pl.program_id(1)
    i = qi_ref[st]
    jt = kt_ref[st]
    flags = fl_ref[st]
    q0 = i * tq
    rows = NSA_REP * tq
    lane = _lane_iota(tq)
    lo = lane < 64

    @pl.when((flags & STEP_FIRST) != 0)
    def _():
        m_s[...] = jnp.full(m_s.shape, NEG, F32)
        l_s[...] = jnp.zeros(l_s.shape, F32)
        acc[...] = jnp.zeros(acc.shape, F32)
        for g in range(NSA_GROUPS):
            if mode == "sel":
                sb = pltpu.roll(selb_ref[0, :, g * LANES:(g + 1) * LANES], 64, 1)
                sb = jnp.where((lane >= 64) & (lane < 96), sb, 0.0)
            for r in range(NSA_REP):
                h = g * NSA_REP + r
                q = q_ref[0, :, h * LANES:(h + 1) * LANES]
                if mode == "sel":
                    q = (q.astype(F32) + sb).astype(BF)
                qs[g, r * tq:(r + 1) * tq, :] = q

    need_mask = (flags & STEP_MASKED) != 0

    def step(masked):
        for g in range(NSA_GROUPS):
            s = _dot_nt(k_ref[:, g * LANES:(g + 1) * LANES], qs[g])
            if masked:
                kpos = jt * tk + lax.broadcasted_iota(I32, (tk, rows), 0)
                rpos = q0 + (lax.broadcasted_iota(I32, (tk, rows), 1) & (tq - 1))
                vis = kpos <= rpos
                if mode == "win":
                    vis = vis & (kpos > rpos - WINDOW)
                s = jnp.where(vis, s, NEG)
            _online_t(s, v_ref[0, g * HEAD_DIM:(g + 1) * HEAD_DIM, :], m_s, l_s, acc, g)

    @pl.when(need_mask)
    def _():
        step(True)

    @pl.when(jnp.logical_not(need_mask))
    def _():
        step(False)

    @pl.when((flags & STEP_LAST) != 0)
    def _():
        for g in range(NSA_GROUPS):
            on = acc[g] / l_s[g]
            for c2 in range(2):
                pair = jnp.concatenate([on[:, (2 * c2) * tq:(2 * c2 + 1) * tq],
                                        on[:, (2 * c2 + 1) * tq:(2 * c2 + 2) * tq]], axis=0)
                o_ref[0, :, (g * 2 + c2) * LANES:(g * 2 + c2 + 1) * LANES] = jnp.concatenate(
                    [jnp.transpose(pair[:, c * LANES:(c + 1) * LANES]) for c in range(tq // LANES)], axis=0)


def _nsa_flash(mode, q3, selb3, k2, v2, nbat, t, tq, tk):
    nq = t // tq
    rows = NSA_REP * tq
    table = _step_table(nq, tq, tk, None if mode == "sel" else WINDOW)
    qmap = lambda b, s, qi, kt, fl: (b * nq + qi[s], 0, 0)
    in_specs = [pl.BlockSpec((1, tq, 1024), qmap)]
    args = [q3]
    if mode == "sel":
        in_specs.append(pl.BlockSpec((1, tq, 2 * LANES), qmap))
        args.append(selb3)
    nkv = t // tk
    in_specs += [pl.BlockSpec((tk, 2 * LANES), lambda b, s, qi, kt, fl: (b * nkv + kt[s], 0)),
                 pl.BlockSpec((1, LANES, tk), lambda b, s, qi, kt, fl: (b, 0, kt[s]))]
    args += [k2, v2]
    return pl.pallas_call(
        functools.partial(_nsa_flash_body, mode=mode, tq=tq, tk=tk),
        grid_spec=pltpu.PrefetchScalarGridSpec(
            num_scalar_prefetch=3, grid=(nbat, int(table[0].shape[0])), in_specs=in_specs,
            out_specs=pl.BlockSpec((1, tq, 512), qmap),
            scratch_shapes=[pltpu.VMEM((NSA_GROUPS, rows, LANES), BF), pltpu.VMEM((NSA_GROUPS, 1, rows), F32),
                            pltpu.VMEM((NSA_GROUPS, 1, rows), F32), pltpu.VMEM((NSA_GROUPS, HEAD_DIM, rows), F32)]),
        out_shape=jax.ShapeDtypeStruct((nbat * nq, tq, 512), F32),
        compiler_params=_params(("parallel", "arbitrary")),
        name="nsa_flash_" + mode,
    )(*table, *args)


def _diff_lambda(dl, lambda_init):
    a = jnp.sum(dl[0:1] * dl[1:2], axis=1, keepdims=True)
    b = jnp.sum(dl[2:3] * dl[3:4], axis=1, keepdims=True)
    return jnp.exp(a) - jnp.exp(b) + lambda_init


def _diff_finish(a0, a1, lam, subln, lambda_init):
    o = a0 - lam * a1
    o = o * lax.rsqrt(jnp.mean(o * o, axis=-1, keepdims=True) + RMS_EPS) * subln
    return o * (1.0 - lambda_init)


def _diff_flash_body(qi_ref, kt_ref, fl_ref, q_ref, k_ref, v_ref, dl_ref, sub_ref, o_ref, qs, m_s, l_s, acc, *,
                     tq, tk, lambda_init):
    st = pl.program_id(1)
    i = qi_ref[st]
    j = kt_ref[st]
    flags = fl_ref[st]
    q0 = i * tq
    rows = 2 * tq
    lo = _lane_iota(tq) < 64

    @pl.when((flags & STEP_FIRST) != 0)
    def _():
        m_s[...] = jnp.full(m_s.shape, NEG, F32)
        l_s[...] = jnp.zeros(l_s.shape, F32)
        acc[...] = jnp.zeros(acc.shape, F32)
        for h in range(DIFF_HEADS):
            q = q_ref[0, :, h * LANES:(h + 1) * LANES].astype(F32)
            qs[h, 0:tq, :] = jnp.where(lo, q, 0.0).astype(BF)
            qs[h, tq:2 * tq, :] = jnp.where(lo, 0.0, q).astype(BF)

    need_mask = (flags & STEP_MASKED) != 0

    def step(masked):
        for h in range(DIFF_HEADS):
            s = _dot_nt(k_ref[:, h * LANES:(h + 1) * LANES], qs[h])
            if masked:
                kpos = j * tk + lax.broadcasted_iota(I32, (tk, rows), 0)
                rpos = q0 + (lax.broadcasted_iota(I32, (tk, rows), 1) & (tq - 1))
                s = jnp.where(kpos <= rpos, s, NEG)
            _online_t(s, v_ref[0, h * LANES:(h + 1) * LANES, :], m_s, l_s, acc, h)

    @pl.when(need_mask)
    def _():
        step(True)

    @pl.when(jnp.logical_not(need_mask))
    def _():
        step(False)

    @pl.when((flags & STEP_LAST) != 0)
    def _():
        lam = _diff_lambda(dl_ref[...], lambda_init)
        for h in range(DIFF_HEADS):
            on = acc[h] / l_s[h]
            a0, a1 = [jnp.concatenate([jnp.transpose(on[:, c0 + c * LANES:c0 + (c + 1) * LANES])
                                       for c in range(tq // LANES)], axis=0) for c0 in (0, tq)]
            o_ref[0, :, h * LANES:(h + 1) * LANES] = _diff_finish(a0, a1, lam, sub_ref[...], lambda_init)


def _diff_flash(q3, k2, v2, dl, subln, nbat, t, tq, tk, lambda_init):
    nq = t // tq
    nk = t // tk
    table = _step_table(nq, tq, tk)
    qmap = lambda b, s, qi, kt, fl: (b * nq + qi[s], 0, 0)
    kmap = lambda b, s, qi, kt, fl: (b * nk + kt[s], 0)
    vmap = lambda b, s, qi, kt, fl: (b, 0, kt[s])
    const = lambda b, s, qi, kt, fl: (0, 0)
    rows = 2 * tq
    return pl.pallas_call(
        functools.partial(_diff_flash_body, tq=tq, tk=tk, lambda_init=lambda_init),
        grid_spec=pltpu.PrefetchScalarGridSpec(
            num_scalar_prefetch=3, grid=(nbat, int(table[0].shape[0])),
            in_specs=[pl.BlockSpec((1, tq, 512), qmap), pl.BlockSpec((tk, 512), kmap),
                      pl.BlockSpec((1, 512, tk), vmap), pl.BlockSpec((4, HEAD_DIM), const),
                      pl.BlockSpec((1, DIFF_DV), const)],
            out_specs=pl.BlockSpec((1, tq, 512), qmap),
            scratch_shapes=[pltpu.VMEM((DIFF_HEADS, rows, LANES), BF), pltpu.VMEM((DIFF_HEADS, 1, rows), F32),
                            pltpu.VMEM((DIFF_HEADS, 1, rows), F32), pltpu.VMEM((DIFF_HEADS, DIFF_DV, rows), F32)]),
        out_shape=jax.ShapeDtypeStruct((nbat * nq, tq, 512), F32),
        compiler_params=_params(("parallel", "arbitrary")),
        name="diff_flash",
    )(*table, q3, k2, v2, dl, subln)


def _paged_body(pt_ref, *refs, mode, n_pages, past_len, lambda_init):
    q_ref = refs[0]
    k = 1
    if mode == "sel":
        selb_ref = refs[k]
        k += 1
    kpages = refs[k:k + n_pages]
    k += n_pages
    if mode == "diff":
        vpages = refs[k:k + n_pages]
        k += n_pages
    newk_ref = refs[k]
    k += 1
    if mode == "diff":
        newv_ref, dl_ref, sub_ref = refs[k:k + 3]
        k += 3
    o_ref = refs[k]
    k += 1
    if mode == "sel":
        qbd, bias, m_s, l_s, acc = refs[k:]
    else:
        qbd, m_s, l_s, acc = refs[k:]
    kw = WINDOW if mode == "win" else LANES
    st = pl.program_id(1)
    nq = 8
    rows = 64
    lane8 = _lane_iota(nq)
    lo8 = lane8 < 64
    t_row = lax.broadcasted_iota(I32, (rows, LANES), 0) & (nq - 1)
    col = lax.broadcasted_iota(I32, (rows, LANES), 1)

    def update(s, pv):
        m_old = m_s[0]
        m_new = jnp.maximum(m_old, jnp.max(s, axis=-1, keepdims=True))
        alpha = jnp.exp(m_old - m_new)
        p = jnp.exp(s - m_new)
        l_s[0] = alpha * l_s[0] + jnp.sum(p, axis=-1, keepdims=True)
        acc[0] = alpha * acc[0] + pv(p.astype(BF))
        m_s[0] = m_new

    def per_head(p, v_of_head):
        return jnp.concatenate([_dot(p[h * 16:(h + 1) * 16], v_of_head(h)) for h in range(DIFF_HEADS)], axis=0)

    @pl.when(st == 0)
    def _():
        m_s[0] = jnp.full((rows, 1), NEG, F32)
        l_s[0] = jnp.zeros((rows, 1), F32)
        acc[0] = jnp.zeros(acc.shape[1:], F32)
        if mode == "diff":
            qbd[...] = jnp.zeros(qbd.shape, F32)
            for h in range(DIFF_HEADS):
                q = q_ref[0, :, h * LANES:(h + 1) * LANES].astype(F32)
                qbd[(2 * h) * nq:(2 * h + 1) * nq, h * LANES:(h + 1) * LANES] = jnp.where(lo8, q, 0.0)
                qbd[(2 * h + 1) * nq:(2 * h + 2) * nq, h * LANES:(h + 1) * LANES] = jnp.where(lo8, 0.0, q)
        else:
            for h in range(NSA_HEADS):
                g = h // NSA_REP
                q = q_ref[0, :, h * LANES:(h + 1) * LANES].astype(F32)
                qbd[h * nq:(h + 1) * nq, :] = q if g == 0 else pltpu.roll(q, 64, 1)
                if mode == "sel":
                    bias[h * nq:(h + 1) * nq, :] = selb_ref[0, :, g * 256:(g + 1) * 256]
        kn = newk_ref[0]
        if mode == "diff":
            kk, vv = kn, newv_ref[0]
        else:
            kk, vv = kn[:, 0:LANES], kn[:, LANES:2 * LANES]
        kpad = jnp.concatenate([kk, jnp.zeros((LANES - nq, kk.shape[1]), F32)], axis=0).astype(BF)
        vpad = jnp.concatenate([vv, jnp.zeros((LANES - nq, vv.shape[1]), F32)], axis=0).astype(BF)
        s = _dot_nt(qbd[...].astype(BF), kpad)
        if mode == "sel":
            jnew = past_len // SEL_BLOCK
            s = s + bias[:, jnew:jnew + 1]
        s = jnp.where((col < nq) & (col <= t_row), s, NEG)
        if mode == "diff":
            update(s, lambda p: per_head(p, lambda h: vpad[:, h * LANES:(h + 1) * LANES]))
        else:
            update(s, lambda p: _dot(p, vpad))

    qb = qbd[...].astype(BF)
    nkeys = n_pages * kw
    if mode == "diff":
        kt = jnp.concatenate([kpages[i][0].reshape(4 * LANES, LANES).astype(BF) for i in range(n_pages)], axis=1)
    else:
        kt = jnp.concatenate([kpages[i][0, 0].reshape(LANES, kw).astype(BF) for i in range(n_pages)], axis=1)
    s = _dot(qb, kt)
    if mode == "sel":
        krow = lax.broadcasted_iota(I32, (nkeys, 256), 0)
        jbi = lax.broadcasted_iota(I32, (nkeys, 256), 1)
        ep = jnp.where(jbi == 2 * n_pages * st + jnp.right_shift(krow, 6), 1.0, 0.0).astype(BF)
        s = s + _dot_nt(bias[...].astype(BF), ep)
    if mode == "win":
        t_w = lax.broadcasted_iota(I32, (rows, nkeys), 0) & (nq - 1)
        s = jnp.where(lax.broadcasted_iota(I32, (rows, nkeys), 1) > t_w, s, NEG)
    if mode == "diff":
        update(s, lambda p: per_head(p, lambda h: jnp.concatenate(
            [vpages[i][0, :, h, :].astype(BF) for i in range(n_pages)], axis=0)))
    else:
        vt = jnp.concatenate([kpages[i][0, 1].reshape(LANES, kw).astype(BF) for i in range(n_pages)], axis=1)
        update(s, lambda p: _dot_nt(p, vt))

    @pl.when(st == pl.num_programs(1) - 1)
    def _():
        on = acc[0] / l_s[0]
        if mode == "diff":
            lam = _diff_lambda(dl_ref[...], lambda_init)
            for h in range(DIFF_HEADS):
                a0 = on[h * 16:h * 16 + nq]
                a1 = on[h * 16 + nq:h * 16 + 2 * nq]
                o_ref[0, :, h * LANES:(h + 1) * LANES] = _diff_finish(a0, a1, lam, sub_ref[...], lambda_init)
        else:
            for c in range(4):
                a = on[(2 * c) * nq:(2 * c + 1) * nq]
                b = on[(2 * c + 1) * nq:(2 * c + 2) * nq]
                o_ref[0, :, c * LANES:(c + 1) * LANES] = _pair(lo8, a, b, c // 2)


def _paged_attend(mode, q3, selb3, pool_k, pool_v, page_table, newk3, newv3, dl, subln, n_pages, past_len, lambda_init):
    nbat, pages = page_table.shape
    steps = pages // n_pages
    wq = q3.shape[2]
    wk = newk3.shape[2]
    bmap = lambda b, s, pt: (b, 0, 0)
    c2 = lambda b, s, pt: (0, 0)

    def pmap(i, rank):
        return lambda b, s, pt: (pt[b, s * n_pages + i],) + (0,) * (rank - 1)

    in_specs = [pl.BlockSpec((1, 8, wq), bmap)]
    args = [q3]
    if mode == "sel":
        in_specs.append(pl.BlockSpec((1, 8, 512), bmap))
        args.append(selb3)
    kblock = (1,) + pool_k.shape[1:]
    in_specs += [pl.BlockSpec(kblock, pmap(i, len(kblock))) for i in range(n_pages)]
    args += [pool_k] * n_pages
    if mode == "diff":
        in_specs += [pl.BlockSpec((1, LANES, DIFF_HEADS, DIFF_DV), pmap(i, 4)) for i in range(n_pages)]
        args += [pool_v] * n_pages
    in_specs.append(pl.BlockSpec((1, 8, wk), bmap))
    args.append(newk3)
    if mode == "diff":
        in_specs += [pl.BlockSpec((1, 8, 512), bmap), pl.BlockSpec((4, HEAD_DIM), c2), pl.BlockSpec((1, DIFF_DV), c2)]
        args += [newv3, dl, subln]
    ck = 512 if mode == "diff" else LANES
    scratch = [pltpu.VMEM((64, ck), F32)]
    if mode == "sel":
        scratch.append(pltpu.VMEM((64, 256), F32))
    scratch += [pltpu.VMEM((1, 64, 1), F32), pltpu.VMEM((1, 64, 1), F32), pltpu.VMEM((1, 64, LANES), F32)]
    return pl.pallas_call(
        functools.partial(_paged_body, mode=mode, n_pages=n_pages, past_len=past_len, lambda_init=lambda_init),
        grid_spec=pltpu.PrefetchScalarGridSpec(
            num_scalar_prefetch=1, grid=(nbat, steps), in_specs=in_specs,
            out_specs=pl.BlockSpec((1, 8, 512), bmap), scratch_shapes=scratch),
        out_shape=jax.ShapeDtypeStruct((nbat, 8, 512), F32),
        compiler_params=_params(("parallel", "arbitrary")),
        name="paged_" + mode,
    )(page_table, *args)


def _merge_body(x_ref, oc_ref, os_ref, ow_ref, od_ref, gx_ref, nmix_ref, nffn_ref, wg_ref, wbn_ref, wbd_ref, wo_ref,
                wr_ref, br_ref, eexp_ref, x1_ref, h_ref, idx_ref, gate_ref):
    x = x_ref[...]
    tm = x.shape[0]
    xn = (x * lax.rsqrt(jnp.mean(x * x, axis=-1, keepdims=True) + RMS_EPS) * nmix_ref[...]).astype(BF)
    gx = gx_ref[...]
    if gx.shape[1] == LANES:
        hi = gx.astype(BF)
        r1 = gx - hi.astype(F32)
        mid = r1.astype(BF)
        lo = (r1 - mid.astype(F32)).astype(BF)
        gx = _dot(hi, eexp_ref[...]) + _dot(mid, eexp_ref[...]) + _dot(lo, eexp_ref[...])
    o_nsa = gx[:, 0:512] * oc_ref[...] + gx[:, 512:1024] * os_ref[...] + gx[:, 1024:1536] * ow_ref[...]
    y_nsa = _dot(o_nsa.astype(BF), wbn_ref[...])
    y_diff = _dot(od_ref[...].astype(BF), wbd_ref[...])
    gates = jax.nn.sigmoid(_dot(xn, wg_ref[...]))
    mrg = gates[:, 0:D_MODEL] * y_nsa + gates[:, D_MODEL:2 * D_MODEL] * y_diff
    x1 = x + _dot(mrg.astype(BF), wo_ref[...])
    x1_ref[...] = x1
    h = x1 * lax.rsqrt(jnp.mean(x1 * x1, axis=-1, keepdims=True) + RMS_EPS) * nffn_ref[...]
    h_ref[...] = h
    logits = _dot_hi(h, wr_ref[...]) + br_ref[...]
    lane = _lane_iota(tm)
    vals, idxs = [], []
    for _ in range(TOP_K):
        m = jnp.max(logits, axis=-1, keepdims=True)
        idx = jnp.min(jnp.where(logits == m, lane, LANES), axis=-1, keepdims=True)
        vals.append(m)
        idxs.append(idx)
        logits = jnp.where(lane == idx, -jnp.inf, logits)
    es = [jnp.exp(v - vals[0]) for v in vals]
    den = es[0] + es[1] + es[2] + es[3]
    idx_out = jnp.zeros((tm, LANES), I32)
    gate_out = jnp.zeros((tm, LANES), F32)
    for k in range(TOP_K):
        idx_out = jnp.where(lane == k, idxs[k], idx_out)
        gate_out = jnp.where(lane == k, es[k] / den, gate_out)
    idx_ref[...] = idx_out
    gate_ref[...] = gate_out


def _merge(x2d, oc, osel, ow, od, gx, nmix, nffn, wg, wbn, wbd, wo, wr, br, eexp, tm):
    n = x2d.shape[0]
    row = lambda i: (i, 0)
    const = lambda i: (0, 0)
    return pl.pallas_call(
        _merge_body,
        grid=(n // tm,),
        in_specs=[pl.BlockSpec((tm, D_MODEL), row)] + [pl.BlockSpec((tm, 512), row)] * 4 +
                 [pl.BlockSpec((tm, gx.shape[1]), row), pl.BlockSpec((1, D_MODEL), const),
                  pl.BlockSpec((1, D_MODEL), const),
                  pl.BlockSpec((D_MODEL, 2 * D_MODEL), const), pl.BlockSpec((512, D_MODEL), const),
                  pl.BlockSpec((512, D_MODEL), const), pl.BlockSpec((D_MODEL, D_MODEL), const),
                  pl.BlockSpec((D_MODEL, LANES), const), pl.BlockSpec((1, LANES), const),
                  pl.BlockSpec((LANES, 1536), const)],
        out_specs=[pl.BlockSpec((tm, D_MODEL), row), pl.BlockSpec((tm, D_MODEL), row),
                   pl.BlockSpec((tm, LANES), row), pl.BlockSpec((tm, LANES), row)],
        out_shape=[jax.ShapeDtypeStruct((n, D_MODEL), F32), jax.ShapeDtypeStruct((n, D_MODEL), F32),
                   jax.ShapeDtypeStruct((n, LANES), I32), jax.ShapeDtypeStruct((n, LANES), F32)],
        compiler_params=_params(("parallel",)),
        name="merge",
    )(x2d, oc, osel, ow, od, gx, nmix, nffn, wg, wbn, wbd, wo, wr, br, eexp)


def _route_body(idx_ref, dest_ref, meta_ref, cnt_s, base_s, *, tm, tmoe, nbp):
    ph = pl.program_id(0)
    i = pl.program_id(1)
    shift = int(math.log2(tmoe))
    idx_t = jnp.transpose(idx_ref[...])
    sub = lax.broadcasted_iota(I32, (LANES, tm), 0)
    ohs = [idx_t[k:k + 1, :] == sub for k in range(TOP_K)]
    tok = jnp.zeros((LANES, tm), F32)
    for oh in ohs:
        tok = tok + jnp.where(oh, 1.0, 0.0)
    tile_cnt = jnp.broadcast_to(jnp.sum(tok, axis=1, keepdims=True), (LANES, LANES))

    @pl.when((ph == 0) & (i == 0))
    def _():
        cnt_s[...] = jnp.zeros((LANES, LANES), F32)

    @pl.when(ph == 0)
    def _():
        cnt_s[...] = cnt_s[...] + tile_cnt

    @pl.when((ph == 1) & (i == 0))
    def _():
        cnt = cnt_s[...].astype(I32)
        padded = jnp.left_shift(jnp.right_shift(cnt + (tmoe - 1), shift), shift)
        er = lax.broadcasted_iota(I32, (LANES, LANES), 0)
        ec = lax.broadcasted_iota(I32, (LANES, LANES), 1)
        pad_start = _dot_hi(jnp.where(ec < er, 1.0, 0.0), padded.astype(F32))
        base_s[...] = pad_start
        pad_end = pad_start.astype(I32) + padded
        blk_start = lax.broadcasted_iota(I32, (LANES, nbp), 1) * tmoe
        ends = jnp.broadcast_to(pad_end[:, 0:1], (LANES, nbp))
        be = jnp.minimum(jnp.sum(jnp.where(ends <= blk_start, 1, 0), axis=0, keepdims=True), N_EXPERTS - 1)
        n_used = jnp.right_shift(jnp.max(pad_end, axis=0, keepdims=True)[:, 0:1], shift)
        last_blk = jnp.sum(jnp.where(er == ec, jnp.where(padded > 0, jnp.right_shift(pad_end, shift) - 1, -1), 0),
                           axis=0, keepdims=True)
        lb = jnp.concatenate([last_blk, jnp.full((1, nbp - LANES), -1, I32)], axis=1)
        row = lax.broadcasted_iota(I32, (8, nbp), 0)
        meta_ref[...] = jnp.where(row == 0, be, jnp.where(row == 1, n_used, jnp.where(row == 2, lb, 0)))

    @pl.when(ph == 1)
    def _():
        nr = lax.broadcasted_iota(I32, (tm, tm), 0)
        nc = lax.broadcasted_iota(I32, (tm, tm), 1)
        before = _dot(tok.astype(BF), jnp.where(nr < nc, 1.0, 0.0).astype(BF))
        pos = before + base_s[:, 0:1]
        rows = [jnp.sum(jnp.where(oh, pos, 0.0), axis=0, keepdims=True) for oh in ohs]
        dest_ref[...] = jnp.concatenate(rows + [jnp.zeros((8 - TOP_K, tm), F32)], axis=0).astype(I32)
        base_s[...] = base_s[...] + tile_cnt


def _route(idx_all, tm, tmoe, nbp):
    n_tok = idx_all.shape[0]
    return pl.pallas_call(
        functools.partial(_route_body, tm=tm, tmoe=tmoe, nbp=nbp),
        grid=(2, n_tok // tm),
        in_specs=[pl.BlockSpec((tm, LANES), lambda ph, i: (i, 0))],
        out_specs=[pl.BlockSpec((8, tm), lambda ph, i: (0, i * ph)), pl.BlockSpec((8, nbp), lambda ph, i: (0, 0))],
        out_shape=[jax.ShapeDtypeStruct((8, n_tok), I32), jax.ShapeDtypeStruct((8, nbp), I32)],
        scratch_shapes=[pltpu.VMEM((LANES, LANES), F32), pltpu.VMEM((LANES, LANES), F32)],
        compiler_params=_params(("arbitrary", "arbitrary")),
        name="route",
    )(idx_all)


def _dispatch_body(dest_ref, lb_ref, nu_ref, h_ref, xs_hbm, zbuf, zsem, sem, *, tm, tmoe, n_tok):
    i = pl.program_id(0)

    @pl.when(i == 0)
    def _():
        zbuf[...] = jnp.zeros(zbuf.shape, F32)

        def fill(e):
            return pltpu.make_async_copy(zbuf, xs_hbm.at[pl.ds(pl.multiple_of(lb_ref[e] * tmoe, tmoe), tmoe)], zsem)
        for e in range(N_EXPERTS):
            @pl.when(lb_ref[e] >= 0)
            def _():
                fill(e).start()
        for e in range(N_EXPERTS):
            @pl.when(lb_ref[e] >= 0)
            def _():
                fill(e).wait()

        def tail(b):
            return pltpu.make_async_copy(zbuf, xs_hbm.at[pl.ds(pl.multiple_of(b * tmoe, tmoe), tmoe)], zsem)
        n_blk = xs_hbm.shape[0] // tmoe
        lax.fori_loop(nu_ref[0], n_blk, lambda b, c: (tail(b).start(), c)[1], 0)
        lax.fori_loop(nu_ref[0], n_blk, lambda b, c: (tail(b).wait(), c)[1], 0)

    def body(n, c):
        for k in range(TOP_K):
            d = dest_ref[k * n_tok + i * tm + n]
            pltpu.make_async_copy(h_ref.at[pl.ds(n, 1)], xs_hbm.at[pl.ds(d, 1)], sem).start(priority=k % 2)
        return c
    lax.fori_loop(0, tm, body, 0, unroll=8)
    pltpu.make_async_copy(xs_hbm.at[pl.ds(0, TOP_K * tm)], xs_hbm.at[pl.ds(0, TOP_K * tm)], sem).wait()


def _dispatch(dest_flat, last_blk, n_used, h_all, n_rows, tm, tmoe):
    n_tok = h_all.shape[0]
    return pl.pallas_call(
        functools.partial(_dispatch_body, tm=tm, tmoe=tmoe, n_tok=n_tok),
        grid_spec=pltpu.PrefetchScalarGridSpec(
            num_scalar_prefetch=3, grid=(n_tok // tm,),
            in_specs=[pl.BlockSpec((tm, D_MODEL), lambda i, d, lb, nu: (i, 0))],
            out_specs=pl.BlockSpec(memory_space=pl.ANY),
            scratch_shapes=[pltpu.VMEM((tmoe, D_MODEL), F32), pltpu.SemaphoreType.DMA(()), pltpu.SemaphoreType.DMA(())]),
        out_shape=jax.ShapeDtypeStruct((n_rows, D_MODEL), F32),
        compiler_params=pltpu.CompilerParams(dimension_semantics=("arbitrary",), vmem_limit_bytes=VMEM_LIMIT,
                                             disable_bounds_checks=True),
        name="moe_dispatch",
    )(dest_flat, last_blk, n_used, h_all)


def _moe_body(be_ref, nused_ref, x_ref, wgu_ref, bgu_ref, wd_ref, bd_ref, o_ref, wgu_s, wd_s):
    i = pl.program_id(0)
    e = be_ref[i]
    prev = be_ref[jnp.maximum(i - 1, 0)]

    @pl.when((i == 0) | (e != prev))
    def _():
        def cast_rows(c, carry):
            r0 = pl.multiple_of(c * LANES, LANES)
            wgu_s[pl.ds(r0, LANES), :] = wgu_ref[0, pl.ds(r0, LANES), :].astype(BF)
            wd_s[pl.ds(r0, LANES), :] = wd_ref[0, pl.ds(r0, LANES), :].astype(BF)
            return carry
        lax.fori_loop(0, D_MODEL // LANES, cast_rows, 0)

    @pl.when(i < nused_ref[0])
    def _():
        gu = _dot(x_ref[...].astype(BF), wgu_s[...]) + bgu_ref[0]
        gate = jnp.minimum(gu[:, 0:D_FF], SWIGLU_LIMIT)
        up = jnp.clip(gu[:, D_FF:2 * D_FF], -SWIGLU_LIMIT, SWIGLU_LIMIT)
        act = (up + 1.0) * gate * jax.nn.sigmoid(SWIGLU_ALPHA * gate)
        o_ref[...] = _dot(act.astype(BF), wd_s[...]) + bd_ref[0]

    @pl.when(i >= nused_ref[0])
    def _():
        o_ref[...] = jnp.zeros(o_ref.shape, F32)


def _moe(block_expert, n_used, xs, wgu, bgu, wd, bd, tmoe):
    n_rows = xs.shape[0]
    nblk = n_rows // tmoe
    xmap = lambda i, be, nu: (jnp.minimum(i, jnp.maximum(nu[0] - 1, 0)), 0)
    emap = lambda i, be, nu: (be[i], 0, 0)
    return pl.pallas_call(
        _moe_body,
        grid_spec=pltpu.PrefetchScalarGridSpec(
            num_scalar_prefetch=2, grid=(nblk,),
            in_specs=[pl.BlockSpec((tmoe, D_MODEL), xmap),
                      pl.BlockSpec((1, D_MODEL, 2 * D_FF), emap), pl.BlockSpec((1, 1, 2 * D_FF), emap),
                      pl.BlockSpec((1, D_FF, D_MODEL), emap), pl.BlockSpec((1, 1, D_MODEL), emap)],
            out_specs=pl.BlockSpec((tmoe, D_MODEL), lambda i, be, nu: (i, 0)),
            scratch_shapes=[pltpu.VMEM((D_MODEL, 2 * D_FF), BF), pltpu.VMEM((D_FF, D_MODEL), BF)]),
        out_shape=jax.ShapeDtypeStruct((n_rows, D_MODEL), F32),
        compiler_params=_params(("arbitrary",)),
        name="moe_experts",
    )(block_expert, n_used, xs, wgu, bgu, wd, bd)


def _final_body(dest_ref, x1_ref, gate_ref, nf_ref, ys_hbm, y_ref, ybuf, sem, *, tm, n_tok, block0):
    i = pl.program_id(0)

    def gather(tile, sl):
        def body(n, c):
            for k in range(TOP_K):
                d = dest_ref[k * n_tok + (block0 + tile) * tm + n]
                pltpu.make_async_copy(ys_hbm.at[pl.ds(d, 1)], ybuf.at[sl, k, pl.ds(n, 1)],
                                      sem.at[sl]).start(priority=k % 2)
            return c
        lax.fori_loop(0, tm, body, 0, unroll=8)

    @pl.when(i == 0)
    def _():
        gather(0, 0)

    @pl.when(i + 1 < pl.num_programs(0))
    def _():
        gather(i + 1, (i + 1) % 2)

    slot = i % 2
    pltpu.make_async_copy(ybuf.at[slot], ybuf.at[slot], sem.at[slot]).wait()
    gate = gate_ref[...]
    x = x1_ref[...]
    for k in range(TOP_K):
        x = x + gate[:, k:k + 1] * ybuf[slot, k]
    y_ref[...] = x * lax.rsqrt(jnp.mean(x * x, axis=-1, keepdims=True) + RMS_EPS) * nf_ref[...]


def _final(dest_flat, x1, gates, nf, ys, tm, n_tok, block0):
    n = x1.shape[0]
    row = lambda i, d: (i, 0)
    return pl.pallas_call(
        functools.partial(_final_body, tm=tm, n_tok=n_tok, block0=block0),
        grid_spec=pltpu.PrefetchScalarGridSpec(
            num_scalar_prefetch=1, grid=(n // tm,),
            in_specs=[pl.BlockSpec((tm, D_MODEL), row), pl.BlockSpec((tm, LANES), row),
                      pl.BlockSpec((1, D_MODEL), lambda i, d: (0, 0)), pl.BlockSpec(memory_space=pl.ANY)],
            out_specs=pl.BlockSpec((tm, D_MODEL), row),
            scratch_shapes=[pltpu.VMEM((2, TOP_K, tm, D_MODEL), F32), pltpu.SemaphoreType.DMA((2,))]),
        out_shape=jax.ShapeDtypeStruct((n, D_MODEL), F32),
        compiler_params=pltpu.CompilerParams(dimension_semantics=("arbitrary",), vmem_limit_bytes=VMEM_LIMIT,
                                             disable_bounds_checks=True),
        name="final_norm",
    )(dest_flat, x1, gates, nf, ys)


def _rope_tables(pos, nb_lanes):
    half = ROT_DIM // 2
    inv = ROPE_THETA ** (-jnp.arange(half, dtype=F32) / half)
    ang = pos.astype(F32)[:, None] * inv[None, :]
    cos, sin = jnp.cos(ang), jnp.sin(ang)
    t = pos.shape[0]
    c64 = jnp.concatenate([cos, cos, jnp.ones((t, HEAD_DIM - ROT_DIM), F32)], axis=1)
    s64 = jnp.concatenate([-sin, sin, jnp.zeros((t, HEAD_DIM - ROT_DIM), F32)], axis=1)
    lane = jnp.arange(LANES)[None, :]
    e = ((lane >= 64) & (lane < 64 + nb_lanes) & ((pos // SEL_BLOCK)[:, None] == lane - 64)).astype(F32)
    return jnp.tile(c64, (1, 2)), jnp.tile(s64, (1, 2)), e


def _imp_matrix(rt, n_cmp, nb, nbp):
    m = np.zeros((rt, nbp), np.float32)
    sub = SEL_BLOCK // CMP_STRIDE
    for n in range(n_cmp):
        for shift in range(CMP_BLOCK // CMP_STRIDE):
            j = (n + shift) // sub
            if j < nb:
                m[n, j] += 1.0
    return jnp.asarray(m)


def kernel(x_prompt, x_sample, cache_cmp_kv, cache_sel_kv, state_win_kv, cache_diff_k, cache_diff_v, page_table, norm_mix, w_in, cmp_pe, w_cmp1, w_cmp2, diff_lambda, diff_subln, w_br_nsa, w_br_diff, w_out, norm_ffn, w_router, b_router, w_gate_up, b_gate_up, w_down, b_down, norm_final):
    bp, t, _ = x_prompt.shape
    bs, ts, _ = x_sample.shape
    assert ts == 8 and t % 256 == 0 and t <= 2048 and w_in.shape[0] == 1
    n_pool, page = cache_cmp_kv.shape[1], cache_cmp_kv.shape[2]
    pages = page_table.shape[1]
    past_len = pages * page
    assert page == LANES and past_len % SEL_BLOCK == 0
    lambda_init = 0.8 - 0.6 * math.exp(-0.3 * 0)
    n_p, n_s = bp * t, bs * ts

    w = w_in[0]
    gcols = w[:, 1280:1304].reshape(D_MODEL, NSA_HEADS, 3).transpose(0, 2, 1)
    gexp = jnp.broadcast_to(gcols[..., None], (D_MODEL, 3, NSA_HEADS, HEAD_DIM)).reshape(D_MODEL, 1536)
    w_proj = jnp.concatenate([w[:, 0:1280], w[:, 1304:2840], gexp], axis=1).astype(BF)
    w_gates = w[:, 2840:4888].astype(BF)
    nmix = norm_mix[0][None, :]
    nffn = norm_ffn[0][None, :]
    pe = cmp_pe[0]
    pea = pe[:, :CMP_STRIDE].reshape(2, 1, 1024)
    peb = pe[:, CMP_STRIDE:].reshape(2, 1, 1024)
    w1a = w_cmp1[0][:, :1024].astype(BF)
    w1b = w_cmp1[0][:, 1024:].astype(BF)
    w2p = jnp.pad(w_cmp2[0], ((0, 0), (0, 0), (0, LANES - HEAD_DIM))).astype(BF)
    wbn = w_br_nsa[0].astype(BF)
    wbd = w_br_diff[0].astype(BF)
    wo = w_out[0].astype(BF)
    wr = jnp.pad(w_router[0], ((0, 0), (0, LANES - N_EXPERTS)))
    br = jnp.concatenate([b_router[0], jnp.full((LANES - N_EXPERTS,), NEG, F32)])[None, :]
    dl = diff_lambda[0]
    subln = diff_subln[0][None, :]

    pos_p = jnp.arange(t, dtype=I32)
    pos_s = jnp.tile(past_len + jnp.arange(ts, dtype=I32), bs)
    tm = 512
    gpad = jnp.pad(w[:, 1280:1304], ((0, 0), (0, LANES - 3 * NSA_HEADS)))
    w_rows = jnp.concatenate([w[:, 0:768], w[:, 1304:1816], w[:, 2328:2840], gpad, w[:, 768:896], w[:, 1024:1152],
                              w[:, 1816:2328]], axis=1).astype(BF)
    eexp_np = np.zeros((LANES, 1536), np.float32)
    for hh in range(NSA_HEADS):
        for brn in range(3):
            eexp_np[hh * 3 + brn, brn * 512 + hh * HEAD_DIM:brn * 512 + (hh + 1) * HEAD_DIM] = 1.0
    eexp = jnp.asarray(eexp_np).astype(BF)
    w_cols = jnp.concatenate([w[:, 512:1280], w[:, 1816:2840]], axis=1).T.astype(BF)
    cos_p, sin_p, _ = _rope_tables(pos_p, 0)
    outs_p = _inproj_t(x_prompt.reshape(n_p, D_MODEL), nmix, w_rows, w_cols, cos_p, sin_p,
                       cos_p[:, 0:8].T, -sin_p[:, 0:8].T, bp, t, tm)
    outs_s = _inproj(x_sample.reshape(n_s, D_MODEL), nmix, w_proj, *_rope_tables(pos_s, 0), n_s)
    (qraw_p, qrot_p, kvc_p, kvct_p, kvst_p, kvwt_p, ksa_p, kwa_p, vst_p, vwt_p, qd_p, kdt_p, kdb_p, vd4_p, vdtb_p,
     gx_p) = outs_p
    (qraw_s, qrot_s, kvc_s, kvs_s, kvw_s, _, _, _, _, qd_s, kd_s, _, vd_s, _, gx_s) = outs_s

    rt_p = t // CMP_STRIDE
    n_cmp_p = (t - CMP_BLOCK) // CMP_STRIDE + 1
    nb_p = t // SEL_BLOCK
    kvcmp_p = _compress(kvc_p.reshape(bp, rt_p, 4096), jnp.arange(bp, dtype=I32)[:, None], pea, peb, w1a, w1b, w2p, 1)
    tq = 128
    nq = t // tq
    ocmp_p, selb_p = _cmp_attend(qraw_p.reshape(bp * nq, tq, 1024), kvcmp_p, _imp_matrix(rt_p, n_cmp_p, nb_p, LANES),
                                 nq, nb_p, LANES, 0)
    tq_d = 256
    qrot3 = qrot_p.reshape(n_p // tq_d, tq_d, 1024)
    osel_p = _nsa_flash("sel", qrot3, selb_p.reshape(n_p // tq_d, tq_d, 2 * LANES), ksa_p, vst_p, bp, t, tq_d, 512)
    owin_p = _nsa_flash("win", qrot3, None, kwa_p, vwt_p, bp, t, tq_d, 256)
    odiff_p = _diff_flash(qd_p.reshape(n_p // tq_d, tq_d, 512), kdb_p, vdtb_p, dl, subln, bp, t, tq_d, 512, lambda_init)

    n_cmp_s = (past_len + ts - CMP_BLOCK) // CMP_STRIDE + 1
    rt_s = past_len // CMP_STRIDE
    assert n_cmp_s <= rt_s
    nb_s = (past_len + ts + SEL_BLOCK - 1) // SEL_BLOCK
    assert nb_s <= 256
    ppg = 16 if pages % 16 == 0 else (8 if pages % 8 == 0 else 1)
    kvcmp_s = _compress(jnp.transpose(cache_cmp_kv[0], (0, 2, 3, 4, 1)), page_table, pea, peb, w1a, w1b, w2p, ppg)
    ocmp_s, selb_s = _cmp_attend(qraw_s.reshape(bs, ts, 1024), kvcmp_s, _imp_matrix(rt_s, n_cmp_s, nb_s, 256),
                                 1, nb_s, 256, past_len)
    qrot_s3 = qrot_s.reshape(bs, ts, 1024)
    sel_t = jnp.transpose(cache_sel_kv[0], (0, 2, 3, 4, 1))
    win_t = jnp.transpose(state_win_kv[0], (0, 2, 3, 4, 1))
    dk_t = jnp.transpose(cache_diff_k[0], (0, 2, 3, 4, 1))
    osel_s = _paged_attend("sel", qrot_s3, selb_s, sel_t, None, page_table,
                           kvs_s.reshape(bs, ts, 256), None, None, None, ppg, past_len, lambda_init)
    wbuf = state_win_kv.shape[2]
    assert wbuf == WINDOW
    owin_s = _paged_attend("win", qrot_s3, None, win_t, None, jnp.arange(bs, dtype=I32)[:, None],
                           kvw_s.reshape(bs, ts, 256), None, None, None, 1, past_len, lambda_init)
    odiff_s = _paged_attend("diff", qd_s.reshape(bs, ts, 512), None, dk_t, cache_diff_v[0], page_table,
                            kd_s.reshape(bs, ts, 512), vd_s.reshape(bs, ts, 512), dl, subln, ppg, past_len, lambda_init)

    x1_p, h_p, idx_p, gate_p = _merge(x_prompt.reshape(n_p, D_MODEL), ocmp_p.reshape(n_p, 512), osel_p.reshape(n_p, 512),
                                      owin_p.reshape(n_p, 512), odiff_p.reshape(n_p, 512), gx_p, nmix, nffn, w_gates,
                                      wbn, wbd, wo, wr, br, eexp, tm)
    x1_s, h_s, idx_s, gate_s = _merge(x_sample.reshape(n_s, D_MODEL), ocmp_s.reshape(n_s, 512), osel_s.reshape(n_s, 512),
                                      owin_s.reshape(n_s, 512), odiff_s.reshape(n_s, 512), gx_s, nmix, nffn, w_gates,
                                      wbn, wbd, wo, wr, br, eexp, n_s)

    n_tok = n_p + n_s
    n_assign = n_tok * TOP_K
    tmoe = 512
    assert n_p % n_s == 0 and n_tok % n_s == 0
    n_blocks = (n_assign + N_EXPERTS * (tmoe - 1) + tmoe - 1) // tmoe
    nbp = (n_blocks + LANES - 1) // LANES * LANES
    dest, meta = _route(jnp.concatenate([idx_p, idx_s], axis=0), n_s, tmoe, nbp)
    dest_flat = dest.reshape(-1)
    xs = _dispatch(dest_flat, meta[2, :N_EXPERTS], meta[1, :1], jnp.concatenate([h_p, h_s], axis=0), n_blocks * tmoe,
                   n_s, tmoe)
    ys = _moe(meta[0, :n_blocks], meta[1, :1], xs, w_gate_up[0], b_gate_up[0][:, None, :], w_down[0],
              b_down[0][:, None, :], tmoe)

    nf = norm_final[None, :]
    y_p = _final(dest_flat, x1_p, gate_p, nf, ys, n_s, n_tok, 0)
    y_s = _final(dest_flat, x1_s, gate_s, nf, ys, n_s, n_tok, n_p // n_s)

    g, hd = NSA_GROUPS, HEAD_DIM
    def from_t(a, heads, length):
        return jnp.transpose(a.reshape(bp, heads, 2, hd, length), (0, 4, 1, 2, 3))[None]

    new_win_p = from_t(kvwt_p[:, :, t - wbuf:], 2, wbuf)
    new_win_s = jnp.concatenate([state_win_kv[0].reshape(bs, wbuf, 256)[:, ts:], kvw_s.reshape(bs, ts, 256)], axis=1)
    return (y_p.reshape(bp, t, D_MODEL), y_s.reshape(bs, ts, D_MODEL),
            from_t(kvct_p, 2, t), kvc_s.reshape(1, bs, ts, 2, g, hd),
            from_t(kvst_p, 2, t), kvs_s.reshape(1, bs, ts, 2, g, hd),
            new_win_p, new_win_s.reshape(1, bs, wbuf, 2, g, hd),
            from_t(kdt_p, DIFF_HEADS, t), kd_s.reshape(1, bs, ts, DIFF_HEADS, 2, hd),
            vd4_p[None], vd_s.reshape(1, bs, ts, DIFF_HEADS, DIFF_DV))
```

```python
import functools
import math

import numpy as np
import jax
import jax.numpy as jnp
from jax import lax
from jax.experimental import pallas as pl
from jax.experimental.pallas import tpu as pltpu

D_MODEL = 1024
HEAD_DIM = 64
NSA_HEADS = 8
NSA_GROUPS = 2
NSA_REP = 4
CMP_BLOCK = 32
CMP_STRIDE = 16
SEL_BLOCK = 64
N_SEL = 16
WINDOW = 512
DIFF_HEADS = 4
DIFF_DV = 128
ROT_DIM = 16
ROPE_THETA = 500000.0
N_EXPERTS = 32
TOP_K = 4
D_FF = 1024
SWIGLU_LIMIT = 7.0
SWIGLU_ALPHA = 1.702
RMS_EPS = 1e-5
FORCE_BONUS = 1e4
SCALE = HEAD_DIM ** -0.5
LANES = 128

BF = jnp.bfloat16
F32 = jnp.float32
I32 = jnp.int32
NEG = -1e30
VMEM_LIMIT = 56 * 1024 * 1024

OFF_Q, OFF_KVC, OFF_KVS, OFF_KVW, OFF_QD, OFF_KD, OFF_VD, OFF_GX = 0, 512, 768, 1024, 1280, 1792, 2304, 2816
IN_COLS_K = 4352
TOFF_Q, TOFF_KVC, TOFF_QD, TOFF_VD, TOFF_GX, TOFF_KS, TOFF_KW, TOFF_KD = 0, 512, 768, 1280, 1792, 1920, 2048, 2176
TIN_COLS = 2688
TIN_ROWS = 1792


def _dot(a, b):
    return jnp.dot(a, b, preferred_element_type=F32)


def _dot_nt(a, b):
    return lax.dot_general(a, b, (((1,), (1,)), ((), ())), preferred_element_type=F32)


def _dot_hi(a, b):
    return jnp.dot(a, b, preferred_element_type=F32, precision=lax.Precision.HIGHEST)


def _params(sem):
    return pltpu.CompilerParams(dimension_semantics=sem, vmem_limit_bytes=VMEM_LIMIT)


def _lane_iota(rows):
    return lax.broadcasted_iota(I32, (rows, LANES), 1)


def _pair(lo, a, b, g):
    if g == 0:
        return jnp.where(lo, a, pltpu.roll(b, 64, 1))
    return jnp.where(lo, pltpu.roll(a, 64, 1), b)


def _inproj_body(x_ref, nw_ref, w_ref, cos_ref, sin_ref, e_ref,
                 qraw_ref, qrot_ref, kvc_ref, kvs_ref, kvw_ref, ksa_ref, kwa_ref, vsb_ref, vwb_ref,
                 qd_ref, kd_ref, kdb_ref, vd_ref, vdb_ref, gx_ref):
    x = x_ref[...]
    tm = x.shape[0]
    xn = (x * lax.rsqrt(jnp.mean(x * x, axis=-1, keepdims=True) + RMS_EPS) * nw_ref[...]).astype(BF)
    cos = cos_ref[...]
    sin = sin_ref[...]
    epat = e_ref[...]
    lane = _lane_iota(tm)
    lo = lane < 64
    first8 = (lane & 63) < 8

    def mm(off):
        return _dot(xn, w_ref[:, off:off + LANES])

    def rope(y):
        sw = jnp.where(first8, pltpu.roll(y, LANES - 8, 1), pltpu.roll(y, 8, 1))
        return y * cos + sw * sin

    for c in range(4):
        y = mm(OFF_Q + c * LANES) * SCALE
        yr = rope(y)
        ys = pltpu.roll(y, 64, 1)
        yrs = pltpu.roll(yr, 64, 1)
        qraw_ref[:, (2 * c) * LANES:(2 * c + 1) * LANES] = jnp.where(lo, y, 0.0).astype(BF)
        qraw_ref[:, (2 * c + 1) * LANES:(2 * c + 2) * LANES] = jnp.where(lo, ys, 0.0).astype(BF)
        qrot_ref[:, (2 * c) * LANES:(2 * c + 1) * LANES] = jnp.where(lo, yr, 0.0).astype(BF)
        qrot_ref[:, (2 * c + 1) * LANES:(2 * c + 2) * LANES] = jnp.where(lo, yrs, 0.0).astype(BF)

    for c in range(2):
        kvc_ref[:, c * LANES:(c + 1) * LANES] = mm(OFF_KVC + c * LANES)

    for off, kv_ref, ka_ref, vb_ref in ((OFF_KVS, kvs_ref, ksa_ref, vsb_ref), (OFF_KVW, kvw_ref, kwa_ref, vwb_ref)):
        kr = rope(mm(off))
        v = mm(off + LANES)
        kv_ref[:, 0:LANES] = kr
        kv_ref[:, LANES:2 * LANES] = v
        ka_ref[:, 0:LANES] = jnp.where(lo, kr, epat).astype(BF)
        ka_ref[:, LANES:2 * LANES] = jnp.where(lo, pltpu.roll(kr, 64, 1), epat).astype(BF)
        vb_ref[...] = v.astype(BF)

    for c in range(4):
        sl = slice(c * LANES, (c + 1) * LANES)
        qd_ref[:, sl] = rope(mm(OFF_QD + c * LANES) * SCALE).astype(BF)
        kr = rope(mm(OFF_KD + c * LANES))
        kd_ref[:, sl] = kr
        kdb_ref[:, sl] = kr.astype(BF)
        v = mm(OFF_VD + c * LANES)
        vd_ref[:, sl] = v
        vdb_ref[:, sl] = v.astype(BF)

    for c in range(12):
        gx_ref[:, c * LANES:(c + 1) * LANES] = jax.nn.sigmoid(mm(OFF_GX + c * LANES))


def _inproj(x2d, norm_w, w_bf, cos_t, sin_t, e_t, tm):
    n = x2d.shape[0]
    nt = cos_t.shape[0] // tm
    row = lambda i: (i, 0)
    tab = lambda i: (i % nt, 0)
    const = lambda i: (0, 0)
    outs = [(1024, BF), (1024, BF), (256, F32), (256, F32), (256, F32), (256, BF), (256, BF), (128, BF), (128, BF),
            (512, BF), (512, F32), (512, BF), (512, F32), (512, BF), (1536, F32)]
    return pl.pallas_call(
        _inproj_body,
        grid=(n // tm,),
        in_specs=[pl.BlockSpec((tm, D_MODEL), row), pl.BlockSpec((1, D_MODEL), const),
                  pl.BlockSpec((D_MODEL, IN_COLS_K), const),
                  pl.BlockSpec((tm, LANES), tab), pl.BlockSpec((tm, LANES), tab), pl.BlockSpec((tm, LANES), tab)],
        out_specs=[pl.BlockSpec((tm, w), row) for w, _ in outs],
        out_shape=[jax.ShapeDtypeStruct((n, w), dt) for w, dt in outs],
        compiler_params=_params(("parallel",)),
        name="inproj",
    )(x2d, norm_w, w_bf, cos_t, sin_t, e_t)


def _inproj_t_body(x_ref, nw_ref, w_ref, wt_ref, cos_ref, sin_ref, cost_ref, sint_ref,
                   qraw_ref, qrot_ref, kvc_ref, kvct_ref, kvst_ref, kvwt_ref, ksa_ref, kwa_ref, vst_ref, vwt_ref,
                   qd_ref, kdt_ref, kdb_ref, vd4_ref, vdtb_ref, gx_ref):
    x = x_ref[...]
    tm = x.shape[0]
    xn = (x * lax.rsqrt(jnp.mean(x * x, axis=-1, keepdims=True) + RMS_EPS) * nw_ref[...]).astype(BF)
    cos = cos_ref[...]
    sin = sin_ref[...]
    cos_t = cost_ref[...]
    sin_t = sint_ref[...]
    lane = _lane_iota(tm)
    lo = lane < 64
    first8 = (lane & 63) < 8

    def mm(off):
        return _dot(xn, w_ref[:, off:off + LANES])

    def mm_t(off, n):
        return _dot_nt(wt_ref[off:off + n, :], xn)

    def rope(y):
        sw = jnp.where(first8, pltpu.roll(y, LANES - 8, 1), pltpu.roll(y, 8, 1))
        return y * cos + sw * sin

    def rope_t(y):
        parts = []
        for hb in range(0, y.shape[0], HEAD_DIM):
            x1 = y[hb:hb + 8]
            x2 = y[hb + 8:hb + 16]
            parts += [x1 * cos_t - x2 * sin_t, x1 * sin_t + x2 * cos_t, y[hb + 16:hb + HEAD_DIM]]
        return jnp.concatenate(parts, axis=0)

    for c in range(4):
        y = mm(TOFF_Q + c * LANES) * SCALE
        yr = rope(y)
        ys = pltpu.roll(y, 64, 1)
        yrs = pltpu.roll(yr, 64, 1)
        qraw_ref[:, (2 * c) * LANES:(2 * c + 1) * LANES] = jnp.where(lo, y, 0.0).astype(BF)
        qraw_ref[:, (2 * c + 1) * LANES:(2 * c + 2) * LANES] = jnp.where(lo, ys, 0.0).astype(BF)
        qrot_ref[:, (2 * c) * LANES:(2 * c + 1) * LANES] = jnp.where(lo, yr, 0.0).astype(BF)
        qrot_ref[:, (2 * c + 1) * LANES:(2 * c + 2) * LANES] = jnp.where(lo, yrs, 0.0).astype(BF)

    for c in range(2):
        kvc_ref[:, c * LANES:(c + 1) * LANES] = mm(TOFF_KVC + c * LANES)
    kvct_ref[0] = mm_t(0, 256)

    blk = jnp.right_shift(pl.program_id(1) * tm + lax.broadcasted_iota(I32, (tm, LANES), 0), 6)
    epat = jnp.where((lane >= 64) & (lane < 96) & (blk == lane - 64), 1.0, 0.0)
    for off, koff, kvt_ref, ka_ref, vt_ref in ((256, TOFF_KS, kvst_ref, ksa_ref, vst_ref),
                                               (512, TOFF_KW, kvwt_ref, kwa_ref, vwt_ref)):
        y = mm_t(off, 256)
        v = y[128:256]
        kvt_ref[0, 0:128, :] = rope_t(y[0:128])
        kvt_ref[0, 128:256, :] = v
        vt_ref[0] = v.astype(BF)
        kr = rope(mm(koff))
        ka_ref[:, 0:LANES] = jnp.where(lo, kr, epat).astype(BF)
        ka_ref[:, LANES:2 * LANES] = jnp.where(lo, pltpu.roll(kr, 64, 1), epat).astype(BF)

    kdt_ref[0] = rope_t(mm_t(768, 512))
    vdtb_ref[0] = mm_t(1280, 512).astype(BF)
    for c in range(4):
        sl = slice(c * LANES, (c + 1) * LANES)
        qd_ref[:, sl] = rope(mm(TOFF_QD + c * LANES) * SCALE).astype(BF)
        kdb_ref[:, sl] = rope(mm(TOFF_KD + c * LANES)).astype(BF)
        vd4_ref[0, :, c, :] = mm(TOFF_VD + c * LANES)

    gx_ref[...] = jax.nn.sigmoid(mm(TOFF_GX))


def _inproj_t(x2d, norm_w, w_bf, wt_bf, cos_t, sin_t, cos_tt, sin_tt, nbat, t, tm):
    n = x2d.shape[0]
    nt = t // tm
    row = lambda b, i: (b * nt + i, 0)
    tab = lambda b, i: (i, 0)
    tab_t = lambda b, i: (0, i)
    const = lambda b, i: (0, 0)
    tr = lambda b, i: (b, 0, i)
    rm = lambda w, dt: (pl.BlockSpec((tm, w), row), jax.ShapeDtypeStruct((n, w), dt))
    tp = lambda r, dt: (pl.BlockSpec((1, r, tm), tr), jax.ShapeDtypeStruct((nbat, r, t), dt))
    vd4 = (pl.BlockSpec((1, tm, DIFF_HEADS, DIFF_DV), lambda b, i: (b, i, 0, 0)),
           jax.ShapeDtypeStruct((nbat, t, DIFF_HEADS, DIFF_DV), F32))
    outs = [rm(1024, BF), rm(1024, BF), rm(256, F32), tp(256, F32), tp(256, F32), tp(256, F32), rm(256, BF),
            rm(256, BF), tp(128, BF), tp(128, BF), rm(512, BF), tp(512, F32), rm(512, BF), vd4, tp(512, BF),
            rm(LANES, F32)]
    return pl.pallas_call(
        _inproj_t_body,
        grid=(nbat, nt),
        in_specs=[pl.BlockSpec((tm, D_MODEL), row), pl.BlockSpec((1, D_MODEL), const),
                  pl.BlockSpec((D_MODEL, TIN_COLS), const), pl.BlockSpec((TIN_ROWS, D_MODEL), const),
                  pl.BlockSpec((tm, LANES), tab), pl.BlockSpec((tm, LANES), tab),
                  pl.BlockSpec((8, tm), tab_t), pl.BlockSpec((8, tm), tab_t)],
        out_specs=[o[0] for o in outs],
        out_shape=[o[1] for o in outs],
        compiler_params=_params(("parallel", "parallel")),
        name="inproj_t",
    )(x2d, norm_w, w_bf, wt_bf, cos_t, sin_t, cos_tt, sin_tt)


def _compress_body(pt_ref, *refs, n_pages, rp, rt, native):
    page_refs = refs[:n_pages]
    pea_ref, peb_ref, w1a_ref, w1b_ref, w2_ref, out_ref, y_scr = refs[n_pages:n_pages + 7]
    s = pl.program_id(1)
    for i in range(n_pages):
        row0 = pl.multiple_of((s * n_pages + i) * rp, 8)
        if native:
            xbuf = refs[n_pages + 7]
            lo = _lane_iota(rp) < 64
            for c in range(2):
                xbuf[c] = jnp.transpose(page_refs[i][0, c].reshape(LANES, LANES))
            for c in range(2):
                pieces = [xbuf[c, pl.ds(l, rp, stride=CMP_STRIDE), :] for l in range(CMP_STRIDE)]
                for g in range(2):
                    y_scr[c * 2 + g, pl.ds(row0, rp), :] = jnp.concatenate(
                        [_pair(lo, pieces[2 * j], pieces[2 * j + 1], g) for j in range(CMP_STRIDE // 2)], axis=1)
            continue
        x = page_refs[i][0]
        for cg in range(4):
            ycg = jnp.concatenate(
                [x[:, l * 256 + cg * 64:l * 256 + cg * 64 + 64] for l in range(CMP_STRIDE)], axis=1)
            y_scr[cg, pl.ds(row0, rp), :] = ycg

    @pl.when(s == pl.num_programs(1) - 1)
    def _():
        for c in range(2):
            for g in range(2):
                cg = c * 2 + g
                y = y_scr[cg]
                za = _dot((y + pea_ref[c]).astype(BF), w1a_ref[c])
                zb = _dot((y + peb_ref[c]).astype(BF), w1b_ref[c])
                hid = jax.nn.gelu(za + pltpu.roll(zb, rt - 1, 0))
                out_ref[0, :, cg * LANES:(cg + 1) * LANES] = _dot(hid.astype(BF), w2_ref[c]).astype(BF)


def _compress(pool, page_table, pea, peb, w1a, w1b, w2p, n_pages):
    nb, pages = page_table.shape
    native = pool.ndim == 5
    rp = LANES // CMP_STRIDE if native else pool.shape[1]
    rt = pages * rp
    steps = pages // n_pages
    pblock = (1,) + pool.shape[1:]
    page_specs = [pl.BlockSpec(pblock, functools.partial(
        lambda b, s, pt, i: (pt[b, s * n_pages + i],) + (0,) * (len(pblock) - 1), i=i)) for i in range(n_pages)]
    c3 = lambda b, s, pt: (0, 0, 0)
    scratch = [pltpu.VMEM((4, rt, 1024), F32)] + ([pltpu.VMEM((2, LANES, LANES), F32)] if native else [])
    return pl.pallas_call(
        functools.partial(_compress_body, n_pages=n_pages, rp=rp, rt=rt, native=native),
        grid_spec=pltpu.PrefetchScalarGridSpec(
            num_scalar_prefetch=1,
            grid=(nb, steps),
            in_specs=page_specs + [pl.BlockSpec((2, 1, 1024), c3), pl.BlockSpec((2, 1, 1024), c3),
                                   pl.BlockSpec((2, 1024, LANES), c3), pl.BlockSpec((2, 1024, LANES), c3),
                                   pl.BlockSpec((2, LANES, LANES), c3)],
            out_specs=pl.BlockSpec((1, rt, 512), lambda b, s, pt: (b, 0, 0)),
            scratch_shapes=scratch),
        out_shape=jax.ShapeDtypeStruct((nb, rt, 512), BF),
        compiler_params=_params(("parallel", "arbitrary")),
        name="compress",
    )(page_table, *([pool] * n_pages), pea, peb, w1a, w1b, w2p)


def _cmp_body(q_ref, kvc_ref, mimp_ref, ocmp_ref, selb_ref, *, tq, rt, nb, nbp, pos0):
    i = pl.program_id(1)
    pos_n = pos0 + i * tq + lax.broadcasted_iota(I32, (tq, rt), 0)
    n_i = lax.broadcasted_iota(I32, (tq, rt), 1)
    vis = (n_i * CMP_STRIDE + (CMP_BLOCK - 1)) <= pos_n
    lo = _lane_iota(tq) < 64
    pos_b = pos0 + i * tq + lax.broadcasted_iota(I32, (tq, nbp), 0)
    jb = lax.broadcasted_iota(I32, (tq, nbp), 1)
    cur = jnp.right_shift(pos_b, 6)
    valid = (jb * SEL_BLOCK <= pos_b) & (jb < nb)
    forced = (jb == 0) | (jb == cur) | (jb == cur - 1)
    mimp = mimp_ref[...]
    for g in range(NSA_GROUPS):
        kc = kvc_ref[0, :, g * LANES:(g + 1) * LANES]
        vc = kvc_ref[0, :, (2 + g) * LANES:(3 + g) * LANES]
        pg = jnp.zeros((tq, rt), F32)
        og = []
        for r in range(NSA_REP):
            h = g * NSA_REP + r
            q = q_ref[0, :, h * LANES:(h + 1) * LANES]
            s = jnp.where(vis, _dot_nt(q, kc), -jnp.inf)
            m = jnp.max(s, axis=-1, keepdims=True)
            m = jnp.where(m == -jnp.inf, 0.0, m)
            e = jnp.exp(s - m)
            d = jnp.sum(e, axis=-1, keepdims=True)
            p = e / jnp.where(d > 0, d, 1.0)
            og.append(_dot(p.astype(BF), vc))
            pg = pg + p
        imp = _dot_hi(pg, mimp)
        score = jnp.where(valid, jnp.where(forced, imp + FORCE_BONUS, imp), -jnp.inf)
        if tq == nbp and nb % 8 == 0:
            st = jnp.transpose(score)[0:nb]
            jrow = lax.broadcasted_iota(I32, (nb, tq), 0)
            rank_t = jnp.zeros((nb, tq), F32)
            for jj in range(nb):
                cj = st[jj:jj + 1, :]
                rank_t = rank_t + jnp.where((cj > st) | ((cj == st) & (jrow > jj)), 1.0, 0.0)
            rank = jnp.transpose(jnp.concatenate([rank_t, jnp.full((nbp - nb, tq), float(nbp), F32)], axis=0))
            sel = (rank < float(N_SEL)) & valid
        else:
            rank = jnp.zeros((tq, nbp), I32)
            for jj in range(nb):
                cj = score[:, jj:jj + 1]
                beats = (cj > score) | ((cj == score) & (jb > jj))
                rank = rank + beats.astype(I32)
            sel = (rank < N_SEL) & valid
        selb_ref[0, :, g * nbp:(g + 1) * nbp] = jnp.where(sel, 0.0, NEG)
        for c2 in range(2):
            ocmp_ref[0, :, (g * 2 + c2) * LANES:(g * 2 + c2 + 1) * LANES] = _pair(lo, og[2 * c2], og[2 * c2 + 1], 0)


def _cmp_attend(qraw3, kvc, mimp, nq, nb, nbp, pos0):
    nbat = kvc.shape[0]
    _, tq, _ = qraw3.shape
    rt = kvc.shape[1]
    return pl.pallas_call(
        functools.partial(_cmp_body, tq=tq, rt=rt, nb=nb, nbp=nbp, pos0=pos0),
        grid=(nbat, nq),
        in_specs=[pl.BlockSpec((1, tq, 1024), lambda b, i: (b * nq + i, 0, 0)),
                  pl.BlockSpec((1, rt, 512), lambda b, i: (b, 0, 0)),
                  pl.BlockSpec((rt, nbp), lambda b, i: (0, 0))],
        out_specs=[pl.BlockSpec((1, tq, 512), lambda b, i: (b * nq + i, 0, 0)),
                   pl.BlockSpec((1, tq, 2 * nbp), lambda b, i: (b * nq + i, 0, 0))],
        out_shape=[jax.ShapeDtypeStruct((nbat * nq, tq, 512), F32),
                   jax.ShapeDtypeStruct((nbat * nq, tq, 2 * nbp), F32)],
        compiler_params=_params(("parallel", "parallel")),
        name="cmp_attend",
    )(qraw3, kvc, mimp)


def _online_t(s, vt, m_s, l_s, acc, idx):
    m_old = m_s[idx]
    m_new = jnp.maximum(m_old, jnp.max(s, axis=0, keepdims=True))
    alpha = jnp.exp(m_old - m_new)
    p = jnp.exp(s - m_new)
    l_s[idx] = alpha * l_s[idx] + jnp.sum(p, axis=0, keepdims=True)
    acc[idx] = alpha * acc[idx] + _dot(vt, p.astype(BF))
    m_s[idx] = m_new


STEP_FIRST, STEP_LAST, STEP_MASKED = 1, 2, 4


def _step_table(nq, tq, tk, window=None):
    qi, kt, fl = [], [], []
    for i in range(nq):
        first = 0 if window is None else max(0, (i * tq - window + 1) // tk)
        last = (i * tq + tq - 1) // tk
        for j in range(first, last + 1):
            masked = j * tk + tk - 1 > i * tq
            if window is not None:
                masked = masked or j * tk <= i * tq + tq - 1 - window
            qi.append(i)
            kt.append(j)
            fl.append((STEP_FIRST if j == first else 0) | (STEP_LAST if j == last else 0)
                      | (STEP_MASKED if masked else 0))
    return (jnp.asarray(np.array(qi, np.int32)), jnp.asarray(np.array(kt, np.int32)),
            jnp.asarray(np.array(fl, np.int32)))


def _nsa_flash_body(qi_ref, kt_ref, fl_ref, *refs, mode, tq, tk):
    if mode == "sel":
        q_ref, selb_ref, k_ref, v_ref, o_ref, qs, m_s, l_s, acc = refs
    else:
        q_ref, k_ref, v_ref, o_ref, qs, m_s, l_s, acc = refs
    st = pl.program_id(1)
    i = qi_ref[st]
    jt = kt_ref[st]
    flags = fl_ref[st]
    q0 = i * tq
    rows = NSA_REP * tq
    lane = _lane_iota(tq)
    lo = lane < 64

    @pl.when((flags & STEP_FIRST) != 0)
    def _():
        m_s[...] = jnp.full(m_s.shape, NEG, F32)
        l_s[...] = jnp.zeros(l_s.shape, F32)
        acc[...] = jnp.zeros(acc.shape, F32)
        for g in range(NSA_GROUPS):
            if mode == "sel":
                sb = pltpu.roll(selb_ref[0, :, g * LANES:(g + 1) * LANES], 64, 1)
                sb = jnp.where((lane >= 64) & (lane < 96), sb, 0.0)
            for r in range(NSA_REP):
                h = g * NSA_REP + r
                q = q_ref[0, :, h * LANES:(h + 1) * LANES]
                if mode == "sel":
                    q = (q.astype(F32) + sb).astype(BF)
                qs[g, r * tq:(r + 1) * tq, :] = q

    need_mask = (flags & STEP_MASKED) != 0

    def step(masked):
        for g in range(NSA_GROUPS):
            s = _dot_nt(k_ref[:, g * LANES:(g + 1) * LANES], qs[g])
            if masked:
                kpos = jt * tk + lax.broadcasted_iota(I32, (tk, rows), 0)
                rpos = q0 + (lax.broadcasted_iota(I32, (tk, rows), 1) & (tq - 1))
                vis = kpos <= rpos
                if mode == "win":
                    vis = vis & (kpos > rpos - WINDOW)
                s = jnp.where(vis, s, NEG)
            _online_t(s, v_ref[0, g * HEAD_DIM:(g + 1) * HEAD_DIM, :], m_s, l_s, acc, g)

    @pl.when(need_mask)
    def _():
        step(True)

    @pl.when(jnp.logical_not(need_mask))
    def _():
        step(False)

    @pl.when((flags & STEP_LAST) != 0)
    def _():
        for g in range(NSA_GROUPS):
            on = acc[g] / l_s[g]
            for c2 in range(2):
                pair = jnp.concatenate([on[:, (2 * c2) * tq:(2 * c2 + 1) * tq],
                                        on[:, (2 * c2 + 1) * tq:(2 * c2 + 2) * tq]], axis=0)
                o_ref[0, :, (g * 2 + c2) * LANES:(g * 2 + c2 + 1) * LANES] = jnp.concatenate(
                    [jnp.transpose(pair[:, c * LANES:(c + 1) * LANES]) for c in range(tq // LANES)], axis=0)


def _nsa_flash(mode, q3, selb3, k2, v2, nbat, t, tq, tk):
    nq = t // tq
    rows = NSA_REP * tq
    table = _step_table(nq, tq, tk, None if mode == "sel" else WINDOW)
    qmap = lambda b, s, qi, kt, fl: (b * nq + qi[s], 0, 0)
    in_specs = [pl.BlockSpec((1, tq, 1024), qmap)]
    args = [q3]
    if mode == "sel":
        in_specs.append(pl.BlockSpec((1, tq, 2 * LANES), qmap))
        args.append(selb3)
    nkv = t // tk
    in_specs += [pl.BlockSpec((tk, 2 * LANES), lambda b, s, qi, kt, fl: (b * nkv + kt[s], 0)),
                 pl.BlockSpec((1, LANES, tk), lambda b, s, qi, kt, fl: (b, 0, kt[s]))]
    args += [k2, v2]
    return pl.pallas_call(
        functools.partial(_nsa_flash_body, mode=mode, tq=tq, tk=tk),
        grid_spec=pltpu.PrefetchScalarGridSpec(
            num_scalar_prefetch=3, grid=(nbat, int(table[0].shape[0])), in_specs=in_specs,
            out_specs=pl.BlockSpec((1, tq, 512), qmap),
            scratch_shapes=[pltpu.VMEM((NSA_GROUPS, rows, LANES), BF), pltpu.VMEM((NSA_GROUPS, 1, rows), F32),
                            pltpu.VMEM((NSA_GROUPS, 1, rows), F32), pltpu.VMEM((NSA_GROUPS, HEAD_DIM, rows), F32)]),
        out_shape=jax.ShapeDtypeStruct((nbat * nq, tq, 512), F32),
        compiler_params=_params(("parallel", "arbitrary")),
        name="nsa_flash_" + mode,
    )(*table, *args)


def _diff_lambda(dl, lambda_init):
    a = jnp.sum(dl[0:1] * dl[1:2], axis=1, keepdims=True)
    b = jnp.sum(dl[2:3] * dl[3:4], axis=1, keepdims=True)
    return jnp.exp(a) - jnp.exp(b) + lambda_init


def _diff_finish(a0, a1, lam, subln, lambda_init):
    o = a0 - lam * a1
    o = o * lax.rsqrt(jnp.mean(o * o, axis=-1, keepdims=True) + RMS_EPS) * subln
    return o * (1.0 - lambda_init)


def _diff_flash_body(qi_ref, kt_ref, fl_ref, q_ref, k_ref, v_ref, dl_ref, sub_ref, o_ref, qs, m_s, l_s, acc, *,
                     tq, tk, lambda_init):
    st = pl.program_id(1)
    i = qi_ref[st]
    j = kt_ref[st]
    flags = fl_ref[st]
    q0 = i * tq
    rows = 2 * tq
    lo = _lane_iota(tq) < 64

    @pl.when((flags & STEP_FIRST) != 0)
    def _():
        m_s[...] = jnp.full(m_s.shape, NEG, F32)
        l_s[...] = jnp.zeros(l_s.shape, F32)
        acc[...] = jnp.zeros(acc.shape, F32)
        for h in range(DIFF_HEADS):
            q = q_ref[0, :, h * LANES:(h + 1) * LANES].astype(F32)
            qs[h, 0:tq, :] = jnp.where(lo, q, 0.0).astype(BF)
            qs[h, tq:2 * tq, :] = jnp.where(lo, 0.0, q).astype(BF)

    need_mask = (flags & STEP_MASKED) != 0

    def step(masked):
        for h in range(DIFF_HEADS):
            s = _dot_nt(k_ref[:, h * LANES:(h + 1) * LANES], qs[h])
            if masked:
                kpos = j * tk + lax.broadcasted_iota(I32, (tk, rows), 0)
                rpos = q0 + (lax.broadcasted_iota(I32, (tk, rows), 1) & (tq - 1))
                s = jnp.where(kpos <= rpos, s, NEG)
            _online_t(s, v_ref[0, h * LANES:(h + 1) * LANES, :], m_s, l_s, acc, h)

    @pl.when(need_mask)
    def _():
        step(True)

    @pl.when(jnp.logical_not(need_mask))
    def _():
        step(False)

    @pl.when((flags & STEP_LAST) != 0)
    def _():
        lam = _diff_lambda(dl_ref[...], lambda_init)
        for h in range(DIFF_HEADS):
            on = acc[h] / l_s[h]
            a0, a1 = [jnp.concatenate([jnp.transpose(on[:, c0 + c * LANES:c0 + (c + 1) * LANES])
                                       for c in range(tq // LANES)], axis=0) for c0 in (0, tq)]
            o_ref[0, :, h * LANES:(h + 1) * LANES] = _diff_finish(a0, a1, lam, sub_ref[...], lambda_init)


def _diff_flash(q3, k2, v2, dl, subln, nbat, t, tq, tk, lambda_init):
    nq = t // tq
    nk = t // tk
    table = _step_table(nq, tq, tk)
    qmap = lambda b, s, qi, kt, fl: (b * nq + qi[s], 0, 0)
    kmap = lambda b, s, qi, kt, fl: (b * nk + kt[s], 0)
    vmap = lambda b, s, qi, kt, fl: (b, 0, kt[s])
    const = lambda b, s, qi, kt, fl: (0, 0)
    rows = 2 * tq
    return pl.pallas_call(
        functools.partial(_diff_flash_body, tq=tq, tk=tk, lambda_init=lambda_init),
        grid_spec=pltpu.PrefetchScalarGridSpec(
            num_scalar_prefetch=3, grid=(nbat, int(table[0].shape[0])),
            in_specs=[pl.BlockSpec((1, tq, 512), qmap), pl.BlockSpec((tk, 512), kmap),
                      pl.BlockSpec((1, 512, tk), vmap), pl.BlockSpec((4, HEAD_DIM), const),
                      pl.BlockSpec((1, DIFF_DV), const)],
            out_specs=pl.BlockSpec((1, tq, 512), qmap),
            scratch_shapes=[pltpu.VMEM((DIFF_HEADS, rows, LANES), BF), pltpu.VMEM((DIFF_HEADS, 1, rows), F32),
                            pltpu.VMEM((DIFF_HEADS, 1, rows), F32), pltpu.VMEM((DIFF_HEADS, DIFF_DV, rows), F32)]),
        out_shape=jax.ShapeDtypeStruct((nbat * nq, tq, 512), F32),
        compiler_params=_params(("parallel", "arbitrary")),
        name="diff_flash",
    )(*table, q3, k2, v2, dl, subln)


def _paged_body(pt_ref, *refs, mode, n_pages, past_len, lambda_init):
    q_ref = refs[0]
    k = 1
    if mode == "sel":
        selb_ref = refs[k]
        k += 1
    kpages = refs[k:k + n_pages]
    k += n_pages
    if mode == "diff":
        vpages = refs[k:k + n_pages]
        k += n_pages
    newk_ref = refs[k]
    k += 1
    if mode == "diff":
        newv_ref, dl_ref, sub_ref = refs[k:k + 3]
        k += 3
    o_ref = refs[k]
    k += 1
    if mode == "sel":
        qbd, bias, m_s, l_s, acc = refs[k:]
    else:
        qbd, m_s, l_s, acc = refs[k:]
    kw = WINDOW if mode == "win" else LANES
    st = pl.program_id(1)
    nq = 8
    rows = 64
    lane8 = _lane_iota(nq)
    lo8 = lane8 < 64
    t_row = lax.broadcasted_iota(I32, (rows, LANES), 0) & (nq - 1)
    col = lax.broadcasted_iota(I32, (rows, LANES), 1)

    def update(s, pv):
        m_old = m_s[0]
        m_new = jnp.maximum(m_old, jnp.max(s, axis=-1, keepdims=True))
        alpha = jnp.exp(m_old - m_new)
        p = jnp.exp(s - m_new)
        l_s[0] = alpha * l_s[0] + jnp.sum(p, axis=-1, keepdims=True)
        acc[0] = alpha * acc[0] + pv(p.astype(BF))
        m_s[0] = m_new

    def per_head(p, v_of_head):
        return jnp.concatenate([_dot(p[h * 16:(h + 1) * 16], v_of_head(h)) for h in range(DIFF_HEADS)], axis=0)

    @pl.when(st == 0)
    def _():
        m_s[0] = jnp.full((rows, 1), NEG, F32)
        l_s[0] = jnp.zeros((rows, 1), F32)
        acc[0] = jnp.zeros(acc.shape[1:], F32)
        if mode == "diff":
            qbd[...] = jnp.zeros(qbd.shape, F32)
            for h in range(DIFF_HEADS):
                q = q_ref[0, :, h * LANES:(h + 1) * LANES].astype(F32)
                qbd[(2 * h) * nq:(2 * h + 1) * nq, h * LANES:(h + 1) * LANES] = jnp.where(lo8, q, 0.0)
                qbd[(2 * h + 1) * nq:(2 * h + 2) * nq, h * LANES:(h + 1) * LANES] = jnp.where(lo8, 0.0, q)
        else:
            for h in range(NSA_HEADS):
                g = h // NSA_REP
                q = q_ref[0, :, h * LANES:(h + 1) * LANES].astype(F32)
                qbd[h * nq:(h + 1) * nq, :] = q if g == 0 else pltpu.roll(q, 64, 1)
                if mode == "sel":
                    bias[h * nq:(h + 1) * nq, :] = selb_ref[0, :, g * 256:(g + 1) * 256]
        kn = newk_ref[0]
        if mode == "diff":
            kk, vv = kn, newv_ref[0]
        else:
            kk, vv = kn[:, 0:LANES], kn[:, LANES:2 * LANES]
        kpad = jnp.concatenate([kk, jnp.zeros((LANES - nq, kk.shape[1]), F32)], axis=0).astype(BF)
        vpad = jnp.concatenate([vv, jnp.zeros((LANES - nq, vv.shape[1]), F32)], axis=0).astype(BF)
        s = _dot_nt(qbd[...].astype(BF), kpad)
        if mode == "sel":
            jnew = past_len // SEL_BLOCK
            s = s + bias[:, jnew:jnew + 1]
        s = jnp.where((col < nq) & (col <= t_row), s, NEG)
        if mode == "diff":
            update(s, lambda p: per_head(p, lambda h: vpad[:, h * LANES:(h + 1) * LANES]))
        else:
            update(s, lambda p: _dot(p, vpad))

    qb = qbd[...].astype(BF)
    nkeys = n_pages * kw
    if mode == "diff":
        kt = jnp.concatenate([kpages[i][0].reshape(4 * LANES, LANES).astype(BF) for i in range(n_pages)], axis=1)
    else:
        kt = jnp.concatenate([kpages[i][0, 0].reshape(LANES, kw).astype(BF) for i in range(n_pages)], axis=1)
    s = _dot(qb, kt)
    if mode == "sel":
        krow = lax.broadcasted_iota(I32, (nkeys, 256), 0)
        jbi = lax.broadcasted_iota(I32, (nkeys, 256), 1)
        ep = jnp.where(jbi == 2 * n_pages * st + jnp.right_shift(krow, 6), 1.0, 0.0).astype(BF)
        s = s + _dot_nt(bias[...].astype(BF), ep)
    if mode == "win":
        t_w = lax.broadcasted_iota(I32, (rows, nkeys), 0) & (nq - 1)
        s = jnp.where(lax.broadcasted_iota(I32, (rows, nkeys), 1) > t_w, s, NEG)
    if mode == "diff":
        update(s, lambda p: per_head(p, lambda h: jnp.concatenate(
            [vpages[i][0, :, h, :].astype(BF) for i in range(n_pages)], axis=0)))
    else:
        vt = jnp.concatenate([kpages[i][0, 1].reshape(LANES, kw).astype(BF) for i in range(n_pages)], axis=1)
        update(s, lambda p: _dot_nt(p, vt))

    @pl.when(st == pl.num_programs(1) - 1)
    def _():
        on = acc[0] / l_s[0]
        if mode == "diff":
            lam = _diff_lambda(dl_ref[...], lambda_init)
            for h in range(DIFF_HEADS):
                a0 = on[h * 16:h * 16 + nq]
                a1 = on[h * 16 + nq:h * 16 + 2 * nq]
                o_ref[0, :, h * LANES:(h + 1) * LANES] = _diff_finish(a0, a1, lam, sub_ref[...], lambda_init)
        else:
            for c in range(4):
                a = on[(2 * c) * nq:(2 * c + 1) * nq]
                b = on[(2 * c + 1) * nq:(2 * c + 2) * nq]
                o_ref[0, :, c * LANES:(c + 1) * LANES] = _pair(lo8, a, b, c // 2)


def _paged_attend(mode, q3, selb3, pool_k, pool_v, page_table, newk3, newv3, dl, subln, n_pages, past_len, lambda_init):
    nbat, pages = page_table.shape
    steps = pages // n_pages
    wq = q3.shape[2]
    wk = newk3.shape[2]
    bmap = lambda b, s, pt: (b, 0, 0)
    c2 = lambda b, s, pt: (0, 0)

    def pmap(i, rank):
        return lambda b, s, pt: (pt[b, s * n_pages + i],) + (0,) * (rank - 1)

    in_specs = [pl.BlockSpec((1, 8, wq), bmap)]
    args = [q3]
    if mode == "sel":
        in_specs.append(pl.BlockSpec((1, 8, 512), bmap))
        args.append(selb3)
    kblock = (1,) + pool_k.shape[1:]
    in_specs += [pl.BlockSpec(kblock, pmap(i, len(kblock))) for i in range(n_pages)]
    args += [pool_k] * n_pages
    if mode == "diff":
        in_specs += [pl.BlockSpec((1, LANES, DIFF_HEADS, DIFF_DV), pmap(i, 4)) for i in range(n_pages)]
        args += [pool_v] * n_pages
    in_specs.append(pl.BlockSpec((1, 8, wk), bmap))
    args.append(newk3)
    if mode == "diff":
        in_specs += [pl.BlockSpec((1, 8, 512), bmap), pl.BlockSpec((4, HEAD_DIM), c2), pl.BlockSpec((1, DIFF_DV), c2)]
        args += [newv3, dl, subln]
    ck = 512 if mode == "diff" else LANES
    scratch = [pltpu.VMEM((64, ck), F32)]
    if mode == "sel":
        scratch.append(pltpu.VMEM((64, 256), F32))
    scratch += [pltpu.VMEM((1, 64, 1), F32), pltpu.VMEM((1, 64, 1), F32), pltpu.VMEM((1, 64, LANES), F32)]
    return pl.pallas_call(
        functools.partial(_paged_body, mode=mode, n_pages=n_pages, past_len=past_len, lambda_init=lambda_init),
        grid_spec=pltpu.PrefetchScalarGridSpec(
            num_scalar_prefetch=1, grid=(nbat, steps), in_specs=in_specs,
            out_specs=pl.BlockSpec((1, 8, 512), bmap), scratch_shapes=scratch),
        out_shape=jax.ShapeDtypeStruct((nbat, 8, 512), F32),
        compiler_params=_params(("parallel", "arbitrary")),
        name="paged_" + mode,
    )(page_table, *args)


def _merge_body(x_ref, oc_ref, os_ref, ow_ref, od_ref, gx_ref, nmix_ref, nffn_ref, wg_ref, wbn_ref, wbd_ref, wo_ref,
                wr_ref, br_ref, eexp_ref, x1_ref, h_ref, idx_ref, gate_ref):
    x = x_ref[...]
    tm = x.shape[0]
    xn = (x * lax.rsqrt(jnp.mean(x * x, axis=-1, keepdims=True) + RMS_EPS) * nmix_ref[...]).astype(BF)
    gx = gx_ref[...]
    if gx.shape[1] == LANES:
        hi = gx.astype(BF)
        r1 = gx - hi.astype(F32)
        mid = r1.astype(BF)
        lo = (r1 - mid.astype(F32)).astype(BF)
        gx = _dot(hi, eexp_ref[...]) + _dot(mid, eexp_ref[...]) + _dot(lo, eexp_ref[...])
    o_nsa = gx[:, 0:512] * oc_ref[...] + gx[:, 512:1024] * os_ref[...] + gx[:, 1024:1536] * ow_ref[...]
    y_nsa = _dot(o_nsa.astype(BF), wbn_ref[...])
    y_diff = _dot(od_ref[...].astype(BF), wbd_ref[...])
    gates = jax.nn.sigmoid(_dot(xn, wg_ref[...]))
    mrg = gates[:, 0:D_MODEL] * y_nsa + gates[:, D_MODEL:2 * D_MODEL] * y_diff
    x1 = x + _dot(mrg.astype(BF), wo_ref[...])
    x1_ref[...] = x1
    h = x1 * lax.rsqrt(jnp.mean(x1 * x1, axis=-1, keepdims=True) + RMS_EPS) * nffn_ref[...]
    h_ref[...] = h
    logits = _dot_hi(h, wr_ref[...]) + br_ref[...]
    lane = _lane_iota(tm)
    vals, idxs = [], []
    for _ in range(TOP_K):
        m = jnp.max(logits, axis=-1, keepdims=True)
        idx = jnp.min(jnp.where(logits == m, lane, LANES), axis=-1, keepdims=True)
        vals.append(m)
        idxs.append(idx)
        logits = jnp.where(lane == idx, -jnp.inf, logits)
    es = [jnp.exp(v - vals[0]) for v in vals]
    den = es[0] + es[1] + es[2] + es[3]
    idx_out = jnp.zeros((tm, LANES), I32)
    gate_out = jnp.zeros((tm, LANES), F32)
    for k in range(TOP_K):
        idx_out = jnp.where(lane == k, idxs[k], idx_out)
        gate_out = jnp.where(lane == k, es[k] / den, gate_out)
    idx_ref[...] = idx_out
    gate_ref[...] = gate_out


def _merge(x2d, oc, osel, ow, od, gx, nmix, nffn, wg, wbn, wbd, wo, wr, br, eexp, tm):
    n = x2d.shape[0]
    row = lambda i: (i, 0)
    const = lambda i: (0, 0)
    return pl.pallas_call(
        _merge_body,
        grid=(n // tm,),
        in_specs=[pl.BlockSpec((tm, D_MODEL), row)] + [pl.BlockSpec((tm, 512), row)] * 4 +
                 [pl.BlockSpec((tm, gx.shape[1]), row), pl.BlockSpec((1, D_MODEL), const),
                  pl.BlockSpec((1, D_MODEL), const),
                  pl.BlockSpec((D_MODEL, 2 * D_MODEL), const), pl.BlockSpec((512, D_MODEL), const),
                  pl.BlockSpec((512, D_MODEL), const), pl.BlockSpec((D_MODEL, D_MODEL), const),
                  pl.BlockSpec((D_MODEL, LANES), const), pl.BlockSpec((1, LANES), const),
                  pl.BlockSpec((LANES, 1536), const)],
        out_specs=[pl.BlockSpec((tm, D_MODEL), row), pl.BlockSpec((tm, D_MODEL), row),
                   pl.BlockSpec((tm, LANES), row), pl.BlockSpec((tm, LANES), row)],
        out_shape=[jax.ShapeDtypeStruct((n, D_MODEL), F32), jax.ShapeDtypeStruct((n, D_MODEL), F32),
                   jax.ShapeDtypeStruct((n, LANES), I32), jax.ShapeDtypeStruct((n, LANES), F32)],
        compiler_params=_params(("parallel",)),
        name="merge",
    )(x2d, oc, osel, ow, od, gx, nmix, nffn, wg, wbn, wbd, wo, wr, br, eexp)


def _route_body(idx_ref, dest_ref, meta_ref, cnt_s, base_s, *, tm, tmoe, nbp):
    ph = pl.program_id(0)
    i = pl.program_id(1)
    shift = int(math.log2(tmoe))
    idx_t = jnp.transpose(idx_ref[...])
    sub = lax.broadcasted_iota(I32, (LANES, tm), 0)
    ohs = [idx_t[k:k + 1, :] == sub for k in range(TOP_K)]
    tok = jnp.zeros((LANES, tm), F32)
    for oh in ohs:
        tok = tok + jnp.where(oh, 1.0, 0.0)
    tile_cnt = jnp.broadcast_to(jnp.sum(tok, axis=1, keepdims=True), (LANES, LANES))

    @pl.when((ph == 0) & (i == 0))
    def _():
        cnt_s[...] = jnp.zeros((LANES, LANES), F32)

    @pl.when(ph == 0)
    def _():
        cnt_s[...] = cnt_s[...] + tile_cnt

    @pl.when((ph == 1) & (i == 0))
    def _():
        cnt = cnt_s[...].astype(I32)
        padded = jnp.left_shift(jnp.right_shift(cnt + (tmoe - 1), shift), shift)
        er = lax.broadcasted_iota(I32, (LANES, LANES), 0)
        ec = lax.broadcasted_iota(I32, (LANES, LANES), 1)
        pad_start = _dot_hi(jnp.where(ec < er, 1.0, 0.0), padded.astype(F32))
        base_s[...] = pad_start
        pad_end = pad_start.astype(I32) + padded
        blk_start = lax.broadcasted_iota(I32, (LANES, nbp), 1) * tmoe
        ends = jnp.broadcast_to(pad_end[:, 0:1], (LANES, nbp))
        be = jnp.minimum(jnp.sum(jnp.where(ends <= blk_start, 1, 0), axis=0, keepdims=True), N_EXPERTS - 1)
        n_used = jnp.right_shift(jnp.max(pad_end, axis=0, keepdims=True)[:, 0:1], shift)
        last_blk = jnp.sum(jnp.where(er == ec, jnp.where(padded > 0, jnp.right_shift(pad_end, shift) - 1, -1), 0),
                           axis=0, keepdims=True)
        lb = jnp.concatenate([last_blk, jnp.full((1, nbp - LANES), -1, I32)], axis=1)
        row = lax.broadcasted_iota(I32, (8, nbp), 0)
        meta_ref[...] = jnp.where(row == 0, be, jnp.where(row == 1, n_used, jnp.where(row == 2, lb, 0)))

    @pl.when(ph == 1)
    def _():
        nr = lax.broadcasted_iota(I32, (tm, tm), 0)
        nc = lax.broadcasted_iota(I32, (tm, tm), 1)
        before = _dot(tok.astype(BF), jnp.where(nr < nc, 1.0, 0.0).astype(BF))
        pos = before + base_s[:, 0:1]
        rows = [jnp.sum(jnp.where(oh, pos, 0.0), axis=0, keepdims=True) for oh in ohs]
        dest_ref[...] = jnp.concatenate(rows + [jnp.zeros((8 - TOP_K, tm), F32)], axis=0).astype(I32)
        base_s[...] = base_s[...] + tile_cnt


def _route(idx_all, tm, tmoe, nbp):
    n_tok = idx_all.shape[0]
    return pl.pallas_call(
        functools.partial(_route_body, tm=tm, tmoe=tmoe, nbp=nbp),
        grid=(2, n_tok // tm),
        in_specs=[pl.BlockSpec((tm, LANES), lambda ph, i: (i, 0))],
        out_specs=[pl.BlockSpec((8, tm), lambda ph, i: (0, i * ph)), pl.BlockSpec((8, nbp), lambda ph, i: (0, 0))],
        out_shape=[jax.ShapeDtypeStruct((8, n_tok), I32), jax.ShapeDtypeStruct((8, nbp), I32)],
        scratch_shapes=[pltpu.VMEM((LANES, LANES), F32), pltpu.VMEM((LANES, LANES), F32)],
        compiler_params=_params(("arbitrary", "arbitrary")),
        name="route",
    )(idx_all)


def _dispatch_body(dest_ref, lb_ref, nu_ref, h_ref, xs_hbm, zbuf, zsem, sem, *, tm, tmoe, n_tok):
    i = pl.program_id(0)

    @pl.when(i == 0)
    def _():
        zbuf[...] = jnp.zeros(zbuf.shape, F32)

        def fill(e):
            return pltpu.make_async_copy(zbuf, xs_hbm.at[pl.ds(pl.multiple_of(lb_ref[e] * tmoe, tmoe), tmoe)], zsem)
        for e in range(N_EXPERTS):
            @pl.when(lb_ref[e] >= 0)
            def _():
                fill(e).start()
        for e in range(N_EXPERTS):
            @pl.when(lb_ref[e] >= 0)
            def _():
                fill(e).wait()

        def tail(b):
            return pltpu.make_async_copy(zbuf, xs_hbm.at[pl.ds(pl.multiple_of(b * tmoe, tmoe), tmoe)], zsem)
        n_blk = xs_hbm.shape[0] // tmoe
        lax.fori_loop(nu_ref[0], n_blk, lambda b, c: (tail(b).start(), c)[1], 0)
        lax.fori_loop(nu_ref[0], n_blk, lambda b, c: (tail(b).wait(), c)[1], 0)

    def body(n, c):
        for k in range(TOP_K):
            d = dest_ref[k * n_tok + i * tm + n]
            pltpu.make_async_copy(h_ref.at[pl.ds(n, 1)], xs_hbm.at[pl.ds(d, 1)], sem).start(priority=k % 2)
        return c
    lax.fori_loop(0, tm, body, 0, unroll=8)
    pltpu.make_async_copy(xs_hbm.at[pl.ds(0, TOP_K * tm)], xs_hbm.at[pl.ds(0, TOP_K * tm)], sem).wait()


def _dispatch(dest_flat, last_blk, n_used, h_all, n_rows, tm, tmoe):
    n_tok = h_all.shape[0]
    return pl.pallas_call(
        functools.partial(_dispatch_body, tm=tm, tmoe=tmoe, n_tok=n_tok),
        grid_spec=pltpu.PrefetchScalarGridSpec(
            num_scalar_prefetch=3, grid=(n_tok // tm,),
            in_specs=[pl.BlockSpec((tm, D_MODEL), lambda i, d, lb, nu: (i, 0))],
            out_specs=pl.BlockSpec(memory_space=pl.ANY),
            scratch_shapes=[pltpu.VMEM((tmoe, D_MODEL), F32), pltpu.SemaphoreType.DMA(()), pltpu.SemaphoreType.DMA(())]),
        out_shape=jax.ShapeDtypeStruct((n_rows, D_MODEL), F32),
        compiler_params=pltpu.CompilerParams(dimension_semantics=("arbitrary",), vmem_limit_bytes=VMEM_LIMIT,
                                             disable_bounds_checks=True),
        name="moe_dispatch",
    )(dest_flat, last_blk, n_used, h_all)


def _moe_body(be_ref, nused_ref, x_ref, wgu_ref, bgu_ref, wd_ref, bd_ref, o_ref, wgu_s, wd_s):
    i = pl.program_id(0)
    e = be_ref[i]
    prev = be_ref[jnp.maximum(i - 1, 0)]

    @pl.when((i == 0) | (e != prev))
    def _():
        def cast_rows(c, carry):
            r0 = pl.multiple_of(c * LANES, LANES)
            wgu_s[pl.ds(r0, LANES), :] = wgu_ref[0, pl.ds(r0, LANES), :].astype(BF)
            wd_s[pl.ds(r0, LANES), :] = wd_ref[0, pl.ds(r0, LANES), :].astype(BF)
            return carry
        lax.fori_loop(0, D_MODEL // LANES, cast_rows, 0)

    @pl.when(i < nused_ref[0])
    def _():
        gu = _dot(x_ref[...].astype(BF), wgu_s[...]) + bgu_ref[0]
        gate = jnp.minimum(gu[:, 0:D_FF], SWIGLU_LIMIT)
        up = jnp.clip(gu[:, D_FF:2 * D_FF], -SWIGLU_LIMIT, SWIGLU_LIMIT)
        act = (up + 1.0) * gate * jax.nn.sigmoid(SWIGLU_ALPHA * gate)
        o_ref[...] = _dot(act.astype(BF), wd_s[...]) + bd_ref[0]

    @pl.when(i >= nused_ref[0])
    def _():
        o_ref[...] = jnp.zeros(o_ref.shape, F32)


def _moe(block_expert, n_used, xs, wgu, bgu, wd, bd, tmoe):
    n_rows = xs.shape[0]
    nblk = n_rows // tmoe
    xmap = lambda i, be, nu: (jnp.minimum(i, jnp.maximum(nu[0] - 1, 0)), 0)
    emap = lambda i, be, nu: (be[i], 0, 0)
    return pl.pallas_call(
        _moe_body,
        grid_spec=pltpu.PrefetchScalarGridSpec(
            num_scalar_prefetch=2, grid=(nblk,),
            in_specs=[pl.BlockSpec((tmoe, D_MODEL), xmap),
                      pl.BlockSpec((1, D_MODEL, 2 * D_FF), emap), pl.BlockSpec((1, 1, 2 * D_FF), emap),
                      pl.BlockSpec((1, D_FF, D_MODEL), emap), pl.BlockSpec((1, 1, D_MODEL), emap)],
            out_specs=pl.BlockSpec((tmoe, D_MODEL), lambda i, be, nu: (i, 0)),
            scratch_shapes=[pltpu.VMEM((D_MODEL, 2 * D_FF), BF), pltpu.VMEM((D_FF, D_MODEL), BF)]),
        out_shape=jax.ShapeDtypeStruct((n_rows, D_MODEL), F32),
        compiler_params=_params(("arbitrary",)),
        name="moe_experts",
    )(block_expert, n_used, xs, wgu, bgu, wd, bd)


def _final_body(dest_ref, x1_ref, gate_ref, nf_ref, ys_hbm, y_ref, ybuf, sem, *, tm, n_tok, block0):
    i = pl.program_id(0)

    def gather(tile, sl):
        def body(n, c):
            for k in range(TOP_K):
                d = dest_ref[k * n_tok + (block0 + tile) * tm + n]
                pltpu.make_async_copy(ys_hbm.at[pl.ds(d, 1)], ybuf.at[sl, k, pl.ds(n, 1)],
                                      sem.at[sl]).start(priority=k % 2)
            return c
        lax.fori_loop(0, tm, body, 0, unroll=8)

    @pl.when(i == 0)
    def _():
        gather(0, 0)

    @pl.when(i + 1 < pl.num_programs(0))
    def _():
        gather(i + 1, (i + 1) % 2)

    slot = i % 2
    pltpu.make_async_copy(ybuf.at[slot], ybuf.at[slot], sem.at[slot]).wait()
    gate = gate_ref[...]
    x = x1_ref[...]
    for k in range(TOP_K):
        x = x + gate[:, k:k + 1] * ybuf[slot, k]
    y_ref[...] = x * lax.rsqrt(jnp.mean(x * x, axis=-1, keepdims=True) + RMS_EPS) * nf_ref[...]


def _final(dest_flat, x1, gates, nf, ys, tm, n_tok, block0):
    n = x1.shape[0]
    row = lambda i, d: (i, 0)
    return pl.pallas_call(
        functools.partial(_final_body, tm=tm, n_tok=n_tok, block0=block0),
        grid_spec=pltpu.PrefetchScalarGridSpec(
            num_scalar_prefetch=1, grid=(n // tm,),
            in_specs=[pl.BlockSpec((tm, D_MODEL), row), pl.BlockSpec((tm, LANES), row),
                      pl.BlockSpec((1, D_MODEL), lambda i, d: (0, 0)), pl.BlockSpec(memory_space=pl.ANY)],
            out_specs=pl.BlockSpec((tm, D_MODEL), row),
            scratch_shapes=[pltpu.VMEM((2, TOP_K, tm, D_MODEL), F32), pltpu.SemaphoreType.DMA((2,))]),
        out_shape=jax.ShapeDtypeStruct((n, D_MODEL), F32),
        compiler_params=pltpu.CompilerParams(dimension_semantics=("arbitrary",), vmem_limit_bytes=VMEM_LIMIT,
                                             disable_bounds_checks=True),
        name="final_norm",
    )(dest_flat, x1, gates, nf, ys)


def _rope_tables(pos, nb_lanes):
    half = ROT_DIM // 2
    inv = ROPE_THETA ** (-jnp.arange(half, dtype=F32) / half)
    ang = pos.astype(F32)[:, None] * inv[None, :]
    cos, sin = jnp.cos(ang), jnp.sin(ang)
    t = pos.shape[0]
    c64 = jnp.concatenate([cos, cos, jnp.ones((t, HEAD_DIM - ROT_DIM), F32)], axis=1)
    s64 = jnp.concatenate([-sin, sin, jnp.zeros((t, HEAD_DIM - ROT_DIM), F32)], axis=1)
    lane = jnp.arange(LANES)[None, :]
    e = ((lane >= 64) & (lane < 64 + nb_lanes) & ((pos // SEL_BLOCK)[:, None] == lane - 64)).astype(F32)
    return jnp.tile(c64, (1, 2)), jnp.tile(s64, (1, 2)), e


def _imp_matrix(rt, n_cmp, nb, nbp):
    m = np.zeros((rt, nbp), np.float32)
    sub = SEL_BLOCK // CMP_STRIDE
    for n in range(n_cmp):
        for shift in range(CMP_BLOCK // CMP_STRIDE):
            j = (n + shift) // sub
            if j < nb:
                m[n, j] += 1.0
    return jnp.asarray(m)


def kernel(x_prompt, x_sample, cache_cmp_kv, cache_sel_kv, state_win_kv, cache_diff_k, cache_diff_v, page_table, norm_mix, w_in, cmp_pe, w_cmp1, w_cmp2, diff_lambda, diff_subln, w_br_nsa, w_br_diff, w_out, norm_ffn, w_router, b_router, w_gate_up, b_gate_up, w_down, b_down, norm_final):
    bp, t, _ = x_prompt.shape
    bs, ts, _ = x_sample.shape
    assert ts == 8 and t % 256 == 0 and t <= 2048 and w_in.shape[0] == 1
    n_pool, page = cache_cmp_kv.shape[1], cache_cmp_kv.shape[2]
    pages = page_table.shape[1]
    past_len = pages * page
    assert page == LANES and past_len % SEL_BLOCK == 0
    lambda_init = 0.8 - 0.6 * math.exp(-0.3 * 0)
    n_p, n_s = bp * t, bs * ts

    w = w_in[0]
    gcols = w[:, 1280:1304].reshape(D_MODEL, NSA_HEADS, 3).transpose(0, 2, 1)
    gexp = jnp.broadcast_to(gcols[..., None], (D_MODEL, 3, NSA_HEADS, HEAD_DIM)).reshape(D_MODEL, 1536)
    w_proj = jnp.concatenate([w[:, 0:1280], w[:, 1304:2840], gexp], axis=1).astype(BF)
    w_gates = w[:, 2840:4888].astype(BF)
    nmix = norm_mix[0][None, :]
    nffn = norm_ffn[0][None, :]
    pe = cmp_pe[0]
    pea = pe[:, :CMP_STRIDE].reshape(2, 1, 1024)
    peb = pe[:, CMP_STRIDE:].reshape(2, 1, 1024)
    w1a = w_cmp1[0][:, :1024].astype(BF)
    w1b = w_cmp1[0][:, 1024:].astype(BF)
    w2p = jnp.pad(w_cmp2[0], ((0, 0), (0, 0), (0, LANES - HEAD_DIM))).astype(BF)
    wbn = w_br_nsa[0].astype(BF)
    wbd = w_br_diff[0].astype(BF)
    wo = w_out[0].astype(BF)
    wr = jnp.pad(w_router[0], ((0, 0), (0, LANES - N_EXPERTS)))
    br = jnp.concatenate([b_router[0], jnp.full((LANES - N_EXPERTS,), NEG, F32)])[None, :]
    dl = diff_lambda[0]
    subln = diff_subln[0][None, :]

    pos_p = jnp.arange(t, dtype=I32)
    pos_s = jnp.tile(past_len + jnp.arange(ts, dtype=I32), bs)
    tm = 512
    gpad = jnp.pad(w[:, 1280:1304], ((0, 0), (0, LANES - 3 * NSA_HEADS)))
    w_rows = jnp.concatenate([w[:, 0:768], w[:, 1304:1816], w[:, 2328:2840], gpad, w[:, 768:896], w[:, 1024:1152],
                              w[:, 1816:2328]], axis=1).astype(BF)
    eexp_np = np.zeros((LANES, 1536), np.float32)
    for hh in range(NSA_HEADS):
        for brn in range(3):
            eexp_np[hh * 3 + brn, brn * 512 + hh * HEAD_DIM:brn * 512 + (hh + 1) * HEAD_DIM] = 1.0
    eexp = jnp.asarray(eexp_np).astype(BF)
    w_cols = jnp.concatenate([w[:, 512:1280], w[:, 1816:2840]], axis=1).T.astype(BF)
    cos_p, sin_p, _ = _rope_tables(pos_p, 0)
    outs_p = _inproj_t(x_prompt.reshape(n_p, D_MODEL), nmix, w_rows, w_cols, cos_p, sin_p,
                       cos_p[:, 0:8].T, -sin_p[:, 0:8].T, bp, t, tm)
    outs_s = _inproj(x_sample.reshape(n_s, D_MODEL), nmix, w_proj, *_rope_tables(pos_s, 0), n_s)
    (qraw_p, qrot_p, kvc_p, kvct_p, kvst_p, kvwt_p, ksa_p, kwa_p, vst_p, vwt_p, qd_p, kdt_p, kdb_p, vd4_p, vdtb_p,
     gx_p) = outs_p
    (qraw_s, qrot_s, kvc_s, kvs_s, kvw_s, _, _, _, _, qd_s, kd_s, _, vd_s, _, gx_s) = outs_s

    rt_p = t // CMP_STRIDE
    n_cmp_p = (t - CMP_BLOCK) // CMP_STRIDE + 1
    nb_p = t // SEL_BLOCK
    kvcmp_p = _compress(kvc_p.reshape(bp, rt_p, 4096), jnp.arange(bp, dtype=I32)[:, None], pea, peb, w1a, w1b, w2p, 1)
    tq = 128
    nq = t // tq
    ocmp_p, selb_p = _cmp_attend(qraw_p.reshape(bp * nq, tq, 1024), kvcmp_p, _imp_matrix(rt_p, n_cmp_p, nb_p, LANES),
                                 nq, nb_p, LANES, 0)
    tq_d = 256
    qrot3 = qrot_p.reshape(n_p // tq_d, tq_d, 1024)
    osel_p = _nsa_flash("sel", qrot3, selb_p.reshape(n_p // tq_d, tq_d, 2 * LANES), ksa_p, vst_p, bp, t, tq_d, 512)
    owin_p = _nsa_flash("win", qrot3, None, kwa_p, vwt_p, bp, t, tq_d, 256)
    odiff_p = _diff_flash(qd_p.reshape(n_p // tq_d, tq_d, 512), kdb_p, vdtb_p, dl, subln, bp, t, tq_d, 512, lambda_init)

    n_cmp_s = (past_len + ts - CMP_BLOCK) // CMP_STRIDE + 1
    rt_s = past_len // CMP_STRIDE
    assert n_cmp_s <= rt_s
    nb_s = (past_len + ts + SEL_BLOCK - 1) // SEL_BLOCK
    assert nb_s <= 256
    ppg = 16 if pages % 16 == 0 else (8 if pages % 8 == 0 else 1)
    kvcmp_s = _compress(jnp.transpose(cache_cmp_kv[0], (0, 2, 3, 4, 1)), page_table, pea, peb, w1a, w1b, w2p, ppg)
    ppg = 32 if pages % 32 == 0 else ppg
    ocmp_s, selb_s = _cmp_attend(qraw_s.reshape(bs, ts, 1024), kvcmp_s, _imp_matrix(rt_s, n_cmp_s, nb_s, 256),
                                 1, nb_s, 256, past_len)
    qrot_s3 = qrot_s.reshape(bs, ts, 1024)
    sel_t = jnp.transpose(cache_sel_kv[0], (0, 2, 3, 4, 1))
    win_t = jnp.transpose(state_win_kv[0], (0, 2, 3, 4, 1))
    dk_t = jnp.transpose(cache_diff_k[0], (0, 2, 3, 4, 1))
    osel_s = _paged_attend("sel", qrot_s3, selb_s, sel_t, None, page_table,
                           kvs_s.reshape(bs, ts, 256), None, None, None, ppg, past_len, lambda_init)
    wbuf = state_win_kv.shape[2]
    assert wbuf == WINDOW
    owin_s = _paged_attend("win", qrot_s3, None, win_t, None, jnp.arange(bs, dtype=I32)[:, None],
                           kvw_s.reshape(bs, ts, 256), None, None, None, 1, past_len, lambda_init)
    odiff_s = _paged_attend("diff", qd_s.reshape(bs, ts, 512), None, dk_t, cache_diff_v[0], page_table,
                            kd_s.reshape(bs, ts, 512), vd_s.reshape(bs, ts, 512), dl, subln, ppg, past_len, lambda_init)

    x1_p, h_p, idx_p, gate_p = _merge(x_prompt.reshape(n_p, D_MODEL), ocmp_p.reshape(n_p, 512), osel_p.reshape(n_p, 512),
                                      owin_p.reshape(n_p, 512), odiff_p.reshape(n_p, 512), gx_p, nmix, nffn, w_gates,
                                      wbn, wbd, wo, wr, br, eexp, tm)
    x1_s, h_s, idx_s, gate_s = _merge(x_sample.reshape(n_s, D_MODEL), ocmp_s.reshape(n_s, 512), osel_s.reshape(n_s, 512),
                                      owin_s.reshape(n_s, 512), odiff_s.reshape(n_s, 512), gx_s, nmix, nffn, w_gates,
                                      wbn, wbd, wo, wr, br, eexp, n_s)

    n_tok = n_p + n_s
    n_assign = n_tok * TOP_K
    tmoe = 512
    assert n_p % n_s == 0 and n_tok % n_s == 0
    n_blocks = (n_assign + N_EXPERTS * (tmoe - 1) + tmoe - 1) // tmoe
    nbp = (n_blocks + LANES - 1) // LANES * LANES
    dest, meta = _route(jnp.concatenate([idx_p, idx_s], axis=0), n_s, tmoe, nbp)
    dest_flat = dest.reshape(-1)
    xs = _dispatch(dest_flat, meta[2, :N_EXPERTS], meta[1, :1], jnp.concatenate([h_p, h_s], axis=0), n_blocks * tmoe,
                   n_s, tmoe)
    ys = _moe(meta[0, :n_blocks], meta[1, :1], xs, w_gate_up[0], b_gate_up[0][:, None, :], w_down[0],
              b_down[0][:, None, :], tmoe)

    nf = norm_final[None, :]
    y_p = _final(dest_flat, x1_p, gate_p, nf, ys, n_s, n_tok, 0)
    y_s = _final(dest_flat, x1_s, gate_s, nf, ys, n_s, n_tok, n_p // n_s)

    g, hd = NSA_GROUPS, HEAD_DIM
    def from_t(a, heads, length):
        return jnp.transpose(a.reshape(bp, heads, 2, hd, length), (0, 4, 1, 2, 3))[None]

    new_win_p = from_t(kvwt_p[:, :, t - wbuf:], 2, wbuf)
    new_win_s = jnp.concatenate([state_win_kv[0].reshape(bs, wbuf, 256)[:, ts:], kvw_s.reshape(bs, ts, 256)], axis=1)
    return (y_p.reshape(bp, t, D_MODEL), y_s.reshape(bs, ts, D_MODEL),
            from_t(kvct_p, 2, t), kvc_s.reshape(1, bs, ts, 2, g, hd),
            from_t(kvst_p, 2, t), kvs_s.reshape(1, bs, ts, 2, g, hd),
            new_win_p, new_win_s.reshape(1, bs, wbuf, 2, g, hd),
            from_t(kdt_p, DIFF_HEADS, t), kd_s.reshape(1, bs, ts, DIFF_HEADS, 2, hd),
            vd4_p[None], vd_s.reshape(1, bs, ts, DIFF_HEADS, DIFF_DV))
```
